```python
import jax, jax.numpy as jnp
from jax import lax
import numpy as np

D_MODEL = 2048
BATCH = 8
SEQ = 4096
DEPTH = 1

D_MIX = D_MODEL
HEAD_DIM = 64
RWKV_WIDTH = D_MIX // 2
FOX_WIDTH = D_MIX - RWKV_WIDTH
RWKV_HEADS = RWKV_WIDTH // HEAD_DIM
FOX_HEADS = FOX_WIDTH // HEAD_DIM
DECAY_LORA = max(32, int(round(1.8 * D_MODEL ** 0.5 / 32)) * 32)
ICLR_LORA = max(32, int(round(1.8 * D_MODEL ** 0.5 / 32)) * 32)
GATE_LORA = max(32, int(round(0.6 * D_MODEL ** 0.8 / 32)) * 32)
RWKV_COLS = 3 * RWKV_WIDTH + DECAY_LORA + ICLR_LORA + GATE_LORA
FOX_COLS = 3 * FOX_WIDTH + FOX_HEADS
N_IN = RWKV_COLS + FOX_COLS
D_FF = -(-(8 * D_MODEL) // (3 * 256)) * 256
PLE_DIM = 256
BLOCK_Q = 128
RMS_EPS = 1e-6
GN_EPS = 64e-5

kernel_name = "rwkv7_fox_hybrid_layer"


def rmsnorm(x, g, eps=RMS_EPS):
    xf = x.astype(jnp.float32)
    y = xf * lax.rsqrt(jnp.mean(xf * xf, axis=-1, keepdims=True) + eps)
    return (y * g.astype(jnp.float32)).astype(x.dtype)


def token_shift(u):
    return jnp.pad(u, ((0, 0), (1, 0), (0, 0)))[:, :-1]


def rwkv7_scan(r, w, k, v, kk, a):
    b, s, h, n = r.shape
    seq_first = lambda t: jnp.swapaxes(t, 0, 1)

    def step(state, inp):
        r_t, w_t, k_t, v_t, kk_t, a_t = inp
        sa = jnp.einsum('bhvk,bhk->bhv', state, -kk_t)
        state = (state * w_t[:, :, None, :]
                 + sa[..., None] * (kk_t * a_t)[:, :, None, :]
                 + v_t[..., None] * k_t[:, :, None, :])
        y_t = jnp.einsum('bhvk,bhk->bhv', state, r_t)
        return state, y_t

    state0 = jnp.zeros((b, h, n, n), jnp.float32)
    _, y = lax.scan(step, state0, tuple(seq_first(t) for t in (r, w, k, v, kk, a)))
    return seq_first(y)


def rwkv7_mix(u, mu, w0, w2, a0, a2, g2, k_k, k_a, r_k, lnx_g, lnx_b):
    b, s, _ = u.shape
    u = (u + (token_shift(u) - u) * mu).astype(jnp.float32)
    r, k, v, xw, xa, xg = jnp.split(
        u, np.cumsum([RWKV_WIDTH, RWKV_WIDTH, RWKV_WIDTH, DECAY_LORA, ICLR_LORA]).tolist(), axis=-1)
    w_log = -jax.nn.softplus(-(w0 + jnp.tanh(xw) @ w2)) - 0.5
    decay = jnp.exp(-jnp.exp(w_log))
    a = jax.nn.sigmoid(a0 + xa @ a2)
    g = jax.nn.sigmoid(xg) @ g2
    heads = lambda t: t.reshape(b, s, RWKV_HEADS, HEAD_DIM)
    kk = heads(k * k_k)
    kk = kk / jnp.maximum(jnp.linalg.norm(kk, axis=-1, keepdims=True), 1e-12)
    k = k * (1.0 + (a - 1.0) * k_a)
    rh, kh, vh = heads(r), heads(k), heads(v)
    y = rwkv7_scan(rh, heads(decay), kh, vh, kk, heads(a))
    mean = jnp.mean(y, axis=-1, keepdims=True)
    var = jnp.mean(jnp.square(y - mean), axis=-1, keepdims=True)
    y = ((y - mean) * lax.rsqrt(var + GN_EPS)).reshape(b, s, RWKV_WIDTH) * lnx_g + lnx_b
    bonus = jnp.sum(rh * kh * r_k, axis=-1, keepdims=True) * vh
    y = (y + bonus.reshape(b, s, RWKV_WIDTH)) * g
    return y


def fox_mix(u, q_norm_g, k_norm_g, fgate_b):
    b, s, _ = u.shape
    q, k, v, f_raw = jnp.split(u, [FOX_WIDTH, 2 * FOX_WIDTH, 3 * FOX_WIDTH], axis=-1)
    heads = lambda t: t.reshape(b, s, FOX_HEADS, HEAD_DIM).astype(jnp.float32)
    q = rmsnorm(heads(q), q_norm_g)
    k = rmsnorm(heads(k), k_norm_g)
    v = heads(v)
    log_f = jax.nn.log_sigmoid(f_raw.astype(jnp.float32) + fgate_b)
    c = jnp.cumsum(log_f, axis=1).transpose(0, 2, 1)
    scale = HEAD_DIM ** -0.5
    outs = []
    for blk in range(s // BLOCK_Q):
        s0, s1 = blk * BLOCK_Q, (blk + 1) * BLOCK_Q
        scores = jnp.einsum('bqhd,bkhd->bhqk', q[:, s0:s1], k[:, :s1]) * scale
        bias = c[:, :, s0:s1, None] - c[:, :, None, :s1]
        causal = jnp.arange(s1)[None, :] <= (s0 + jnp.arange(BLOCK_Q))[:, None]
        scores = jnp.where(causal, scores + bias, -jnp.inf)
        probs = jax.nn.softmax(scores, axis=-1)
        outs.append(jnp.einsum('bhqk,bkhd->bqhd', probs, v[:, :s1]))
    return jnp.concatenate(outs, axis=1).reshape(b, s, FOX_WIDTH)


def _fwd_setup_inputs(seed: int = 0) -> dict:
    key = jax.random.key(seed)
    ks = jax.random.split(key, 32)
    nrm = lambda k, shape, sc: jax.random.normal(k, shape, jnp.float32) * sc
    uni = lambda k, shape, lo, hi: jax.random.uniform(k, shape, jnp.float32, lo, hi)
    L = DEPTH
    return {
        "x": nrm(ks[0], (BATCH, SEQ, D_MODEL), 1.0),
        "p": nrm(ks[1], (DEPTH, BATCH, SEQ, PLE_DIM), 1.0),
        "attn_norm_g": 1.0 + nrm(ks[2], (L, D_MODEL), 0.02),
        "w_in": nrm(ks[3], (L, D_MODEL, N_IN), D_MODEL ** -0.5),
        "shift_mu": uni(ks[4], (L, RWKV_COLS), 0.0, 1.0),
        "w0": uni(ks[5], (L, RWKV_WIDTH), -6.0, 1.0),
        "w2": nrm(ks[6], (L, DECAY_LORA, RWKV_WIDTH), 0.5 * DECAY_LORA ** -0.5),
        "a0": nrm(ks[7], (L, RWKV_WIDTH), 0.1),
        "a2": nrm(ks[8], (L, ICLR_LORA, RWKV_WIDTH), 0.5 * ICLR_LORA ** -0.5),
        "g2": nrm(ks[9], (L, GATE_LORA, RWKV_WIDTH), GATE_LORA ** -0.5),
        "k_k": 0.85 + nrm(ks[10], (L, RWKV_WIDTH), 0.05),
        "k_a": 1.0 + nrm(ks[11], (L, RWKV_WIDTH), 0.05),
        "r_k": nrm(ks[12], (L, RWKV_HEADS, HEAD_DIM), 0.1),
        "lnx_g": 1.0 + nrm(ks[13], (L, RWKV_WIDTH), 0.02),
        "lnx_b": nrm(ks[14], (L, RWKV_WIDTH), 0.02),
        "q_norm_g": 1.0 + nrm(ks[15], (L, HEAD_DIM), 0.02),
        "k_norm_g": 1.0 + nrm(ks[16], (L, HEAD_DIM), 0.02),
        "fgate_b": uni(ks[17], (L, FOX_HEADS), 1.0, 6.0),
        "w_out": nrm(ks[18], (L, D_MIX, D_MODEL), D_MIX ** -0.5),
        "ffn_norm_g": 1.0 + nrm(ks[19], (L, D_MODEL), 0.02),
        "w_gate": nrm(ks[20], (L, D_MODEL, D_FF), D_MODEL ** -0.5),
        "w_up": nrm(ks[21], (L, D_MODEL, D_FF), D_MODEL ** -0.5),
        "w_down": nrm(ks[22], (L, D_FF, D_MODEL), D_FF ** -0.5),
        "ple_proj": nrm(ks[23], (L, PLE_DIM, D_MODEL), PLE_DIM ** -0.5),
        "ple_norm_g": 1.0 + nrm(ks[24], (L, D_MODEL), 0.02),
        "ple_gate_norm_g": 1.0 + nrm(ks[25], (L, D_MODEL), 0.02),
        "ple_gate_w": nrm(ks[26], (L, D_MODEL, D_MODEL), D_MODEL ** -0.5),
        "ple_gate_b": nrm(ks[27], (L, D_MODEL), 0.02),
    }


def _fwd_reference(x, p, attn_norm_g, w_in, shift_mu, w0, w2, a0, a2, g2, k_k, k_a, r_k,
              lnx_g, lnx_b, q_norm_g, k_norm_g, fgate_b, w_out, ffn_norm_g, w_gate,
              w_up, w_down, ple_proj, ple_norm_g, ple_gate_norm_g, ple_gate_w, ple_gate_b):
    h = x
    for i in range(DEPTH):
        u = rmsnorm(h, attn_norm_g[i]) @ w_in[i]
        u_rwkv, u_fox = u[..., :RWKV_COLS], u[..., RWKV_COLS:]
        y_rwkv = rwkv7_mix(u_rwkv, shift_mu[i], w0[i], w2[i], a0[i], a2[i], g2[i],
                           k_k[i], k_a[i], r_k[i], lnx_g[i], lnx_b[i])
        y_fox = fox_mix(u_fox, q_norm_g[i], k_norm_g[i], fgate_b[i])
        y = jnp.concatenate([y_rwkv, y_fox], axis=-1).astype(h.dtype)
        h = h + y @ w_out[i]
        hn = rmsnorm(h, ffn_norm_g[i])
        h = h + (jax.nn.silu(hn @ w_gate[i]) * (hn @ w_up[i])) @ w_down[i]
        e = rmsnorm(p[i] @ ple_proj[i], ple_norm_g[i])
        gate = jax.nn.sigmoid(rmsnorm(h, ple_gate_norm_g[i]) @ ple_gate_w[i] + ple_gate_b[i])
        h = h + gate * e
    return h


import jax as _jax
import jax.numpy as _jnp

TWIN_FORMAT = 'train_step'
FWD_PARAMS = ['x', 'p', 'attn_norm_g', 'w_in', 'shift_mu', 'w0', 'w2', 'a0', 'a2', 'g2', 'k_k', 'k_a', 'r_k', 'lnx_g', 'lnx_b', 'q_norm_g', 'k_norm_g', 'fgate_b', 'w_out', 'ffn_norm_g', 'w_gate', 'w_up', 'w_down', 'ple_proj', 'ple_norm_g', 'ple_gate_norm_g', 'ple_gate_w', 'ple_gate_b']
TWIN_WEIGHTS = ['attn_norm_g', 'w_in', 'shift_mu', 'w0', 'w2', 'a0', 'a2', 'g2', 'k_k', 'k_a', 'r_k', 'lnx_g', 'lnx_b', 'q_norm_g', 'k_norm_g', 'fgate_b', 'w_out', 'ffn_norm_g', 'w_gate', 'w_up', 'w_down', 'ple_proj', 'ple_norm_g', 'ple_gate_norm_g', 'ple_gate_w', 'ple_gate_b']
TWIN_DIFF_INPUT = 'x'
TWIN_INPUTS = ['x', 'p', 'attn_norm_g', 'w_in', 'shift_mu', 'w0', 'w2', 'a0', 'a2', 'g2', 'k_k', 'k_a', 'r_k', 'lnx_g', 'lnx_b', 'q_norm_g', 'k_norm_g', 'fgate_b', 'w_out', 'ffn_norm_g', 'w_gate', 'w_up', 'w_down', 'ple_proj', 'ple_norm_g', 'ple_gate_norm_g', 'ple_gate_w', 'ple_gate_b', 'loss_target', 'm_attn_norm_g', 'm_w_in', 'm_shift_mu', 'm_w0', 'm_w2', 'm_a0', 'm_a2', 'm_g2', 'm_k_k', 'm_k_a', 'm_r_k', 'm_lnx_g', 'm_lnx_b', 'm_q_norm_g', 'm_k_norm_g', 'm_fgate_b', 'm_w_out', 'm_ffn_norm_g', 'm_w_gate', 'm_w_up', 'm_w_down', 'm_ple_proj', 'm_ple_norm_g', 'm_ple_gate_norm_g', 'm_ple_gate_w', 'm_ple_gate_b', 'v_attn_norm_g', 'v_w_in', 'v_shift_mu', 'v_w0', 'v_w2', 'v_a0', 'v_a2', 'v_g2', 'v_k_k', 'v_k_a', 'v_r_k', 'v_lnx_g', 'v_lnx_b', 'v_q_norm_g', 'v_k_norm_g', 'v_fgate_b', 'v_w_out', 'v_ffn_norm_g', 'v_w_gate', 'v_w_up', 'v_w_down', 'v_ple_proj', 'v_ple_norm_g', 'v_ple_gate_norm_g', 'v_ple_gate_w', 'v_ple_gate_b']
TWIN_OUTPUTS = ['loss', 'grad_x', 'grad_attn_norm_g', 'grad_w_in', 'grad_shift_mu', 'grad_w0', 'grad_w2', 'grad_a0', 'grad_a2', 'grad_g2', 'grad_k_k', 'grad_k_a', 'grad_r_k', 'grad_lnx_g', 'grad_lnx_b', 'grad_q_norm_g', 'grad_k_norm_g', 'grad_fgate_b', 'grad_w_out', 'grad_ffn_norm_g', 'grad_w_gate', 'grad_w_up', 'grad_w_down', 'grad_ple_proj', 'grad_ple_norm_g', 'grad_ple_gate_norm_g', 'grad_ple_gate_w', 'grad_ple_gate_b', 'delta_attn_norm_g', 'delta_w_in', 'delta_shift_mu', 'delta_w0', 'delta_w2', 'delta_a0', 'delta_a2', 'delta_g2', 'delta_k_k', 'delta_k_a', 'delta_r_k', 'delta_lnx_g', 'delta_lnx_b', 'delta_q_norm_g', 'delta_k_norm_g', 'delta_fgate_b', 'delta_w_out', 'delta_ffn_norm_g', 'delta_w_gate', 'delta_w_up', 'delta_w_down', 'delta_ple_proj', 'delta_ple_norm_g', 'delta_ple_gate_norm_g', 'delta_ple_gate_w', 'delta_ple_gate_b', 'new_m_attn_norm_g', 'new_m_w_in', 'new_m_shift_mu', 'new_m_w0', 'new_m_w2', 'new_m_a0', 'new_m_a2', 'new_m_g2', 'new_m_k_k', 'new_m_k_a', 'new_m_r_k', 'new_m_lnx_g', 'new_m_lnx_b', 'new_m_q_norm_g', 'new_m_k_norm_g', 'new_m_fgate_b', 'new_m_w_out', 'new_m_ffn_norm_g', 'new_m_w_gate', 'new_m_w_up', 'new_m_w_down', 'new_m_ple_proj', 'new_m_ple_norm_g', 'new_m_ple_gate_norm_g', 'new_m_ple_gate_w', 'new_m_ple_gate_b', 'new_v_attn_norm_g', 'new_v_w_in', 'new_v_shift_mu', 'new_v_w0', 'new_v_w2', 'new_v_a0', 'new_v_a2', 'new_v_g2', 'new_v_k_k', 'new_v_k_a', 'new_v_r_k', 'new_v_lnx_g', 'new_v_lnx_b', 'new_v_q_norm_g', 'new_v_k_norm_g', 'new_v_fgate_b', 'new_v_w_out', 'new_v_ffn_norm_g', 'new_v_w_gate', 'new_v_w_up', 'new_v_w_down', 'new_v_ple_proj', 'new_v_ple_norm_g', 'new_v_ple_gate_norm_g', 'new_v_ple_gate_w', 'new_v_ple_gate_b']
TWIN_LEAF_KINDS = {'loss': 'loss', 'grad_x': 'grad_x', 'grad_attn_norm_g': 'grad_w', 'grad_w_in': 'grad_w', 'grad_shift_mu': 'grad_w', 'grad_w0': 'grad_w', 'grad_w2': 'grad_w', 'grad_a0': 'grad_w', 'grad_a2': 'grad_w', 'grad_g2': 'grad_w', 'grad_k_k': 'grad_w', 'grad_k_a': 'grad_w', 'grad_r_k': 'grad_w', 'grad_lnx_g': 'grad_w', 'grad_lnx_b': 'grad_w', 'grad_q_norm_g': 'grad_w', 'grad_k_norm_g': 'grad_w', 'grad_fgate_b': 'grad_w', 'grad_w_out': 'grad_w', 'grad_ffn_norm_g': 'grad_w', 'grad_w_gate': 'grad_w', 'grad_w_up': 'grad_w', 'grad_w_down': 'grad_w', 'grad_ple_proj': 'grad_w', 'grad_ple_norm_g': 'grad_w', 'grad_ple_gate_norm_g': 'grad_w', 'grad_ple_gate_w': 'grad_w', 'grad_ple_gate_b': 'grad_w', 'delta_attn_norm_g': 'delta_w', 'delta_w_in': 'delta_w', 'delta_shift_mu': 'delta_w', 'delta_w0': 'delta_w', 'delta_w2': 'delta_w', 'delta_a0': 'delta_w', 'delta_a2': 'delta_w', 'delta_g2': 'delta_w', 'delta_k_k': 'delta_w', 'delta_k_a': 'delta_w', 'delta_r_k': 'delta_w', 'delta_lnx_g': 'delta_w', 'delta_lnx_b': 'delta_w', 'delta_q_norm_g': 'delta_w', 'delta_k_norm_g': 'delta_w', 'delta_fgate_b': 'delta_w', 'delta_w_out': 'delta_w', 'delta_ffn_norm_g': 'delta_w', 'delta_w_gate': 'delta_w', 'delta_w_up': 'delta_w', 'delta_w_down': 'delta_w', 'delta_ple_proj': 'delta_w', 'delta_ple_norm_g': 'delta_w', 'delta_ple_gate_norm_g': 'delta_w', 'delta_ple_gate_w': 'delta_w', 'delta_ple_gate_b': 'delta_w', 'new_m_attn_norm_g': 'new_m', 'new_m_w_in': 'new_m', 'new_m_shift_mu': 'new_m', 'new_m_w0': 'new_m', 'new_m_w2': 'new_m', 'new_m_a0': 'new_m', 'new_m_a2': 'new_m', 'new_m_g2': 'new_m', 'new_m_k_k': 'new_m', 'new_m_k_a': 'new_m', 'new_m_r_k': 'new_m', 'new_m_lnx_g': 'new_m', 'new_m_lnx_b': 'new_m', 'new_m_q_norm_g': 'new_m', 'new_m_k_norm_g': 'new_m', 'new_m_fgate_b': 'new_m', 'new_m_w_out': 'new_m', 'new_m_ffn_norm_g': 'new_m', 'new_m_w_gate': 'new_m', 'new_m_w_up': 'new_m', 'new_m_w_down': 'new_m', 'new_m_ple_proj': 'new_m', 'new_m_ple_norm_g': 'new_m', 'new_m_ple_gate_norm_g': 'new_m', 'new_m_ple_gate_w': 'new_m', 'new_m_ple_gate_b': 'new_m', 'new_v_attn_norm_g': 'new_v', 'new_v_w_in': 'new_v', 'new_v_shift_mu': 'new_v', 'new_v_w0': 'new_v', 'new_v_w2': 'new_v', 'new_v_a0': 'new_v', 'new_v_a2': 'new_v', 'new_v_g2': 'new_v', 'new_v_k_k': 'new_v', 'new_v_k_a': 'new_v', 'new_v_r_k': 'new_v', 'new_v_lnx_g': 'new_v', 'new_v_lnx_b': 'new_v', 'new_v_q_norm_g': 'new_v', 'new_v_k_norm_g': 'new_v', 'new_v_fgate_b': 'new_v', 'new_v_w_out': 'new_v', 'new_v_ffn_norm_g': 'new_v', 'new_v_w_gate': 'new_v', 'new_v_w_up': 'new_v', 'new_v_w_down': 'new_v', 'new_v_ple_proj': 'new_v', 'new_v_ple_norm_g': 'new_v', 'new_v_ple_gate_norm_g': 'new_v', 'new_v_ple_gate_w': 'new_v', 'new_v_ple_gate_b': 'new_v'}


def _forward(args):
    return _fwd_reference(*[args[k] for k in FWD_PARAMS])


def _output_shape():
    def fwd():
        inp = _fwd_setup_inputs(0)
        return _fwd_reference(*[inp[k] for k in FWD_PARAMS])
    out = _jax.eval_shape(fwd)
    return out.shape, out.dtype

N_MICROBATCH = 1
ADAM_LR = 0.001
ADAM_B1 = 0.9
ADAM_B2 = 0.999
ADAM_EPS = 1e-08
ADAM_WD = 0.01
ADAM_STEP = 10
PER_EXAMPLE_BATCH_AXIS = {'x': 0, 'p': 1, 'loss_target': 0}
SHARED_INPUTS = []
_WEIGHT_DTYPES = {'attn_norm_g': _jnp.float32, 'w_in': _jnp.float32, 'shift_mu': _jnp.float32, 'w0': _jnp.float32, 'w2': _jnp.float32, 'a0': _jnp.float32, 'a2': _jnp.float32, 'g2': _jnp.float32, 'k_k': _jnp.float32, 'k_a': _jnp.float32, 'r_k': _jnp.float32, 'lnx_g': _jnp.float32, 'lnx_b': _jnp.float32, 'q_norm_g': _jnp.float32, 'k_norm_g': _jnp.float32, 'fgate_b': _jnp.float32, 'w_out': _jnp.float32, 'ffn_norm_g': _jnp.float32, 'w_gate': _jnp.float32, 'w_up': _jnp.float32, 'w_down': _jnp.float32, 'ple_proj': _jnp.float32, 'ple_norm_g': _jnp.float32, 'ple_gate_norm_g': _jnp.float32, 'ple_gate_w': _jnp.float32, 'ple_gate_b': _jnp.float32}
MOMENT_SCALE = {'attn_norm_g': 1.243959e+00, 'w_in': 1.105797e-01, 'shift_mu': 2.061780e+00, 'w0': 5.588043e-02, 'w2': 5.991671e-03, 'a0': 2.625994e-01, 'a2': 5.113503e-02, 'g2': 4.213150e+00, 'k_k': 2.124572e-01, 'k_a': 5.300391e-01, 'r_k': 3.146799e+00, 'lnx_g': 7.112018e+00, 'lnx_b': 1.410080e+00, 'q_norm_g': 8.223908e+00, 'k_norm_g': 8.210660e+00, 'fgate_b': 3.140579e+01, 'w_out': 1.296750e-01, 'ffn_norm_g': 1.239778e+01, 'w_gate': 7.675887e-02, 'w_up': 7.965045e-02, 'w_down': 1.211877e-01, 'ple_proj': 7.904973e-02, 'ple_norm_g': 4.721112e+00, 'ple_gate_norm_g': 4.956398e-01, 'ple_gate_w': 3.768600e-02, 'ple_gate_b': 1.666578e+00}


def _to_microbatches(a, axis):
    t = _jnp.moveaxis(a, axis, 0)
    t = t.reshape((N_MICROBATCH, t.shape[0] // N_MICROBATCH) + t.shape[1:])
    return _jnp.moveaxis(t, 1, axis + 1)


def setup_inputs(seed: int = 0) -> dict:
    inp = _fwd_setup_inputs(seed)
    key = _jax.random.fold_in(_jax.random.key(seed), 7919)
    shape, _ = _output_shape()
    out = dict(inp)
    out["loss_target"] = _jax.random.normal(_jax.random.fold_in(key, 0), shape, _jnp.float32)
    for i, name in enumerate(TWIN_WEIGHTS):
        w = inp[name].astype(_jnp.float32)
        if MOMENT_SCALE is None:
            s = _jnp.sqrt(_jnp.mean(_jnp.square(w)) + 1e-30)
        else:
            s = MOMENT_SCALE[name]
        km, kv = _jax.random.split(_jax.random.fold_in(key, i + 1))
        out[name] = w
        out["m_" + name] = s * _jax.random.normal(km, w.shape, _jnp.float32)
        out["v_" + name] = (s * s) * _jax.random.uniform(kv, w.shape, _jnp.float32, 0.5, 1.5)
    if N_MICROBATCH > 1:
        for name, axis in PER_EXAMPLE_BATCH_AXIS.items():
            out[name] = _to_microbatches(out[name], axis)
    return {'x': out['x'], 'p': out['p'], 'attn_norm_g': out['attn_norm_g'], 'w_in': out['w_in'], 'shift_mu': out['shift_mu'], 'w0': out['w0'], 'w2': out['w2'], 'a0': out['a0'], 'a2': out['a2'], 'g2': out['g2'], 'k_k': out['k_k'], 'k_a': out['k_a'], 'r_k': out['r_k'], 'lnx_g': out['lnx_g'], 'lnx_b': out['lnx_b'], 'q_norm_g': out['q_norm_g'], 'k_norm_g': out['k_norm_g'], 'fgate_b': out['fgate_b'], 'w_out': out['w_out'], 'ffn_norm_g': out['ffn_norm_g'], 'w_gate': out['w_gate'], 'w_up': out['w_up'], 'w_down': out['w_down'], 'ple_proj': out['ple_proj'], 'ple_norm_g': out['ple_norm_g'], 'ple_gate_norm_g': out['ple_gate_norm_g'], 'ple_gate_w': out['ple_gate_w'], 'ple_gate_b': out['ple_gate_b'], 'loss_target': out['loss_target'], 'm_attn_norm_g': out['m_attn_norm_g'], 'm_w_in': out['m_w_in'], 'm_shift_mu': out['m_shift_mu'], 'm_w0': out['m_w0'], 'm_w2': out['m_w2'], 'm_a0': out['m_a0'], 'm_a2': out['m_a2'], 'm_g2': out['m_g2'], 'm_k_k': out['m_k_k'], 'm_k_a': out['m_k_a'], 'm_r_k': out['m_r_k'], 'm_lnx_g': out['m_lnx_g'], 'm_lnx_b': out['m_lnx_b'], 'm_q_norm_g': out['m_q_norm_g'], 'm_k_norm_g': out['m_k_norm_g'], 'm_fgate_b': out['m_fgate_b'], 'm_w_out': out['m_w_out'], 'm_ffn_norm_g': out['m_ffn_norm_g'], 'm_w_gate': out['m_w_gate'], 'm_w_up': out['m_w_up'], 'm_w_down': out['m_w_down'], 'm_ple_proj': out['m_ple_proj'], 'm_ple_norm_g': out['m_ple_norm_g'], 'm_ple_gate_norm_g': out['m_ple_gate_norm_g'], 'm_ple_gate_w': out['m_ple_gate_w'], 'm_ple_gate_b': out['m_ple_gate_b'], 'v_attn_norm_g': out['v_attn_norm_g'], 'v_w_in': out['v_w_in'], 'v_shift_mu': out['v_shift_mu'], 'v_w0': out['v_w0'], 'v_w2': out['v_w2'], 'v_a0': out['v_a0'], 'v_a2': out['v_a2'], 'v_g2': out['v_g2'], 'v_k_k': out['v_k_k'], 'v_k_a': out['v_k_a'], 'v_r_k': out['v_r_k'], 'v_lnx_g': out['v_lnx_g'], 'v_lnx_b': out['v_lnx_b'], 'v_q_norm_g': out['v_q_norm_g'], 'v_k_norm_g': out['v_k_norm_g'], 'v_fgate_b': out['v_fgate_b'], 'v_w_out': out['v_w_out'], 'v_ffn_norm_g': out['v_ffn_norm_g'], 'v_w_gate': out['v_w_gate'], 'v_w_up': out['v_w_up'], 'v_w_down': out['v_w_down'], 'v_ple_proj': out['v_ple_proj'], 'v_ple_norm_g': out['v_ple_norm_g'], 'v_ple_gate_norm_g': out['v_ple_gate_norm_g'], 'v_ple_gate_w': out['v_ple_gate_w'], 'v_ple_gate_b': out['v_ple_gate_b']}


def _loss(weights, diff, rest, loss_target):
    with _jax.named_scope("forward"):
        args = {**rest, TWIN_DIFF_INPUT: diff, **{k: w.astype(_WEIGHT_DTYPES[k]) for k, w in weights.items()}}
        y = _forward(args)
    with _jax.named_scope("loss_head"):
        err = _jnp.square(y.astype(_jnp.float32) - loss_target)
        return 0.5 * _jnp.sum(_jnp.mean(err, axis=-1)) if err.ndim else 0.5 * err


def _adamw(w, g, m, v):
    m = ADAM_B1 * m + (1.0 - ADAM_B1) * g
    v = ADAM_B2 * v + (1.0 - ADAM_B2) * _jnp.square(g)
    m_hat = m / (1.0 - ADAM_B1 ** ADAM_STEP)
    v_hat = v / (1.0 - ADAM_B2 ** ADAM_STEP)
    delta = -ADAM_LR * (m_hat / (_jnp.sqrt(v_hat) + ADAM_EPS) + ADAM_WD * w)
    return delta, m, v


def reference(x, p, attn_norm_g, w_in, shift_mu, w0, w2, a0, a2, g2, k_k, k_a, r_k, lnx_g, lnx_b, q_norm_g, k_norm_g, fgate_b, w_out, ffn_norm_g, w_gate, w_up, w_down, ple_proj, ple_norm_g, ple_gate_norm_g, ple_gate_w, ple_gate_b, loss_target, m_attn_norm_g, m_w_in, m_shift_mu, m_w0, m_w2, m_a0, m_a2, m_g2, m_k_k, m_k_a, m_r_k, m_lnx_g, m_lnx_b, m_q_norm_g, m_k_norm_g, m_fgate_b, m_w_out, m_ffn_norm_g, m_w_gate, m_w_up, m_w_down, m_ple_proj, m_ple_norm_g, m_ple_gate_norm_g, m_ple_gate_w, m_ple_gate_b, v_attn_norm_g, v_w_in, v_shift_mu, v_w0, v_w2, v_a0, v_a2, v_g2, v_k_k, v_k_a, v_r_k, v_lnx_g, v_lnx_b, v_q_norm_g, v_k_norm_g, v_fgate_b, v_w_out, v_ffn_norm_g, v_w_gate, v_w_up, v_w_down, v_ple_proj, v_ple_norm_g, v_ple_gate_norm_g, v_ple_gate_w, v_ple_gate_b):
    given = dict(x=x, p=p, attn_norm_g=attn_norm_g, w_in=w_in, shift_mu=shift_mu, w0=w0, w2=w2, a0=a0, a2=a2, g2=g2, k_k=k_k, k_a=k_a, r_k=r_k, lnx_g=lnx_g, lnx_b=lnx_b, q_norm_g=q_norm_g, k_norm_g=k_norm_g, fgate_b=fgate_b, w_out=w_out, ffn_norm_g=ffn_norm_g, w_gate=w_gate, w_up=w_up, w_down=w_down, ple_proj=ple_proj, ple_norm_g=ple_norm_g, ple_gate_norm_g=ple_gate_norm_g, ple_gate_w=ple_gate_w, ple_gate_b=ple_gate_b, loss_target=loss_target, m_attn_norm_g=m_attn_norm_g, m_w_in=m_w_in, m_shift_mu=m_shift_mu, m_w0=m_w0, m_w2=m_w2, m_a0=m_a0, m_a2=m_a2, m_g2=m_g2, m_k_k=m_k_k, m_k_a=m_k_a, m_r_k=m_r_k, m_lnx_g=m_lnx_g, m_lnx_b=m_lnx_b, m_q_norm_g=m_q_norm_g, m_k_norm_g=m_k_norm_g, m_fgate_b=m_fgate_b, m_w_out=m_w_out, m_ffn_norm_g=m_ffn_norm_g, m_w_gate=m_w_gate, m_w_up=m_w_up, m_w_down=m_w_down, m_ple_proj=m_ple_proj, m_ple_norm_g=m_ple_norm_g, m_ple_gate_norm_g=m_ple_gate_norm_g, m_ple_gate_w=m_ple_gate_w, m_ple_gate_b=m_ple_gate_b, v_attn_norm_g=v_attn_norm_g, v_w_in=v_w_in, v_shift_mu=v_shift_mu, v_w0=v_w0, v_w2=v_w2, v_a0=v_a0, v_a2=v_a2, v_g2=v_g2, v_k_k=v_k_k, v_k_a=v_k_a, v_r_k=v_r_k, v_lnx_g=v_lnx_g, v_lnx_b=v_lnx_b, v_q_norm_g=v_q_norm_g, v_k_norm_g=v_k_norm_g, v_fgate_b=v_fgate_b, v_w_out=v_w_out, v_ffn_norm_g=v_ffn_norm_g, v_w_gate=v_w_gate, v_w_up=v_w_up, v_w_down=v_w_down, v_ple_proj=v_ple_proj, v_ple_norm_g=v_ple_norm_g, v_ple_gate_norm_g=v_ple_gate_norm_g, v_ple_gate_w=v_ple_gate_w, v_ple_gate_b=v_ple_gate_b)
    weights = {n: given[n] for n in TWIN_WEIGHTS}
    shared = {n: given[n] for n in SHARED_INPUTS}
    per_example = {n: given[n] for n in ['x', 'p']}
    grad_fn = _jax.value_and_grad(_loss, argnums=(0, 1))

    def one_microbatch(ex, loss_target):
        ex = dict(ex)
        diff = ex.pop(TWIN_DIFF_INPUT)
        return grad_fn(weights, diff, {**shared, **ex}, loss_target)

    if N_MICROBATCH == 1:
        loss, (grad_w, grad_x) = one_microbatch(per_example, given["loss_target"])
    else:
        def body(carry, xs):
            loss_sum, grad_sum = carry
            l_k, (gw_k, gx_k) = one_microbatch(xs[0], xs[1])
            with _jax.named_scope("update"):
                return (loss_sum + l_k, _jax.tree.map(_jnp.add, grad_sum, gw_k)), gx_k

        init = (_jnp.zeros((), _jnp.float32), _jax.tree.map(_jnp.zeros_like, weights))
        (loss, grad_w), grad_x = _jax.lax.scan(body, init, (per_example, given["loss_target"]))
    with _jax.named_scope("update"):
        delta_w, new_m, new_v = {}, {}, {}
        for n in TWIN_WEIGHTS:
            delta_w[n], new_m[n], new_v[n] = _adamw(weights[n], grad_w[n], given["m_" + n], given["v_" + n])
    return (loss, grad_x, *[grad_w[n] for n in TWIN_WEIGHTS], *[delta_w[n] for n in TWIN_WEIGHTS],
            *[new_m[n] for n in TWIN_WEIGHTS], *[new_v[n] for n in TWIN_WEIGHTS])
```

```python
import functools
import math

import jax
import jax.numpy as jnp
from jax import lax
from jax.experimental import pallas as pl
from jax.experimental.pallas import tpu as pltpu

F32 = jnp.float32
BF16 = jnp.bfloat16
HIGHEST = lax.Precision.HIGHEST

N_DEV = 8
MESH_AXES = ("x", "y", "c")
HEAD_DIM = 64
SCAN_CHUNK = 64
ATTN_BLOCK_Q = 128
LANES = 128
PACK_COLS = 1024
PACK_ROW_QUANTUM = 64
VMEM_LIMIT = 48 * 1024 * 1024
RMS_EPS = 1e-6
GN_EPS = 64e-5
ADAM_LR, ADAM_B1, ADAM_B2, ADAM_EPS, ADAM_WD, ADAM_STEP = 0.001, 0.9, 0.999, 1e-08, 0.01, 10

WEIGHT_NAMES = ['attn_norm_g', 'w_in', 'shift_mu', 'w0', 'w2', 'a0', 'a2', 'g2', 'k_k', 'k_a', 'r_k', 'lnx_g', 'lnx_b',
                'q_norm_g', 'k_norm_g', 'fgate_b', 'w_out', 'ffn_norm_g', 'w_gate', 'w_up', 'w_down', 'ple_proj',
                'ple_norm_g', 'ple_gate_norm_g', 'ple_gate_w', 'ple_gate_b']
SHARDED = {'w_in': True, 'w2': True, 'a2': True, 'g2': True, 'w_out': False, 'w_gate': True, 'w_up': True,
           'w_down': False, 'ple_proj': True, 'ple_gate_w': False}
SHARDED_NAMES = [n for n in WEIGHT_NAMES if n in SHARDED]
SMALL_NAMES = [n for n in WEIGHT_NAMES if n not in SHARDED]


def _round_up(n, q):
    return -(-n // q) * q


def _pick_tile(n, cap):
    for t in (1024, 512, 256, 128):
        if t <= cap and n % t == 0:
            return t
    return n


def _mm_call(a, b, mode, name):
    if mode == "nn":
        (I, C), (_, J) = a.shape, b.shape
    elif mode == "nt":
        (I, C), (J, _) = a.shape, b.shape
    else:
        (C, I), (_, J) = a.shape, b.shape
    ti, tj, tc = _pick_tile(I, 1024), _pick_tile(J, 1024), _pick_tile(C, 512)
    n_c = C // tc
    if mode == "nn":
        a_spec = pl.BlockSpec((ti, tc), lambda i, j, c: (i, c))
        b_spec = pl.BlockSpec((tc, tj), lambda i, j, c: (c, j))
        dims = (((1,), (0,)), ((), ()))
    elif mode == "nt":
        a_spec = pl.BlockSpec((ti, tc), lambda i, j, c: (i, c))
        b_spec = pl.BlockSpec((tj, tc), lambda i, j, c: (j, c))
        dims = (((1,), (1,)), ((), ()))
    else:
        a_spec = pl.BlockSpec((tc, ti), lambda i, j, c: (c, i))
        b_spec = pl.BlockSpec((tc, tj), lambda i, j, c: (c, j))
        dims = (((0,), (0,)), ((), ()))

    def body(a_ref, b_ref, o_ref, acc):
        c = pl.program_id(2)

        @pl.when(c == 0)
        def _():
            acc[...] = jnp.zeros_like(acc)

        acc[...] += lax.dot_general(a_ref[...].astype(BF16), b_ref[...].astype(BF16), dims,
                                    preferred_element_type=F32)

        @pl.when(c == n_c - 1)
        def _():
            o_ref[...] = acc[...]

    return pl.pallas_call(
        body, name=name, grid=(I // ti, J // tj, n_c),
        in_specs=[a_spec, b_spec],
        out_specs=pl.BlockSpec((ti, tj), lambda i, j, c: (i, j)),
        out_shape=jax.ShapeDtypeStruct((I, J), F32),
        scratch_shapes=[pltpu.VMEM((ti, tj), F32)],
        compiler_params=pltpu.CompilerParams(dimension_semantics=("parallel", "parallel", "arbitrary"),
                                             vmem_limit_bytes=VMEM_LIMIT),
    )(a, b)


def _mm(a, b, name):
    @jax.custom_vjp
    def run(a, b):
        return _mm_call(a, b, "nn", "mm_" + name)

    def fwd(a, b):
        return run(a, b), (a, b)

    def bwd(res, g):
        a, b = res
        return _mm_call(g, b, "nt", "mm_" + name + "_da"), _mm_call(a, g, "tn", "mm_" + name + "_db")

    run.defvjp(fwd, bwd)
    return run(a, b)


def _stage_fwd_call(name, fn, rows, params, tm):
    G, T, _ = rows[0].shape
    nr, npar = len(rows), len(params)
    out_avals = jax.eval_shape(
        lambda *a: tuple(fn(*a)),
        *[jax.ShapeDtypeStruct((tm, r.shape[2]), r.dtype) for r in rows],
        *[jax.ShapeDtypeStruct((1, p.shape[2]), p.dtype) for p in params])

    def body(*refs):
        vals = [r[0] for r in refs[:nr + npar]]
        for o_ref, o in zip(refs[nr + npar:], fn(*vals)):
            o_ref[0] = o

    def row_spec(c):
        return pl.BlockSpec((1, tm, c), lambda g, t: (g, t, 0))

    def par_spec(c):
        return pl.BlockSpec((1, 1, c), lambda g, t: (g, 0, 0))

    return pl.pallas_call(
        body, name=name, grid=(G, T // tm),
        in_specs=[row_spec(r.shape[2]) for r in rows] + [par_spec(p.shape[2]) for p in params],
        out_specs=[row_spec(o.shape[1]) for o in out_avals],
        out_shape=[jax.ShapeDtypeStruct((G, T, o.shape[1]), o.dtype) for o in out_avals],
        compiler_params=pltpu.CompilerParams(dimension_semantics=("parallel", "parallel"),
                                             vmem_limit_bytes=VMEM_LIMIT),
    )(*rows, *params)


def _stage_bwd_call(name, fn, rows, params, cts, tm):
    G, T, _ = rows[0].shape
    nr, npar, nout = len(rows), len(params), len(cts)

    def body(*refs):
        vals = [r[0] for r in refs[:nr + npar]]
        ct_vals = tuple(r[0] for r in refs[nr + npar:nr + npar + nout])
        d_refs = refs[nr + npar + nout:]
        _, vjp_fn = jax.vjp(lambda *a: tuple(fn(*a)), *vals)
        grads = vjp_fn(ct_vals)
        for i in range(nr):
            d_refs[i][0] = grads[i]

        if npar:
            @pl.when(pl.program_id(1) == 0)
            def _():
                for j in range(npar):
                    d_refs[nr + j][...] = jnp.zeros_like(d_refs[nr + j])

        for j in range(npar):
            d_refs[nr + j][0] += grads[nr + j]

    def row_spec(c):
        return pl.BlockSpec((1, tm, c), lambda g, t: (g, t, 0))

    def par_spec(c):
        return pl.BlockSpec((1, 1, c), lambda g, t: (g, 0, 0))

    outs = pl.pallas_call(
        body, name=name + "_bwd", grid=(G, T // tm),
        in_specs=([row_spec(r.shape[2]) for r in rows] + [par_spec(p.shape[2]) for p in params]
                  + [row_spec(c.shape[2]) for c in cts]),
        out_specs=[row_spec(r.shape[2]) for r in rows] + [par_spec(p.shape[2]) for p in params],
        out_shape=([jax.ShapeDtypeStruct(r.shape, r.dtype) for r in rows]
                   + [jax.ShapeDtypeStruct(p.shape, p.dtype) for p in params]),
        compiler_params=pltpu.CompilerParams(dimension_semantics=("parallel", "arbitrary"),
                                             vmem_limit_bytes=VMEM_LIMIT),
    )(*rows, *params, *cts)
    return tuple(outs[:nr]), tuple(outs[nr:])


def _stage(name, fn, rows, params, tm):
    @jax.custom_vjp
    def run(rows, params):
        return tuple(_stage_fwd_call(name, fn, rows, params, tm))

    def fwd(rows, params):
        return run(rows, params), (rows, params)

    def bwd(res, cts):
        rows, params = res
        return _stage_bwd_call(name, fn, rows, params, tuple(cts), tm)

    run.defvjp(fwd, bwd)
    return run(tuple(rows), tuple(params))


def _sigmoid(x):
    return 0.5 * (jnp.tanh(0.5 * x) + 1.0)


def _softplus(x):
    return jnp.maximum(x, 0.0) + jnp.log(1.0 + jnp.exp(-jnp.abs(x)))


def _rms(x, g, eps=RMS_EPS):
    return x * lax.rsqrt(jnp.mean(x * x, axis=-1, keepdims=True) + eps) * g


def _fn_rmsnorm(x, g):
    return (_rms(x, g),)


def _fn_swiglu(gate, up):
    return (gate * _sigmoid(gate) * up,)


def _make_fn_lora_mix(p1, p2):
    def fn(u, u_prev, mu):
        um = u + (u_prev - u) * mu
        return jnp.tanh(um[:, :p1]), um[:, p1:p1 + p2], _sigmoid(um[:, p1 + p2:])
    return fn


def _fn_rwkv_prep(ru, ru_p, ku, ku_p, vu, vu_p, w_lin, a_lin, mu_r, mu_k, mu_v, w0, a0, k_k, k_a):
    r = ru + (ru_p - ru) * mu_r
    k = ku + (ku_p - ku) * mu_k
    v = vu + (vu_p - vu) * mu_v
    w_log = -_softplus(-(w0 + w_lin)) - 0.5
    lw = -jnp.exp(w_log)
    a = _sigmoid(a0 + a_lin)
    kk = k * k_k
    kk = kk / jnp.maximum(jnp.sqrt(jnp.sum(kk * kk, axis=-1, keepdims=True)), 1e-12)
    k_mod = k * (1.0 + (a - 1.0) * k_a)
    return r, lw, k_mod, v, kk, kk * a


def _fn_rwkv_post(y, r, k_mod, v, g, lnx_g, lnx_b, r_k):
    mean = jnp.mean(y, axis=-1, keepdims=True)
    yc = y - mean
    var = jnp.mean(yc * yc, axis=-1, keepdims=True)
    yn = yc * lax.rsqrt(var + GN_EPS) * lnx_g + lnx_b
    bonus = jnp.sum(r * k_mod * r_k, axis=-1, keepdims=True) * v
    return ((yn + bonus) * g,)


def _fn_fox_prep(q, k, qg, kg):
    return _rms(q, qg), _rms(k, kg)


def _fn_log_forget(f_raw, b):
    x = f_raw + b
    return (jnp.minimum(x, 0.0) - jnp.log(1.0 + jnp.exp(-jnp.abs(x))),)


def _fn_final(z, e_raw, h2, target, gate_b, ple_g):
    gate = _sigmoid(z + gate_b)
    out = h2 + gate * _rms(e_raw, ple_g)
    err = out - target
    return (0.5 * jnp.mean(err * err, axis=-1, keepdims=True),)


def _dot(a, b, ca, cb):
    return lax.dot_general(a, b, (((ca,), (cb,)), ((), ())), precision=HIGHEST, preferred_element_type=F32)


def _scan_chunk(S0, r, lw, k, v, kk, b):
    L = r.shape[0]
    row = lax.broadcasted_iota(jnp.int32, (L, L), 0)
    col = lax.broadcasted_iota(jnp.int32, (L, L), 1)
    incl = col <= row
    strict = col < row
    cum = _dot(incl.astype(F32), lw, 1, 0)
    g_in, g_ex, g_inv = jnp.exp(cum), jnp.exp(cum - lw), jnp.exp(-cum)
    kkg, kd, bd, rg = kk * g_ex, k * g_inv, b * g_inv, r * g_in
    a_k = jnp.where(strict, _dot(kkg, kd, 1, 1), 0.0)
    a_b = jnp.where(strict, _dot(kkg, bd, 1, 1), 0.0)
    pw = -a_b
    inv = (row == col).astype(F32) + pw
    for _ in range(int(math.log2(L)) - 1):
        pw = _dot(pw, pw, 1, 0)
        inv = inv + _dot(inv, pw, 1, 0)
    sa = -_dot(inv, _dot(kkg, S0, 1, 1) + _dot(a_k, v, 1, 0), 1, 0)
    r_k = jnp.where(incl, _dot(rg, kd, 1, 1), 0.0)
    r_b = jnp.where(incl, _dot(rg, bd, 1, 1), 0.0)
    y = _dot(rg, S0, 1, 1) + _dot(r_k, v, 1, 0) + _dot(r_b, sa, 1, 0)
    g_end = jnp.exp(jnp.sum(lw, axis=0, keepdims=True))
    S1 = S0 * g_end + _dot(v, kd * g_end, 0, 0) + _dot(sa, bd * g_end, 0, 0)
    return y, S1


def _scan_fwd_call(r, lw, k, v, kk, b):
    H, T, N = r.shape
    L = SCAN_CHUNK
    n_chunks = T // L

    def body(r_ref, lw_ref, k_ref, v_ref, kk_ref, b_ref, y_ref, s0_ref, state):
        @pl.when(pl.program_id(1) == 0)
        def _():
            state[...] = jnp.zeros_like(state)

        S0 = state[...]
        s0_ref[0, 0] = S0
        y, S1 = _scan_chunk(S0, r_ref[0], lw_ref[0], k_ref[0], v_ref[0], kk_ref[0], b_ref[0])
        y_ref[0] = y
        state[...] = S1

    blk = pl.BlockSpec((1, L, N), lambda h, c: (h, c, 0))
    return pl.pallas_call(
        body, name="rwkv_scan_fwd", grid=(H, n_chunks),
        in_specs=[blk] * 6,
        out_specs=[blk, pl.BlockSpec((1, 1, N, N), lambda h, c: (h, c, 0, 0))],
        out_shape=[jax.ShapeDtypeStruct((H, T, N), F32), jax.ShapeDtypeStruct((H, n_chunks, N, N), F32)],
        scratch_shapes=[pltpu.VMEM((N, N), F32)],
        compiler_params=pltpu.CompilerParams(dimension_semantics=("parallel", "arbitrary")),
    )(r, lw, k, v, kk, b)


def _scan_bwd_call(r, lw, k, v, kk, b, s0s, dy):
    H, T, N = r.shape
    L = SCAN_CHUNK
    n_chunks = T // L

    def body(r_ref, lw_ref, k_ref, v_ref, kk_ref, b_ref, s0_ref, dy_ref, dr, dlw, dk, dv, dkk, db, d_state):
        @pl.when(pl.program_id(1) == 0)
        def _():
            d_state[...] = jnp.zeros_like(d_state)

        _, vjp_fn = jax.vjp(_scan_chunk, s0_ref[0, 0], r_ref[0], lw_ref[0], k_ref[0], v_ref[0], kk_ref[0], b_ref[0])
        grads = vjp_fn((dy_ref[0], d_state[...]))
        d_state[...] = grads[0]
        for o_ref, g in zip((dr, dlw, dk, dv, dkk, db), grads[1:]):
            o_ref[0] = g

    blk = pl.BlockSpec((1, L, N), lambda h, c: (h, n_chunks - 1 - c, 0))
    return pl.pallas_call(
        body, name="rwkv_scan_bwd", grid=(H, n_chunks),
        in_specs=[blk] * 6 + [pl.BlockSpec((1, 1, N, N), lambda h, c: (h, n_chunks - 1 - c, 0, 0)), blk],
        out_specs=[blk] * 6,
        out_shape=[jax.ShapeDtypeStruct((H, T, N), F32)] * 6,
        scratch_shapes=[pltpu.VMEM((N, N), F32)],
        compiler_params=pltpu.CompilerParams(dimension_semantics=("parallel", "arbitrary")),
    )(r, lw, k, v, kk, b, s0s, dy)


@jax.custom_vjp
def _rwkv_scan(r, lw, k, v, kk, b):
    return _scan_fwd_call(r, lw, k, v, kk, b)[0]


def _rwkv_scan_fwd(r, lw, k, v, kk, b):
    y, s0s = _scan_fwd_call(r, lw, k, v, kk, b)
    return y, (r, lw, k, v, kk, b, s0s)


def _rwkv_scan_bwd(res, dy):
    return tuple(_scan_bwd_call(*res, dy))


_rwkv_scan.defvjp(_rwkv_scan_fwd, _rwkv_scan_bwd)


def _attn_block(q, k, v, c_col, c_row, q0):
    bq, t = q.shape[0], k.shape[0]
    s = lax.dot_general(q.astype(BF16), k.astype(BF16), (((1,), (1,)), ((), ())), preferred_element_type=F32)
    s = s * (HEAD_DIM ** -0.5) + (c_col - c_row)
    qi = q0 + lax.broadcasted_iota(jnp.int32, (bq, t), 0)
    ki = lax.broadcasted_iota(jnp.int32, (bq, t), 1)
    s = jnp.where(ki <= qi, s, -jnp.inf)
    m = lax.stop_gradient(jnp.max(s, axis=-1, keepdims=True))
    p = jnp.exp(s - m)
    p = p / jnp.sum(p, axis=-1, keepdims=True)
    return lax.dot_general(p.astype(BF16), v.astype(BF16), (((1,), (0,)), ((), ())), preferred_element_type=F32)


def _attn_specs(T, N, bq):
    q_spec = pl.BlockSpec((1, bq, N), lambda h, i: (h, i, 0))
    kv_spec = pl.BlockSpec((1, T, N), lambda h, i: (h, 0, 0))
    col_spec = pl.BlockSpec((1, bq, 1), lambda h, i: (h, i, 0))
    row_spec = pl.BlockSpec((1, 1, T), lambda h, i: (h, 0, 0))
    return q_spec, kv_spec, col_spec, row_spec


def _attn_fwd_call(q, k, v, c_col, c_row):
    H, T, N = q.shape
    bq = ATTN_BLOCK_Q
    q_spec, kv_spec, col_spec, row_spec = _attn_specs(T, N, bq)

    def body(q_ref, k_ref, v_ref, cc_ref, cr_ref, o_ref):
        o_ref[0] = _attn_block(q_ref[0], k_ref[0], v_ref[0], cc_ref[0], cr_ref[0], pl.program_id(1) * bq)

    return pl.pallas_call(
        body, name="fox_attn_fwd", grid=(H, T // bq),
        in_specs=[q_spec, kv_spec, kv_spec, col_spec, row_spec],
        out_specs=q_spec,
        out_shape=jax.ShapeDtypeStruct((H, T, N), F32),
        compiler_params=pltpu.CompilerParams(dimension_semantics=("parallel", "parallel"),
                                             vmem_limit_bytes=VMEM_LIMIT),
    )(q, k, v, c_col, c_row)


def _attn_bwd_call(q, k, v, c_col, c_row, do):
    H, T, N = q.shape
    bq = ATTN_BLOCK_Q
    q_spec, kv_spec, col_spec, row_spec = _attn_specs(T, N, bq)

    def body(q_ref, k_ref, v_ref, cc_ref, cr_ref, do_ref, dq_ref, dk_ref, dv_ref, dcc_ref, dcr_ref):
        i = pl.program_id(1)
        _, vjp_fn = jax.vjp(functools.partial(_attn_block, q0=i * bq), q_ref[0], k_ref[0], v_ref[0], cc_ref[0],
                            cr_ref[0])
        dq, dk, dv, dcc, dcr = vjp_fn(do_ref[0])
        dq_ref[0] = dq
        dcc_ref[0] = dcc

        @pl.when(i == 0)
        def _():
            dk_ref[...] = jnp.zeros_like(dk_ref)
            dv_ref[...] = jnp.zeros_like(dv_ref)
            dcr_ref[...] = jnp.zeros_like(dcr_ref)

        dk_ref[0] += dk
        dv_ref[0] += dv
        dcr_ref[0] += dcr

    return pl.pallas_call(
        body, name="fox_attn_bwd", grid=(H, T // bq),
        in_specs=[q_spec, kv_spec, kv_spec, col_spec, row_spec, q_spec],
        out_specs=[q_spec, kv_spec, kv_spec, col_spec, row_spec],
        out_shape=[jax.ShapeDtypeStruct((H, T, N), F32)] * 3
        + [jax.ShapeDtypeStruct((H, T, 1), F32), jax.ShapeDtypeStruct((H, 1, T), F32)],
        compiler_params=pltpu.CompilerParams(dimension_semantics=("parallel", "arbitrary"),
                                             vmem_limit_bytes=VMEM_LIMIT),
    )(q, k, v, c_col, c_row, do)


@jax.custom_vjp
def _fox_attn(q, k, v, c_col, c_row):
    return _attn_fwd_call(q, k, v, c_col, c_row)


def _fox_attn_fwd(q, k, v, c_col, c_row):
    return _attn_fwd_call(q, k, v, c_col, c_row), (q, k, v, c_col, c_row)


def _fox_attn_bwd(res, do):
    return tuple(_attn_bwd_call(*res, do))


_fox_attn.defvjp(_fox_attn_fwd, _fox_attn_bwd)


def _all_gather(x, name):
    def body(x_ref, out_ref, send_sems, recv_sems, local_sem):
        x_, y_, c_ = lax.axis_index("x"), lax.axis_index("y"), lax.axis_index("c")
        me, sibling = (x_, y_, c_), (x_, y_, 1 - c_)
        chips = [(1 - x_, y_), (x_, 1 - y_), (1 - x_, 1 - y_)]

        def slot(px, py, pc):
            return out_ref.at[4 * px + 2 * py + pc]

        def copy(k, block, to, src=None):
            return pltpu.make_async_remote_copy(
                src_ref=slot(*block) if src is None else src, dst_ref=slot(*block),
                send_sem=send_sems.at[k], recv_sem=recv_sems.at[k],
                device_id=to, device_id_type=pl.DeviceIdType.MESH)

        mine = pltpu.make_async_copy(x_ref, slot(*me), local_sem)
        mine.start()
        first = [copy(0, me, sibling, src=x_ref)]
        first += [copy(1 + j, me, (*chip, c_), src=x_ref) for j, chip in enumerate(chips)]
        for cp in first:
            cp.start()
        passed = [copy(4 + j, (*chip, c_), sibling) for j, chip in enumerate(chips)]
        for j, chip in enumerate(chips):
            copy(1 + j, (*chip, c_), me).wait_recv()
            passed[j].start()
        copy(0, sibling, me).wait_recv()
        for j, chip in enumerate(chips):
            copy(4 + j, (*chip, 1 - c_), me).wait_recv()
        for cp in first + passed:
            cp.wait_send()
        mine.wait()

    return pl.pallas_call(
        body, name=name,
        out_shape=jax.ShapeDtypeStruct((N_DEV,) + x.shape, x.dtype),
        in_specs=[pl.BlockSpec(memory_space=pl.ANY)],
        out_specs=pl.BlockSpec(memory_space=pl.ANY),
        scratch_shapes=[pltpu.SemaphoreType.DMA((7,)), pltpu.SemaphoreType.DMA((7,)), pltpu.SemaphoreType.DMA],
    )(x)


def _all_to_all(a, name):
    def body(a_ref, b_ref, send_sems, recv_sems, local_sem):
        x_, y_, c_ = lax.axis_index("x"), lax.axis_index("y"), lax.axis_index("c")
        me_idx = 4 * x_ + 2 * y_ + c_
        mine = pltpu.make_async_copy(a_ref.at[me_idx], b_ref.at[me_idx], local_sem)
        mine.start()
        copies = []
        for rel in range(1, N_DEV):
            px = 1 - x_ if rel & 4 else x_
            py = 1 - y_ if rel & 2 else y_
            pc = 1 - c_ if rel & 1 else c_
            copies.append(pltpu.make_async_remote_copy(
                src_ref=a_ref.at[4 * px + 2 * py + pc], dst_ref=b_ref.at[me_idx],
                send_sem=send_sems.at[rel - 1], recv_sem=recv_sems.at[rel - 1],
                device_id=(px, py, pc), device_id_type=pl.DeviceIdType.MESH))
        for cp in copies:
            cp.start()
        for cp in copies:
            cp.wait_recv()
        for cp in copies:
            cp.wait_send()
        mine.wait()

    return pl.pallas_call(
        body, name=name,
        out_shape=jax.ShapeDtypeStruct(a.shape, a.dtype),
        in_specs=[pl.BlockSpec(memory_space=pl.ANY)],
        out_specs=pl.BlockSpec(memory_space=pl.ANY),
        scratch_shapes=[pltpu.SemaphoreType.DMA((7,)), pltpu.SemaphoreType.DMA((7,)), pltpu.SemaphoreType.DMA],
    )(a)


def _reduce_adamw(parts, w, m, v, name):
    R, C = w.shape
    tr = max(t for t in (256, 128, PACK_ROW_QUANTUM) if R % t == 0)

    def body(p_ref, w_ref, m_ref, v_ref, g_out, d_out, m_out, v_out):
        g = p_ref[0]
        for i in range(1, N_DEV):
            g = g + p_ref[i]
        m_new = ADAM_B1 * m_ref[...] + (1.0 - ADAM_B1) * g
        v_new = ADAM_B2 * v_ref[...] + (1.0 - ADAM_B2) * (g * g)
        m_hat = m_new / (1.0 - ADAM_B1 ** ADAM_STEP)
        v_hat = v_new / (1.0 - ADAM_B2 ** ADAM_STEP)
        g_out[...] = g
        d_out[...] = -ADAM_LR * (m_hat / (jnp.sqrt(v_hat) + ADAM_EPS) + ADAM_WD * w_ref[...])
        m_out[...] = m_new
        v_out[...] = v_new

    spec = pl.BlockSpec((tr, C), lambda i: (i, 0))
    return pl.pallas_call(
        body, name=name, grid=(R // tr,),
        in_specs=[pl.BlockSpec((N_DEV, tr, C), lambda i: (0, i, 0)), spec, spec, spec],
        out_specs=[spec] * 4,
        out_shape=[jax.ShapeDtypeStruct((R, C), F32)] * 4,
        compiler_params=pltpu.CompilerParams(dimension_semantics=("parallel",), vmem_limit_bytes=VMEM_LIMIT),
    )(parts, w, m, v)


def _pack(arrays, dtype):
    flat = jnp.concatenate([a.reshape(-1).astype(dtype) for a in arrays])
    rows = _round_up(-(-flat.shape[0] // PACK_COLS), PACK_ROW_QUANTUM)
    flat = jnp.pad(flat, (0, rows * PACK_COLS - flat.shape[0]))
    return flat.reshape(rows, PACK_COLS)


def _unpack(packed, shapes):
    lead = packed.shape[:-2]
    flat = packed.reshape(lead + (-1,))
    out, off = [], 0
    for s in shapes:
        n = math.prod(s)
        out.append(flat[..., off:off + n].reshape(lead + tuple(s)))
        off += n
    return out


def _assemble(name, stacked):
    _, r, c = stacked.shape
    if SHARDED[name]:
        return jnp.transpose(stacked, (1, 0, 2)).reshape(r, N_DEV * c)
    return stacked.reshape(N_DEV * r, c)


def _to_heads(u, n_groups):
    T = u.shape[0]
    uh = jnp.transpose(u.reshape(T, n_groups, -1, HEAD_DIM), (1, 2, 0, 3))
    return [uh[i] for i in range(n_groups)]


def _from_heads(uh):
    H, T, N = uh.shape
    return jnp.transpose(uh, (1, 0, 2)).reshape(T, H * N)


def _shift(uh):
    return jnp.pad(uh, ((0, 0), (1, 0), (0, 0)))[:, :-1]


def _pad_cols(a, width):
    return jnp.pad(a, ((0, 0), (0, width - a.shape[1])))


def _local_loss(stacked_f32, small, x, p, target, shard_shapes):
    T, D = x.shape
    blocks = _unpack(stacked_f32, [shard_shapes[n] for n in SHARDED_NAMES])
    W = {n: _assemble(n, b) for n, b in zip(SHARDED_NAMES, blocks)}
    rw = small['w0'].shape[-1]
    fw = W['w_out'].shape[0] - rw
    heads_r, heads_f = rw // HEAD_DIM, fw // HEAD_DIM
    dl, al, gl = W['w2'].shape[0], W['a2'].shape[0], W['g2'].shape[0]
    dl_p, al_p, gl_p = _round_up(dl, LANES), _round_up(al, LANES), _round_up(gl, LANES)
    f_p = _round_up(heads_f, LANES)
    rwkv_cols = 3 * rw + dl + al + gl
    tm_wide = 128
    tm_head = 512 if T % 512 == 0 else T

    w_in = W['w_in']
    o_w, o_a, o_g = 3 * rw, 3 * rw + dl, 3 * rw + dl + al
    w_rkv = w_in[:, :3 * rw]
    w_lora = jnp.concatenate([_pad_cols(w_in[:, o_w:o_a], dl_p), _pad_cols(w_in[:, o_a:o_g], al_p),
                              _pad_cols(w_in[:, o_g:rwkv_cols], gl_p)], axis=1)
    w_qkv = w_in[:, rwkv_cols:rwkv_cols + 3 * fw]
    w_f = _pad_cols(w_in[:, rwkv_cols + 3 * fw:], f_p)
    mu = small['shift_mu'].reshape(1, -1)
    mu_lora = jnp.concatenate([_pad_cols(mu[:, o_w:o_a], dl_p), _pad_cols(mu[:, o_a:o_g], al_p),
                               _pad_cols(mu[:, o_g:rwkv_cols], gl_p)], axis=1)

    def vec(a):
        return a.reshape(1, 1, -1)

    def head_vec(a):
        return a.reshape(-1, 1, HEAD_DIM)

    (xn,) = _stage("attn_norm", _fn_rmsnorm, [x[None]], [vec(small['attn_norm_g'])], tm_wide)
    xn = xn[0]
    u_rkv = _mm(xn, w_rkv, "in_rkv")
    u_lora = _mm(xn, w_lora, "in_lora")
    u_qkv = _mm(xn, w_qkv, "in_qkv")
    f_raw = _mm(xn, w_f, "in_f")

    u_lora3 = u_lora[None]
    xw_t, xa_m, xg_s = _stage("lora_mix", _make_fn_lora_mix(dl_p, al_p), [u_lora3, _shift(u_lora3)],
                              [vec(mu_lora)], tm_wide)
    w_lin = _mm(xw_t[0], jnp.pad(W['w2'], ((0, dl_p - dl), (0, 0))), "w2")
    a_lin = _mm(xa_m[0], jnp.pad(W['a2'], ((0, al_p - al), (0, 0))), "a2")
    gate_r = _mm(xg_s[0], jnp.pad(W['g2'], ((0, gl_p - gl), (0, 0))), "g2")
    ru, ku, vu = _to_heads(u_rkv, 3)
    (w_lin_h,), (a_lin_h,), (gate_h,) = _to_heads(w_lin, 1), _to_heads(a_lin, 1), _to_heads(gate_r, 1)
    mu_r, mu_k, mu_v = (head_vec(mu[:, i * rw:(i + 1) * rw]) for i in range(3))
    r, lw, k_mod, v, kk, b = _stage(
        "rwkv_prep", _fn_rwkv_prep,
        [ru, _shift(ru), ku, _shift(ku), vu, _shift(vu), w_lin_h, a_lin_h],
        [mu_r, mu_k, mu_v, head_vec(small['w0']), head_vec(small['a0']), head_vec(small['k_k']),
         head_vec(small['k_a'])], tm_head)
    y_scan = _rwkv_scan(r, lw, k_mod, v, kk, b)
    (y_rwkv,) = _stage("rwkv_post", _fn_rwkv_post, [y_scan, r, k_mod, v, gate_h],
                       [head_vec(small['lnx_g']), head_vec(small['lnx_b']), head_vec(small['r_k'])], tm_head)

    qu, kf, vf = _to_heads(u_qkv, 3)
    qg = jnp.broadcast_to(vec(small['q_norm_g']), (heads_f, 1, HEAD_DIM))
    kg = jnp.broadcast_to(vec(small['k_norm_g']), (heads_f, 1, HEAD_DIM))
    qn, kn = _stage("fox_prep", _fn_fox_prep, [qu, kf], [qg, kg], tm_head)
    fb = _pad_cols(small['fgate_b'].reshape(1, -1), f_p)
    (log_f,) = _stage("log_forget", _fn_log_forget, [f_raw[None]], [vec(fb)], tm_head)
    c = jnp.cumsum(log_f[0][:, :heads_f], axis=0).T
    y_fox = _fox_attn(qn, kn, vf, c[:, :, None], c[:, None, :])

    y_cat = jnp.concatenate([_from_heads(y_rwkv), _from_heads(y_fox)], axis=-1)
    h1 = x + _mm(y_cat, W['w_out'], "out")
    (hn,) = _stage("ffn_norm", _fn_rmsnorm, [h1[None]], [vec(small['ffn_norm_g'])], tm_wide)
    gate = _mm(hn[0], W['w_gate'], "gate")
    up = _mm(hn[0], W['w_up'], "up")
    (act,) = _stage("swiglu", _fn_swiglu, [gate[None], up[None]], [], tm_wide)
    h2 = h1 + _mm(act[0], W['w_down'], "down")
    e_raw = _mm(p, W['ple_proj'], "ple_proj")
    (hg,) = _stage("ple_gate_norm", _fn_rmsnorm, [h2[None]], [vec(small['ple_gate_norm_g'])], tm_wide)
    z = _mm(hg[0], W['ple_gate_w'], "ple_gate")
    (loss_rows,) = _stage("final", _fn_final, [z[None], e_raw[None], h2[None], target[None]],
                          [vec(small['ple_gate_b']), vec(small['ple_norm_g'])], tm_wide)
    return jnp.sum(loss_rows)


def kernel(x, p, attn_norm_g, w_in, shift_mu, w0, w2, a0, a2, g2, k_k, k_a, r_k, lnx_g, lnx_b, q_norm_g, k_norm_g, fgate_b, w_out, ffn_norm_g, w_gate, w_up, w_down, ple_proj, ple_norm_g, ple_gate_norm_g, ple_gate_w, ple_gate_b, loss_target, m_attn_norm_g, m_w_in, m_shift_mu, m_w0, m_w2, m_a0, m_a2, m_g2, m_k_k, m_k_a, m_r_k, m_lnx_g, m_lnx_b, m_q_norm_g, m_k_norm_g, m_fgate_b, m_w_out, m_ffn_norm_g, m_w_gate, m_w_up, m_w_down, m_ple_proj, m_ple_norm_g, m_ple_gate_norm_g, m_ple_gate_w, m_ple_gate_b, v_attn_norm_g, v_w_in, v_shift_mu, v_w0, v_w2, v_a0, v_a2, v_g2, v_k_k, v_k_a, v_r_k, v_lnx_g, v_lnx_b, v_q_norm_g, v_k_norm_g, v_fgate_b, v_w_out, v_ffn_norm_g, v_w_gate, v_w_up, v_w_down, v_ple_proj, v_ple_norm_g, v_ple_gate_norm_g, v_ple_gate_w, v_ple_gate_b):
    weights = dict(zip(WEIGHT_NAMES, (attn_norm_g, w_in, shift_mu, w0, w2, a0, a2, g2, k_k, k_a, r_k, lnx_g, lnx_b,
                                      q_norm_g, k_norm_g, fgate_b, w_out, ffn_norm_g, w_gate, w_up, w_down, ple_proj,
                                      ple_norm_g, ple_gate_norm_g, ple_gate_w, ple_gate_b)))
    m_in = dict(zip(WEIGHT_NAMES, (m_attn_norm_g, m_w_in, m_shift_mu, m_w0, m_w2, m_a0, m_a2, m_g2, m_k_k, m_k_a, m_r_k,
                                   m_lnx_g, m_lnx_b, m_q_norm_g, m_k_norm_g, m_fgate_b, m_w_out, m_ffn_norm_g, m_w_gate,
                                   m_w_up, m_w_down, m_ple_proj, m_ple_norm_g, m_ple_gate_norm_g, m_ple_gate_w,
                                   m_ple_gate_b)))
    v_in = dict(zip(WEIGHT_NAMES, (v_attn_norm_g, v_w_in, v_shift_mu, v_w0, v_w2, v_a0, v_a2, v_g2, v_k_k, v_k_a, v_r_k,
                                   v_lnx_g, v_lnx_b, v_q_norm_g, v_k_norm_g, v_fgate_b, v_w_out, v_ffn_norm_g, v_w_gate,
                                   v_w_up, v_w_down, v_ple_proj, v_ple_norm_g, v_ple_gate_norm_g, v_ple_gate_w,
                                   v_ple_gate_b)))
    shard_shapes = {n: weights[n].shape[1:] for n in SHARDED_NAMES}
    small_shapes = [weights[n].shape for n in SMALL_NAMES]

    gathered = _all_gather(_pack([weights[n] for n in SHARDED_NAMES], BF16), "gather_weights")
    stacked_f32 = gathered.astype(F32)

    small = {n: weights[n] for n in SMALL_NAMES}
    loss_local, (d_stacked, d_small, d_x) = jax.value_and_grad(_local_loss, argnums=(0, 1, 2))(
        stacked_f32, small, x[0], p[0, 0], loss_target[0], shard_shapes)
    loss = lax.psum(loss_local, MESH_AXES)

    parts = _all_to_all(d_stacked, "scatter_grads")
    small_parts = _all_gather(_pack([d_small[n] for n in SMALL_NAMES], F32), "gather_small_grads")

    def pack_f32(tree, names):
        return _pack([tree[n] for n in names], F32)

    big = _reduce_adamw(parts, pack_f32(weights, SHARDED_NAMES), pack_f32(m_in, SHARDED_NAMES),
                        pack_f32(v_in, SHARDED_NAMES), "adamw_sharded")
    sml = _reduce_adamw(small_parts, pack_f32(weights, SMALL_NAMES), pack_f32(m_in, SMALL_NAMES),
                        pack_f32(v_in, SMALL_NAMES), "adamw_replicated")
    big_shapes = [weights[n].shape for n in SHARDED_NAMES]
    outs = []
    for kind in range(4):
        by_name = dict(zip(SHARDED_NAMES, _unpack(big[kind], big_shapes)))
        by_name.update(zip(SMALL_NAMES, _unpack(sml[kind], small_shapes)))
        outs.extend(by_name[n] for n in WEIGHT_NAMES)
    return (loss, d_x[None], *outs)
```

```python
import functools
import math

import jax
import jax.numpy as jnp
from jax import lax
from jax.experimental import pallas as pl
from jax.experimental.pallas import tpu as pltpu

F32 = jnp.float32
BF16 = jnp.bfloat16
HIGHEST = lax.Precision.HIGHEST

N_DEV = 8
MESH_AXES = ("x", "y", "c")
HEAD_DIM = 64
SCAN_CHUNK = 64
SCAN_HEADS_PER_STEP = 16
ATTN_BLOCK_Q = 128
ATTN_CAUSAL_GROUPS = 4
LANES = 128
PACK_COLS = 1024
PACK_ROW_QUANTUM = 64
VMEM_LIMIT = 48 * 1024 * 1024
RMS_EPS = 1e-6
GN_EPS = 64e-5
ADAM_LR, ADAM_B1, ADAM_B2, ADAM_EPS, ADAM_WD, ADAM_STEP = 0.001, 0.9, 0.999, 1e-08, 0.01, 10

WEIGHT_NAMES = ['attn_norm_g', 'w_in', 'shift_mu', 'w0', 'w2', 'a0', 'a2', 'g2', 'k_k', 'k_a', 'r_k', 'lnx_g', 'lnx_b',
                'q_norm_g', 'k_norm_g', 'fgate_b', 'w_out', 'ffn_norm_g', 'w_gate', 'w_up', 'w_down', 'ple_proj',
                'ple_norm_g', 'ple_gate_norm_g', 'ple_gate_w', 'ple_gate_b']
SHARDED = {'w_in': True, 'w2': True, 'a2': True, 'g2': True, 'w_out': False, 'w_gate': True, 'w_up': True,
           'w_down': False, 'ple_proj': True, 'ple_gate_w': False}
SHARDED_NAMES = [n for n in WEIGHT_NAMES if n in SHARDED]
SMALL_NAMES = [n for n in WEIGHT_NAMES if n not in SHARDED]


def _round_up(n, q):
    return -(-n // q) * q


def _pick_tile(n, cap):
    for t in (1024, 512, 256, 128):
        if t <= cap and n % t == 0:
            return t
    return n


def _mm_call(a, b, mode, name):
    if mode == "nn":
        (I, C), (_, J) = a.shape, b.shape
    elif mode == "nt":
        (I, C), (J, _) = a.shape, b.shape
    else:
        (C, I), (_, J) = a.shape, b.shape
    ti, tj, tc = _pick_tile(I, 1024), _pick_tile(J, 1024), _pick_tile(C, 512)
    n_c = C // tc
    if mode == "nn":
        a_spec = pl.BlockSpec((ti, tc), lambda i, j, c: (i, c))
        b_spec = pl.BlockSpec((tc, tj), lambda i, j, c: (c, j))
        dims = (((1,), (0,)), ((), ()))
    elif mode == "nt":
        a_spec = pl.BlockSpec((ti, tc), lambda i, j, c: (i, c))
        b_spec = pl.BlockSpec((tj, tc), lambda i, j, c: (j, c))
        dims = (((1,), (1,)), ((), ()))
    else:
        a_spec = pl.BlockSpec((tc, ti), lambda i, j, c: (c, i))
        b_spec = pl.BlockSpec((tc, tj), lambda i, j, c: (c, j))
        dims = (((0,), (0,)), ((), ()))

    def body(a_ref, b_ref, o_ref, acc):
        c = pl.program_id(2)

        @pl.when(c == 0)
        def _():
            acc[...] = jnp.zeros_like(acc)

        acc[...] += lax.dot_general(a_ref[...].astype(BF16), b_ref[...].astype(BF16), dims,
                                    preferred_element_type=F32)

        @pl.when(c == n_c - 1)
        def _():
            o_ref[...] = acc[...]

    return pl.pallas_call(
        body, name=name, grid=(I // ti, J // tj, n_c),
        in_specs=[a_spec, b_spec],
        out_specs=pl.BlockSpec((ti, tj), lambda i, j, c: (i, j)),
        out_shape=jax.ShapeDtypeStruct((I, J), F32),
        scratch_shapes=[pltpu.VMEM((ti, tj), F32)],
        compiler_params=pltpu.CompilerParams(dimension_semantics=("parallel", "parallel", "arbitrary"),
                                             vmem_limit_bytes=VMEM_LIMIT),
    )(a, b)


def _mm(a, b, name):
    @jax.custom_vjp
    def run(a, b):
        return _mm_call(a, b, "nn", "mm_" + name)

    def fwd(a, b):
        return run(a, b), (a, b)

    def bwd(res, g):
        a, b = res
        return _mm_call(g, b, "nt", "mm_" + name + "_da"), _mm_call(a, g, "tn", "mm_" + name + "_db")

    run.defvjp(fwd, bwd)
    return run(a, b)


def _mm_t(a, wt, name):
    @jax.custom_vjp
    def run(a, wt):
        return _mm_call(a, wt, "nt", "mmt_" + name)

    def fwd(a, wt):
        return run(a, wt), (a, wt)

    def bwd(res, g):
        a, wt = res
        return _mm_call(g, wt, "nn", "mmt_" + name + "_da"), _mm_call(g, a, "tn", "mmt_" + name + "_dw")

    run.defvjp(fwd, bwd)
    return run(a, wt)


def _stage_fwd_call(name, fn, rows, params, tm):
    G, T, _ = rows[0].shape
    nr, npar = len(rows), len(params)
    out_avals = jax.eval_shape(
        lambda *a: tuple(fn(*a)),
        *[jax.ShapeDtypeStruct((tm, r.shape[2]), r.dtype) for r in rows],
        *[jax.ShapeDtypeStruct((1, p.shape[2]), p.dtype) for p in params])

    def body(*refs):
        vals = [r[0] for r in refs[:nr + npar]]
        for o_ref, o in zip(refs[nr + npar:], fn(*vals)):
            o_ref[0] = o

    def row_spec(c):
        return pl.BlockSpec((1, tm, c), lambda g, t: (g, t, 0))

    def par_spec(c):
        return pl.BlockSpec((1, 1, c), lambda g, t: (g, 0, 0))

    return pl.pallas_call(
        body, name=name, grid=(G, T // tm),
        in_specs=[row_spec(r.shape[2]) for r in rows] + [par_spec(p.shape[2]) for p in params],
        out_specs=[row_spec(o.shape[1]) for o in out_avals],
        out_shape=[jax.ShapeDtypeStruct((G, T, o.shape[1]), o.dtype) for o in out_avals],
        compiler_params=pltpu.CompilerParams(dimension_semantics=("parallel", "parallel"),
                                             vmem_limit_bytes=VMEM_LIMIT),
    )(*rows, *params)


def _stage_bwd_call(name, fn, rows, params, cts, tm):
    G, T, _ = rows[0].shape
    nr, npar, nout = len(rows), len(params), len(cts)

    def body(*refs):
        vals = [r[0] for r in refs[:nr + npar]]
        ct_vals = tuple(r[0] for r in refs[nr + npar:nr + npar + nout])
        d_refs = refs[nr + npar + nout:]
        _, vjp_fn = jax.vjp(lambda *a: tuple(fn(*a)), *vals)
        grads = vjp_fn(ct_vals)
        for i in range(nr):
            d_refs[i][0] = grads[i]

        if npar:
            @pl.when(pl.program_id(1) == 0)
            def _():
                for j in range(npar):
                    d_refs[nr + j][...] = jnp.zeros_like(d_refs[nr + j])

        for j in range(npar):
            d_refs[nr + j][0] += grads[nr + j]

    def row_spec(c):
        return pl.BlockSpec((1, tm, c), lambda g, t: (g, t, 0))

    def par_spec(c):
        return pl.BlockSpec((1, 1, c), lambda g, t: (g, 0, 0))

    outs = pl.pallas_call(
        body, name=name + "_bwd", grid=(G, T // tm),
        in_specs=([row_spec(r.shape[2]) for r in rows] + [par_spec(p.shape[2]) for p in params]
                  + [row_spec(c.shape[2]) for c in cts]),
        out_specs=[row_spec(r.shape[2]) for r in rows] + [par_spec(p.shape[2]) for p in params],
        out_shape=([jax.ShapeDtypeStruct(r.shape, r.dtype) for r in rows]
                   + [jax.ShapeDtypeStruct(p.shape, p.dtype) for p in params]),
        compiler_params=pltpu.CompilerParams(dimension_semantics=("parallel", "arbitrary"),
                                             vmem_limit_bytes=VMEM_LIMIT),
    )(*rows, *params, *cts)
    return tuple(outs[:nr]), tuple(outs[nr:])


def _stage(name, fn, rows, params, tm):
    @jax.custom_vjp
    def run(rows, params):
        return tuple(_stage_fwd_call(name, fn, rows, params, tm))

    def fwd(rows, params):
        return run(rows, params), (rows, params)

    def bwd(res, cts):
        rows, params = res
        return _stage_bwd_call(name, fn, rows, params, tuple(cts), tm)

    run.defvjp(fwd, bwd)
    return run(tuple(rows), tuple(params))


def _sigmoid(x):
    return 0.5 * (jnp.tanh(0.5 * x) + 1.0)


def _softplus(x):
    return jnp.maximum(x, 0.0) + jnp.log(1.0 + jnp.exp(-jnp.abs(x)))


def _rms(x, g, eps=RMS_EPS):
    return x * lax.rsqrt(jnp.mean(x * x, axis=-1, keepdims=True) + eps) * g


def _fn_rmsnorm(x, g):
    return (_rms(x, g),)


def _fn_swiglu(gate, up):
    return (gate * _sigmoid(gate) * up,)


def _make_fn_lora_mix(p1, p2):
    def fn(u, u_prev, mu):
        um = u + (u_prev - u) * mu
        return jnp.tanh(um[:, :p1]), um[:, p1:p1 + p2], _sigmoid(um[:, p1 + p2:])
    return fn


def _fn_rwkv_prep(ru, ru_p, ku, ku_p, vu, vu_p, w_lin, a_lin, mu_r, mu_k, mu_v, w0, a0, k_k, k_a):
    r = ru + (ru_p - ru) * mu_r
    k = ku + (ku_p - ku) * mu_k
    v = vu + (vu_p - vu) * mu_v
    w_log = -_softplus(-(w0 + w_lin)) - 0.5
    lw = -jnp.exp(w_log)
    a = _sigmoid(a0 + a_lin)
    kk = k * k_k
    kk = kk / jnp.maximum(jnp.sqrt(jnp.sum(kk * kk, axis=-1, keepdims=True)), 1e-12)
    k_mod = k * (1.0 + (a - 1.0) * k_a)
    return r, lw, k_mod, v, kk, kk * a


def _fn_rwkv_post(y, r, k_mod, v, g, lnx_g, lnx_b, r_k):
    mean = jnp.mean(y, axis=-1, keepdims=True)
    yc = y - mean
    var = jnp.mean(yc * yc, axis=-1, keepdims=True)
    yn = yc * lax.rsqrt(var + GN_EPS) * lnx_g + lnx_b
    bonus = jnp.sum(r * k_mod * r_k, axis=-1, keepdims=True) * v
    return ((yn + bonus) * g,)


def _fn_fox_prep(q, k, qg, kg):
    return _rms(q, qg), _rms(k, kg)


def _fn_log_forget(f_raw, b):
    x = f_raw + b
    return (jnp.minimum(x, 0.0) - jnp.log(1.0 + jnp.exp(-jnp.abs(x))),)


def _fn_final(z, e_raw, h2, target, gate_b, ple_g):
    gate = _sigmoid(z + gate_b)
    out = h2 + gate * _rms(e_raw, ple_g)
    err = out - target
    return (0.5 * jnp.mean(err * err, axis=-1, keepdims=True),)


def _dot_bf16(a, b, ca, cb):
    return lax.dot_general(a.astype(BF16), b.astype(BF16), (((ca,), (cb,)), ((0,), (0,))),
                           preferred_element_type=F32)


@functools.partial(jax.custom_vjp, nondiff_argnums=(2, 3))
def _dot(a, b, ca, cb):
    return _dot_bf16(a, b, ca, cb)


def _dot_fwd(a, b, ca, cb):
    return _dot_bf16(a, b, ca, cb), (a, b)


def _dot_bwd(ca, cb, res, g):
    a, b = res
    ia, jb = 3 - ca, 3 - cb
    da = _dot_bf16(g, b, 2, jb) if ca == 2 else _dot_bf16(b, g, jb, 2)
    db = _dot_bf16(a, g, ia, 1) if cb == 1 else _dot_bf16(g, a, 1, ia)
    return da, db


_dot.defvjp(_dot_fwd, _dot_bwd)


def _scan_chunk(S0, r, lw, k, v, kk, b):
    B, L, _ = r.shape
    row = lax.broadcasted_iota(jnp.int32, (B, L, L), 1)
    col = lax.broadcasted_iota(jnp.int32, (B, L, L), 2)
    incl = col <= row
    strict = col < row
    cum = lax.dot_general(incl.astype(F32), lw, (((2,), (1,)), ((0,), (0,))), precision=HIGHEST,
                          preferred_element_type=F32)
    g_in, g_ex, g_inv = jnp.exp(cum), jnp.exp(cum - lw), jnp.exp(-cum)
    kkg, kd, bd, rg = kk * g_ex, k * g_inv, b * g_inv, r * g_in
    a_k = jnp.where(strict, _dot(kkg, kd, 2, 2), 0.0)
    a_b = jnp.where(strict, _dot(kkg, bd, 2, 2), 0.0)
    pw = -a_b
    inv = (row == col).astype(F32) + pw
    for _ in range(int(math.log2(L)) - 1):
        pw = _dot(pw, pw, 2, 1)
        inv = inv + _dot(inv, pw, 2, 1)
    sa = -_dot(inv, _dot(kkg, S0, 2, 2) + _dot(a_k, v, 2, 1), 2, 1)
    r_k = jnp.where(incl, _dot(rg, kd, 2, 2), 0.0)
    r_b = jnp.where(incl, _dot(rg, bd, 2, 2), 0.0)
    y = _dot(rg, S0, 2, 2) + _dot(r_k, v, 2, 1) + _dot(r_b, sa, 2, 1)
    g_end = jnp.exp(jnp.sum(lw, axis=1, keepdims=True))
    S1 = S0 * g_end + _dot(v, kd * g_end, 1, 1) + _dot(sa, bd * g_end, 1, 1)
    return y, S1


def _scan_heads_per_step(H):
    return next(hb for hb in (SCAN_HEADS_PER_STEP, 2, 1) if H % hb == 0)


def _scan_fwd_call(r, lw, k, v, kk, b):
    H, T, N = r.shape
    L = SCAN_CHUNK
    n_chunks = T // L
    hb = _scan_heads_per_step(H)

    def body(r_ref, lw_ref, k_ref, v_ref, kk_ref, b_ref, y_ref, s0_ref, state):
        @pl.when(pl.program_id(1) == 0)
        def _():
            state[...] = jnp.zeros_like(state)

        S0 = state[...]
        s0_ref[:, 0] = S0
        y, S1 = _scan_chunk(S0, r_ref[...], lw_ref[...], k_ref[...], v_ref[...], kk_ref[...], b_ref[...])
        y_ref[...] = y
        state[...] = S1

    blk = pl.BlockSpec((hb, L, N), lambda h, c: (h, c, 0))
    return pl.pallas_call(
        body, name="rwkv_scan_fwd", grid=(H // hb, n_chunks),
        in_specs=[blk] * 6,
        out_specs=[blk, pl.BlockSpec((hb, 1, N, N), lambda h, c: (h, c, 0, 0))],
        out_shape=[jax.ShapeDtypeStruct((H, T, N), F32), jax.ShapeDtypeStruct((H, n_chunks, N, N), F32)],
        scratch_shapes=[pltpu.VMEM((hb, N, N), F32)],
        compiler_params=pltpu.CompilerParams(dimension_semantics=("parallel", "arbitrary")),
    )(r, lw, k, v, kk, b)


def _scan_bwd_call(r, lw, k, v, kk, b, s0s, dy):
    H, T, N = r.shape
    L = SCAN_CHUNK
    n_chunks = T // L
    hb = _scan_heads_per_step(H)

    def body(r_ref, lw_ref, k_ref, v_ref, kk_ref, b_ref, s0_ref, dy_ref, dr, dlw, dk, dv, dkk, db, d_state):
        @pl.when(pl.program_id(1) == 0)
        def _():
            d_state[...] = jnp.zeros_like(d_state)

        _, vjp_fn = jax.vjp(_scan_chunk, s0_ref[:, 0], r_ref[...], lw_ref[...], k_ref[...], v_ref[...], kk_ref[...],
                            b_ref[...])
        grads = vjp_fn((dy_ref[...], d_state[...]))
        d_state[...] = grads[0]
        for o_ref, g in zip((dr, dlw, dk, dv, dkk, db), grads[1:]):
            o_ref[...] = g

    blk = pl.BlockSpec((hb, L, N), lambda h, c: (h, n_chunks - 1 - c, 0))
    return pl.pallas_call(
        body, name="rwkv_scan_bwd", grid=(H // hb, n_chunks),
        in_specs=[blk] * 6 + [pl.BlockSpec((hb, 1, N, N), lambda h, c: (h, n_chunks - 1 - c, 0, 0)), blk],
        out_specs=[blk] * 6,
        out_shape=[jax.ShapeDtypeStruct((H, T, N), F32)] * 6,
        scratch_shapes=[pltpu.VMEM((hb, N, N), F32)],
        compiler_params=pltpu.CompilerParams(dimension_semantics=("parallel", "arbitrary")),
    )(r, lw, k, v, kk, b, s0s, dy)


@jax.custom_vjp
def _rwkv_scan(r, lw, k, v, kk, b):
    return _scan_fwd_call(r, lw, k, v, kk, b)[0]


def _rwkv_scan_fwd(r, lw, k, v, kk, b):
    y, s0s = _scan_fwd_call(r, lw, k, v, kk, b)
    return y, (r, lw, k, v, kk, b, s0s)


def _rwkv_scan_bwd(res, dy):
    return tuple(_scan_bwd_call(*res, dy))


_rwkv_scan.defvjp(_rwkv_scan_fwd, _rwkv_scan_bwd)


def _attn_block(q, k, v, c_col, c_row, q0):
    bq, t = q.shape[0], k.shape[0]
    s = lax.dot_general(q.astype(BF16), k.astype(BF16), (((1,), (1,)), ((), ())), preferred_element_type=F32)
    s = s * (HEAD_DIM ** -0.5) + (c_col - c_row)
    qi = q0 + lax.broadcasted_iota(jnp.int32, (bq, t), 0)
    ki = lax.broadcasted_iota(jnp.int32, (bq, t), 1)
    s = jnp.where(ki <= qi, s, -jnp.inf)
    m = lax.stop_gradient(jnp.max(s, axis=-1, keepdims=True))
    p = jnp.exp(s - m)
    p = p / jnp.sum(p, axis=-1, keepdims=True)
    return lax.dot_general(p.astype(BF16), v.astype(BF16), (((1,), (0,)), ((), ())), preferred_element_type=F32)


def _attn_specs(tk, N, bq):
    q_spec = pl.BlockSpec((1, bq, N), lambda h, i: (h, i, 0))
    kv_spec = pl.BlockSpec((1, tk, N), lambda h, i: (h, 0, 0))
    col_spec = pl.BlockSpec((1, bq, 1), lambda h, i: (h, i, 0))
    row_spec = pl.BlockSpec((1, 1, tk), lambda h, i: (h, 0, 0))
    return q_spec, kv_spec, col_spec, row_spec


def _attn_fwd_call(q, k, v, c_col, c_row, q_off, name):
    H, tq, N = q.shape
    bq = ATTN_BLOCK_Q
    q_spec, kv_spec, col_spec, row_spec = _attn_specs(k.shape[1], N, bq)

    def body(q_ref, k_ref, v_ref, cc_ref, cr_ref, o_ref):
        o_ref[0] = _attn_block(q_ref[0], k_ref[0], v_ref[0], cc_ref[0], cr_ref[0], q_off + pl.program_id(1) * bq)

    return pl.pallas_call(
        body, name=name, grid=(H, tq // bq),
        in_specs=[q_spec, kv_spec, kv_spec, col_spec, row_spec],
        out_specs=q_spec,
        out_shape=jax.ShapeDtypeStruct((H, tq, N), F32),
        compiler_params=pltpu.CompilerParams(dimension_semantics=("parallel", "parallel"),
                                             vmem_limit_bytes=VMEM_LIMIT),
    )(q, k, v, c_col, c_row)


def _attn_bwd_call(q, k, v, c_col, c_row, do, q_off, name):
    H, tq, N = q.shape
    tk = k.shape[1]
    bq = ATTN_BLOCK_Q
    q_spec, kv_spec, col_spec, row_spec = _attn_specs(tk, N, bq)

    def body(q_ref, k_ref, v_ref, cc_ref, cr_ref, do_ref, dq_ref, dk_ref, dv_ref, dcc_ref, dcr_ref):
        i = pl.program_id(1)
        _, vjp_fn = jax.vjp(functools.partial(_attn_block, q0=q_off + i * bq), q_ref[0], k_ref[0], v_ref[0],
                            cc_ref[0], cr_ref[0])
        dq, dk, dv, dcc, dcr = vjp_fn(do_ref[0])
        dq_ref[0] = dq
        dcc_ref[0] = dcc

        @pl.when(i == 0)
        def _():
            dk_ref[...] = jnp.zeros_like(dk_ref)
            dv_ref[...] = jnp.zeros_like(dv_ref)
            dcr_ref[...] = jnp.zeros_like(dcr_ref)

        dk_ref[0] += dk
        dv_ref[0] += dv
        dcr_ref[0] += dcr

    return pl.pallas_call(
        body, name=name, grid=(H, tq // bq),
        in_specs=[q_spec, kv_spec, kv_spec, col_spec, row_spec, q_spec],
        out_specs=[q_spec, kv_spec, kv_spec, col_spec, row_spec],
        out_shape=[jax.ShapeDtypeStruct((H, tq, N), F32), jax.ShapeDtypeStruct((H, tk, N), F32),
                   jax.ShapeDtypeStruct((H, tk, N), F32), jax.ShapeDtypeStruct((H, tq, 1), F32),
                   jax.ShapeDtypeStruct((H, 1, tk), F32)],
        compiler_params=pltpu.CompilerParams(dimension_semantics=("parallel", "arbitrary"),
                                             vmem_limit_bytes=VMEM_LIMIT),
    )(q, k, v, c_col, c_row, do)


def _fox_attn_group(q, k, v, c_col, c_row, q_off, tag):
    @jax.custom_vjp
    def run(q, k, v, c_col, c_row):
        return _attn_fwd_call(q, k, v, c_col, c_row, q_off, "fox_attn_fwd_" + tag)

    def fwd(q, k, v, c_col, c_row):
        return run(q, k, v, c_col, c_row), (q, k, v, c_col, c_row)

    def bwd(res, do):
        return tuple(_attn_bwd_call(*res, do, q_off, "fox_attn_bwd_" + tag))

    run.defvjp(fwd, bwd)
    return run(q, k, v, c_col, c_row)


def _fox_attn(q, k, v, c):
    T = q.shape[1]
    groups = next(g for g in (ATTN_CAUSAL_GROUPS, 2, 1) if T % (g * ATTN_BLOCK_Q) == 0)
    tg = T // groups
    outs = []
    for g in range(groups):
        lo, hi = g * tg, (g + 1) * tg
        outs.append(_fox_attn_group(q[:, lo:hi], k[:, :hi], v[:, :hi], c[:, lo:hi, None], c[:, None, :hi], lo, str(g)))
    return jnp.concatenate(outs, axis=1)


def _all_gather(x, name):
    def body(x_ref, out_ref, send_sems, recv_sems, local_sem):
        x_, y_, c_ = lax.axis_index("x"), lax.axis_index("y"), lax.axis_index("c")
        me, sibling = (x_, y_, c_), (x_, y_, 1 - c_)
        chips = [(1 - x_, y_), (x_, 1 - y_), (1 - x_, 1 - y_)]

        def slot(px, py, pc):
            return out_ref.at[4 * px + 2 * py + pc]

        def copy(k, block, to, src=None):
            return pltpu.make_async_remote_copy(
                src_ref=slot(*block) if src is None else src, dst_ref=slot(*block),
                send_sem=send_sems.at[k], recv_sem=recv_sems.at[k],
                device_id=to, device_id_type=pl.DeviceIdType.MESH)

        mine = pltpu.make_async_copy(x_ref, slot(*me), local_sem)
        mine.start()
        first = [copy(0, me, sibling, src=x_ref)]
        first += [copy(1 + j, me, (*chip, c_), src=x_ref) for j, chip in enumerate(chips)]
        for cp in first:
            cp.start()
        passed = [copy(4 + j, (*chip, c_), sibling) for j, chip in enumerate(chips)]
        for j, chip in enumerate(chips):
            copy(1 + j, (*chip, c_), me).wait_recv()
            passed[j].start()
        copy(0, sibling, me).wait_recv()
        for j, chip in enumerate(chips):
            copy(4 + j, (*chip, 1 - c_), me).wait_recv()
        for cp in first + passed:
            cp.wait_send()
        mine.wait()

    return pl.pallas_call(
        body, name=name,
        out_shape=jax.ShapeDtypeStruct((N_DEV,) + x.shape, x.dtype),
        in_specs=[pl.BlockSpec(memory_space=pl.ANY)],
        out_specs=pl.BlockSpec(memory_space=pl.ANY),
        scratch_shapes=[pltpu.SemaphoreType.DMA((7,)), pltpu.SemaphoreType.DMA((7,)), pltpu.SemaphoreType.DMA],
    )(x)


def _all_to_all(a, name):
    def body(a_ref, b_ref, send_sems, recv_sems, local_sem):
        x_, y_, c_ = lax.axis_index("x"), lax.axis_index("y"), lax.axis_index("c")
        me_idx = 4 * x_ + 2 * y_ + c_
        mine = pltpu.make_async_copy(a_ref.at[me_idx], b_ref.at[me_idx], local_sem)
        mine.start()
        copies = []
        for rel in range(1, N_DEV):
            px = 1 - x_ if rel & 4 else x_
            py = 1 - y_ if rel & 2 else y_
            pc = 1 - c_ if rel & 1 else c_
            copies.append(pltpu.make_async_remote_copy(
                src_ref=a_ref.at[4 * px + 2 * py + pc], dst_ref=b_ref.at[me_idx],
                send_sem=send_sems.at[rel - 1], recv_sem=recv_sems.at[rel - 1],
                device_id=(px, py, pc), device_id_type=pl.DeviceIdType.MESH))
        for cp in copies:
            cp.start()
        for cp in copies:
            cp.wait_recv()
        for cp in copies:
            cp.wait_send()
        mine.wait()

    return pl.pallas_call(
        body, name=name,
        out_shape=jax.ShapeDtypeStruct(a.shape, a.dtype),
        in_specs=[pl.BlockSpec(memory_space=pl.ANY)],
        out_specs=pl.BlockSpec(memory_space=pl.ANY),
        scratch_shapes=[pltpu.SemaphoreType.DMA((7,)), pltpu.SemaphoreType.DMA((7,)), pltpu.SemaphoreType.DMA],
    )(a)


def _reduce_adamw(parts, w, m, v, name):
    R, C = w.shape
    tr = max(t for t in (256, 128, PACK_ROW_QUANTUM) if R % t == 0)

    def body(p_ref, w_ref, m_ref, v_ref, g_out, d_out, m_out, v_out):
        g = p_ref[0]
        for i in range(1, N_DEV):
            g = g + p_ref[i]
        m_new = ADAM_B1 * m_ref[...] + (1.0 - ADAM_B1) * g
        v_new = ADAM_B2 * v_ref[...] + (1.0 - ADAM_B2) * (g * g)
        m_hat = m_new / (1.0 - ADAM_B1 ** ADAM_STEP)
        v_hat = v_new / (1.0 - ADAM_B2 ** ADAM_STEP)
        g_out[...] = g
        d_out[...] = -ADAM_LR * (m_hat / (jnp.sqrt(v_hat) + ADAM_EPS) + ADAM_WD * w_ref[...])
        m_out[...] = m_new
        v_out[...] = v_new

    spec = pl.BlockSpec((tr, C), lambda i: (i, 0))
    return pl.pallas_call(
        body, name=name, grid=(R // tr,),
        in_specs=[pl.BlockSpec((N_DEV, tr, C), lambda i: (0, i, 0)), spec, spec, spec],
        out_specs=[spec] * 4,
        out_shape=[jax.ShapeDtypeStruct((R, C), F32)] * 4,
        compiler_params=pltpu.CompilerParams(dimension_semantics=("parallel",), vmem_limit_bytes=VMEM_LIMIT),
    )(parts, w, m, v)


def _pack(arrays, dtype):
    flat = jnp.concatenate([a.reshape(-1).astype(dtype) for a in arrays])
    rows = _round_up(-(-flat.shape[0] // PACK_COLS), PACK_ROW_QUANTUM)
    flat = jnp.pad(flat, (0, rows * PACK_COLS - flat.shape[0]))
    return flat.reshape(rows, PACK_COLS)


def _unpack(packed, shapes):
    lead = packed.shape[:-2]
    flat = packed.reshape(lead + (-1,))
    out, off = [], 0
    for s in shapes:
        n = math.prod(s)
        out.append(flat[..., off:off + n].reshape(lead + tuple(s)))
        off += n
    return out


def _travel_layout(name, block):
    return block.T if SHARDED[name] else block


def _reduce_parts(parts, name):
    _, R, C = parts.shape
    tr = max(t for t in (256, 128, PACK_ROW_QUANTUM) if R % t == 0)

    def body(p_ref, o_ref):
        g = p_ref[0].astype(F32)
        for i in range(1, N_DEV):
            g = g + p_ref[i].astype(F32)
        o_ref[...] = g

    return pl.pallas_call(
        body, name=name, grid=(R // tr,),
        in_specs=[pl.BlockSpec((N_DEV, tr, C), lambda i: (0, i, 0))],
        out_specs=pl.BlockSpec((tr, C), lambda i: (i, 0)),
        out_shape=jax.ShapeDtypeStruct((R, C), F32),
        compiler_params=pltpu.CompilerParams(dimension_semantics=("parallel",), vmem_limit_bytes=VMEM_LIMIT),
    )(parts)


def _adamw(g, w, m, v, name):
    R, C = w.shape
    tr = next((t for t in (512, 256, 128, 64, 32, 16, 8) if R % t == 0 and t * C * 4 <= 2 * 1024 * 1024), R)

    def body(g_ref, w_ref, m_ref, v_ref, d_out, m_out, v_out):
        g_ = g_ref[...]
        m_new = ADAM_B1 * m_ref[...] + (1.0 - ADAM_B1) * g_
        v_new = ADAM_B2 * v_ref[...] + (1.0 - ADAM_B2) * (g_ * g_)
        m_hat = m_new / (1.0 - ADAM_B1 ** ADAM_STEP)
        v_hat = v_new / (1.0 - ADAM_B2 ** ADAM_STEP)
        d_out[...] = -ADAM_LR * (m_hat / (jnp.sqrt(v_hat) + ADAM_EPS) + ADAM_WD * w_ref[...])
        m_out[...] = m_new
        v_out[...] = v_new

    spec = pl.BlockSpec((tr, C), lambda i: (i, 0))
    return pl.pallas_call(
        body, name=name, grid=(R // tr,),
        in_specs=[spec] * 4, out_specs=[spec] * 3,
        out_shape=[jax.ShapeDtypeStruct((R, C), F32)] * 3,
        compiler_params=pltpu.CompilerParams(dimension_semantics=("parallel",), vmem_limit_bytes=VMEM_LIMIT),
    )(g, w, m, v)


def _to_heads(u, n_groups):
    T = u.shape[0]
    uh = jnp.transpose(u.reshape(T, n_groups, -1, HEAD_DIM), (1, 2, 0, 3))
    return [uh[i] for i in range(n_groups)]


def _from_heads(uh):
    H, T, N = uh.shape
    return jnp.transpose(uh, (1, 0, 2)).reshape(T, H * N)


def _shift(uh):
    return jnp.pad(uh, ((0, 0), (1, 0), (0, 0)))[:, :-1]


def _pad_cols(a, width):
    return jnp.pad(a, ((0, 0), (0, width - a.shape[1])))


def _unpack_stacked(stacked, shapes):
    @jax.custom_vjp
    def run(stacked):
        return tuple(b.reshape(N_DEV * s[0], s[1]) for b, s in zip(_unpack(stacked, shapes), shapes))

    def fwd(stacked):
        return run(stacked), None

    def bwd(_, cts):
        flat = jnp.concatenate([ct.reshape(N_DEV, -1) for ct in cts], axis=1)
        flat = jnp.pad(flat, ((0, 0), (0, stacked.shape[1] * stacked.shape[2] - flat.shape[1])))
        return (flat.reshape(stacked.shape),)

    run.defvjp(fwd, bwd)
    return run(stacked)


def _pad_rows(a, height):
    return jnp.pad(a, ((0, height - a.shape[0]), (0, 0)))


def _local_loss(stacked_f32, small, x, p, target, shard_shapes):
    T, D = x.shape
    W = dict(zip(SHARDED_NAMES, _unpack_stacked(stacked_f32, [shard_shapes[n] for n in SHARDED_NAMES])))
    rw = small['w0'].shape[-1]
    fw = W['w_out'].shape[0] - rw
    heads_r, heads_f = rw // HEAD_DIM, fw // HEAD_DIM
    dl, al, gl = W['w2'].shape[1], W['a2'].shape[1], W['g2'].shape[1]
    dl_p, al_p, gl_p = _round_up(dl, LANES), _round_up(al, LANES), _round_up(gl, LANES)
    f_p = _round_up(heads_f, LANES)
    rwkv_cols = 3 * rw + dl + al + gl
    tm_wide = 128
    tm_head = 512 if T % 512 == 0 else T

    w_in = W['w_in']
    o_w, o_a, o_g = 3 * rw, 3 * rw + dl, 3 * rw + dl + al
    w_rkv = w_in[:3 * rw]
    w_lora = jnp.concatenate([_pad_rows(w_in[o_w:o_a], dl_p), _pad_rows(w_in[o_a:o_g], al_p),
                              _pad_rows(w_in[o_g:rwkv_cols], gl_p)], axis=0)
    w_qkv = w_in[rwkv_cols:rwkv_cols + 3 * fw]
    w_f = _pad_rows(w_in[rwkv_cols + 3 * fw:], f_p)
    mu = small['shift_mu'].reshape(1, -1)
    mu_lora = jnp.concatenate([_pad_cols(mu[:, o_w:o_a], dl_p), _pad_cols(mu[:, o_a:o_g], al_p),
                               _pad_cols(mu[:, o_g:rwkv_cols], gl_p)], axis=1)

    def vec(a):
        return a.reshape(1, 1, -1)

    def head_vec(a):
        return a.reshape(-1, 1, HEAD_DIM)

    (xn,) = _stage("attn_norm", _fn_rmsnorm, [x[None]], [vec(small['attn_norm_g'])], tm_wide)
    xn = xn[0]
    u_rkv = _mm_t(xn, w_rkv, "in_rkv")
    u_lora = _mm_t(xn, w_lora, "in_lora")
    u_qkv = _mm_t(xn, w_qkv, "in_qkv")
    f_raw = _mm_t(xn, w_f, "in_f")

    u_lora3 = u_lora[None]
    xw_t, xa_m, xg_s = _stage("lora_mix", _make_fn_lora_mix(dl_p, al_p), [u_lora3, _shift(u_lora3)],
                              [vec(mu_lora)], tm_wide)
    w_lin = _mm_t(xw_t[0], _pad_cols(W['w2'], dl_p), "w2")
    a_lin = _mm_t(xa_m[0], _pad_cols(W['a2'], al_p), "a2")
    gate_r = _mm_t(xg_s[0], _pad_cols(W['g2'], gl_p), "g2")
    ru, ku, vu = _to_heads(u_rkv, 3)
    (w_lin_h,), (a_lin_h,), (gate_h,) = _to_heads(w_lin, 1), _to_heads(a_lin, 1), _to_heads(gate_r, 1)
    mu_r, mu_k, mu_v = (head_vec(mu[:, i * rw:(i + 1) * rw]) for i in range(3))
    r, lw, k_mod, v, kk, b = _stage(
        "rwkv_prep", _fn_rwkv_prep,
        [ru, _shift(ru), ku, _shift(ku), vu, _shift(vu), w_lin_h, a_lin_h],
        [mu_r, mu_k, mu_v, head_vec(small['w0']), head_vec(small['a0']), head_vec(small['k_k']),
         head_vec(small['k_a'])], tm_head)
    y_scan = _rwkv_scan(r, lw, k_mod, v, kk, b)
    (y_rwkv,) = _stage("rwkv_post", _fn_rwkv_post, [y_scan, r, k_mod, v, gate_h],
                       [head_vec(small['lnx_g']), head_vec(small['lnx_b']), head_vec(small['r_k'])], tm_head)

    qu, kf, vf = _to_heads(u_qkv, 3)
    qg = jnp.broadcast_to(vec(small['q_norm_g']), (heads_f, 1, HEAD_DIM))
    kg = jnp.broadcast_to(vec(small['k_norm_g']), (heads_f, 1, HEAD_DIM))
    qn, kn = _stage("fox_prep", _fn_fox_prep, [qu, kf], [qg, kg], tm_head)
    fb = _pad_cols(small['fgate_b'].reshape(1, -1), f_p)
    (log_f,) = _stage("log_forget", _fn_log_forget, [f_raw[None]], [vec(fb)], tm_head)
    c = jnp.cumsum(log_f[0][:, :heads_f], axis=0).T
    y_fox = _fox_attn(qn, kn, vf, c)

    y_cat = jnp.concatenate([_from_heads(y_rwkv), _from_heads(y_fox)], axis=-1)
    h1 = x + _mm(y_cat, W['w_out'], "out")
    (hn,) = _stage("ffn_norm", _fn_rmsnorm, [h1[None]], [vec(small['ffn_norm_g'])], tm_wide)
    gate = _mm_t(hn[0], W['w_gate'], "gate")
    up = _mm_t(hn[0], W['w_up'], "up")
    (act,) = _stage("swiglu", _fn_swiglu, [gate[None], up[None]], [], tm_wide)
    h2 = h1 + _mm(act[0], W['w_down'], "down")
    e_raw = _mm_t(p, W['ple_proj'], "ple_proj")
    (hg,) = _stage("ple_gate_norm", _fn_rmsnorm, [h2[None]], [vec(small['ple_gate_norm_g'])], tm_wide)
    z = _mm(hg[0], W['ple_gate_w'], "ple_gate")
    (loss_rows,) = _stage("final", _fn_final, [z[None], e_raw[None], h2[None], target[None]],
                          [vec(small['ple_gate_b']), vec(small['ple_norm_g'])], tm_wide)
    return jnp.sum(loss_rows)


def kernel(x, p, attn_norm_g, w_in, shift_mu, w0, w2, a0, a2, g2, k_k, k_a, r_k, lnx_g, lnx_b, q_norm_g, k_norm_g, fgate_b, w_out, ffn_norm_g, w_gate, w_up, w_down, ple_proj, ple_norm_g, ple_gate_norm_g, ple_gate_w, ple_gate_b, loss_target, m_attn_norm_g, m_w_in, m_shift_mu, m_w0, m_w2, m_a0, m_a2, m_g2, m_k_k, m_k_a, m_r_k, m_lnx_g, m_lnx_b, m_q_norm_g, m_k_norm_g, m_fgate_b, m_w_out, m_ffn_norm_g, m_w_gate, m_w_up, m_w_down, m_ple_proj, m_ple_norm_g, m_ple_gate_norm_g, m_ple_gate_w, m_ple_gate_b, v_attn_norm_g, v_w_in, v_shift_mu, v_w0, v_w2, v_a0, v_a2, v_g2, v_k_k, v_k_a, v_r_k, v_lnx_g, v_lnx_b, v_q_norm_g, v_k_norm_g, v_fgate_b, v_w_out, v_ffn_norm_g, v_w_gate, v_w_up, v_w_down, v_ple_proj, v_ple_norm_g, v_ple_gate_norm_g, v_ple_gate_w, v_ple_gate_b):
    weights = dict(zip(WEIGHT_NAMES, (attn_norm_g, w_in, shift_mu, w0, w2, a0, a2, g2, k_k, k_a, r_k, lnx_g, lnx_b,
                                      q_norm_g, k_norm_g, fgate_b, w_out, ffn_norm_g, w_gate, w_up, w_down, ple_proj,
                                      ple_norm_g, ple_gate_norm_g, ple_gate_w, ple_gate_b)))
    m_in = dict(zip(WEIGHT_NAMES, (m_attn_norm_g, m_w_in, m_shift_mu, m_w0, m_w2, m_a0, m_a2, m_g2, m_k_k, m_k_a, m_r_k,
                                   m_lnx_g, m_lnx_b, m_q_norm_g, m_k_norm_g, m_fgate_b, m_w_out, m_ffn_norm_g, m_w_gate,
                                   m_w_up, m_w_down, m_ple_proj, m_ple_norm_g, m_ple_gate_norm_g, m_ple_gate_w,
                                   m_ple_gate_b)))
    v_in = dict(zip(WEIGHT_NAMES, (v_attn_norm_g, v_w_in, v_shift_mu, v_w0, v_w2, v_a0, v_a2, v_g2, v_k_k, v_k_a, v_r_k,
                                   v_lnx_g, v_lnx_b, v_q_norm_g, v_k_norm_g, v_fgate_b, v_w_out, v_ffn_norm_g, v_w_gate,
                                   v_w_up, v_w_down, v_ple_proj, v_ple_norm_g, v_ple_gate_norm_g, v_ple_gate_w,
                                   v_ple_gate_b)))
    travelling = {n: _travel_layout(n, weights[n][0]) for n in SHARDED_NAMES}
    shard_shapes = {n: travelling[n].shape for n in SHARDED_NAMES}
    small_shapes = [weights[n].shape for n in SMALL_NAMES]

    gathered = _all_gather(_pack([travelling[n] for n in SHARDED_NAMES], BF16), "gather_weights")
    stacked_f32 = gathered.astype(F32)

    small = {n: weights[n] for n in SMALL_NAMES}
    loss_local, (d_stacked, d_small, d_x) = jax.value_and_grad(_local_loss, argnums=(0, 1, 2))(
        stacked_f32, small, x[0], p[0, 0], loss_target[0], shard_shapes)
    loss = lax.psum(loss_local, MESH_AXES)

    parts = _all_to_all(d_stacked.astype(BF16), "scatter_grads")
    small_parts = _all_gather(_pack([d_small[n] for n in SMALL_NAMES], F32), "gather_small_grads")

    def pack_f32(tree, names):
        return _pack([tree[n] for n in names], F32)

    sml = _reduce_adamw(small_parts, pack_f32(weights, SMALL_NAMES), pack_f32(m_in, SMALL_NAMES),
                        pack_f32(v_in, SMALL_NAMES), "adamw_replicated")
    by_kind = [dict(zip(SMALL_NAMES, _unpack(sml[kind], small_shapes))) for kind in range(4)]
    g_travel = _unpack(_reduce_parts(parts, "reduce_grads"), [shard_shapes[n] for n in SHARDED_NAMES])
    for n, g_t in zip(SHARDED_NAMES, g_travel):
        g = _travel_layout(n, g_t)
        upd = _adamw(g, weights[n][0], m_in[n][0], v_in[n][0], "adamw_" + n)
        for kind, val in enumerate((g, *upd)):
            by_kind[kind][n] = val[None]
    outs = [by_kind[kind][n] for kind in range(4) for n in WEIGHT_NAMES]
    return (loss, d_x[None], *outs)
```

```python
import functools
import math

import jax
import jax.numpy as jnp
from jax import lax
from jax.experimental import pallas as pl
from jax.experimental.pallas import tpu as pltpu

F32 = jnp.float32
BF16 = jnp.bfloat16
HIGHEST = lax.Precision.HIGHEST

N_DEV = 8
MESH_AXES = ("x", "y", "c")
HEAD_DIM = 64
SCAN_CHUNK = 64
SCAN_HEADS_PER_STEP = 16
ATTN_BLOCK_Q = 128
ATTN_CAUSAL_GROUPS = 4
LANES = 128
PACK_COLS = 1024
PACK_ROW_QUANTUM = 64
VMEM_LIMIT = 48 * 1024 * 1024
RMS_EPS = 1e-6
GN_EPS = 64e-5
ADAM_LR, ADAM_B1, ADAM_B2, ADAM_EPS, ADAM_WD, ADAM_STEP = 0.001, 0.9, 0.999, 1e-08, 0.01, 10

WEIGHT_NAMES = ['attn_norm_g', 'w_in', 'shift_mu', 'w0', 'w2', 'a0', 'a2', 'g2', 'k_k', 'k_a', 'r_k', 'lnx_g', 'lnx_b',
                'q_norm_g', 'k_norm_g', 'fgate_b', 'w_out', 'ffn_norm_g', 'w_gate', 'w_up', 'w_down', 'ple_proj',
                'ple_norm_g', 'ple_gate_norm_g', 'ple_gate_w', 'ple_gate_b']
SHARDED = {'w_in': True, 'w2': True, 'a2': True, 'g2': True, 'w_out': False, 'w_gate': True, 'w_up': True,
           'w_down': False, 'ple_proj': True, 'ple_gate_w': False}
SHARDED_NAMES = [n for n in WEIGHT_NAMES if n in SHARDED]
SMALL_NAMES = [n for n in WEIGHT_NAMES if n not in SHARDED]


def _round_up(n, q):
    return -(-n // q) * q


def _tile(n, cap):
    if n <= cap:
        return n
    return next((t for t in range(cap - cap % LANES, 0, -LANES) if n % t == 0), n)


def _mm_call(a, b, mode, name, out_dtype):
    if mode == "nn":
        (I, C), (_, J) = a.shape, b.shape
    elif mode == "nt":
        (I, C), (J, _) = a.shape, b.shape
    else:
        (C, I), (_, J) = a.shape, b.shape
    if mode == "tn":
        ti, tj, tc = _tile(I, 1024), _tile(J, 1024), _tile(C, 1024)
    else:
        ti, tj, tc = _tile(I, 1024), _tile(J, 512), _tile(C, 2048)
    n_c = C // tc
    if mode == "nn":
        a_spec = pl.BlockSpec((ti, tc), lambda i, j, c: (i, c))
        b_spec = pl.BlockSpec((tc, tj), lambda i, j, c: (c, j))
        dims = (((1,), (0,)), ((), ()))
    elif mode == "nt":
        a_spec = pl.BlockSpec((ti, tc), lambda i, j, c: (i, c))
        b_spec = pl.BlockSpec((tj, tc), lambda i, j, c: (j, c))
        dims = (((1,), (1,)), ((), ()))
    else:
        a_spec = pl.BlockSpec((tc, ti), lambda i, j, c: (c, i))
        b_spec = pl.BlockSpec((tc, tj), lambda i, j, c: (c, j))
        dims = (((0,), (0,)), ((), ()))

    def product(a_ref, b_ref):
        return lax.dot_general(a_ref[...].astype(BF16), b_ref[...].astype(BF16), dims, preferred_element_type=F32)

    def body_single(a_ref, b_ref, o_ref):
        o_ref[...] = product(a_ref, b_ref).astype(o_ref.dtype)

    def body_accumulate(a_ref, b_ref, o_ref, acc):
        c = pl.program_id(2)

        @pl.when(c == 0)
        def _():
            acc[...] = jnp.zeros_like(acc)

        acc[...] += product(a_ref, b_ref)

        @pl.when(c == n_c - 1)
        def _():
            o_ref[...] = acc[...].astype(o_ref.dtype)

    return pl.pallas_call(
        body_single if n_c == 1 else body_accumulate, name=name, grid=(I // ti, J // tj, n_c),
        in_specs=[a_spec, b_spec],
        out_specs=pl.BlockSpec((ti, tj), lambda i, j, c: (i, j)),
        out_shape=jax.ShapeDtypeStruct((I, J), out_dtype),
        scratch_shapes=[] if n_c == 1 else [pltpu.VMEM((ti, tj), F32)],
        compiler_params=pltpu.CompilerParams(dimension_semantics=("parallel", "parallel", "arbitrary"),
                                             vmem_limit_bytes=VMEM_LIMIT),
    )(a, b)


def _mm(a, b, name):
    @jax.custom_vjp
    def run(a, b):
        return _mm_call(a, b, "nn", "mm_" + name, F32)

    def fwd(a, b):
        return run(a, b), (a, b)

    def bwd(res, g):
        a, b = res
        return (_mm_call(g, b, "nt", "mm_" + name + "_da", a.dtype),
                _mm_call(a, g, "tn", "mm_" + name + "_db", b.dtype))

    run.defvjp(fwd, bwd)
    return run(a, b)


def _mm_t(a, wt, name):
    @jax.custom_vjp
    def run(a, wt):
        return _mm_call(a, wt, "nt", "mmt_" + name, F32)

    def fwd(a, wt):
        return run(a, wt), (a, wt)

    def bwd(res, g):
        a, wt = res
        return (_mm_call(g, wt, "nn", "mmt_" + name + "_da", a.dtype),
                _mm_call(g, a, "tn", "mmt_" + name + "_dw", wt.dtype))

    run.defvjp(fwd, bwd)
    return run(a, wt)


def _stage_fwd_call(name, fn, rows, params, tm):
    G, T, _ = rows[0].shape
    nr, npar = len(rows), len(params)
    out_avals = jax.eval_shape(
        lambda *a: tuple(fn(*a)),
        *[jax.ShapeDtypeStruct((tm, r.shape[2]), r.dtype) for r in rows],
        *[jax.ShapeDtypeStruct((1, p.shape[2]), p.dtype) for p in params])

    def body(*refs):
        vals = [r[0] for r in refs[:nr + npar]]
        for o_ref, o in zip(refs[nr + npar:], fn(*vals)):
            o_ref[0] = o

    def row_spec(c):
        return pl.BlockSpec((1, tm, c), lambda g, t: (g, t, 0))

    def par_spec(c):
        return pl.BlockSpec((1, 1, c), lambda g, t: (g, 0, 0))

    return pl.pallas_call(
        body, name=name, grid=(G, T // tm),
        in_specs=[row_spec(r.shape[2]) for r in rows] + [par_spec(p.shape[2]) for p in params],
        out_specs=[row_spec(o.shape[1]) for o in out_avals],
        out_shape=[jax.ShapeDtypeStruct((G, T, o.shape[1]), o.dtype) for o in out_avals],
        compiler_params=pltpu.CompilerParams(dimension_semantics=("parallel", "parallel"),
                                             vmem_limit_bytes=VMEM_LIMIT),
    )(*rows, *params)


def _stage_bwd_call(name, fn, rows, params, cts, tm):
    G, T, _ = rows[0].shape
    nr, npar, nout = len(rows), len(params), len(cts)

    def body(*refs):
        vals = [r[0] for r in refs[:nr + npar]]
        ct_vals = tuple(r[0] for r in refs[nr + npar:nr + npar + nout])
        d_refs = refs[nr + npar + nout:]
        _, vjp_fn = jax.vjp(lambda *a: tuple(fn(*a)), *vals)
        grads = vjp_fn(ct_vals)
        for i in range(nr):
            d_refs[i][0] = grads[i]

        if npar:
            @pl.when(pl.program_id(1) == 0)
            def _():
                for j in range(npar):
                    d_refs[nr + j][...] = jnp.zeros_like(d_refs[nr + j])

        for j in range(npar):
            d_refs[nr + j][0] += grads[nr + j]

    def row_spec(c):
        return pl.BlockSpec((1, tm, c), lambda g, t: (g, t, 0))

    def par_spec(c):
        return pl.BlockSpec((1, 1, c), lambda g, t: (g, 0, 0))

    outs = pl.pallas_call(
        body, name=name + "_bwd", grid=(G, T // tm),
        in_specs=([row_spec(r.shape[2]) for r in rows] + [par_spec(p.shape[2]) for p in params]
                  + [row_spec(c.shape[2]) for c in cts]),
        out_specs=[row_spec(r.shape[2]) for r in rows] + [par_spec(p.shape[2]) for p in params],
        out_shape=([jax.ShapeDtypeStruct(r.shape, r.dtype) for r in rows]
                   + [jax.ShapeDtypeStruct(p.shape, p.dtype) for p in params]),
        compiler_params=pltpu.CompilerParams(dimension_semantics=("parallel", "arbitrary"),
                                             vmem_limit_bytes=VMEM_LIMIT),
    )(*rows, *params, *cts)
    return tuple(outs[:nr]), tuple(outs[nr:])


def _stage(name, fn, rows, params, tm):
    @jax.custom_vjp
    def run(rows, params):
        return tuple(_stage_fwd_call(name, fn, rows, params, tm))

    def fwd(rows, params):
        return run(rows, params), (rows, params)

    def bwd(res, cts):
        rows, params = res
        return _stage_bwd_call(name, fn, rows, params, tuple(cts), tm)

    run.defvjp(fwd, bwd)
    return run(tuple(rows), tuple(params))


def _sigmoid(x):
    return 0.5 * (jnp.tanh(0.5 * x) + 1.0)


def _softplus(x):
    return jnp.maximum(x, 0.0) + jnp.log(1.0 + jnp.exp(-jnp.abs(x)))


def _rms(x, g, eps=RMS_EPS):
    return x * lax.rsqrt(jnp.mean(x * x, axis=-1, keepdims=True) + eps) * g


def _fn_rmsnorm(x, g):
    return (_rms(x, g).astype(BF16),)


def _fn_swiglu(gate, up):
    return ((gate * _sigmoid(gate) * up).astype(BF16),)


def _make_fn_lora_mix(p1, p2):
    def fn(u, u_prev, mu):
        um = u + (u_prev - u) * mu
        return (jnp.tanh(um[:, :p1]).astype(BF16), um[:, p1:p1 + p2].astype(BF16),
                _sigmoid(um[:, p1 + p2:]).astype(BF16))
    return fn


def _fn_rwkv_prep(ru, ru_p, ku, ku_p, vu, vu_p, w_lin, a_lin, mu_r, mu_k, mu_v, w0, a0, k_k, k_a):
    r = ru + (ru_p - ru) * mu_r
    k = ku + (ku_p - ku) * mu_k
    v = vu + (vu_p - vu) * mu_v
    w_log = -_softplus(-(w0 + w_lin)) - 0.5
    lw = -jnp.exp(w_log)
    a = _sigmoid(a0 + a_lin)
    kk = k * k_k
    kk = kk / jnp.maximum(jnp.sqrt(jnp.sum(kk * kk, axis=-1, keepdims=True)), 1e-12)
    k_mod = k * (1.0 + (a - 1.0) * k_a)
    return r, lw, k_mod, v, kk, kk * a


def _fn_rwkv_post(y, r, k_mod, v, g, lnx_g, lnx_b, r_k):
    mean = jnp.mean(y, axis=-1, keepdims=True)
    yc = y - mean
    var = jnp.mean(yc * yc, axis=-1, keepdims=True)
    yn = yc * lax.rsqrt(var + GN_EPS) * lnx_g + lnx_b
    bonus = jnp.sum(r * k_mod * r_k, axis=-1, keepdims=True) * v
    return (((yn + bonus) * g).astype(BF16),)


def _fn_fox_prep(q, k, qg, kg):
    return _rms(q, qg), _rms(k, kg)


def _fn_log_forget(f_raw, b):
    x = f_raw + b
    return (jnp.minimum(x, 0.0) - jnp.log(1.0 + jnp.exp(-jnp.abs(x))),)


def _fn_final(z, e_raw, h2, target, gate_b, ple_g):
    gate = _sigmoid(z + gate_b)
    out = h2 + gate * _rms(e_raw, ple_g)
    err = out - target
    return (0.5 * jnp.mean(err * err, axis=-1, keepdims=True),)


def _dot_bf16(a, b, ca, cb):
    return lax.dot_general(a.astype(BF16), b.astype(BF16), (((ca,), (cb,)), ((0,), (0,))),
                           preferred_element_type=F32)


@functools.partial(jax.custom_vjp, nondiff_argnums=(2, 3))
def _dot(a, b, ca, cb):
    return _dot_bf16(a, b, ca, cb)


def _dot_fwd(a, b, ca, cb):
    return _dot_bf16(a, b, ca, cb), (a, b)


def _dot_bwd(ca, cb, res, g):
    a, b = res
    ia, jb = 3 - ca, 3 - cb
    da = _dot_bf16(g, b, 2, jb) if ca == 2 else _dot_bf16(b, g, jb, 2)
    db = _dot_bf16(a, g, ia, 1) if cb == 1 else _dot_bf16(g, a, 1, ia)
    return da, db


_dot.defvjp(_dot_fwd, _dot_bwd)


def _scan_chunk(S0, r, lw, k, v, kk, b):
    B, L, _ = r.shape
    row = lax.broadcasted_iota(jnp.int32, (B, L, L), 1)
    col = lax.broadcasted_iota(jnp.int32, (B, L, L), 2)
    incl = col <= row
    strict = col < row
    cum = lax.dot_general(incl.astype(F32), lw, (((2,), (1,)), ((0,), (0,))), precision=HIGHEST,
                          preferred_element_type=F32)
    g_in, g_ex, g_inv = jnp.exp(cum), jnp.exp(cum - lw), jnp.exp(-cum)
    kkg, kd, bd, rg = kk * g_ex, k * g_inv, b * g_inv, r * g_in
    a_k = jnp.where(strict, _dot(kkg, kd, 2, 2), 0.0)
    a_b = jnp.where(strict, _dot(kkg, bd, 2, 2), 0.0)
    pw = -a_b
    inv = (row == col).astype(F32) + pw
    for _ in range(int(math.log2(L)) - 1):
        pw = _dot(pw, pw, 2, 1)
        inv = inv + _dot(inv, pw, 2, 1)
    sa = -_dot(inv, _dot(kkg, S0, 2, 2) + _dot(a_k, v, 2, 1), 2, 1)
    r_k = jnp.where(incl, _dot(rg, kd, 2, 2), 0.0)
    r_b = jnp.where(incl, _dot(rg, bd, 2, 2), 0.0)
    y = _dot(rg, S0, 2, 2) + _dot(r_k, v, 2, 1) + _dot(r_b, sa, 2, 1)
    g_end = jnp.exp(jnp.sum(lw, axis=1, keepdims=True))
    S1 = S0 * g_end + _dot(v, kd * g_end, 1, 1) + _dot(sa, bd * g_end, 1, 1)
    return y, S1


def _scan_heads_per_step(H):
    return next(hb for hb in (SCAN_HEADS_PER_STEP, 2, 1) if H % hb == 0)


def _scan_fwd_call(r, lw, k, v, kk, b):
    H, T, N = r.shape
    L = SCAN_CHUNK
    n_chunks = T // L
    hb = _scan_heads_per_step(H)

    def body(r_ref, lw_ref, k_ref, v_ref, kk_ref, b_ref, y_ref, s0_ref, state):
        @pl.when(pl.program_id(1) == 0)
        def _():
            state[...] = jnp.zeros_like(state)

        S0 = state[...]
        s0_ref[:, 0] = S0
        y, S1 = _scan_chunk(S0, r_ref[...], lw_ref[...], k_ref[...], v_ref[...], kk_ref[...], b_ref[...])
        y_ref[...] = y
        state[...] = S1

    blk = pl.BlockSpec((hb, L, N), lambda h, c: (h, c, 0))
    return pl.pallas_call(
        body, name="rwkv_scan_fwd", grid=(H // hb, n_chunks),
        in_specs=[blk] * 6,
        out_specs=[blk, pl.BlockSpec((hb, 1, N, N), lambda h, c: (h, c, 0, 0))],
        out_shape=[jax.ShapeDtypeStruct((H, T, N), F32), jax.ShapeDtypeStruct((H, n_chunks, N, N), F32)],
        scratch_shapes=[pltpu.VMEM((hb, N, N), F32)],
        compiler_params=pltpu.CompilerParams(dimension_semantics=("parallel", "arbitrary")),
    )(r, lw, k, v, kk, b)


def _scan_bwd_call(r, lw, k, v, kk, b, s0s, dy):
    H, T, N = r.shape
    L = SCAN_CHUNK
    n_chunks = T // L
    hb = _scan_heads_per_step(H)

    def body(r_ref, lw_ref, k_ref, v_ref, kk_ref, b_ref, s0_ref, dy_ref, dr, dlw, dk, dv, dkk, db, d_state):
        @pl.when(pl.program_id(1) == 0)
        def _():
            d_state[...] = jnp.zeros_like(d_state)

        _, vjp_fn = jax.vjp(_scan_chunk, s0_ref[:, 0], r_ref[...], lw_ref[...], k_ref[...], v_ref[...], kk_ref[...],
                            b_ref[...])
        grads = vjp_fn((dy_ref[...], d_state[...]))
        d_state[...] = grads[0]
        for o_ref, g in zip((dr, dlw, dk, dv, dkk, db), grads[1:]):
            o_ref[...] = g

    blk = pl.BlockSpec((hb, L, N), lambda h, c: (h, n_chunks - 1 - c, 0))
    return pl.pallas_call(
        body, name="rwkv_scan_bwd", grid=(H // hb, n_chunks),
        in_specs=[blk] * 6 + [pl.BlockSpec((hb, 1, N, N), lambda h, c: (h, n_chunks - 1 - c, 0, 0)), blk],
        out_specs=[blk] * 6,
        out_shape=[jax.ShapeDtypeStruct((H, T, N), F32)] * 6,
        scratch_shapes=[pltpu.VMEM((hb, N, N), F32)],
        compiler_params=pltpu.CompilerParams(dimension_semantics=("parallel", "arbitrary")),
    )(r, lw, k, v, kk, b, s0s, dy)


@jax.custom_vjp
def _rwkv_scan(r, lw, k, v, kk, b):
    return _scan_fwd_call(r, lw, k, v, kk, b)[0]


def _rwkv_scan_fwd(r, lw, k, v, kk, b):
    y, s0s = _scan_fwd_call(r, lw, k, v, kk, b)
    return y, (r, lw, k, v, kk, b, s0s)


def _rwkv_scan_bwd(res, dy):
    return tuple(_scan_bwd_call(*res, dy))


_rwkv_scan.defvjp(_rwkv_scan_fwd, _rwkv_scan_bwd)


def _attn_block(q, k, v, c_col, c_row, q0):
    bq, t = q.shape[0], k.shape[0]
    s = lax.dot_general(q.astype(BF16), k.astype(BF16), (((1,), (1,)), ((), ())), preferred_element_type=F32)
    s = s * (HEAD_DIM ** -0.5) + (c_col - c_row)
    qi = q0 + lax.broadcasted_iota(jnp.int32, (bq, t), 0)
    ki = lax.broadcasted_iota(jnp.int32, (bq, t), 1)
    s = jnp.where(ki <= qi, s, -jnp.inf)
    m = lax.stop_gradient(jnp.max(s, axis=-1, keepdims=True))
    p = jnp.exp(s - m)
    p = p / jnp.sum(p, axis=-1, keepdims=True)
    return lax.dot_general(p.astype(BF16), v.astype(BF16), (((1,), (0,)), ((), ())), preferred_element_type=F32)


def _attn_specs(tk, N, bq):
    q_spec = pl.BlockSpec((1, bq, N), lambda h, i: (h, i, 0))
    kv_spec = pl.BlockSpec((1, tk, N), lambda h, i: (h, 0, 0))
    col_spec = pl.BlockSpec((1, bq, 1), lambda h, i: (h, i, 0))
    row_spec = pl.BlockSpec((1, 1, tk), lambda h, i: (h, 0, 0))
    return q_spec, kv_spec, col_spec, row_spec


def _attn_fwd_call(q, k, v, c_col, c_row, q_off, name):
    H, tq, N = q.shape
    bq = ATTN_BLOCK_Q
    q_spec, kv_spec, col_spec, row_spec = _attn_specs(k.shape[1], N, bq)

    def body(q_ref, k_ref, v_ref, cc_ref, cr_ref, o_ref):
        o = _attn_block(q_ref[0], k_ref[0], v_ref[0], cc_ref[0], cr_ref[0], q_off + pl.program_id(1) * bq)
        o_ref[0] = o.astype(o_ref.dtype)

    return pl.pallas_call(
        body, name=name, grid=(H, tq // bq),
        in_specs=[q_spec, kv_spec, kv_spec, col_spec, row_spec],
        out_specs=q_spec,
        out_shape=jax.ShapeDtypeStruct((H, tq, N), BF16),
        compiler_params=pltpu.CompilerParams(dimension_semantics=("parallel", "parallel"),
                                             vmem_limit_bytes=VMEM_LIMIT),
    )(q, k, v, c_col, c_row)


def _attn_bwd_call(q, k, v, c_col, c_row, do, q_off, name):
    H, tq, N = q.shape
    tk = k.shape[1]
    bq = ATTN_BLOCK_Q
    q_spec, kv_spec, col_spec, row_spec = _attn_specs(tk, N, bq)

    def body(q_ref, k_ref, v_ref, cc_ref, cr_ref, do_ref, dq_ref, dk_ref, dv_ref, dcc_ref, dcr_ref):
        i = pl.program_id(1)
        _, vjp_fn = jax.vjp(functools.partial(_attn_block, q0=q_off + i * bq), q_ref[0], k_ref[0], v_ref[0],
                            cc_ref[0], cr_ref[0])
        dq, dk, dv, dcc, dcr = vjp_fn(do_ref[0].astype(F32))
        dq_ref[0] = dq
        dcc_ref[0] = dcc

        @pl.when(i == 0)
        def _():
            dk_ref[...] = jnp.zeros_like(dk_ref)
            dv_ref[...] = jnp.zeros_like(dv_ref)
            dcr_ref[...] = jnp.zeros_like(dcr_ref)

        dk_ref[0] += dk
        dv_ref[0] += dv
        dcr_ref[0] += dcr

    return pl.pallas_call(
        body, name=name, grid=(H, tq // bq),
        in_specs=[q_spec, kv_spec, kv_spec, col_spec, row_spec, q_spec],
        out_specs=[q_spec, kv_spec, kv_spec, col_spec, row_spec],
        out_shape=[jax.ShapeDtypeStruct((H, tq, N), F32), jax.ShapeDtypeStruct((H, tk, N), F32),
                   jax.ShapeDtypeStruct((H, tk, N), F32), jax.ShapeDtypeStruct((H, tq, 1), F32),
                   jax.ShapeDtypeStruct((H, 1, tk), F32)],
        compiler_params=pltpu.CompilerParams(dimension_semantics=("parallel", "arbitrary"),
                                             vmem_limit_bytes=VMEM_LIMIT),
    )(q, k, v, c_col, c_row, do)


def _fox_attn_group(q, k, v, c_col, c_row, q_off, tag):
    @jax.custom_vjp
    def run(q, k, v, c_col, c_row):
        return _attn_fwd_call(q, k, v, c_col, c_row, q_off, "fox_attn_fwd_" + tag)

    def fwd(q, k, v, c_col, c_row):
        return run(q, k, v, c_col, c_row), (q, k, v, c_col, c_row)

    def bwd(res, do):
        return tuple(_attn_bwd_call(*res, do, q_off, "fox_attn_bwd_" + tag))

    run.defvjp(fwd, bwd)
    return run(q, k, v, c_col, c_row)


def _fox_attn(q, k, v, c):
    T = q.shape[1]
    groups = next(g for g in (ATTN_CAUSAL_GROUPS, 2, 1) if T % (g * ATTN_BLOCK_Q) == 0)
    tg = T // groups
    outs = []
    for g in range(groups):
        lo, hi = g * tg, (g + 1) * tg
        outs.append(_fox_attn_group(q[:, lo:hi], k[:, :hi], v[:, :hi], c[:, lo:hi, None], c[:, None, :hi], lo, str(g)))
    return jnp.concatenate(outs, axis=1)


N_PEERS = N_DEV - 1


def _all_gather(xs, name):
    n = len(xs)

    def body(*refs):
        x_refs, out_refs = refs[:n], refs[n:2 * n]
        send_sems, recv_sems, local_sems = refs[2 * n:]
        x_, y_, c_ = lax.axis_index("x"), lax.axis_index("y"), lax.axis_index("c")
        me, sibling = (x_, y_, c_), (x_, y_, 1 - c_)
        chips = [(1 - x_, y_), (x_, 1 - y_), (1 - x_, 1 - y_)]

        def slot(t, px, py, pc):
            return out_refs[t].at[4 * px + 2 * py + pc]

        def copy(t, k, block, to, src=None):
            return pltpu.make_async_remote_copy(
                src_ref=slot(t, *block) if src is None else src, dst_ref=slot(t, *block),
                send_sem=send_sems.at[k * n + t], recv_sem=recv_sems.at[k * n + t],
                device_id=to, device_id_type=pl.DeviceIdType.MESH)

        mine = [pltpu.make_async_copy(x_refs[t], slot(t, *me), local_sems.at[t]) for t in range(n)]
        for cp in mine:
            cp.start()
        first = [copy(t, 0, me, sibling, src=x_refs[t]) for t in range(n)]
        first += [copy(t, 1 + j, me, (*chip, c_), src=x_refs[t]) for j, chip in enumerate(chips) for t in range(n)]
        for cp in first:
            cp.start()
        passed = []
        for j, chip in enumerate(chips):
            for t in range(n):
                copy(t, 1 + j, (*chip, c_), me).wait_recv()
                passed.append(copy(t, 4 + j, (*chip, c_), sibling))
                passed[-1].start()
        for t in range(n):
            copy(t, 0, sibling, me).wait_recv()
        for j, chip in enumerate(chips):
            for t in range(n):
                copy(t, 4 + j, (*chip, 1 - c_), me).wait_recv()
        for cp in first + passed:
            cp.wait_send()
        for cp in mine:
            cp.wait()

    any_spec = pl.BlockSpec(memory_space=pl.ANY)
    return pl.pallas_call(
        body, name=name,
        out_shape=[jax.ShapeDtypeStruct((N_DEV,) + x.shape, x.dtype) for x in xs],
        in_specs=[any_spec] * n, out_specs=[any_spec] * n,
        scratch_shapes=[pltpu.SemaphoreType.DMA((N_PEERS * n,)), pltpu.SemaphoreType.DMA((N_PEERS * n,)),
                        pltpu.SemaphoreType.DMA((n,))],
    )(*xs)


def _all_to_all(parts, name):
    n = len(parts)

    def body(*refs):
        a_refs, b_refs = refs[:n], refs[n:2 * n]
        send_sems, recv_sems, local_sems = refs[2 * n:]
        x_, y_, c_ = lax.axis_index("x"), lax.axis_index("y"), lax.axis_index("c")
        me_idx = 4 * x_ + 2 * y_ + c_
        mine = [pltpu.make_async_copy(a_refs[t].at[me_idx], b_refs[t].at[me_idx], local_sems.at[t]) for t in range(n)]
        for cp in mine:
            cp.start()
        copies = []
        for rel in range(1, N_DEV):
            px = 1 - x_ if rel & 4 else x_
            py = 1 - y_ if rel & 2 else y_
            pc = 1 - c_ if rel & 1 else c_
            for t in range(n):
                copies.append(pltpu.make_async_remote_copy(
                    src_ref=a_refs[t].at[4 * px + 2 * py + pc], dst_ref=b_refs[t].at[me_idx],
                    send_sem=send_sems.at[(rel - 1) * n + t], recv_sem=recv_sems.at[(rel - 1) * n + t],
                    device_id=(px, py, pc), device_id_type=pl.DeviceIdType.MESH))
        for cp in copies:
            cp.start()
        for cp in copies:
            cp.wait_recv()
        for cp in copies:
            cp.wait_send()
        for cp in mine:
            cp.wait()

    any_spec = pl.BlockSpec(memory_space=pl.ANY)
    return pl.pallas_call(
        body, name=name,
        out_shape=[jax.ShapeDtypeStruct(a.shape, a.dtype) for a in parts],
        in_specs=[any_spec] * n, out_specs=[any_spec] * n,
        scratch_shapes=[pltpu.SemaphoreType.DMA((N_PEERS * n,)), pltpu.SemaphoreType.DMA((N_PEERS * n,)),
                        pltpu.SemaphoreType.DMA((n,))],
    )(*parts)


def _reduce_adamw(parts, w, m, v, name):
    R, C = w.shape
    tr = max(t for t in (256, 128, PACK_ROW_QUANTUM) if R % t == 0)

    def body(p_ref, w_ref, m_ref, v_ref, g_out, d_out, m_out, v_out):
        g = p_ref[0]
        for i in range(1, N_DEV):
            g = g + p_ref[i]
        m_new = ADAM_B1 * m_ref[...] + (1.0 - ADAM_B1) * g
        v_new = ADAM_B2 * v_ref[...] + (1.0 - ADAM_B2) * (g * g)
        m_hat = m_new / (1.0 - ADAM_B1 ** ADAM_STEP)
        v_hat = v_new / (1.0 - ADAM_B2 ** ADAM_STEP)
        g_out[...] = g
        d_out[...] = -ADAM_LR * (m_hat / (jnp.sqrt(v_hat) + ADAM_EPS) + ADAM_WD * w_ref[...])
        m_out[...] = m_new
        v_out[...] = v_new

    spec = pl.BlockSpec((tr, C), lambda i: (i, 0))
    return pl.pallas_call(
        body, name=name, grid=(R // tr,),
        in_specs=[pl.BlockSpec((N_DEV, tr, C), lambda i: (0, i, 0)), spec, spec, spec],
        out_specs=[spec] * 4,
        out_shape=[jax.ShapeDtypeStruct((R, C), F32)] * 4,
        compiler_params=pltpu.CompilerParams(dimension_semantics=("parallel",), vmem_limit_bytes=VMEM_LIMIT),
    )(parts, w, m, v)


def _pack(arrays, dtype):
    flat = jnp.concatenate([a.reshape(-1).astype(dtype) for a in arrays])
    rows = _round_up(-(-flat.shape[0] // PACK_COLS), PACK_ROW_QUANTUM)
    flat = jnp.pad(flat, (0, rows * PACK_COLS - flat.shape[0]))
    return flat.reshape(rows, PACK_COLS)


def _unpack(packed, shapes):
    lead = packed.shape[:-2]
    flat = packed.reshape(lead + (-1,))
    out, off = [], 0
    for s in shapes:
        n = math.prod(s)
        out.append(flat[..., off:off + n].reshape(lead + tuple(s)))
        off += n
    return out


def _travel_layout(name, block):
    return block.T if SHARDED[name] else block


REDUCE_BLOCK_BYTES = 4 * 1024 * 1024


def _reduce_parts(parts, name):
    _, R, C = parts.shape
    per_col = N_DEV * R * parts.dtype.itemsize
    tc = next((t for t in range(C - C % LANES, 0, -LANES) if C % t == 0 and t * per_col <= REDUCE_BLOCK_BYTES), C)

    def body(p_ref, o_ref):
        g = p_ref[0].astype(F32)
        for i in range(1, N_DEV):
            g = g + p_ref[i].astype(F32)
        o_ref[...] = g

    return pl.pallas_call(
        body, name=name, grid=(C // tc,),
        in_specs=[pl.BlockSpec((N_DEV, R, tc), lambda j: (0, 0, j))],
        out_specs=pl.BlockSpec((R, tc), lambda j: (0, j)),
        out_shape=jax.ShapeDtypeStruct((R, C), F32),
        compiler_params=pltpu.CompilerParams(dimension_semantics=("parallel",), vmem_limit_bytes=VMEM_LIMIT),
    )(parts)


def _adamw(g, w, m, v, name):
    R, C = w.shape
    tr = next((t for t in (512, 256, 128, 64, 32, 16, 8) if R % t == 0 and t * C * 4 <= 2 * 1024 * 1024), R)

    def body(g_ref, w_ref, m_ref, v_ref, d_out, m_out, v_out):
        g_ = g_ref[...]
        m_new = ADAM_B1 * m_ref[...] + (1.0 - ADAM_B1) * g_
        v_new = ADAM_B2 * v_ref[...] + (1.0 - ADAM_B2) * (g_ * g_)
        m_hat = m_new / (1.0 - ADAM_B1 ** ADAM_STEP)
        v_hat = v_new / (1.0 - ADAM_B2 ** ADAM_STEP)
        d_out[...] = -ADAM_LR * (m_hat / (jnp.sqrt(v_hat) + ADAM_EPS) + ADAM_WD * w_ref[...])
        m_out[...] = m_new
        v_out[...] = v_new

    spec = pl.BlockSpec((tr, C), lambda i: (i, 0))
    return pl.pallas_call(
        body, name=name, grid=(R // tr,),
        in_specs=[spec] * 4, out_specs=[spec] * 3,
        out_shape=[jax.ShapeDtypeStruct((R, C), F32)] * 3,
        compiler_params=pltpu.CompilerParams(dimension_semantics=("parallel",), vmem_limit_bytes=VMEM_LIMIT),
    )(g, w, m, v)


def _to_heads(u, n_groups):
    T = u.shape[0]
    uh = jnp.transpose(u.reshape(T, n_groups, -1, HEAD_DIM), (1, 2, 0, 3))
    return [uh[i] for i in range(n_groups)]


def _from_heads(uh):
    H, T, N = uh.shape
    return jnp.transpose(uh, (1, 0, 2)).reshape(T, H * N)


def _shift(uh):
    return jnp.pad(uh, ((0, 0), (1, 0), (0, 0)))[:, :-1]


def _pad_cols(a, width):
    return jnp.pad(a, ((0, 0), (0, width - a.shape[1])))


def _split_rows(w, sizes):
    offsets = [sum(sizes[:i]) for i in range(len(sizes))]

    @jax.custom_vjp
    def run(w):
        return tuple(w[o:o + s] for o, s in zip(offsets, sizes))

    def fwd(w):
        return run(w), None

    def bwd(_, cts):
        return (jnp.concatenate(cts, axis=0),)

    run.defvjp(fwd, bwd)
    return run(w)


def _pad_rows(a, height):
    return jnp.pad(a, ((0, height - a.shape[0]), (0, 0)))


def _local_loss(W, small, x, p, target):
    T, D = x.shape
    rw = small['w0'].shape[-1]
    fw = W['w_out'].shape[0] - rw
    heads_r, heads_f = rw // HEAD_DIM, fw // HEAD_DIM
    dl, al, gl = W['w2'].shape[1], W['a2'].shape[1], W['g2'].shape[1]
    dl_p, al_p, gl_p = _round_up(dl, LANES), _round_up(al, LANES), _round_up(gl, LANES)
    f_p = _round_up(heads_f, LANES)
    rwkv_cols = 3 * rw + dl + al + gl
    tm_wide = 128
    tm_head = 512 if T % 512 == 0 else T

    o_w, o_a, o_g = 3 * rw, 3 * rw + dl, 3 * rw + dl + al
    w_rkv, w_xw, w_xa, w_xg, w_qkv, w_fg = _split_rows(W['w_in'], (3 * rw, dl, al, gl, 3 * fw, heads_f))
    w_lora = jnp.concatenate([_pad_rows(w_xw, dl_p), _pad_rows(w_xa, al_p), _pad_rows(w_xg, gl_p)], axis=0)
    w_f = _pad_rows(w_fg, f_p)
    mu = small['shift_mu'].reshape(1, -1)
    mu_lora = jnp.concatenate([_pad_cols(mu[:, o_w:o_a], dl_p), _pad_cols(mu[:, o_a:o_g], al_p),
                               _pad_cols(mu[:, o_g:rwkv_cols], gl_p)], axis=1)

    def vec(a):
        return a.reshape(1, 1, -1)

    def head_vec(a):
        return a.reshape(-1, 1, HEAD_DIM)

    (xn,) = _stage("attn_norm", _fn_rmsnorm, [x[None]], [vec(small['attn_norm_g'])], tm_wide)
    xn = xn[0]
    u_rkv = _mm_t(xn, w_rkv, "in_rkv")
    u_lora = _mm_t(xn, w_lora, "in_lora")
    u_qkv = _mm_t(xn, w_qkv, "in_qkv")
    f_raw = _mm_t(xn, w_f, "in_f")

    u_lora3 = u_lora[None]
    xw_t, xa_m, xg_s = _stage("lora_mix", _make_fn_lora_mix(dl_p, al_p), [u_lora3, _shift(u_lora3)],
                              [vec(mu_lora)], tm_wide)
    w_lin = _mm_t(xw_t[0], _pad_cols(W['w2'], dl_p), "w2")
    a_lin = _mm_t(xa_m[0], _pad_cols(W['a2'], al_p), "a2")
    gate_r = _mm_t(xg_s[0], _pad_cols(W['g2'], gl_p), "g2")
    ru, ku, vu = _to_heads(u_rkv, 3)
    (w_lin_h,), (a_lin_h,), (gate_h,) = _to_heads(w_lin, 1), _to_heads(a_lin, 1), _to_heads(gate_r, 1)
    mu_r, mu_k, mu_v = (head_vec(mu[:, i * rw:(i + 1) * rw]) for i in range(3))
    r, lw, k_mod, v, kk, b = _stage(
        "rwkv_prep", _fn_rwkv_prep,
        [ru, _shift(ru), ku, _shift(ku), vu, _shift(vu), w_lin_h, a_lin_h],
        [mu_r, mu_k, mu_v, head_vec(small['w0']), head_vec(small['a0']), head_vec(small['k_k']),
         head_vec(small['k_a'])], tm_head)
    y_scan = _rwkv_scan(r, lw, k_mod, v, kk, b)
    (y_rwkv,) = _stage("rwkv_post", _fn_rwkv_post, [y_scan, r, k_mod, v, gate_h],
                       [head_vec(small['lnx_g']), head_vec(small['lnx_b']), head_vec(small['r_k'])], tm_head)

    qu, kf, vf = _to_heads(u_qkv, 3)
    qg = jnp.broadcast_to(vec(small['q_norm_g']), (heads_f, 1, HEAD_DIM))
    kg = jnp.broadcast_to(vec(small['k_norm_g']), (heads_f, 1, HEAD_DIM))
    qn, kn = _stage("fox_prep", _fn_fox_prep, [qu, kf], [qg, kg], tm_head)
    fb = _pad_cols(small['fgate_b'].reshape(1, -1), f_p)
    (log_f,) = _stage("log_forget", _fn_log_forget, [f_raw[None]], [vec(fb)], tm_head)
    c = jnp.cumsum(log_f[0][:, :heads_f], axis=0).T
    y_fox = _fox_attn(qn, kn, vf, c)

    y_cat = jnp.concatenate([_from_heads(y_rwkv), _from_heads(y_fox)], axis=-1)
    h1 = x + _mm(y_cat, W['w_out'], "out")
    (hn,) = _stage("ffn_norm", _fn_rmsnorm, [h1[None]], [vec(small['ffn_norm_g'])], tm_wide)
    gate = _mm_t(hn[0], W['w_gate'], "gate")
    up = _mm_t(hn[0], W['w_up'], "up")
    (act,) = _stage("swiglu", _fn_swiglu, [gate[None], up[None]], [], tm_wide)
    h2 = h1 + _mm(act[0], W['w_down'], "down")
    e_raw = _mm_t(p, W['ple_proj'], "ple_proj")
    (hg,) = _stage("ple_gate_norm", _fn_rmsnorm, [h2[None]], [vec(small['ple_gate_norm_g'])], tm_wide)
    z = _mm(hg[0], W['ple_gate_w'], "ple_gate")
    (loss_rows,) = _stage("final", _fn_final, [z[None], e_raw[None], h2[None], target[None]],
                          [vec(small['ple_gate_b']), vec(small['ple_norm_g'])], tm_wide)
    return jnp.sum(loss_rows)


def kernel(x, p, attn_norm_g, w_in, shift_mu, w0, w2, a0, a2, g2, k_k, k_a, r_k, lnx_g, lnx_b, q_norm_g, k_norm_g, fgate_b, w_out, ffn_norm_g, w_gate, w_up, w_down, ple_proj, ple_norm_g, ple_gate_norm_g, ple_gate_w, ple_gate_b, loss_target, m_attn_norm_g, m_w_in, m_shift_mu, m_w0, m_w2, m_a0, m_a2, m_g2, m_k_k, m_k_a, m_r_k, m_lnx_g, m_lnx_b, m_q_norm_g, m_k_norm_g, m_fgate_b, m_w_out, m_ffn_norm_g, m_w_gate, m_w_up, m_w_down, m_ple_proj, m_ple_norm_g, m_ple_gate_norm_g, m_ple_gate_w, m_ple_gate_b, v_attn_norm_g, v_w_in, v_shift_mu, v_w0, v_w2, v_a0, v_a2, v_g2, v_k_k, v_k_a, v_r_k, v_lnx_g, v_lnx_b, v_q_norm_g, v_k_norm_g, v_fgate_b, v_w_out, v_ffn_norm_g, v_w_gate, v_w_up, v_w_down, v_ple_proj, v_ple_norm_g, v_ple_gate_norm_g, v_ple_gate_w, v_ple_gate_b):
    weights = dict(zip(WEIGHT_NAMES, (attn_norm_g, w_in, shift_mu, w0, w2, a0, a2, g2, k_k, k_a, r_k, lnx_g, lnx_b,
                                      q_norm_g, k_norm_g, fgate_b, w_out, ffn_norm_g, w_gate, w_up, w_down, ple_proj,
                                      ple_norm_g, ple_gate_norm_g, ple_gate_w, ple_gate_b)))
    m_in = dict(zip(WEIGHT_NAMES, (m_attn_norm_g, m_w_in, m_shift_mu, m_w0, m_w2, m_a0, m_a2, m_g2, m_k_k, m_k_a, m_r_k,
                                   m_lnx_g, m_lnx_b, m_q_norm_g, m_k_norm_g, m_fgate_b, m_w_out, m_ffn_norm_g, m_w_gate,
                                   m_w_up, m_w_down, m_ple_proj, m_ple_norm_g, m_ple_gate_norm_g, m_ple_gate_w,
                                   m_ple_gate_b)))
    v_in = dict(zip(WEIGHT_NAMES, (v_attn_norm_g, v_w_in, v_shift_mu, v_w0, v_w2, v_a0, v_a2, v_g2, v_k_k, v_k_a, v_r_k,
                                   v_lnx_g, v_lnx_b, v_q_norm_g, v_k_norm_g, v_fgate_b, v_w_out, v_ffn_norm_g, v_w_gate,
                                   v_w_up, v_w_down, v_ple_proj, v_ple_norm_g, v_ple_gate_norm_g, v_ple_gate_w,
                                   v_ple_gate_b)))
    small_shapes = [weights[n].shape for n in SMALL_NAMES]

    travelling = [_travel_layout(n, weights[n][0]).astype(BF16) for n in SHARDED_NAMES]
    gathered = _all_gather(travelling, "gather_weights")
    W = {n: g.reshape(N_DEV * g.shape[1], g.shape[2]) for n, g in zip(SHARDED_NAMES, gathered)}

    small = {n: weights[n] for n in SMALL_NAMES}
    loss_local, (d_w, d_small, d_x) = jax.value_and_grad(_local_loss, argnums=(0, 1, 2))(
        W, small, x[0], p[0, 0], loss_target[0])
    loss = lax.psum(loss_local, MESH_AXES)

    parts = _all_to_all([d_w[n].reshape(g.shape) for n, g in zip(SHARDED_NAMES, gathered)], "scatter_grads")
    (small_parts,) = _all_gather([_pack([d_small[n] for n in SMALL_NAMES], F32)], "gather_small_grads")

    def pack_f32(tree, names):
        return _pack([tree[n] for n in names], F32)

    sml = _reduce_adamw(small_parts, pack_f32(weights, SMALL_NAMES), pack_f32(m_in, SMALL_NAMES),
                        pack_f32(v_in, SMALL_NAMES), "adamw_replicated")
    by_kind = [dict(zip(SMALL_NAMES, _unpack(sml[kind], small_shapes))) for kind in range(4)]
    for n, part in zip(SHARDED_NAMES, parts):
        g = _travel_layout(n, _reduce_parts(part, "reduce_" + n))
        upd = _adamw(g, weights[n][0], m_in[n][0], v_in[n][0], "adamw_" + n)
        for kind, val in enumerate((g, *upd)):
            by_kind[kind][n] = val[None]
    outs = [by_kind[kind][n] for kind in range(4) for n in WEIGHT_NAMES]
    return (loss, d_x[None], *outs)
```

```python
import functools
import math

import jax
import jax.numpy as jnp
from jax import lax
from jax.experimental import pallas as pl
from jax.experimental.pallas import tpu as pltpu

F32 = jnp.float32
BF16 = jnp.bfloat16
HIGHEST = lax.Precision.HIGHEST

N_DEV = 8
MESH_AXES = ("x", "y", "c")
HEAD_DIM = 64
SCAN_CHUNK = 64
SCAN_HEADS_PER_STEP = 16
ATTN_BLOCK_Q = 512
ATTN_BLOCK_K = 512
LANES = 128
PACK_COLS = 1024
PACK_ROW_QUANTUM = 64
VMEM_LIMIT = 48 * 1024 * 1024
RMS_EPS = 1e-6
GN_EPS = 64e-5
ADAM_LR, ADAM_B1, ADAM_B2, ADAM_EPS, ADAM_WD, ADAM_STEP = 0.001, 0.9, 0.999, 1e-08, 0.01, 10

WEIGHT_NAMES = ['attn_norm_g', 'w_in', 'shift_mu', 'w0', 'w2', 'a0', 'a2', 'g2', 'k_k', 'k_a', 'r_k', 'lnx_g', 'lnx_b',
                'q_norm_g', 'k_norm_g', 'fgate_b', 'w_out', 'ffn_norm_g', 'w_gate', 'w_up', 'w_down', 'ple_proj',
                'ple_norm_g', 'ple_gate_norm_g', 'ple_gate_w', 'ple_gate_b']
SHARDED = {'w_in': True, 'w2': True, 'a2': True, 'g2': True, 'w_out': False, 'w_gate': True, 'w_up': True,
           'w_down': False, 'ple_proj': True, 'ple_gate_w': False}
SHARDED_NAMES = [n for n in WEIGHT_NAMES if n in SHARDED]
SMALL_NAMES = [n for n in WEIGHT_NAMES if n not in SHARDED]


def _round_up(n, q):
    return -(-n // q) * q


def _tile(n, cap):
    if n <= cap:
        return n
    return next((t for t in range(cap - cap % LANES, 0, -LANES) if n % t == 0), n)


def _mm_call(a, b, mode, name, out_dtype):
    if mode == "nn":
        (I, C), (_, J) = a.shape, b.shape
    elif mode == "nt":
        (I, C), (J, _) = a.shape, b.shape
    else:
        (C, I), (_, J) = a.shape, b.shape
    if mode == "tn":
        ti, tj, tc = _tile(I, 1024), _tile(J, 1024), _tile(C, 1024)
    else:
        ti, tj, tc = _tile(I, 1024), _tile(J, 512), _tile(C, 2048)
    n_c = C // tc
    if mode == "nn":
        a_spec = pl.BlockSpec((ti, tc), lambda i, j, c: (i, c))
        b_spec = pl.BlockSpec((tc, tj), lambda i, j, c: (c, j))
        dims = (((1,), (0,)), ((), ()))
    elif mode == "nt":
        a_spec = pl.BlockSpec((ti, tc), lambda i, j, c: (i, c))
        b_spec = pl.BlockSpec((tj, tc), lambda i, j, c: (j, c))
        dims = (((1,), (1,)), ((), ()))
    else:
        a_spec = pl.BlockSpec((tc, ti), lambda i, j, c: (c, i))
        b_spec = pl.BlockSpec((tc, tj), lambda i, j, c: (c, j))
        dims = (((0,), (0,)), ((), ()))

    def product(a_ref, b_ref):
        return lax.dot_general(a_ref[...].astype(BF16), b_ref[...].astype(BF16), dims, preferred_element_type=F32)

    def body_single(a_ref, b_ref, o_ref):
        o_ref[...] = product(a_ref, b_ref).astype(o_ref.dtype)

    def body_accumulate(a_ref, b_ref, o_ref, acc):
        c = pl.program_id(2)

        @pl.when(c == 0)
        def _():
            acc[...] = jnp.zeros_like(acc)

        acc[...] += product(a_ref, b_ref)

        @pl.when(c == n_c - 1)
        def _():
            o_ref[...] = acc[...].astype(o_ref.dtype)

    return pl.pallas_call(
        body_single if n_c == 1 else body_accumulate, name=name, grid=(I // ti, J // tj, n_c),
        in_specs=[a_spec, b_spec],
        out_specs=pl.BlockSpec((ti, tj), lambda i, j, c: (i, j)),
        out_shape=jax.ShapeDtypeStruct((I, J), out_dtype),
        scratch_shapes=[] if n_c == 1 else [pltpu.VMEM((ti, tj), F32)],
        compiler_params=pltpu.CompilerParams(dimension_semantics=("parallel", "parallel", "arbitrary"),
                                             vmem_limit_bytes=VMEM_LIMIT),
    )(a, b)


def _mm(a, b, name):
    @jax.custom_vjp
    def run(a, b):
        return _mm_call(a, b, "nn", "mm_" + name, F32)

    def fwd(a, b):
        return run(a, b), (a, b)

    def bwd(res, g):
        a, b = res
        return (_mm_call(g, b, "nt", "mm_" + name + "_da", a.dtype),
                _mm_call(a, g, "tn", "mm_" + name + "_db", b.dtype))

    run.defvjp(fwd, bwd)
    return run(a, b)


def _mm_t(a, wt, name):
    @jax.custom_vjp
    def run(a, wt):
        return _mm_call(a, wt, "nt", "mmt_" + name, F32)

    def fwd(a, wt):
        return run(a, wt), (a, wt)

    def bwd(res, g):
        a, wt = res
        return (_mm_call(g, wt, "nn", "mmt_" + name + "_da", a.dtype),
                _mm_call(g, a, "tn", "mmt_" + name + "_dw", wt.dtype))

    run.defvjp(fwd, bwd)
    return run(a, wt)


def _stage_fwd_call(name, fn, rows, params, tm):
    G, T, _ = rows[0].shape
    nr, npar = len(rows), len(params)
    out_avals = jax.eval_shape(
        lambda *a: tuple(fn(*a)),
        *[jax.ShapeDtypeStruct((tm, r.shape[2]), r.dtype) for r in rows],
        *[jax.ShapeDtypeStruct((1, p.shape[2]), p.dtype) for p in params])

    def body(*refs):
        vals = [r[0] for r in refs[:nr + npar]]
        for o_ref, o in zip(refs[nr + npar:], fn(*vals)):
            o_ref[0] = o

    def row_spec(c):
        return pl.BlockSpec((1, tm, c), lambda g, t: (g, t, 0))

    def par_spec(c):
        return pl.BlockSpec((1, 1, c), lambda g, t: (g, 0, 0))

    return pl.pallas_call(
        body, name=name, grid=(G, T // tm),
        in_specs=[row_spec(r.shape[2]) for r in rows] + [par_spec(p.shape[2]) for p in params],
        out_specs=[row_spec(o.shape[1]) for o in out_avals],
        out_shape=[jax.ShapeDtypeStruct((G, T, o.shape[1]), o.dtype) for o in out_avals],
        compiler_params=pltpu.CompilerParams(dimension_semantics=("parallel", "parallel"),
                                             vmem_limit_bytes=VMEM_LIMIT),
    )(*rows, *params)


def _stage_bwd_call(name, fn, rows, params, cts, tm):
    G, T, _ = rows[0].shape
    nr, npar, nout = len(rows), len(params), len(cts)

    def body(*refs):
        vals = [r[0] for r in refs[:nr + npar]]
        ct_vals = tuple(r[0] for r in refs[nr + npar:nr + npar + nout])
        d_refs = refs[nr + npar + nout:]
        _, vjp_fn = jax.vjp(lambda *a: tuple(fn(*a)), *vals)
        grads = vjp_fn(ct_vals)
        for i in range(nr):
            d_refs[i][0] = grads[i]

        if npar:
            @pl.when(pl.program_id(1) == 0)
            def _():
                for j in range(npar):
                    d_refs[nr + j][...] = jnp.zeros_like(d_refs[nr + j])

        for j in range(npar):
            d_refs[nr + j][0] += grads[nr + j]

    def row_spec(c):
        return pl.BlockSpec((1, tm, c), lambda g, t: (g, t, 0))

    def par_spec(c):
        return pl.BlockSpec((1, 1, c), lambda g, t: (g, 0, 0))

    outs = pl.pallas_call(
        body, name=name + "_bwd", grid=(G, T // tm),
        in_specs=([row_spec(r.shape[2]) for r in rows] + [par_spec(p.shape[2]) for p in params]
                  + [row_spec(c.shape[2]) for c in cts]),
        out_specs=[row_spec(r.shape[2]) for r in rows] + [par_spec(p.shape[2]) for p in params],
        out_shape=([jax.ShapeDtypeStruct(r.shape, r.dtype) for r in rows]
                   + [jax.ShapeDtypeStruct(p.shape, p.dtype) for p in params]),
        compiler_params=pltpu.CompilerParams(dimension_semantics=("parallel", "arbitrary"),
                                             vmem_limit_bytes=VMEM_LIMIT),
    )(*rows, *params, *cts)
    return tuple(outs[:nr]), tuple(outs[nr:])


def _stage(name, fn, rows, params, tm):
    @jax.custom_vjp
    def run(rows, params):
        return tuple(_stage_fwd_call(name, fn, rows, params, tm))

    def fwd(rows, params):
        return run(rows, params), (rows, params)

    def bwd(res, cts):
        rows, params = res
        return _stage_bwd_call(name, fn, rows, params, tuple(cts), tm)

    run.defvjp(fwd, bwd)
    return run(tuple(rows), tuple(params))


def _sigmoid(x):
    return 0.5 * (jnp.tanh(0.5 * x) + 1.0)


def _softplus(x):
    return jnp.maximum(x, 0.0) + jnp.log(1.0 + jnp.exp(-jnp.abs(x)))


def _rms(x, g, eps=RMS_EPS):
    return x * lax.rsqrt(jnp.mean(x * x, axis=-1, keepdims=True) + eps) * g


def _fn_rmsnorm(x, g):
    return (_rms(x, g).astype(BF16),)


def _fn_swiglu(gate, up):
    return ((gate * _sigmoid(gate) * up).astype(BF16),)


def _make_fn_lora_mix(p1, p2):
    def fn(u, u_prev, mu):
        um = u + (u_prev - u) * mu
        return (jnp.tanh(um[:, :p1]).astype(BF16), um[:, p1:p1 + p2].astype(BF16),
                _sigmoid(um[:, p1 + p2:]).astype(BF16))
    return fn


def _fn_rwkv_prep(ru, ru_p, ku, ku_p, vu, vu_p, w_lin, a_lin, mu_r, mu_k, mu_v, w0, a0, k_k, k_a):
    r = ru + (ru_p - ru) * mu_r
    k = ku + (ku_p - ku) * mu_k
    v = vu + (vu_p - vu) * mu_v
    w_log = -_softplus(-(w0 + w_lin)) - 0.5
    lw = -jnp.exp(w_log)
    a = _sigmoid(a0 + a_lin)
    kk = k * k_k
    kk = kk / jnp.maximum(jnp.sqrt(jnp.sum(kk * kk, axis=-1, keepdims=True)), 1e-12)
    k_mod = k * (1.0 + (a - 1.0) * k_a)
    return r, lw, k_mod, v, kk, kk * a


def _fn_rwkv_post(y, r, k_mod, v, g, lnx_g, lnx_b, r_k):
    mean = jnp.mean(y, axis=-1, keepdims=True)
    yc = y - mean
    var = jnp.mean(yc * yc, axis=-1, keepdims=True)
    yn = yc * lax.rsqrt(var + GN_EPS) * lnx_g + lnx_b
    bonus = jnp.sum(r * k_mod * r_k, axis=-1, keepdims=True) * v
    return (((yn + bonus) * g).astype(BF16),)


def _fn_fox_prep(q, k, qg, kg):
    return _rms(q, qg), _rms(k, kg)


def _fn_log_forget(f_raw, b):
    x = f_raw + b
    return (jnp.minimum(x, 0.0) - jnp.log(1.0 + jnp.exp(-jnp.abs(x))),)


def _fn_final(z, e_raw, h2, target, gate_b, ple_g):
    gate = _sigmoid(z + gate_b)
    out = h2 + gate * _rms(e_raw, ple_g)
    err = out - target
    return (0.5 * jnp.mean(err * err, axis=-1, keepdims=True),)


def _dot_bf16(a, b, ca, cb):
    return lax.dot_general(a.astype(BF16), b.astype(BF16), (((ca,), (cb,)), ((0,), (0,))),
                           preferred_element_type=F32)


@functools.partial(jax.custom_vjp, nondiff_argnums=(2, 3))
def _dot(a, b, ca, cb):
    return _dot_bf16(a, b, ca, cb)


def _dot_fwd(a, b, ca, cb):
    return _dot_bf16(a, b, ca, cb), (a, b)


def _dot_bwd(ca, cb, res, g):
    a, b = res
    ia, jb = 3 - ca, 3 - cb
    da = _dot_bf16(g, b, 2, jb) if ca == 2 else _dot_bf16(b, g, jb, 2)
    db = _dot_bf16(a, g, ia, 1) if cb == 1 else _dot_bf16(g, a, 1, ia)
    return da, db


_dot.defvjp(_dot_fwd, _dot_bwd)


def _scan_chunk(S0, r, lw, k, v, kk, b):
    B, L, _ = r.shape
    row = lax.broadcasted_iota(jnp.int32, (B, L, L), 1)
    col = lax.broadcasted_iota(jnp.int32, (B, L, L), 2)
    incl = col <= row
    strict = col < row
    cum = lax.dot_general(incl.astype(F32), lw, (((2,), (1,)), ((0,), (0,))), precision=HIGHEST,
                          preferred_element_type=F32)
    g_in, g_ex, g_inv = jnp.exp(cum), jnp.exp(cum - lw), jnp.exp(-cum)
    kkg, kd, bd, rg = kk * g_ex, k * g_inv, b * g_inv, r * g_in
    a_k = jnp.where(strict, _dot(kkg, kd, 2, 2), 0.0)
    a_b = jnp.where(strict, _dot(kkg, bd, 2, 2), 0.0)
    pw = -a_b
    inv = (row == col).astype(F32) + pw
    for _ in range(int(math.log2(L)) - 1):
        pw = _dot(pw, pw, 2, 1)
        inv = inv + _dot(inv, pw, 2, 1)
    sa = -_dot(inv, _dot(kkg, S0, 2, 2) + _dot(a_k, v, 2, 1), 2, 1)
    r_k = jnp.where(incl, _dot(rg, kd, 2, 2), 0.0)
    r_b = jnp.where(incl, _dot(rg, bd, 2, 2), 0.0)
    y = _dot(rg, S0, 2, 2) + _dot(r_k, v, 2, 1) + _dot(r_b, sa, 2, 1)
    g_end = jnp.exp(jnp.sum(lw, axis=1, keepdims=True))
    S1 = S0 * g_end + _dot(v, kd * g_end, 1, 1) + _dot(sa, bd * g_end, 1, 1)
    return y, S1


def _scan_heads_per_step(H):
    return next(hb for hb in (SCAN_HEADS_PER_STEP, 2, 1) if H % hb == 0)


def _scan_fwd_call(r, lw, k, v, kk, b):
    H, T, N = r.shape
    L = SCAN_CHUNK
    n_chunks = T // L
    hb = _scan_heads_per_step(H)

    def body(r_ref, lw_ref, k_ref, v_ref, kk_ref, b_ref, y_ref, s0_ref, state):
        @pl.when(pl.program_id(1) == 0)
        def _():
            state[...] = jnp.zeros_like(state)

        S0 = state[...]
        s0_ref[:, 0] = S0
        y, S1 = _scan_chunk(S0, r_ref[...], lw_ref[...], k_ref[...], v_ref[...], kk_ref[...], b_ref[...])
        y_ref[...] = y
        state[...] = S1

    blk = pl.BlockSpec((hb, L, N), lambda h, c: (h, c, 0))
    return pl.pallas_call(
        body, name="rwkv_scan_fwd", grid=(H // hb, n_chunks),
        in_specs=[blk] * 6,
        out_specs=[blk, pl.BlockSpec((hb, 1, N, N), lambda h, c: (h, c, 0, 0))],
        out_shape=[jax.ShapeDtypeStruct((H, T, N), F32), jax.ShapeDtypeStruct((H, n_chunks, N, N), F32)],
        scratch_shapes=[pltpu.VMEM((hb, N, N), F32)],
        compiler_params=pltpu.CompilerParams(dimension_semantics=("parallel", "arbitrary")),
    )(r, lw, k, v, kk, b)


def _scan_bwd_call(r, lw, k, v, kk, b, s0s, dy):
    H, T, N = r.shape
    L = SCAN_CHUNK
    n_chunks = T // L
    hb = _scan_heads_per_step(H)

    def body(r_ref, lw_ref, k_ref, v_ref, kk_ref, b_ref, s0_ref, dy_ref, dr, dlw, dk, dv, dkk, db, d_state):
        @pl.when(pl.program_id(1) == 0)
        def _():
            d_state[...] = jnp.zeros_like(d_state)

        _, vjp_fn = jax.vjp(_scan_chunk, s0_ref[:, 0], r_ref[...], lw_ref[...], k_ref[...], v_ref[...], kk_ref[...],
                            b_ref[...])
        grads = vjp_fn((dy_ref[...], d_state[...]))
        d_state[...] = grads[0]
        for o_ref, g in zip((dr, dlw, dk, dv, dkk, db), grads[1:]):
            o_ref[...] = g

    blk = pl.BlockSpec((hb, L, N), lambda h, c: (h, n_chunks - 1 - c, 0))
    return pl.pallas_call(
        body, name="rwkv_scan_bwd", grid=(H // hb, n_chunks),
        in_specs=[blk] * 6 + [pl.BlockSpec((hb, 1, N, N), lambda h, c: (h, n_chunks - 1 - c, 0, 0)), blk],
        out_specs=[blk] * 6,
        out_shape=[jax.ShapeDtypeStruct((H, T, N), F32)] * 6,
        scratch_shapes=[pltpu.VMEM((hb, N, N), F32)],
        compiler_params=pltpu.CompilerParams(dimension_semantics=("parallel", "arbitrary")),
    )(r, lw, k, v, kk, b, s0s, dy)


@jax.custom_vjp
def _rwkv_scan(r, lw, k, v, kk, b):
    return _scan_fwd_call(r, lw, k, v, kk, b)[0]


def _rwkv_scan_fwd(r, lw, k, v, kk, b):
    y, s0s = _scan_fwd_call(r, lw, k, v, kk, b)
    return y, (r, lw, k, v, kk, b, s0s)


def _rwkv_scan_bwd(res, dy):
    return tuple(_scan_bwd_call(*res, dy))


_rwkv_scan.defvjp(_rwkv_scan_fwd, _rwkv_scan_bwd)


def _nt(a, b):
    return lax.dot_general(a.astype(BF16), b.astype(BF16), (((1,), (1,)), ((), ())), preferred_element_type=F32)


def _nn(a, b):
    return lax.dot_general(a.astype(BF16), b.astype(BF16), (((1,), (0,)), ((), ())), preferred_element_type=F32)


def _tn(a, b):
    return lax.dot_general(a.astype(BF16), b.astype(BF16), (((0,), (0,)), ((), ())), preferred_element_type=F32)


def _attn_scores(qs, k_ref, cr_ref, row_bias, j, kb, q0, masked):
    ks = pl.multiple_of(j * kb, kb)
    kj = k_ref[0, pl.ds(ks, kb), :]
    s = _nt(qs, kj) + row_bias - cr_ref[0, j]
    if masked:
        qi = q0 + lax.broadcasted_iota(jnp.int32, s.shape, 0)
        ki = ks + lax.broadcasted_iota(jnp.int32, s.shape, 1)
        s = jnp.where(ki <= qi, s, -jnp.inf)
    return s, kj, ks


def _attn_specs(T, N, bq, kb):
    q_spec = pl.BlockSpec((1, bq, N), lambda h, i: (h, i, 0))
    kv_spec = pl.BlockSpec((1, T, N), lambda h, i: (h, 0, 0))
    col_spec = pl.BlockSpec((1, bq, 1), lambda h, i: (h, i, 0))
    row_spec = pl.BlockSpec((1, T // kb, 1, kb), lambda h, i: (h, 0, 0, 0))
    return q_spec, kv_spec, col_spec, row_spec


def _attn_fwd_call(q, k, v, c_col, c_rows):
    H, T, N = q.shape
    bq, kb = min(ATTN_BLOCK_Q, T), c_rows.shape[3]
    q_spec, kv_spec, col_spec, row_spec = _attn_specs(T, N, bq, kb)

    def body(q_ref, k_ref, v_ref, cc_ref, cr_ref, o_ref, o32_ref, lse_ref):
        q0 = pl.program_id(1) * bq
        qs = (q_ref[0] * (HEAD_DIM ** -0.5)).astype(BF16)
        cc = cc_ref[0]

        def step(j, carry, masked):
            m, l, acc = carry
            s, _, ks = _attn_scores(qs, k_ref, cr_ref, cc, j, kb, q0, masked)
            m_new = jnp.maximum(m, jnp.max(s, axis=-1, keepdims=True))
            alpha = jnp.exp(m - m_new)
            p = jnp.exp(s - m_new)
            l = alpha * l + jnp.sum(p, axis=-1, keepdims=True)
            p_hi = p.astype(BF16)
            p_lo = p - p_hi.astype(F32)
            vj = v_ref[0, pl.ds(ks, kb), :]
            acc = alpha * acc + (_nn(p_hi, vj) + _nn(p_lo, vj))
            return m_new, l, acc

        n_full = q0 // kb
        init = (jnp.full((bq, 1), -jnp.inf, F32), jnp.zeros((bq, 1), F32), jnp.zeros((bq, N), F32))
        carry = lax.fori_loop(0, n_full, functools.partial(step, masked=False), init)
        m, l, acc = step(n_full, carry, masked=True)
        o = acc / l
        o_ref[0] = o.astype(o_ref.dtype)
        o32_ref[0] = o
        lse_ref[0] = m + jnp.log(l)

    return pl.pallas_call(
        body, name="fox_attn_fwd", grid=(H, T // bq),
        in_specs=[q_spec, kv_spec, kv_spec, col_spec, row_spec],
        out_specs=[q_spec, q_spec, col_spec],
        out_shape=[jax.ShapeDtypeStruct((H, T, N), BF16),
                   jax.ShapeDtypeStruct((H, T, N), F32),
                   jax.ShapeDtypeStruct((H, T, 1), F32)],
        compiler_params=pltpu.CompilerParams(dimension_semantics=("parallel", "parallel"),
                                             vmem_limit_bytes=VMEM_LIMIT),
    )(q, k, v, c_col, c_rows)


def _attn_bwd_call(q, k, v, c_col, c_rows, o, lse, do):
    H, T, N = q.shape
    bq, kb = min(ATTN_BLOCK_Q, T), c_rows.shape[3]
    q_spec, kv_spec, col_spec, row_spec = _attn_specs(T, N, bq, kb)

    def body(q_ref, k_ref, v_ref, cc_ref, cr_ref, o_ref, lse_ref, do_ref, dq_ref, dk_ref, dv_ref, dcr_ref):
        i = pl.program_id(1)
        q0 = i * bq

        @pl.when(i == 0)
        def _():
            dk_ref[...] = jnp.zeros_like(dk_ref)
            dv_ref[...] = jnp.zeros_like(dv_ref)
            dcr_ref[...] = jnp.zeros_like(dcr_ref)

        qs = (q_ref[0] * (HEAD_DIM ** -0.5)).astype(BF16)
        do = do_ref[0]
        delta = jnp.sum(do.astype(F32) * o_ref[0], axis=-1, keepdims=True)
        row_bias = cc_ref[0] - lse_ref[0]

        def step(j, dq, masked):
            s, kj, ks = _attn_scores(qs, k_ref, cr_ref, row_bias, j, kb, q0, masked)
            p = jnp.exp(s)
            ds = p * (_nt(do, v_ref[0, pl.ds(ks, kb), :]) - delta)
            ds_b = ds.astype(BF16)
            dk_ref[0, pl.ds(ks, kb), :] += _tn(ds_b, qs)
            dv_ref[0, pl.ds(ks, kb), :] += _tn(p, do)
            dcr_ref[0, j] -= jnp.sum(ds, axis=0, keepdims=True)
            return dq + _nn(ds_b, kj)

        n_full = q0 // kb
        dq = lax.fori_loop(0, n_full, functools.partial(step, masked=False), jnp.zeros((bq, N), F32))
        dq_ref[0] = step(n_full, dq, masked=True) * (HEAD_DIM ** -0.5)

    return pl.pallas_call(
        body, name="fox_attn_bwd", grid=(H, T // bq),
        in_specs=[q_spec, kv_spec, kv_spec, col_spec, row_spec, q_spec, col_spec, q_spec],
        out_specs=[q_spec, kv_spec, kv_spec, row_spec],
        out_shape=[jax.ShapeDtypeStruct((H, T, N), F32)] * 3 + [jax.ShapeDtypeStruct(c_rows.shape, F32)],
        compiler_params=pltpu.CompilerParams(dimension_semantics=("parallel", "arbitrary"),
                                             vmem_limit_bytes=VMEM_LIMIT),
    )(q, k, v, c_col, c_rows, o, lse, do)


@jax.custom_vjp
def _fox_attn(q, k, v, c):
    return _fox_attn_fwd(q, k, v, c)[0]


def _attn_bias_views(c):
    H, T = c.shape
    kb = min(ATTN_BLOCK_K, T)
    return c[:, :, None], c.reshape(H, T // kb, 1, kb)


def _fox_attn_fwd(q, k, v, c):
    o, o32, lse = _attn_fwd_call(q, k, v, *_attn_bias_views(c))
    return o, (q, k, v, c, o32, lse)


def _fox_attn_bwd(res, do):
    q, k, v, c, o32, lse = res
    dq, dk, dv, dc_rows = _attn_bwd_call(q, k, v, *_attn_bias_views(c), o32, lse, do)
    return dq, dk, dv, dc_rows.reshape(c.shape)


_fox_attn.defvjp(_fox_attn_fwd, _fox_attn_bwd)


N_PEERS = N_DEV - 1


def _all_gather(xs, name):
    n = len(xs)

    def body(*refs):
        x_refs, out_refs = refs[:n], refs[n:2 * n]
        send_sems, recv_sems, local_sems = refs[2 * n:]
        x_, y_, c_ = lax.axis_index("x"), lax.axis_index("y"), lax.axis_index("c")
        me, sibling = (x_, y_, c_), (x_, y_, 1 - c_)
        chips = [(1 - x_, y_), (x_, 1 - y_), (1 - x_, 1 - y_)]

        def slot(t, px, py, pc):
            return out_refs[t].at[4 * px + 2 * py + pc]

        def copy(t, k, block, to, src=None):
            return pltpu.make_async_remote_copy(
                src_ref=slot(t, *block) if src is None else src, dst_ref=slot(t, *block),
                send_sem=send_sems.at[k * n + t], recv_sem=recv_sems.at[k * n + t],
                device_id=to, device_id_type=pl.DeviceIdType.MESH)

        mine = [pltpu.make_async_copy(x_refs[t], slot(t, *me), local_sems.at[t]) for t in range(n)]
        for cp in mine:
            cp.start()
        first = [copy(t, 0, me, sibling, src=x_refs[t]) for t in range(n)]
        first += [copy(t, 1 + j, me, (*chip, c_), src=x_refs[t]) for j, chip in enumerate(chips) for t in range(n)]
        for cp in first:
            cp.start()
        passed = []
        for j, chip in enumerate(chips):
            for t in range(n):
                copy(t, 1 + j, (*chip, c_), me).wait_recv()
                passed.append(copy(t, 4 + j, (*chip, c_), sibling))
                passed[-1].start()
        for t in range(n):
            copy(t, 0, sibling, me).wait_recv()
        for j, chip in enumerate(chips):
            for t in range(n):
                copy(t, 4 + j, (*chip, 1 - c_), me).wait_recv()
        for cp in first + passed:
            cp.wait_send()
        for cp in mine:
            cp.wait()

    any_spec = pl.BlockSpec(memory_space=pl.ANY)
    return pl.pallas_call(
        body, name=name,
        out_shape=[jax.ShapeDtypeStruct((N_DEV,) + x.shape, x.dtype) for x in xs],
        in_specs=[any_spec] * n, out_specs=[any_spec] * n,
        scratch_shapes=[pltpu.SemaphoreType.DMA((N_PEERS * n,)), pltpu.SemaphoreType.DMA((N_PEERS * n,)),
                        pltpu.SemaphoreType.DMA((n,))],
    )(*xs)


def _all_to_all(parts, name):
    n = len(parts)

    def body(*refs):
        a_refs, b_refs = refs[:n], refs[n:2 * n]
        send_sems, recv_sems, local_sems = refs[2 * n:]
        x_, y_, c_ = lax.axis_index("x"), lax.axis_index("y"), lax.axis_index("c")
        me_idx = 4 * x_ + 2 * y_ + c_
        mine = [pltpu.make_async_copy(a_refs[t].at[me_idx], b_refs[t].at[me_idx], local_sems.at[t]) for t in range(n)]
        for cp in mine:
            cp.start()
        copies = []
        for rel in range(1, N_DEV):
            px = 1 - x_ if rel & 4 else x_
            py = 1 - y_ if rel & 2 else y_
            pc = 1 - c_ if rel & 1 else c_
            for t in range(n):
                copies.append(pltpu.make_async_remote_copy(
                    src_ref=a_refs[t].at[4 * px + 2 * py + pc], dst_ref=b_refs[t].at[me_idx],
                    send_sem=send_sems.at[(rel - 1) * n + t], recv_sem=recv_sems.at[(rel - 1) * n + t],
                    device_id=(px, py, pc), device_id_type=pl.DeviceIdType.MESH))
        for cp in copies:
            cp.start()
        for cp in copies:
            cp.wait_recv()
        for cp in copies:
            cp.wait_send()
        for cp in mine:
            cp.wait()

    any_spec = pl.BlockSpec(memory_space=pl.ANY)
    return pl.pallas_call(
        body, name=name,
        out_shape=[jax.ShapeDtypeStruct(a.shape, a.dtype) for a in parts],
        in_specs=[any_spec] * n, out_specs=[any_spec] * n,
        scratch_shapes=[pltpu.SemaphoreType.DMA((N_PEERS * n,)), pltpu.SemaphoreType.DMA((N_PEERS * n,)),
                        pltpu.SemaphoreType.DMA((n,))],
    )(*parts)


def _reduce_adamw(parts, w, m, v, name):
    R, C = w.shape
    tr = max(t for t in (256, 128, PACK_ROW_QUANTUM) if R % t == 0)

    def body(p_ref, w_ref, m_ref, v_ref, g_out, d_out, m_out, v_out):
        g = p_ref[0]
        for i in range(1, N_DEV):
            g = g + p_ref[i]
        m_new = ADAM_B1 * m_ref[...] + (1.0 - ADAM_B1) * g
        v_new = ADAM_B2 * v_ref[...] + (1.0 - ADAM_B2) * (g * g)
        m_hat = m_new / (1.0 - ADAM_B1 ** ADAM_STEP)
        v_hat = v_new / (1.0 - ADAM_B2 ** ADAM_STEP)
        g_out[...] = g
        d_out[...] = -ADAM_LR * (m_hat / (jnp.sqrt(v_hat) + ADAM_EPS) + ADAM_WD * w_ref[...])
        m_out[...] = m_new
        v_out[...] = v_new

    spec = pl.BlockSpec((tr, C), lambda i: (i, 0))
    return pl.pallas_call(
        body, name=name, grid=(R // tr,),
        in_specs=[pl.BlockSpec((N_DEV, tr, C), lambda i: (0, i, 0)), spec, spec, spec],
        out_specs=[spec] * 4,
        out_shape=[jax.ShapeDtypeStruct((R, C), F32)] * 4,
        compiler_params=pltpu.CompilerParams(dimension_semantics=("parallel",), vmem_limit_bytes=VMEM_LIMIT),
    )(parts, w, m, v)


def _pack(arrays, dtype):
    flat = jnp.concatenate([a.reshape(-1).astype(dtype) for a in arrays])
    rows = _round_up(-(-flat.shape[0] // PACK_COLS), PACK_ROW_QUANTUM)
    flat = jnp.pad(flat, (0, rows * PACK_COLS - flat.shape[0]))
    return flat.reshape(rows, PACK_COLS)


def _unpack(packed, shapes):
    lead = packed.shape[:-2]
    flat = packed.reshape(lead + (-1,))
    out, off = [], 0
    for s in shapes:
        n = math.prod(s)
        out.append(flat[..., off:off + n].reshape(lead + tuple(s)))
        off += n
    return out


def _travel_layout(name, block):
    return block.T if SHARDED[name] else block


REDUCE_BLOCK_BYTES = 4 * 1024 * 1024


def _reduce_parts(parts, name):
    _, R, C = parts.shape
    per_col = N_DEV * R * parts.dtype.itemsize
    tc = next((t for t in range(C - C % LANES, 0, -LANES) if C % t == 0 and t * per_col <= REDUCE_BLOCK_BYTES), C)

    def body(p_ref, o_ref):
        g = p_ref[0].astype(F32)
        for i in range(1, N_DEV):
            g = g + p_ref[i].astype(F32)
        o_ref[...] = g

    return pl.pallas_call(
        body, name=name, grid=(C // tc,),
        in_specs=[pl.BlockSpec((N_DEV, R, tc), lambda j: (0, 0, j))],
        out_specs=pl.BlockSpec((R, tc), lambda j: (0, j)),
        out_shape=jax.ShapeDtypeStruct((R, C), F32),
        compiler_params=pltpu.CompilerParams(dimension_semantics=("parallel",), vmem_limit_bytes=VMEM_LIMIT),
    )(parts)


def _adamw(g, w, m, v, name):
    R, C = w.shape
    tr = next((t for t in (512, 256, 128, 64, 32, 16, 8) if R % t == 0 and t * C * 4 <= 2 * 1024 * 1024), R)

    def body(g_ref, w_ref, m_ref, v_ref, d_out, m_out, v_out):
        g_ = g_ref[...]
        m_new = ADAM_B1 * m_ref[...] + (1.0 - ADAM_B1) * g_
        v_new = ADAM_B2 * v_ref[...] + (1.0 - ADAM_B2) * (g_ * g_)
        m_hat = m_new / (1.0 - ADAM_B1 ** ADAM_STEP)
        v_hat = v_new / (1.0 - ADAM_B2 ** ADAM_STEP)
        d_out[...] = -ADAM_LR * (m_hat / (jnp.sqrt(v_hat) + ADAM_EPS) + ADAM_WD * w_ref[...])
        m_out[...] = m_new
        v_out[...] = v_new

    spec = pl.BlockSpec((tr, C), lambda i: (i, 0))
    return pl.pallas_call(
        body, name=name, grid=(R // tr,),
        in_specs=[spec] * 4, out_specs=[spec] * 3,
        out_shape=[jax.ShapeDtypeStruct((R, C), F32)] * 3,
        compiler_params=pltpu.CompilerParams(dimension_semantics=("parallel",), vmem_limit_bytes=VMEM_LIMIT),
    )(g, w, m, v)


def _to_heads(u, n_groups):
    T = u.shape[0]
    uh = jnp.transpose(u.reshape(T, n_groups, -1, HEAD_DIM), (1, 2, 0, 3))
    return [uh[i] for i in range(n_groups)]


def _from_heads(uh):
    H, T, N = uh.shape
    return jnp.transpose(uh, (1, 0, 2)).reshape(T, H * N)


def _shift(uh):
    return jnp.pad(uh, ((0, 0), (1, 0), (0, 0)))[:, :-1]


def _pad_cols(a, width):
    return jnp.pad(a, ((0, 0), (0, width - a.shape[1])))


def _split_rows(w, sizes):
    offsets = [sum(sizes[:i]) for i in range(len(sizes))]

    @jax.custom_vjp
    def run(w):
        return tuple(w[o:o + s] for o, s in zip(offsets, sizes))

    def fwd(w):
        return run(w), None

    def bwd(_, cts):
        return (jnp.concatenate(cts, axis=0),)

    run.defvjp(fwd, bwd)
    return run(w)


def _pad_rows(a, height):
    return jnp.pad(a, ((0, height - a.shape[0]), (0, 0)))


def _local_loss(W, small, x, p, target):
    T, D = x.shape
    rw = small['w0'].shape[-1]
    fw = W['w_out'].shape[0] - rw
    heads_r, heads_f = rw // HEAD_DIM, fw // HEAD_DIM
    dl, al, gl = W['w2'].shape[1], W['a2'].shape[1], W['g2'].shape[1]
    dl_p, al_p, gl_p = _round_up(dl, LANES), _round_up(al, LANES), _round_up(gl, LANES)
    f_p = _round_up(heads_f, LANES)
    rwkv_cols = 3 * rw + dl + al + gl
    tm_wide = 128
    tm_head = 512 if T % 512 == 0 else T

    o_w, o_a, o_g = 3 * rw, 3 * rw + dl, 3 * rw + dl + al
    w_rkv, w_xw, w_xa, w_xg, w_qkv, w_fg = _split_rows(W['w_in'], (3 * rw, dl, al, gl, 3 * fw, heads_f))
    w_lora = jnp.concatenate([_pad_rows(w_xw, dl_p), _pad_rows(w_xa, al_p), _pad_rows(w_xg, gl_p)], axis=0)
    w_f = _pad_rows(w_fg, f_p)
    mu = small['shift_mu'].reshape(1, -1)
    mu_lora = jnp.concatenate([_pad_cols(mu[:, o_w:o_a], dl_p), _pad_cols(mu[:, o_a:o_g], al_p),
                               _pad_cols(mu[:, o_g:rwkv_cols], gl_p)], axis=1)

    def vec(a):
        return a.reshape(1, 1, -1)

    def head_vec(a):
        return a.reshape(-1, 1, HEAD_DIM)

    (xn,) = _stage("attn_norm", _fn_rmsnorm, [x[None]], [vec(small['attn_norm_g'])], tm_wide)
    xn = xn[0]
    u_rkv = _mm_t(xn, w_rkv, "in_rkv")
    u_lora = _mm_t(xn, w_lora, "in_lora")
    u_qkv = _mm_t(xn, w_qkv, "in_qkv")
    f_raw = _mm_t(xn, w_f, "in_f")

    u_lora3 = u_lora[None]
    xw_t, xa_m, xg_s = _stage("lora_mix", _make_fn_lora_mix(dl_p, al_p), [u_lora3, _shift(u_lora3)],
                              [vec(mu_lora)], tm_wide)
    w_lin = _mm_t(xw_t[0], _pad_cols(W['w2'], dl_p), "w2")
    a_lin = _mm_t(xa_m[0], _pad_cols(W['a2'], al_p), "a2")
    gate_r = _mm_t(xg_s[0], _pad_cols(W['g2'], gl_p), "g2")
    ru, ku, vu = _to_heads(u_rkv, 3)
    (w_lin_h,), (a_lin_h,), (gate_h,) = _to_heads(w_lin, 1), _to_heads(a_lin, 1), _to_heads(gate_r, 1)
    mu_r, mu_k, mu_v = (head_vec(mu[:, i * rw:(i + 1) * rw]) for i in range(3))
    r, lw, k_mod, v, kk, b = _stage(
        "rwkv_prep", _fn_rwkv_prep,
        [ru, _shift(ru), ku, _shift(ku), vu, _shift(vu), w_lin_h, a_lin_h],
        [mu_r, mu_k, mu_v, head_vec(small['w0']), head_vec(small['a0']), head_vec(small['k_k']),
         head_vec(small['k_a'])], tm_head)
    y_scan = _rwkv_scan(r, lw, k_mod, v, kk, b)
    (y_rwkv,) = _stage("rwkv_post", _fn_rwkv_post, [y_scan, r, k_mod, v, gate_h],
                       [head_vec(small['lnx_g']), head_vec(small['lnx_b']), head_vec(small['r_k'])], tm_head)

    qu, kf, vf = _to_heads(u_qkv, 3)
    qg = jnp.broadcast_to(vec(small['q_norm_g']), (heads_f, 1, HEAD_DIM))
    kg = jnp.broadcast_to(vec(small['k_norm_g']), (heads_f, 1, HEAD_DIM))
    qn, kn = _stage("fox_prep", _fn_fox_prep, [qu, kf], [qg, kg], tm_head)
    fb = _pad_cols(small['fgate_b'].reshape(1, -1), f_p)
    (log_f,) = _stage("log_forget", _fn_log_forget, [f_raw[None]], [vec(fb)], tm_head)
    c = jnp.cumsum(log_f[0][:, :heads_f], axis=0).T
    y_fox = _fox_attn(qn, kn, vf, c)

    y_cat = jnp.concatenate([_from_heads(y_rwkv), _from_heads(y_fox)], axis=-1)
    h1 = x + _mm(y_cat, W['w_out'], "out")
    (hn,) = _stage("ffn_norm", _fn_rmsnorm, [h1[None]], [vec(small['ffn_norm_g'])], tm_wide)
    gate = _mm_t(hn[0], W['w_gate'], "gate")
    up = _mm_t(hn[0], W['w_up'], "up")
    (act,) = _stage("swiglu", _fn_swiglu, [gate[None], up[None]], [], tm_wide)
    h2 = h1 + _mm(act[0], W['w_down'], "down")
    e_raw = _mm_t(p, W['ple_proj'], "ple_proj")
    (hg,) = _stage("ple_gate_norm", _fn_rmsnorm, [h2[None]], [vec(small['ple_gate_norm_g'])], tm_wide)
    z = _mm(hg[0], W['ple_gate_w'], "ple_gate")
    (loss_rows,) = _stage("final", _fn_final, [z[None], e_raw[None], h2[None], target[None]],
                          [vec(small['ple_gate_b']), vec(small['ple_norm_g'])], tm_wide)
    return jnp.sum(loss_rows)


def kernel(x, p, attn_norm_g, w_in, shift_mu, w0, w2, a0, a2, g2, k_k, k_a, r_k, lnx_g, lnx_b, q_norm_g, k_norm_g, fgate_b, w_out, ffn_norm_g, w_gate, w_up, w_down, ple_proj, ple_norm_g, ple_gate_norm_g, ple_gate_w, ple_gate_b, loss_target, m_attn_norm_g, m_w_in, m_shift_mu, m_w0, m_w2, m_a0, m_a2, m_g2, m_k_k, m_k_a, m_r_k, m_lnx_g, m_lnx_b, m_q_norm_g, m_k_norm_g, m_fgate_b, m_w_out, m_ffn_norm_g, m_w_gate, m_w_up, m_w_down, m_ple_proj, m_ple_norm_g, m_ple_gate_norm_g, m_ple_gate_w, m_ple_gate_b, v_attn_norm_g, v_w_in, v_shift_mu, v_w0, v_w2, v_a0, v_a2, v_g2, v_k_k, v_k_a, v_r_k, v_lnx_g, v_lnx_b, v_q_norm_g, v_k_norm_g, v_fgate_b, v_w_out, v_ffn_norm_g, v_w_gate, v_w_up, v_w_down, v_ple_proj, v_ple_norm_g, v_ple_gate_norm_g, v_ple_gate_w, v_ple_gate_b):
    weights = dict(zip(WEIGHT_NAMES, (attn_norm_g, w_in, shift_mu, w0, w2, a0, a2, g2, k_k, k_a, r_k, lnx_g, lnx_b,
                                      q_norm_g, k_norm_g, fgate_b, w_out, ffn_norm_g, w_gate, w_up, w_down, ple_proj,
                                      ple_norm_g, ple_gate_norm_g, ple_gate_w, ple_gate_b)))
    m_in = dict(zip(WEIGHT_NAMES, (m_attn_norm_g, m_w_in, m_shift_mu, m_w0, m_w2, m_a0, m_a2, m_g2, m_k_k, m_k_a, m_r_k,
                                   m_lnx_g, m_lnx_b, m_q_norm_g, m_k_norm_g, m_fgate_b, m_w_out, m_ffn_norm_g, m_w_gate,
                                   m_w_up, m_w_down, m_ple_proj, m_ple_norm_g, m_ple_gate_norm_g, m_ple_gate_w,
                                   m_ple_gate_b)))
    v_in = dict(zip(WEIGHT_NAMES, (v_attn_norm_g, v_w_in, v_shift_mu, v_w0, v_w2, v_a0, v_a2, v_g2, v_k_k, v_k_a, v_r_k,
                                   v_lnx_g, v_lnx_b, v_q_norm_g, v_k_norm_g, v_fgate_b, v_w_out, v_ffn_norm_g, v_w_gate,
                                   v_w_up, v_w_down, v_ple_proj, v_ple_norm_g, v_ple_gate_norm_g, v_ple_gate_w,
                                   v_ple_gate_b)))
    small_shapes = [weights[n].shape for n in SMALL_NAMES]

    travelling = [_travel_layout(n, weights[n][0]).astype(BF16) for n in SHARDED_NAMES]
    gathered = _all_gather(travelling, "gather_weights")
    W = {n: g.reshape(N_DEV * g.shape[1], g.shape[2]) for n, g in zip(SHARDED_NAMES, gathered)}

    small = {n: weights[n] for n in SMALL_NAMES}
    loss_local, (d_w, d_small, d_x) = jax.value_and_grad(_local_loss, argnums=(0, 1, 2))(
        W, small, x[0], p[0, 0], loss_target[0])
    loss = lax.psum(loss_local, MESH_AXES)

    parts = _all_to_all([d_w[n].reshape(g.shape) for n, g in zip(SHARDED_NAMES, gathered)], "scatter_grads")
    (small_parts,) = _all_gather([_pack([d_small[n] for n in SMALL_NAMES], F32)], "gather_small_grads")

    def pack_f32(tree, names):
        return _pack([tree[n] for n in names], F32)

    sml = _reduce_adamw(small_parts, pack_f32(weights, SMALL_NAMES), pack_f32(m_in, SMALL_NAMES),
                        pack_f32(v_in, SMALL_NAMES), "adamw_replicated")
    by_kind = [dict(zip(SMALL_NAMES, _unpack(sml[kind], small_shapes))) for kind in range(4)]
    for n, part in zip(SHARDED_NAMES, parts):
        g = _travel_layout(n, _reduce_parts(part, "reduce_" + n))
        upd = _adamw(g, weights[n][0], m_in[n][0], v_in[n][0], "adamw_" + n)
        for kind, val in enumerate((g, *upd)):
            by_kind[kind][n] = val[None]
    outs = [by_kind[kind][n] for kind in range(4) for n in WEIGHT_NAMES]
    return (loss, d_x[None], *outs)
```

```python
import functools
import math

import jax
import jax.numpy as jnp
from jax import lax
from jax.experimental import pallas as pl
from jax.experimental.pallas import tpu as pltpu

F32 = jnp.float32
BF16 = jnp.bfloat16
HIGHEST = lax.Precision.HIGHEST

N_DEV = 8
MESH_AXES = ("x", "y", "c")
HEAD_DIM = 64
SCAN_CHUNK = 64
SCAN_HEADS_PER_STEP = 16
ATTN_BLOCK_Q = 512
ATTN_BLOCK_K = 512
LANES = 128
PACK_COLS = 1024
PACK_ROW_QUANTUM = 64
VMEM_LIMIT = 48 * 1024 * 1024
RMS_EPS = 1e-6
GN_EPS = 64e-5
ADAM_LR, ADAM_B1, ADAM_B2, ADAM_EPS, ADAM_WD, ADAM_STEP = 0.001, 0.9, 0.999, 1e-08, 0.01, 10

WEIGHT_NAMES = ['attn_norm_g', 'w_in', 'shift_mu', 'w0', 'w2', 'a0', 'a2', 'g2', 'k_k', 'k_a', 'r_k', 'lnx_g', 'lnx_b',
                'q_norm_g', 'k_norm_g', 'fgate_b', 'w_out', 'ffn_norm_g', 'w_gate', 'w_up', 'w_down', 'ple_proj',
                'ple_norm_g', 'ple_gate_norm_g', 'ple_gate_w', 'ple_gate_b']
SHARDED = {'w_in': True, 'w2': True, 'a2': True, 'g2': True, 'w_out': False, 'w_gate': True, 'w_up': True,
           'w_down': False, 'ple_proj': True, 'ple_gate_w': False}
SHARDED_NAMES = [n for n in WEIGHT_NAMES if n in SHARDED]
SMALL_NAMES = [n for n in WEIGHT_NAMES if n not in SHARDED]
EARLY_NAMES = ['w_in', 'w2', 'a2', 'g2', 'w_out']
LATE_NAMES = [n for n in SHARDED_NAMES if n not in EARLY_NAMES]


def _round_up(n, q):
    return -(-n // q) * q


def _tile(n, cap):
    if n <= cap:
        return n
    return next((t for t in range(cap - cap % LANES, 0, -LANES) if n % t == 0), n)


def _mm_call(a, b, mode, name, out_dtype):
    if mode == "nn":
        (I, C), (_, J) = a.shape, b.shape
    elif mode == "nt":
        (I, C), (J, _) = a.shape, b.shape
    else:
        (C, I), (_, J) = a.shape, b.shape
    if mode == "tn":
        ti, tj, tc = _tile(I, 1024), _tile(J, 1024), _tile(C, 1024)
    else:
        ti, tj, tc = _tile(I, 1024), _tile(J, 512), _tile(C, 2048)
    n_c = C // tc
    if mode == "nn":
        a_spec = pl.BlockSpec((ti, tc), lambda i, j, c: (i, c))
        b_spec = pl.BlockSpec((tc, tj), lambda i, j, c: (c, j))
        dims = (((1,), (0,)), ((), ()))
    elif mode == "nt":
        a_spec = pl.BlockSpec((ti, tc), lambda i, j, c: (i, c))
        b_spec = pl.BlockSpec((tj, tc), lambda i, j, c: (j, c))
        dims = (((1,), (1,)), ((), ()))
    else:
        a_spec = pl.BlockSpec((tc, ti), lambda i, j, c: (c, i))
        b_spec = pl.BlockSpec((tc, tj), lambda i, j, c: (c, j))
        dims = (((0,), (0,)), ((), ()))

    def product(a_ref, b_ref):
        return lax.dot_general(a_ref[...].astype(BF16), b_ref[...].astype(BF16), dims, preferred_element_type=F32)

    def body_single(a_ref, b_ref, o_ref):
        o_ref[...] = product(a_ref, b_ref).astype(o_ref.dtype)

    def body_accumulate(a_ref, b_ref, o_ref, acc):
        c = pl.program_id(2)

        @pl.when(c == 0)
        def _():
            acc[...] = jnp.zeros_like(acc)

        acc[...] += product(a_ref, b_ref)

        @pl.when(c == n_c - 1)
        def _():
            o_ref[...] = acc[...].astype(o_ref.dtype)

    return pl.pallas_call(
        body_single if n_c == 1 else body_accumulate, name=name, grid=(I // ti, J // tj, n_c),
        in_specs=[a_spec, b_spec],
        out_specs=pl.BlockSpec((ti, tj), lambda i, j, c: (i, j)),
        out_shape=jax.ShapeDtypeStruct((I, J), out_dtype),
        scratch_shapes=[] if n_c == 1 else [pltpu.VMEM((ti, tj), F32)],
        compiler_params=pltpu.CompilerParams(dimension_semantics=("parallel", "parallel", "arbitrary"),
                                             vmem_limit_bytes=VMEM_LIMIT),
    )(a, b)


def _mm(a, b, name):
    @jax.custom_vjp
    def run(a, b):
        return _mm_call(a, b, "nn", "mm_" + name, F32)

    def fwd(a, b):
        return run(a, b), (a, b)

    def bwd(res, g):
        a, b = res
        return (_mm_call(g, b, "nt", "mm_" + name + "_da", a.dtype),
                _mm_call(a, g, "tn", "mm_" + name + "_db", b.dtype))

    run.defvjp(fwd, bwd)
    return run(a, b)


def _mm_t(a, wt, name):
    @jax.custom_vjp
    def run(a, wt):
        return _mm_call(a, wt, "nt", "mmt_" + name, F32)

    def fwd(a, wt):
        return run(a, wt), (a, wt)

    def bwd(res, g):
        a, wt = res
        return (_mm_call(g, wt, "nn", "mmt_" + name + "_da", a.dtype),
                _mm_call(g, a, "tn", "mmt_" + name + "_dw", wt.dtype))

    run.defvjp(fwd, bwd)
    return run(a, wt)


def _stage_fwd_call(name, fn, rows, params, tm):
    G, T, _ = rows[0].shape
    nr, npar = len(rows), len(params)
    out_avals = jax.eval_shape(
        lambda *a: tuple(fn(*a)),
        *[jax.ShapeDtypeStruct((tm, r.shape[2]), r.dtype) for r in rows],
        *[jax.ShapeDtypeStruct((1, p.shape[2]), p.dtype) for p in params])

    def body(*refs):
        vals = [r[0] for r in refs[:nr + npar]]
        for o_ref, o in zip(refs[nr + npar:], fn(*vals)):
            o_ref[0] = o

    def row_spec(c):
        return pl.BlockSpec((1, tm, c), lambda g, t: (g, t, 0))

    def par_spec(c):
        return pl.BlockSpec((1, 1, c), lambda g, t: (g, 0, 0))

    return pl.pallas_call(
        body, name=name, grid=(G, T // tm),
        in_specs=[row_spec(r.shape[2]) for r in rows] + [par_spec(p.shape[2]) for p in params],
        out_specs=[row_spec(o.shape[1]) for o in out_avals],
        out_shape=[jax.ShapeDtypeStruct((G, T, o.shape[1]), o.dtype) for o in out_avals],
        compiler_params=pltpu.CompilerParams(dimension_semantics=("parallel", "parallel"),
                                             vmem_limit_bytes=VMEM_LIMIT),
    )(*rows, *params)


def _stage_bwd_call(name, fn, rows, params, cts, tm):
    G, T, _ = rows[0].shape
    nr, npar, nout = len(rows), len(params), len(cts)

    def body(*refs):
        vals = [r[0] for r in refs[:nr + npar]]
        ct_vals = tuple(r[0] for r in refs[nr + npar:nr + npar + nout])
        d_refs = refs[nr + npar + nout:]
        _, vjp_fn = jax.vjp(lambda *a: tuple(fn(*a)), *vals)
        grads = vjp_fn(ct_vals)
        for i in range(nr):
            d_refs[i][0] = grads[i]

        if npar:
            @pl.when(pl.program_id(1) == 0)
            def _():
                for j in range(npar):
                    d_refs[nr + j][...] = jnp.zeros_like(d_refs[nr + j])

        for j in range(npar):
            d_refs[nr + j][0] += grads[nr + j]

    def row_spec(c):
        return pl.BlockSpec((1, tm, c), lambda g, t: (g, t, 0))

    def par_spec(c):
        return pl.BlockSpec((1, 1, c), lambda g, t: (g, 0, 0))

    outs = pl.pallas_call(
        body, name=name + "_bwd", grid=(G, T // tm),
        in_specs=([row_spec(r.shape[2]) for r in rows] + [par_spec(p.shape[2]) for p in params]
                  + [row_spec(c.shape[2]) for c in cts]),
        out_specs=[row_spec(r.shape[2]) for r in rows] + [par_spec(p.shape[2]) for p in params],
        out_shape=([jax.ShapeDtypeStruct(r.shape, r.dtype) for r in rows]
                   + [jax.ShapeDtypeStruct(p.shape, p.dtype) for p in params]),
        compiler_params=pltpu.CompilerParams(dimension_semantics=("parallel", "arbitrary"),
                                             vmem_limit_bytes=VMEM_LIMIT),
    )(*rows, *params, *cts)
    return tuple(outs[:nr]), tuple(outs[nr:])


def _stage(name, fn, rows, params, tm):
    @jax.custom_vjp
    def run(rows, params):
        return tuple(_stage_fwd_call(name, fn, rows, params, tm))

    def fwd(rows, params):
        return run(rows, params), (rows, params)

    def bwd(res, cts):
        rows, params = res
        return _stage_bwd_call(name, fn, rows, params, tuple(cts), tm)

    run.defvjp(fwd, bwd)
    return run(tuple(rows), tuple(params))


def _sigmoid(x):
    return 0.5 * (jnp.tanh(0.5 * x) + 1.0)


def _softplus(x):
    return jnp.maximum(x, 0.0) + jnp.log(1.0 + jnp.exp(-jnp.abs(x)))


def _rms(x, g, eps=RMS_EPS):
    return x * lax.rsqrt(jnp.mean(x * x, axis=-1, keepdims=True) + eps) * g


def _fn_rmsnorm(x, g):
    return (_rms(x, g).astype(BF16),)


def _fn_swiglu(gate, up):
    return ((gate * _sigmoid(gate) * up).astype(BF16),)


def _make_fn_lora_mix(p1, p2):
    def fn(u, u_prev, mu):
        um = u + (u_prev - u) * mu
        return (jnp.tanh(um[:, :p1]).astype(BF16), um[:, p1:p1 + p2].astype(BF16),
                _sigmoid(um[:, p1 + p2:]).astype(BF16))
    return fn


def _fn_rwkv_prep(ru, ru_p, ku, ku_p, vu, vu_p, w_lin, a_lin, mu_r, mu_k, mu_v, w0, a0, k_k, k_a):
    r = ru + (ru_p - ru) * mu_r
    k = ku + (ku_p - ku) * mu_k
    v = vu + (vu_p - vu) * mu_v
    w_log = -_softplus(-(w0 + w_lin)) - 0.5
    lw = -jnp.exp(w_log)
    a = _sigmoid(a0 + a_lin)
    kk = k * k_k
    kk = kk / jnp.maximum(jnp.sqrt(jnp.sum(kk * kk, axis=-1, keepdims=True)), 1e-12)
    k_mod = k * (1.0 + (a - 1.0) * k_a)
    return r, lw, k_mod, v, kk, kk * a


def _fn_rwkv_post(y, r, k_mod, v, g, lnx_g, lnx_b, r_k):
    mean = jnp.mean(y, axis=-1, keepdims=True)
    yc = y - mean
    var = jnp.mean(yc * yc, axis=-1, keepdims=True)
    yn = yc * lax.rsqrt(var + GN_EPS) * lnx_g + lnx_b
    bonus = jnp.sum(r * k_mod * r_k, axis=-1, keepdims=True) * v
    return (((yn + bonus) * g).astype(BF16),)


def _fn_fox_prep(q, k, qg, kg):
    return _rms(q, qg), _rms(k, kg)


def _fn_log_forget(f_raw, b):
    x = f_raw + b
    return (jnp.minimum(x, 0.0) - jnp.log(1.0 + jnp.exp(-jnp.abs(x))),)


def _fn_final(z, e_raw, h2, target, gate_b, ple_g):
    gate = _sigmoid(z + gate_b)
    out = h2 + gate * _rms(e_raw, ple_g)
    err = out - target
    return (0.5 * jnp.mean(err * err, axis=-1, keepdims=True),)


def _dot_bf16(a, b, ca, cb):
    return lax.dot_general(a.astype(BF16), b.astype(BF16), (((ca,), (cb,)), ((0,), (0,))),
                           preferred_element_type=F32)


@functools.partial(jax.custom_vjp, nondiff_argnums=(2, 3))
def _dot(a, b, ca, cb):
    return _dot_bf16(a, b, ca, cb)


def _dot_fwd(a, b, ca, cb):
    return _dot_bf16(a, b, ca, cb), (a, b)


def _dot_bwd(ca, cb, res, g):
    a, b = res
    ia, jb = 3 - ca, 3 - cb
    da = _dot_bf16(g, b, 2, jb) if ca == 2 else _dot_bf16(b, g, jb, 2)
    db = _dot_bf16(a, g, ia, 1) if cb == 1 else _dot_bf16(g, a, 1, ia)
    return da, db


_dot.defvjp(_dot_fwd, _dot_bwd)


def _scan_chunk(S0, r, lw, k, v, kk, b):
    B, L, _ = r.shape
    row = lax.broadcasted_iota(jnp.int32, (B, L, L), 1)
    col = lax.broadcasted_iota(jnp.int32, (B, L, L), 2)
    incl = col <= row
    strict = col < row
    cum = lax.dot_general(incl.astype(F32), lw, (((2,), (1,)), ((0,), (0,))), precision=HIGHEST,
                          preferred_element_type=F32)
    g_in, g_ex, g_inv = jnp.exp(cum), jnp.exp(cum - lw), jnp.exp(-cum)
    kkg, kd, bd, rg = kk * g_ex, k * g_inv, b * g_inv, r * g_in
    a_k = jnp.where(strict, _dot(kkg, kd, 2, 2), 0.0)
    a_b = jnp.where(strict, _dot(kkg, bd, 2, 2), 0.0)
    pw = -a_b
    inv = (row == col).astype(F32) + pw
    for _ in range(int(math.log2(L)) - 1):
        pw = _dot(pw, pw, 2, 1)
        inv = inv + _dot(inv, pw, 2, 1)
    sa = -_dot(inv, _dot(kkg, S0, 2, 2) + _dot(a_k, v, 2, 1), 2, 1)
    r_k = jnp.where(incl, _dot(rg, kd, 2, 2), 0.0)
    r_b = jnp.where(incl, _dot(rg, bd, 2, 2), 0.0)
    y = _dot(rg, S0, 2, 2) + _dot(r_k, v, 2, 1) + _dot(r_b, sa, 2, 1)
    g_end = jnp.exp(jnp.sum(lw, axis=1, keepdims=True))
    S1 = S0 * g_end + _dot(v, kd * g_end, 1, 1) + _dot(sa, bd * g_end, 1, 1)
    return y, S1


def _scan_heads_per_step(H):
    return next(hb for hb in (SCAN_HEADS_PER_STEP, 2, 1) if H % hb == 0)


def _scan_fwd_call(r, lw, k, v, kk, b):
    H, T, N = r.shape
    L = SCAN_CHUNK
    n_chunks = T // L
    hb = _scan_heads_per_step(H)

    def body(r_ref, lw_ref, k_ref, v_ref, kk_ref, b_ref, y_ref, s0_ref, state):
        @pl.when(pl.program_id(1) == 0)
        def _():
            state[...] = jnp.zeros_like(state)

        S0 = state[...]
        s0_ref[:, 0] = S0
        y, S1 = _scan_chunk(S0, r_ref[...], lw_ref[...], k_ref[...], v_ref[...], kk_ref[...], b_ref[...])
        y_ref[...] = y
        state[...] = S1

    blk = pl.BlockSpec((hb, L, N), lambda h, c: (h, c, 0))
    return pl.pallas_call(
        body, name="rwkv_scan_fwd", grid=(H // hb, n_chunks),
        in_specs=[blk] * 6,
        out_specs=[blk, pl.BlockSpec((hb, 1, N, N), lambda h, c: (h, c, 0, 0))],
        out_shape=[jax.ShapeDtypeStruct((H, T, N), F32), jax.ShapeDtypeStruct((H, n_chunks, N, N), F32)],
        scratch_shapes=[pltpu.VMEM((hb, N, N), F32)],
        compiler_params=pltpu.CompilerParams(dimension_semantics=("parallel", "arbitrary")),
    )(r, lw, k, v, kk, b)


def _scan_bwd_call(r, lw, k, v, kk, b, s0s, dy):
    H, T, N = r.shape
    L = SCAN_CHUNK
    n_chunks = T // L
    hb = _scan_heads_per_step(H)

    def body(r_ref, lw_ref, k_ref, v_ref, kk_ref, b_ref, s0_ref, dy_ref, dr, dlw, dk, dv, dkk, db, d_state):
        @pl.when(pl.program_id(1) == 0)
        def _():
            d_state[...] = jnp.zeros_like(d_state)

        _, vjp_fn = jax.vjp(_scan_chunk, s0_ref[:, 0], r_ref[...], lw_ref[...], k_ref[...], v_ref[...], kk_ref[...],
                            b_ref[...])
        grads = vjp_fn((dy_ref[...], d_state[...]))
        d_state[...] = grads[0]
        for o_ref, g in zip((dr, dlw, dk, dv, dkk, db), grads[1:]):
            o_ref[...] = g

    blk = pl.BlockSpec((hb, L, N), lambda h, c: (h, n_chunks - 1 - c, 0))
    return pl.pallas_call(
        body, name="rwkv_scan_bwd", grid=(H // hb, n_chunks),
        in_specs=[blk] * 6 + [pl.BlockSpec((hb, 1, N, N), lambda h, c: (h, n_chunks - 1 - c, 0, 0)), blk],
        out_specs=[blk] * 6,
        out_shape=[jax.ShapeDtypeStruct((H, T, N), F32)] * 6,
        scratch_shapes=[pltpu.VMEM((hb, N, N), F32)],
        compiler_params=pltpu.CompilerParams(dimension_semantics=("parallel", "arbitrary")),
    )(r, lw, k, v, kk, b, s0s, dy)


@jax.custom_vjp
def _rwkv_scan(r, lw, k, v, kk, b):
    return _scan_fwd_call(r, lw, k, v, kk, b)[0]


def _rwkv_scan_fwd(r, lw, k, v, kk, b):
    y, s0s = _scan_fwd_call(r, lw, k, v, kk, b)
    return y, (r, lw, k, v, kk, b, s0s)


def _rwkv_scan_bwd(res, dy):
    return tuple(_scan_bwd_call(*res, dy))


_rwkv_scan.defvjp(_rwkv_scan_fwd, _rwkv_scan_bwd)


def _nt(a, b):
    return lax.dot_general(a.astype(BF16), b.astype(BF16), (((1,), (1,)), ((), ())), preferred_element_type=F32)


def _nn(a, b):
    return lax.dot_general(a.astype(BF16), b.astype(BF16), (((1,), (0,)), ((), ())), preferred_element_type=F32)


def _tn(a, b):
    return lax.dot_general(a.astype(BF16), b.astype(BF16), (((0,), (0,)), ((), ())), preferred_element_type=F32)


def _attn_scores(qs, k_ref, cr_ref, row_bias, j, kb, q0, masked):
    ks = pl.multiple_of(j * kb, kb)
    kj = k_ref[0, pl.ds(ks, kb), :]
    s = _nt(qs, kj) + row_bias - cr_ref[0, j]
    if masked:
        qi = q0 + lax.broadcasted_iota(jnp.int32, s.shape, 0)
        ki = ks + lax.broadcasted_iota(jnp.int32, s.shape, 1)
        s = jnp.where(ki <= qi, s, -jnp.inf)
    return s, kj, ks


def _attn_specs(T, N, bq, kb):
    q_spec = pl.BlockSpec((1, bq, N), lambda h, i: (h, i, 0))
    kv_spec = pl.BlockSpec((1, T, N), lambda h, i: (h, 0, 0))
    col_spec = pl.BlockSpec((1, bq, 1), lambda h, i: (h, i, 0))
    row_spec = pl.BlockSpec((1, T // kb, 1, kb), lambda h, i: (h, 0, 0, 0))
    return q_spec, kv_spec, col_spec, row_spec


def _grid_marks(H, n_q):
    h, i = pl.program_id(0), pl.program_id(1)
    return (h == 0) & (i == 0), (h == H // 2) & (i == 0), (h == H - 1) & (i == n_q - 1)


def _attn_fwd_call(q, k, v, c_col, c_rows, gather_xs):
    H, T, N = q.shape
    bq, kb = min(ATTN_BLOCK_Q, T), c_rows.shape[3]
    q_spec, kv_spec, col_spec, row_spec = _attn_specs(T, N, bq, kb)
    n = len(gather_xs)

    def body(q_ref, k_ref, v_ref, cc_ref, cr_ref, *rest):
        x_refs, (o_ref, o32_ref, lse_ref), out_refs, sems = rest[:n], rest[n:n + 3], rest[n + 3:2 * n + 3], rest[2 * n + 3:]
        first, middle, last = _grid_marks(H, T // bq)
        start, relay, finish = _gather_phases(x_refs, out_refs, *sems)
        pl.when(first)(start)
        pl.when(middle)(relay)
        q0 = pl.program_id(1) * bq
        qs = (q_ref[0] * (HEAD_DIM ** -0.5)).astype(BF16)
        cc = cc_ref[0]

        def step(j, carry, masked):
            m, l, acc = carry
            s, _, ks = _attn_scores(qs, k_ref, cr_ref, cc, j, kb, q0, masked)
            m_new = jnp.maximum(m, jnp.max(s, axis=-1, keepdims=True))
            alpha = jnp.exp(m - m_new)
            p = jnp.exp(s - m_new)
            l = alpha * l + jnp.sum(p, axis=-1, keepdims=True)
            p_hi = p.astype(BF16)
            p_lo = p - p_hi.astype(F32)
            vj = v_ref[0, pl.ds(ks, kb), :]
            acc = alpha * acc + (_nn(p_hi, vj) + _nn(p_lo, vj))
            return m_new, l, acc

        n_full = q0 // kb
        init = (jnp.full((bq, 1), -jnp.inf, F32), jnp.zeros((bq, 1), F32), jnp.zeros((bq, N), F32))
        carry = lax.fori_loop(0, n_full, functools.partial(step, masked=False), init)
        m, l, acc = step(n_full, carry, masked=True)
        o = acc / l
        o_ref[0] = o.astype(o_ref.dtype)
        o32_ref[0] = o
        lse_ref[0] = m + jnp.log(l)
        pl.when(last)(finish)

    any_spec = pl.BlockSpec(memory_space=pl.ANY)
    return pl.pallas_call(
        body, name="fox_attn_fwd", grid=(H, T // bq),
        in_specs=[q_spec, kv_spec, kv_spec, col_spec, row_spec] + [any_spec] * n,
        out_specs=[q_spec, q_spec, col_spec] + [any_spec] * n,
        out_shape=[jax.ShapeDtypeStruct((H, T, N), BF16),
                   jax.ShapeDtypeStruct((H, T, N), F32),
                   jax.ShapeDtypeStruct((H, T, 1), F32)] + _gather_out_shapes(gather_xs),
        scratch_shapes=_comm_semaphores(n),
        compiler_params=pltpu.CompilerParams(dimension_semantics=("arbitrary", "arbitrary"),
                                             vmem_limit_bytes=VMEM_LIMIT),
    )(q, k, v, c_col, c_rows, *gather_xs)


def _attn_bwd_call(q, k, v, c_col, c_rows, o, lse, do, scatter_parts):
    H, T, N = q.shape
    bq, kb = min(ATTN_BLOCK_Q, T), c_rows.shape[3]
    q_spec, kv_spec, col_spec, row_spec = _attn_specs(T, N, bq, kb)
    n = len(scatter_parts)

    def body(q_ref, k_ref, v_ref, cc_ref, cr_ref, o_ref, lse_ref, do_ref, *rest):
        a_refs, (dq_ref, dk_ref, dv_ref, dcr_ref), b_refs, sems = rest[:n], rest[n:n + 4], rest[n + 4:2 * n + 4], rest[2 * n + 4:]
        first, _, last = _grid_marks(H, T // bq)
        start, finish = _scatter_phases(a_refs, b_refs, *sems)
        pl.when(first)(start)
        i = pl.program_id(1)
        q0 = i * bq

        @pl.when(i == 0)
        def _():
            dk_ref[...] = jnp.zeros_like(dk_ref)
            dv_ref[...] = jnp.zeros_like(dv_ref)
            dcr_ref[...] = jnp.zeros_like(dcr_ref)

        qs = (q_ref[0] * (HEAD_DIM ** -0.5)).astype(BF16)
        do = do_ref[0]
        delta = jnp.sum(do.astype(F32) * o_ref[0], axis=-1, keepdims=True)
        row_bias = cc_ref[0] - lse_ref[0]

        def step(j, dq, masked):
            s, kj, ks = _attn_scores(qs, k_ref, cr_ref, row_bias, j, kb, q0, masked)
            p = jnp.exp(s)
            ds = p * (_nt(do, v_ref[0, pl.ds(ks, kb), :]) - delta)
            ds_b = ds.astype(BF16)
            dk_ref[0, pl.ds(ks, kb), :] += _tn(ds_b, qs)
            dv_ref[0, pl.ds(ks, kb), :] += _tn(p, do)
            dcr_ref[0, j] -= jnp.sum(ds, axis=0, keepdims=True)
            return dq + _nn(ds_b, kj)

        n_full = q0 // kb
        dq = lax.fori_loop(0, n_full, functools.partial(step, masked=False), jnp.zeros((bq, N), F32))
        dq_ref[0] = step(n_full, dq, masked=True) * (HEAD_DIM ** -0.5)
        pl.when(last)(finish)

    any_spec = pl.BlockSpec(memory_space=pl.ANY)
    return pl.pallas_call(
        body, name="fox_attn_bwd", grid=(H, T // bq),
        in_specs=[q_spec, kv_spec, kv_spec, col_spec, row_spec, q_spec, col_spec, q_spec] + [any_spec] * n,
        out_specs=[q_spec, kv_spec, kv_spec, row_spec] + [any_spec] * n,
        out_shape=[jax.ShapeDtypeStruct((H, T, N), F32)] * 3 + [jax.ShapeDtypeStruct(c_rows.shape, F32)]
        + [jax.ShapeDtypeStruct(a.shape, a.dtype) for a in scatter_parts],
        scratch_shapes=_comm_semaphores(n),
        compiler_params=pltpu.CompilerParams(dimension_semantics=("arbitrary", "arbitrary"),
                                             vmem_limit_bytes=VMEM_LIMIT),
    )(q, k, v, c_col, c_rows, o, lse, do, *scatter_parts)


@jax.custom_vjp
def _fox_attn(q, k, v, c, late_blocks, carrier):
    return _fox_attn_fwd(q, k, v, c, late_blocks, carrier)[0]


def _attn_bias_views(c):
    H, T = c.shape
    kb = min(ATTN_BLOCK_K, T)
    return c[:, :, None], c.reshape(H, T // kb, 1, kb)


def _fox_attn_fwd(q, k, v, c, late_blocks, carrier):
    o, o32, lse, *gathered = _attn_fwd_call(q, k, v, *_attn_bias_views(c), late_blocks)
    return (o, tuple(gathered)), (q, k, v, c, o32, lse)


def _fox_attn_bwd(res, cts):
    q, k, v, c, o32, lse = res
    do, d_gathered = cts
    dq, dk, dv, dc_rows, *parts = _attn_bwd_call(q, k, v, *_attn_bias_views(c), o32, lse, do, d_gathered)
    no_grad = tuple(jnp.zeros(a.shape[1:], a.dtype) for a in parts)
    return dq, dk, dv, dc_rows.reshape(c.shape), no_grad, tuple(parts)


_fox_attn.defvjp(_fox_attn_fwd, _fox_attn_bwd)


N_PEERS = N_DEV - 1


def _all_gather(xs, name):
    n = len(xs)

    def body(*refs):
        start, relay, finish = _gather_phases(refs[:n], refs[n:2 * n], *refs[2 * n:])
        start()
        relay()
        finish()

    any_spec = pl.BlockSpec(memory_space=pl.ANY)
    return pl.pallas_call(
        body, name=name,
        out_shape=_gather_out_shapes(xs),
        in_specs=[any_spec] * n, out_specs=[any_spec] * n,
        scratch_shapes=_comm_semaphores(n),
    )(*xs)


def _gather_out_shapes(xs):
    return [jax.ShapeDtypeStruct((N_DEV,) + x.shape, x.dtype) for x in xs]


def _comm_semaphores(n):
    return [pltpu.SemaphoreType.DMA((N_PEERS * n,)), pltpu.SemaphoreType.DMA((N_PEERS * n,)),
            pltpu.SemaphoreType.DMA((n,))]


def _gather_phases(x_refs, out_refs, send_sems, recv_sems, local_sems):
    n = len(x_refs)
    x_, y_, c_ = lax.axis_index("x"), lax.axis_index("y"), lax.axis_index("c")
    me, sibling = (x_, y_, c_), (x_, y_, 1 - c_)
    chips = [(1 - x_, y_), (x_, 1 - y_), (1 - x_, 1 - y_)]

    def slot(t, px, py, pc):
        return out_refs[t].at[4 * px + 2 * py + pc]

    def copy(t, k, block, to, src=None):
        return pltpu.make_async_remote_copy(
            src_ref=slot(t, *block) if src is None else src, dst_ref=slot(t, *block),
            send_sem=send_sems.at[k * n + t], recv_sem=recv_sems.at[k * n + t],
            device_id=to, device_id_type=pl.DeviceIdType.MESH)

    def mine():
        return [pltpu.make_async_copy(x_refs[t], slot(t, *me), local_sems.at[t]) for t in range(n)]

    def first():
        return ([copy(t, 0, me, sibling, src=x_refs[t]) for t in range(n)]
                + [copy(t, 1 + j, me, (*chip, c_), src=x_refs[t]) for j, chip in enumerate(chips) for t in range(n)])

    def passed():
        return [copy(t, 4 + j, (*chip, c_), sibling) for j, chip in enumerate(chips) for t in range(n)]

    def start():
        for cp in mine() + first():
            cp.start()

    def relay():
        for j, chip in enumerate(chips):
            for t in range(n):
                copy(t, 1 + j, (*chip, c_), me).wait_recv()
                copy(t, 4 + j, (*chip, c_), sibling).start()

    def finish():
        for t in range(n):
            copy(t, 0, sibling, me).wait_recv()
        for j, chip in enumerate(chips):
            for t in range(n):
                copy(t, 4 + j, (*chip, 1 - c_), me).wait_recv()
        for cp in first() + passed():
            cp.wait_send()
        for cp in mine():
            cp.wait()

    return start, relay, finish


def _all_to_all(parts, name):
    n = len(parts)

    def body(*refs):
        start, finish = _scatter_phases(refs[:n], refs[n:2 * n], *refs[2 * n:])
        start()
        finish()

    any_spec = pl.BlockSpec(memory_space=pl.ANY)
    return pl.pallas_call(
        body, name=name,
        out_shape=[jax.ShapeDtypeStruct(a.shape, a.dtype) for a in parts],
        in_specs=[any_spec] * n, out_specs=[any_spec] * n,
        scratch_shapes=_comm_semaphores(n),
    )(*parts)


def _scatter_phases(a_refs, b_refs, send_sems, recv_sems, local_sems):
    n = len(a_refs)
    x_, y_, c_ = lax.axis_index("x"), lax.axis_index("y"), lax.axis_index("c")
    me_idx = 4 * x_ + 2 * y_ + c_

    def copies():
        out = [pltpu.make_async_copy(a_refs[t].at[me_idx], b_refs[t].at[me_idx], local_sems.at[t]) for t in range(n)]
        for rel in range(1, N_DEV):
            px = 1 - x_ if rel & 4 else x_
            py = 1 - y_ if rel & 2 else y_
            pc = 1 - c_ if rel & 1 else c_
            for t in range(n):
                out.append(pltpu.make_async_remote_copy(
                    src_ref=a_refs[t].at[4 * px + 2 * py + pc], dst_ref=b_refs[t].at[me_idx],
                    send_sem=send_sems.at[(rel - 1) * n + t], recv_sem=recv_sems.at[(rel - 1) * n + t],
                    device_id=(px, py, pc), device_id_type=pl.DeviceIdType.MESH))
        return out

    def start():
        for cp in copies():
            cp.start()

    def finish():
        for cp in copies():
            cp.wait()

    return start, finish


def _reduce_adamw(parts, w, m, v, name):
    R, C = w.shape
    tr = max(t for t in (256, 128, PACK_ROW_QUANTUM) if R % t == 0)

    def body(p_ref, w_ref, m_ref, v_ref, g_out, d_out, m_out, v_out):
        g = p_ref[0]
        for i in range(1, N_DEV):
            g = g + p_ref[i]
        m_new = ADAM_B1 * m_ref[...] + (1.0 - ADAM_B1) * g
        v_new = ADAM_B2 * v_ref[...] + (1.0 - ADAM_B2) * (g * g)
        m_hat = m_new / (1.0 - ADAM_B1 ** ADAM_STEP)
        v_hat = v_new / (1.0 - ADAM_B2 ** ADAM_STEP)
        g_out[...] = g
        d_out[...] = -ADAM_LR * (m_hat / (jnp.sqrt(v_hat) + ADAM_EPS) + ADAM_WD * w_ref[...])
        m_out[...] = m_new
        v_out[...] = v_new

    spec = pl.BlockSpec((tr, C), lambda i: (i, 0))
    return pl.pallas_call(
        body, name=name, grid=(R // tr,),
        in_specs=[pl.BlockSpec((N_DEV, tr, C), lambda i: (0, i, 0)), spec, spec, spec],
        out_specs=[spec] * 4,
        out_shape=[jax.ShapeDtypeStruct((R, C), F32)] * 4,
        compiler_params=pltpu.CompilerParams(dimension_semantics=("parallel",), vmem_limit_bytes=VMEM_LIMIT),
    )(parts, w, m, v)


def _pack(arrays, dtype):
    flat = jnp.concatenate([a.reshape(-1).astype(dtype) for a in arrays])
    rows = _round_up(-(-flat.shape[0] // PACK_COLS), PACK_ROW_QUANTUM)
    flat = jnp.pad(flat, (0, rows * PACK_COLS - flat.shape[0]))
    return flat.reshape(rows, PACK_COLS)


def _unpack(packed, shapes):
    lead = packed.shape[:-2]
    flat = packed.reshape(lead + (-1,))
    out, off = [], 0
    for s in shapes:
        n = math.prod(s)
        out.append(flat[..., off:off + n].reshape(lead + tuple(s)))
        off += n
    return out


def _travel_layout(name, block):
    return block.T if SHARDED[name] else block


REDUCE_BLOCK_BYTES = 4 * 1024 * 1024


def _reduce_parts(parts, name):
    _, R, C = parts.shape
    per_col = N_DEV * R * parts.dtype.itemsize
    tc = next((t for t in range(C - C % LANES, 0, -LANES) if C % t == 0 and t * per_col <= REDUCE_BLOCK_BYTES), C)

    def body(p_ref, o_ref):
        g = p_ref[0].astype(F32)
        for i in range(1, N_DEV):
            g = g + p_ref[i].astype(F32)
        o_ref[...] = g

    return pl.pallas_call(
        body, name=name, grid=(C // tc,),
        in_specs=[pl.BlockSpec((N_DEV, R, tc), lambda j: (0, 0, j))],
        out_specs=pl.BlockSpec((R, tc), lambda j: (0, j)),
        out_shape=jax.ShapeDtypeStruct((R, C), F32),
        compiler_params=pltpu.CompilerParams(dimension_semantics=("parallel",), vmem_limit_bytes=VMEM_LIMIT),
    )(parts)


def _adamw(g, w, m, v, name):
    R, C = w.shape
    tr = next((t for t in (512, 256, 128, 64, 32, 16, 8) if R % t == 0 and t * C * 4 <= 2 * 1024 * 1024), R)

    def body(g_ref, w_ref, m_ref, v_ref, d_out, m_out, v_out):
        g_ = g_ref[...]
        m_new = ADAM_B1 * m_ref[...] + (1.0 - ADAM_B1) * g_
        v_new = ADAM_B2 * v_ref[...] + (1.0 - ADAM_B2) * (g_ * g_)
        m_hat = m_new / (1.0 - ADAM_B1 ** ADAM_STEP)
        v_hat = v_new / (1.0 - ADAM_B2 ** ADAM_STEP)
        d_out[...] = -ADAM_LR * (m_hat / (jnp.sqrt(v_hat) + ADAM_EPS) + ADAM_WD * w_ref[...])
        m_out[...] = m_new
        v_out[...] = v_new

    spec = pl.BlockSpec((tr, C), lambda i: (i, 0))
    return pl.pallas_call(
        body, name=name, grid=(R // tr,),
        in_specs=[spec] * 4, out_specs=[spec] * 3,
        out_shape=[jax.ShapeDtypeStruct((R, C), F32)] * 3,
        compiler_params=pltpu.CompilerParams(dimension_semantics=("parallel",), vmem_limit_bytes=VMEM_LIMIT),
    )(g, w, m, v)


def _to_heads(u, n_groups):
    T = u.shape[0]
    uh = jnp.transpose(u.reshape(T, n_groups, -1, HEAD_DIM), (1, 2, 0, 3))
    return [uh[i] for i in range(n_groups)]


def _from_heads(uh):
    H, T, N = uh.shape
    return jnp.transpose(uh, (1, 0, 2)).reshape(T, H * N)


def _shift(uh):
    return jnp.pad(uh, ((0, 0), (1, 0), (0, 0)))[:, :-1]


def _pad_cols(a, width):
    return jnp.pad(a, ((0, 0), (0, width - a.shape[1])))


def _split_rows(w, sizes):
    offsets = [sum(sizes[:i]) for i in range(len(sizes))]

    @jax.custom_vjp
    def run(w):
        return tuple(w[o:o + s] for o, s in zip(offsets, sizes))

    def fwd(w):
        return run(w), None

    def bwd(_, cts):
        return (jnp.concatenate(cts, axis=0),)

    run.defvjp(fwd, bwd)
    return run(w)


def _pad_rows(a, height):
    return jnp.pad(a, ((0, height - a.shape[0]), (0, 0)))


def _vec(a):
    return a.reshape(1, 1, -1)


TM_WIDE = 128


def _mixing_half(W, small, x, late_blocks, carrier):
    T, D = x.shape
    vec = _vec
    tm_wide = TM_WIDE
    rw = small['w0'].shape[-1]
    fw = W['w_out'].shape[0] - rw
    heads_r, heads_f = rw // HEAD_DIM, fw // HEAD_DIM
    dl, al, gl = W['w2'].shape[1], W['a2'].shape[1], W['g2'].shape[1]
    dl_p, al_p, gl_p = _round_up(dl, LANES), _round_up(al, LANES), _round_up(gl, LANES)
    f_p = _round_up(heads_f, LANES)
    rwkv_cols = 3 * rw + dl + al + gl
    tm_head = 512 if T % 512 == 0 else T

    o_w, o_a, o_g = 3 * rw, 3 * rw + dl, 3 * rw + dl + al
    w_rkv, w_xw, w_xa, w_xg, w_qkv, w_fg = _split_rows(W['w_in'], (3 * rw, dl, al, gl, 3 * fw, heads_f))
    w_lora = jnp.concatenate([_pad_rows(w_xw, dl_p), _pad_rows(w_xa, al_p), _pad_rows(w_xg, gl_p)], axis=0)
    w_f = _pad_rows(w_fg, f_p)
    mu = small['shift_mu'].reshape(1, -1)
    mu_lora = jnp.concatenate([_pad_cols(mu[:, o_w:o_a], dl_p), _pad_cols(mu[:, o_a:o_g], al_p),
                               _pad_cols(mu[:, o_g:rwkv_cols], gl_p)], axis=1)

    def head_vec(a):
        return a.reshape(-1, 1, HEAD_DIM)

    (xn,) = _stage("attn_norm", _fn_rmsnorm, [x[None]], [vec(small['attn_norm_g'])], tm_wide)
    xn = xn[0]
    u_rkv = _mm_t(xn, w_rkv, "in_rkv")
    u_lora = _mm_t(xn, w_lora, "in_lora")
    u_qkv = _mm_t(xn, w_qkv, "in_qkv")
    f_raw = _mm_t(xn, w_f, "in_f")

    u_lora3 = u_lora[None]
    xw_t, xa_m, xg_s = _stage("lora_mix", _make_fn_lora_mix(dl_p, al_p), [u_lora3, _shift(u_lora3)],
                              [vec(mu_lora)], tm_wide)
    w_lin = _mm_t(xw_t[0], _pad_cols(W['w2'], dl_p), "w2")
    a_lin = _mm_t(xa_m[0], _pad_cols(W['a2'], al_p), "a2")
    gate_r = _mm_t(xg_s[0], _pad_cols(W['g2'], gl_p), "g2")
    ru, ku, vu = _to_heads(u_rkv, 3)
    (w_lin_h,), (a_lin_h,), (gate_h,) = _to_heads(w_lin, 1), _to_heads(a_lin, 1), _to_heads(gate_r, 1)
    mu_r, mu_k, mu_v = (head_vec(mu[:, i * rw:(i + 1) * rw]) for i in range(3))
    r, lw, k_mod, v, kk, b = _stage(
        "rwkv_prep", _fn_rwkv_prep,
        [ru, _shift(ru), ku, _shift(ku), vu, _shift(vu), w_lin_h, a_lin_h],
        [mu_r, mu_k, mu_v, head_vec(small['w0']), head_vec(small['a0']), head_vec(small['k_k']),
         head_vec(small['k_a'])], tm_head)
    y_scan = _rwkv_scan(r, lw, k_mod, v, kk, b)
    (y_rwkv,) = _stage("rwkv_post", _fn_rwkv_post, [y_scan, r, k_mod, v, gate_h],
                       [head_vec(small['lnx_g']), head_vec(small['lnx_b']), head_vec(small['r_k'])], tm_head)

    qu, kf, vf = _to_heads(u_qkv, 3)
    qg = jnp.broadcast_to(vec(small['q_norm_g']), (heads_f, 1, HEAD_DIM))
    kg = jnp.broadcast_to(vec(small['k_norm_g']), (heads_f, 1, HEAD_DIM))
    qn, kn = _stage("fox_prep", _fn_fox_prep, [qu, kf], [qg, kg], tm_head)
    fb = _pad_cols(small['fgate_b'].reshape(1, -1), f_p)
    (log_f,) = _stage("log_forget", _fn_log_forget, [f_raw[None]], [vec(fb)], tm_head)
    c = jnp.cumsum(log_f[0][:, :heads_f], axis=0).T
    y_fox, gathered_late = _fox_attn(qn, kn, vf, c, late_blocks, carrier)

    y_cat = jnp.concatenate([_from_heads(y_rwkv), _from_heads(y_fox)], axis=-1)
    return x + _mm(y_cat, W['w_out'], "out"), gathered_late


def _channel_half_loss(W, small, h1, p, target):
    vec = _vec
    tm_wide = TM_WIDE
    (hn,) = _stage("ffn_norm", _fn_rmsnorm, [h1[None]], [vec(small['ffn_norm_g'])], tm_wide)
    gate = _mm_t(hn[0], W['w_gate'], "gate")
    up = _mm_t(hn[0], W['w_up'], "up")
    (act,) = _stage("swiglu", _fn_swiglu, [gate[None], up[None]], [], tm_wide)
    h2 = h1 + _mm(act[0], W['w_down'], "down")
    e_raw = _mm_t(p, W['ple_proj'], "ple_proj")
    (hg,) = _stage("ple_gate_norm", _fn_rmsnorm, [h2[None]], [vec(small['ple_gate_norm_g'])], tm_wide)
    z = _mm(hg[0], W['ple_gate_w'], "ple_gate")
    (loss_rows,) = _stage("final", _fn_final, [z[None], e_raw[None], h2[None], target[None]],
                          [vec(small['ple_gate_b']), vec(small['ple_norm_g'])], tm_wide)
    return jnp.sum(loss_rows)


def kernel(x, p, attn_norm_g, w_in, shift_mu, w0, w2, a0, a2, g2, k_k, k_a, r_k, lnx_g, lnx_b, q_norm_g, k_norm_g, fgate_b, w_out, ffn_norm_g, w_gate, w_up, w_down, ple_proj, ple_norm_g, ple_gate_norm_g, ple_gate_w, ple_gate_b, loss_target, m_attn_norm_g, m_w_in, m_shift_mu, m_w0, m_w2, m_a0, m_a2, m_g2, m_k_k, m_k_a, m_r_k, m_lnx_g, m_lnx_b, m_q_norm_g, m_k_norm_g, m_fgate_b, m_w_out, m_ffn_norm_g, m_w_gate, m_w_up, m_w_down, m_ple_proj, m_ple_norm_g, m_ple_gate_norm_g, m_ple_gate_w, m_ple_gate_b, v_attn_norm_g, v_w_in, v_shift_mu, v_w0, v_w2, v_a0, v_a2, v_g2, v_k_k, v_k_a, v_r_k, v_lnx_g, v_lnx_b, v_q_norm_g, v_k_norm_g, v_fgate_b, v_w_out, v_ffn_norm_g, v_w_gate, v_w_up, v_w_down, v_ple_proj, v_ple_norm_g, v_ple_gate_norm_g, v_ple_gate_w, v_ple_gate_b):
    weights = dict(zip(WEIGHT_NAMES, (attn_norm_g, w_in, shift_mu, w0, w2, a0, a2, g2, k_k, k_a, r_k, lnx_g, lnx_b,
                                      q_norm_g, k_norm_g, fgate_b, w_out, ffn_norm_g, w_gate, w_up, w_down, ple_proj,
                                      ple_norm_g, ple_gate_norm_g, ple_gate_w, ple_gate_b)))
    m_in = dict(zip(WEIGHT_NAMES, (m_attn_norm_g, m_w_in, m_shift_mu, m_w0, m_w2, m_a0, m_a2, m_g2, m_k_k, m_k_a, m_r_k,
                                   m_lnx_g, m_lnx_b, m_q_norm_g, m_k_norm_g, m_fgate_b, m_w_out, m_ffn_norm_g, m_w_gate,
                                   m_w_up, m_w_down, m_ple_proj, m_ple_norm_g, m_ple_gate_norm_g, m_ple_gate_w,
                                   m_ple_gate_b)))
    v_in = dict(zip(WEIGHT_NAMES, (v_attn_norm_g, v_w_in, v_shift_mu, v_w0, v_w2, v_a0, v_a2, v_g2, v_k_k, v_k_a, v_r_k,
                                   v_lnx_g, v_lnx_b, v_q_norm_g, v_k_norm_g, v_fgate_b, v_w_out, v_ffn_norm_g, v_w_gate,
                                   v_w_up, v_w_down, v_ple_proj, v_ple_norm_g, v_ple_gate_norm_g, v_ple_gate_w,
                                   v_ple_gate_b)))
    small_shapes = [weights[n].shape for n in SMALL_NAMES]
    small = {n: weights[n] for n in SMALL_NAMES}

    def whole(stacks, names):
        return {n: g.reshape(N_DEV * g.shape[1], g.shape[2]) for n, g in zip(names, stacks)}

    def stacked(tree, names, like):
        return tuple(tree[n].reshape(g.shape) for n, g in zip(names, like))

    travelling = {n: _travel_layout(n, weights[n][0]).astype(BF16) for n in SHARDED_NAMES}
    gathered_early = _all_gather([travelling[n] for n in EARLY_NAMES], "gather_weights")
    late_blocks = tuple(travelling[n] for n in LATE_NAMES)
    carrier = tuple(jnp.zeros((N_DEV,) + b.shape, b.dtype) for b in late_blocks)

    (h1, gathered_late), mixing_vjp = jax.vjp(_mixing_half, whole(gathered_early, EARLY_NAMES), small, x[0],
                                              late_blocks, carrier)
    loss_local, (d_late, d_small_b, d_h1) = jax.value_and_grad(_channel_half_loss, argnums=(0, 1, 2))(
        whole(gathered_late, LATE_NAMES), small, h1, p[0, 0], loss_target[0])
    d_early, d_small_a, d_x, _, parts_late = mixing_vjp((d_h1, stacked(d_late, LATE_NAMES, gathered_late)))
    d_small = {n: d_small_a[n] + d_small_b[n] for n in SMALL_NAMES}
    loss = lax.psum(loss_local, MESH_AXES)

    parts_early = _all_to_all(stacked(d_early, EARLY_NAMES, gathered_early), "scatter_grads")
    parts = dict(zip(EARLY_NAMES + LATE_NAMES, list(parts_early) + list(parts_late)))
    (small_parts,) = _all_gather([_pack([d_small[n] for n in SMALL_NAMES], F32)], "gather_small_grads")

    def pack_f32(tree, names):
        return _pack([tree[n] for n in names], F32)

    sml = _reduce_adamw(small_parts, pack_f32(weights, SMALL_NAMES), pack_f32(m_in, SMALL_NAMES),
                        pack_f32(v_in, SMALL_NAMES), "adamw_replicated")
    by_kind = [dict(zip(SMALL_NAMES, _unpack(sml[kind], small_shapes))) for kind in range(4)]
    for n in SHARDED_NAMES:
        g = _travel_layout(n, _reduce_parts(parts[n], "reduce_" + n))
        upd = _adamw(g, weights[n][0], m_in[n][0], v_in[n][0], "adamw_" + n)
        for kind, val in enumerate((g, *upd)):
            by_kind[kind][n] = val[None]
    outs = [by_kind[kind][n] for kind in range(4) for n in WEIGHT_NAMES]
    return (loss, d_x[None], *outs)
```

```python
import functools
import math

import jax
import jax.numpy as jnp
from jax import lax
from jax.experimental import pallas as pl
from jax.experimental.pallas import tpu as pltpu

F32 = jnp.float32
BF16 = jnp.bfloat16
HIGHEST = lax.Precision.HIGHEST

N_DEV = 8
MESH_AXES = ("x", "y", "c")
HEAD_DIM = 64
SCAN_CHUNK = 64
SCAN_HEADS_PER_STEP = 16
ATTN_BLOCK_Q = 512
ATTN_BLOCK_K = 512
LANES = 128
PACK_COLS = 1024
PACK_ROW_QUANTUM = 64
VMEM_LIMIT = 48 * 1024 * 1024
RMS_EPS = 1e-6
GN_EPS = 64e-5
ADAM_LR, ADAM_B1, ADAM_B2, ADAM_EPS, ADAM_WD, ADAM_STEP = 0.001, 0.9, 0.999, 1e-08, 0.01, 10

WEIGHT_NAMES = ['attn_norm_g', 'w_in', 'shift_mu', 'w0', 'w2', 'a0', 'a2', 'g2', 'k_k', 'k_a', 'r_k', 'lnx_g', 'lnx_b',
                'q_norm_g', 'k_norm_g', 'fgate_b', 'w_out', 'ffn_norm_g', 'w_gate', 'w_up', 'w_down', 'ple_proj',
                'ple_norm_g', 'ple_gate_norm_g', 'ple_gate_w', 'ple_gate_b']
SHARDED = {'w_in': True, 'w2': True, 'a2': True, 'g2': True, 'w_out': False, 'w_gate': True, 'w_up': True,
           'w_down': False, 'ple_proj': True, 'ple_gate_w': False}
SHARDED_NAMES = [n for n in WEIGHT_NAMES if n in SHARDED]
SMALL_NAMES = [n for n in WEIGHT_NAMES if n not in SHARDED]
EARLY_NAMES = ['w_in', 'w2', 'a2', 'g2', 'w_out']
LATE_NAMES = [n for n in SHARDED_NAMES if n not in EARLY_NAMES]


def _round_up(n, q):
    return -(-n // q) * q


def _tile(n, cap):
    if n <= cap:
        return n
    return next((t for t in range(cap - cap % LANES, 0, -LANES) if n % t == 0), n)


def _mm_call(a, b, mode, name, out_dtype, addend=None):
    if mode == "nn":
        (I, C), (_, J) = a.shape, b.shape
    elif mode == "nt":
        (I, C), (J, _) = a.shape, b.shape
    else:
        (C, I), (_, J) = a.shape, b.shape
    if mode == "tn":
        ti, tj, tc = _tile(I, 1024), _tile(J, 1024), _tile(C, 1024)
    else:
        ti, tj, tc = _tile(I, 1024), _tile(J, 512), _tile(C, 2048)
    n_c = C // tc
    if mode == "nn":
        a_spec = pl.BlockSpec((ti, tc), lambda i, j, c: (i, c))
        b_spec = pl.BlockSpec((tc, tj), lambda i, j, c: (c, j))
        dims = (((1,), (0,)), ((), ()))
    elif mode == "nt":
        a_spec = pl.BlockSpec((ti, tc), lambda i, j, c: (i, c))
        b_spec = pl.BlockSpec((tj, tc), lambda i, j, c: (j, c))
        dims = (((1,), (1,)), ((), ()))
    else:
        a_spec = pl.BlockSpec((tc, ti), lambda i, j, c: (c, i))
        b_spec = pl.BlockSpec((tc, tj), lambda i, j, c: (c, j))
        dims = (((0,), (0,)), ((), ()))

    def product(a_ref, b_ref):
        return lax.dot_general(a_ref[...].astype(BF16), b_ref[...].astype(BF16), dims, preferred_element_type=F32)

    def finish(o_ref, r_refs, value):
        for r_ref in r_refs:
            value = r_ref[...] + value
        o_ref[...] = value.astype(o_ref.dtype)

    def body_single(a_ref, b_ref, *rest):
        finish(rest[-1], rest[:-1], product(a_ref, b_ref))

    def body_accumulate(a_ref, b_ref, *rest):
        (*r_refs, o_ref, acc), c = rest, pl.program_id(2)

        @pl.when(c == 0)
        def _():
            acc[...] = jnp.zeros_like(acc)

        acc[...] += product(a_ref, b_ref)

        @pl.when(c == n_c - 1)
        def _():
            finish(o_ref, r_refs, acc[...])

    out_spec = pl.BlockSpec((ti, tj), lambda i, j, c: (i, j))
    addends = [] if addend is None else [addend]
    return pl.pallas_call(
        body_single if n_c == 1 else body_accumulate, name=name, grid=(I // ti, J // tj, n_c),
        in_specs=[a_spec, b_spec] + [out_spec] * len(addends),
        out_specs=out_spec,
        out_shape=jax.ShapeDtypeStruct((I, J), out_dtype),
        scratch_shapes=[] if n_c == 1 else [pltpu.VMEM((ti, tj), F32)],
        compiler_params=pltpu.CompilerParams(dimension_semantics=("parallel", "parallel", "arbitrary"),
                                             vmem_limit_bytes=VMEM_LIMIT),
    )(a, b, *addends)


def _mm_add(residual, a, b, name):
    @jax.custom_vjp
    def run(residual, a, b):
        return _mm_call(a, b, "nn", "mm_" + name, F32, addend=residual)

    def fwd(residual, a, b):
        return run(residual, a, b), (a, b)

    def bwd(res, g):
        a, b = res
        return (g, _mm_call(g, b, "nt", "mm_" + name + "_da", a.dtype),
                _mm_call(a, g, "tn", "mm_" + name + "_db", b.dtype))

    run.defvjp(fwd, bwd)
    return run(residual, a, b)


def _mm(a, b, name):
    @jax.custom_vjp
    def run(a, b):
        return _mm_call(a, b, "nn", "mm_" + name, F32)

    def fwd(a, b):
        return run(a, b), (a, b)

    def bwd(res, g):
        a, b = res
        return (_mm_call(g, b, "nt", "mm_" + name + "_da", a.dtype),
                _mm_call(a, g, "tn", "mm_" + name + "_db", b.dtype))

    run.defvjp(fwd, bwd)
    return run(a, b)


def _mm_t(a, wt, name):
    @jax.custom_vjp
    def run(a, wt):
        return _mm_call(a, wt, "nt", "mmt_" + name, F32)

    def fwd(a, wt):
        return run(a, wt), (a, wt)

    def bwd(res, g):
        a, wt = res
        return (_mm_call(g, wt, "nn", "mmt_" + name + "_da", a.dtype),
                _mm_call(g, a, "tn", "mmt_" + name + "_dw", wt.dtype))

    run.defvjp(fwd, bwd)
    return run(a, wt)


def _stage_fwd_call(name, fn, rows, params, tm):
    G, T, _ = rows[0].shape
    nr, npar = len(rows), len(params)
    out_avals = jax.eval_shape(
        lambda *a: tuple(fn(*a)),
        *[jax.ShapeDtypeStruct((tm, r.shape[2]), r.dtype) for r in rows],
        *[jax.ShapeDtypeStruct((1, p.shape[2]), p.dtype) for p in params])

    def body(*refs):
        vals = [r[0] for r in refs[:nr + npar]]
        for o_ref, o in zip(refs[nr + npar:], fn(*vals)):
            o_ref[0] = o

    def row_spec(c):
        return pl.BlockSpec((1, tm, c), lambda g, t: (g, t, 0))

    def par_spec(c):
        return pl.BlockSpec((1, 1, c), lambda g, t: (g, 0, 0))

    return pl.pallas_call(
        body, name=name, grid=(G, T // tm),
        in_specs=[row_spec(r.shape[2]) for r in rows] + [par_spec(p.shape[2]) for p in params],
        out_specs=[row_spec(o.shape[1]) for o in out_avals],
        out_shape=[jax.ShapeDtypeStruct((G, T, o.shape[1]), o.dtype) for o in out_avals],
        compiler_params=pltpu.CompilerParams(dimension_semantics=("parallel", "parallel"),
                                             vmem_limit_bytes=VMEM_LIMIT),
    )(*rows, *params)


def _stage_bwd_call(name, fn, rows, params, cts, tm):
    G, T, _ = rows[0].shape
    nr, npar, nout = len(rows), len(params), len(cts)

    def body(*refs):
        vals = [r[0] for r in refs[:nr + npar]]
        ct_vals = tuple(r[0] for r in refs[nr + npar:nr + npar + nout])
        d_refs = refs[nr + npar + nout:]
        _, vjp_fn = jax.vjp(lambda *a: tuple(fn(*a)), *vals)
        grads = vjp_fn(ct_vals)
        for i in range(nr):
            d_refs[i][0] = grads[i]

        if npar:
            @pl.when(pl.program_id(1) == 0)
            def _():
                for j in range(npar):
                    d_refs[nr + j][...] = jnp.zeros_like(d_refs[nr + j])

        for j in range(npar):
            d_refs[nr + j][0] += grads[nr + j]

    def row_spec(c):
        return pl.BlockSpec((1, tm, c), lambda g, t: (g, t, 0))

    def par_spec(c):
        return pl.BlockSpec((1, 1, c), lambda g, t: (g, 0, 0))

    outs = pl.pallas_call(
        body, name=name + "_bwd", grid=(G, T // tm),
        in_specs=([row_spec(r.shape[2]) for r in rows] + [par_spec(p.shape[2]) for p in params]
                  + [row_spec(c.shape[2]) for c in cts]),
        out_specs=[row_spec(r.shape[2]) for r in rows] + [par_spec(p.shape[2]) for p in params],
        out_shape=([jax.ShapeDtypeStruct(r.shape, r.dtype) for r in rows]
                   + [jax.ShapeDtypeStruct(p.shape, p.dtype) for p in params]),
        compiler_params=pltpu.CompilerParams(dimension_semantics=("parallel", "arbitrary"),
                                             vmem_limit_bytes=VMEM_LIMIT),
    )(*rows, *params, *cts)
    return tuple(outs[:nr]), tuple(outs[nr:])


def _stage(name, fn, rows, params, tm):
    @jax.custom_vjp
    def run(rows, params):
        return tuple(_stage_fwd_call(name, fn, rows, params, tm))

    def fwd(rows, params):
        return run(rows, params), (rows, params)

    def bwd(res, cts):
        rows, params = res
        return _stage_bwd_call(name, fn, rows, params, tuple(cts), tm)

    run.defvjp(fwd, bwd)
    return run(tuple(rows), tuple(params))


def _sigmoid(x):
    return 0.5 * (jnp.tanh(0.5 * x) + 1.0)


def _softplus(x):
    return jnp.maximum(x, 0.0) + jnp.log(1.0 + jnp.exp(-jnp.abs(x)))


def _rms(x, g, eps=RMS_EPS):
    return x * lax.rsqrt(jnp.mean(x * x, axis=-1, keepdims=True) + eps) * g


def _fn_rmsnorm(x, g):
    return (_rms(x, g).astype(BF16),)


def _fn_swiglu(gate, up):
    return ((gate * _sigmoid(gate) * up).astype(BF16),)


def _make_fn_lora_mix(p1, p2):
    def fn(u, u_prev, mu):
        um = u + (u_prev - u) * mu
        return (jnp.tanh(um[:, :p1]).astype(BF16), um[:, p1:p1 + p2].astype(BF16),
                _sigmoid(um[:, p1 + p2:]).astype(BF16))
    return fn


def _fn_rwkv_prep(ru, ru_p, ku, ku_p, vu, vu_p, w_lin, a_lin, mu_r, mu_k, mu_v, w0, a0, k_k, k_a):
    r = ru + (ru_p - ru) * mu_r
    k = ku + (ku_p - ku) * mu_k
    v = vu + (vu_p - vu) * mu_v
    w_log = -_softplus(-(w0 + w_lin)) - 0.5
    lw = -jnp.exp(w_log)
    a = _sigmoid(a0 + a_lin)
    kk = k * k_k
    kk = kk / jnp.maximum(jnp.sqrt(jnp.sum(kk * kk, axis=-1, keepdims=True)), 1e-12)
    k_mod = k * (1.0 + (a - 1.0) * k_a)
    return r, lw, k_mod, v, kk, kk * a


def _fn_rwkv_post(y, r, k_mod, v, g, lnx_g, lnx_b, r_k):
    mean = jnp.mean(y, axis=-1, keepdims=True)
    yc = y - mean
    var = jnp.mean(yc * yc, axis=-1, keepdims=True)
    yn = yc * lax.rsqrt(var + GN_EPS) * lnx_g + lnx_b
    bonus = jnp.sum(r * k_mod * r_k, axis=-1, keepdims=True) * v
    return (((yn + bonus) * g).astype(BF16),)


def _fn_fox_prep(q, k, qg, kg):
    return _rms(q, qg), _rms(k, kg)


def _fn_log_forget(f_raw, b):
    x = f_raw + b
    return (jnp.minimum(x, 0.0) - jnp.log(1.0 + jnp.exp(-jnp.abs(x))),)


def _fn_final(z, e_raw, h2, target, gate_b, ple_g):
    gate = _sigmoid(z + gate_b)
    out = h2 + gate * _rms(e_raw, ple_g)
    err = out - target
    return (0.5 * jnp.mean(err * err, axis=-1, keepdims=True),)


def _dot_bf16(a, b, ca, cb):
    return lax.dot_general(a.astype(BF16), b.astype(BF16), (((ca,), (cb,)), ((0,), (0,))),
                           preferred_element_type=F32)


@functools.partial(jax.custom_vjp, nondiff_argnums=(2, 3))
def _dot(a, b, ca, cb):
    return _dot_bf16(a, b, ca, cb)


def _dot_fwd(a, b, ca, cb):
    return _dot_bf16(a, b, ca, cb), (a, b)


def _dot_bwd(ca, cb, res, g):
    a, b = res
    ia, jb = 3 - ca, 3 - cb
    da = _dot_bf16(g, b, 2, jb) if ca == 2 else _dot_bf16(b, g, jb, 2)
    db = _dot_bf16(a, g, ia, 1) if cb == 1 else _dot_bf16(g, a, 1, ia)
    return da, db


_dot.defvjp(_dot_fwd, _dot_bwd)


def _scan_chunk(S0, r, lw, k, v, kk, b):
    B, L, _ = r.shape
    row = lax.broadcasted_iota(jnp.int32, (B, L, L), 1)
    col = lax.broadcasted_iota(jnp.int32, (B, L, L), 2)
    incl = col <= row
    strict = col < row
    cum = lax.dot_general(incl.astype(F32), lw, (((2,), (1,)), ((0,), (0,))), precision=HIGHEST,
                          preferred_element_type=F32)
    g_in, g_ex, g_inv = jnp.exp(cum), jnp.exp(cum - lw), jnp.exp(-cum)
    kkg, kd, bd, rg = kk * g_ex, k * g_inv, b * g_inv, r * g_in
    a_k = jnp.where(strict, _dot(kkg, kd, 2, 2), 0.0)
    a_b = jnp.where(strict, _dot(kkg, bd, 2, 2), 0.0)
    pw = -a_b
    inv = (row == col).astype(F32) + pw
    for _ in range(int(math.log2(L)) - 1):
        pw = _dot(pw, pw, 2, 1)
        inv = inv + _dot(inv, pw, 2, 1)
    sa = -_dot(inv, _dot(kkg, S0, 2, 2) + _dot(a_k, v, 2, 1), 2, 1)
    r_k = jnp.where(incl, _dot(rg, kd, 2, 2), 0.0)
    r_b = jnp.where(incl, _dot(rg, bd, 2, 2), 0.0)
    y = _dot(rg, S0, 2, 2) + _dot(r_k, v, 2, 1) + _dot(r_b, sa, 2, 1)
    g_end = jnp.exp(jnp.sum(lw, axis=1, keepdims=True))
    S1 = S0 * g_end + _dot(v, kd * g_end, 1, 1) + _dot(sa, bd * g_end, 1, 1)
    return y, S1


def _scan_heads_per_step(H):
    return next(hb for hb in (SCAN_HEADS_PER_STEP, 2, 1) if H % hb == 0)


def _scan_fwd_call(r, lw, k, v, kk, b):
    H, T, N = r.shape
    L = SCAN_CHUNK
    n_chunks = T // L
    hb = _scan_heads_per_step(H)

    def body(r_ref, lw_ref, k_ref, v_ref, kk_ref, b_ref, y_ref, s0_ref, state):
        @pl.when(pl.program_id(1) == 0)
        def _():
            state[...] = jnp.zeros_like(state)

        S0 = state[...]
        s0_ref[:, 0] = S0
        y, S1 = _scan_chunk(S0, r_ref[...], lw_ref[...], k_ref[...], v_ref[...], kk_ref[...], b_ref[...])
        y_ref[...] = y
        state[...] = S1

    blk = pl.BlockSpec((hb, L, N), lambda h, c: (h, c, 0))
    return pl.pallas_call(
        body, name="rwkv_scan_fwd", grid=(H // hb, n_chunks),
        in_specs=[blk] * 6,
        out_specs=[blk, pl.BlockSpec((hb, 1, N, N), lambda h, c: (h, c, 0, 0))],
        out_shape=[jax.ShapeDtypeStruct((H, T, N), F32), jax.ShapeDtypeStruct((H, n_chunks, N, N), F32)],
        scratch_shapes=[pltpu.VMEM((hb, N, N), F32)],
        compiler_params=pltpu.CompilerParams(dimension_semantics=("parallel", "arbitrary")),
    )(r, lw, k, v, kk, b)


def _scan_bwd_call(r, lw, k, v, kk, b, s0s, dy):
    H, T, N = r.shape
    L = SCAN_CHUNK
    n_chunks = T // L
    hb = _scan_heads_per_step(H)

    def body(r_ref, lw_ref, k_ref, v_ref, kk_ref, b_ref, s0_ref, dy_ref, dr, dlw, dk, dv, dkk, db, d_state):
        @pl.when(pl.program_id(1) == 0)
        def _():
            d_state[...] = jnp.zeros_like(d_state)

        _, vjp_fn = jax.vjp(_scan_chunk, s0_ref[:, 0], r_ref[...], lw_ref[...], k_ref[...], v_ref[...], kk_ref[...],
                            b_ref[...])
        grads = vjp_fn((dy_ref[...], d_state[...]))
        d_state[...] = grads[0]
        for o_ref, g in zip((dr, dlw, dk, dv, dkk, db), grads[1:]):
            o_ref[...] = g

    blk = pl.BlockSpec((hb, L, N), lambda h, c: (h, n_chunks - 1 - c, 0))
    return pl.pallas_call(
        body, name="rwkv_scan_bwd", grid=(H // hb, n_chunks),
        in_specs=[blk] * 6 + [pl.BlockSpec((hb, 1, N, N), lambda h, c: (h, n_chunks - 1 - c, 0, 0)), blk],
        out_specs=[blk] * 6,
        out_shape=[jax.ShapeDtypeStruct((H, T, N), F32)] * 6,
        scratch_shapes=[pltpu.VMEM((hb, N, N), F32)],
        compiler_params=pltpu.CompilerParams(dimension_semantics=("parallel", "arbitrary")),
    )(r, lw, k, v, kk, b, s0s, dy)


@jax.custom_vjp
def _rwkv_scan(r, lw, k, v, kk, b):
    return _scan_fwd_call(r, lw, k, v, kk, b)[0]


def _rwkv_scan_fwd(r, lw, k, v, kk, b):
    y, s0s = _scan_fwd_call(r, lw, k, v, kk, b)
    return y, (r, lw, k, v, kk, b, s0s)


def _rwkv_scan_bwd(res, dy):
    return tuple(_scan_bwd_call(*res, dy))


_rwkv_scan.defvjp(_rwkv_scan_fwd, _rwkv_scan_bwd)


def _nt(a, b):
    return lax.dot_general(a.astype(BF16), b.astype(BF16), (((1,), (1,)), ((), ())), preferred_element_type=F32)


def _nn(a, b):
    return lax.dot_general(a.astype(BF16), b.astype(BF16), (((1,), (0,)), ((), ())), preferred_element_type=F32)


def _tn(a, b):
    return lax.dot_general(a.astype(BF16), b.astype(BF16), (((0,), (0,)), ((), ())), preferred_element_type=F32)


def _attn_scores(qs, k_ref, cr_ref, row_bias, j, kb, q0, masked):
    ks = pl.multiple_of(j * kb, kb)
    kj = k_ref[0, pl.ds(ks, kb), :]
    s = _nt(qs, kj) + row_bias - cr_ref[0, j]
    if masked:
        qi = q0 + lax.broadcasted_iota(jnp.int32, s.shape, 0)
        ki = ks + lax.broadcasted_iota(jnp.int32, s.shape, 1)
        s = jnp.where(ki <= qi, s, -jnp.inf)
    return s, kj, ks


def _attn_specs(T, N, bq, kb):
    q_spec = pl.BlockSpec((1, bq, N), lambda h, i: (h, i, 0))
    kv_spec = pl.BlockSpec((1, T, N), lambda h, i: (h, 0, 0))
    col_spec = pl.BlockSpec((1, bq, 1), lambda h, i: (h, i, 0))
    row_spec = pl.BlockSpec((1, T // kb, 1, kb), lambda h, i: (h, 0, 0, 0))
    return q_spec, kv_spec, col_spec, row_spec


def _grid_marks(H, n_q):
    h, i = pl.program_id(0), pl.program_id(1)
    return (h == 0) & (i == 0), (h == H // 2) & (i == 0), (h == H - 1) & (i == n_q - 1)


def _attn_fwd_call(q, k, v, c_col, c_rows, gather_xs):
    H, T, N = q.shape
    bq, kb = min(ATTN_BLOCK_Q, T), c_rows.shape[3]
    q_spec, kv_spec, col_spec, row_spec = _attn_specs(T, N, bq, kb)
    n = len(gather_xs)

    def body(q_ref, k_ref, v_ref, cc_ref, cr_ref, *rest):
        x_refs, (o_ref, o32_ref, lse_ref), out_refs, sems = rest[:n], rest[n:n + 3], rest[n + 3:2 * n + 3], rest[2 * n + 3:]
        first, middle, last = _grid_marks(H, T // bq)
        start, relay, finish = _gather_phases(x_refs, out_refs, *sems)
        pl.when(first)(start)
        pl.when(middle)(relay)
        q0 = pl.program_id(1) * bq
        qs = (q_ref[0] * (HEAD_DIM ** -0.5)).astype(BF16)
        cc = cc_ref[0]

        def step(j, carry, masked):
            m, l, acc = carry
            s, _, ks = _attn_scores(qs, k_ref, cr_ref, cc, j, kb, q0, masked)
            m_new = jnp.maximum(m, jnp.max(s, axis=-1, keepdims=True))
            alpha = jnp.exp(m - m_new)
            p = jnp.exp(s - m_new)
            l = alpha * l + jnp.sum(p, axis=-1, keepdims=True)
            p_hi = p.astype(BF16)
            p_lo = p - p_hi.astype(F32)
            vj = v_ref[0, pl.ds(ks, kb), :]
            acc = alpha * acc + (_nn(p_hi, vj) + _nn(p_lo, vj))
            return m_new, l, acc

        n_full = q0 // kb
        init = (jnp.full((bq, 1), -jnp.inf, F32), jnp.zeros((bq, 1), F32), jnp.zeros((bq, N), F32))
        carry = lax.fori_loop(0, n_full, functools.partial(step, masked=False), init)
        m, l, acc = step(n_full, carry, masked=True)
        o = acc / l
        o_ref[0] = o.astype(o_ref.dtype)
        o32_ref[0] = o
        lse_ref[0] = m + jnp.log(l)
        pl.when(last)(finish)

    any_spec = pl.BlockSpec(memory_space=pl.ANY)
    return pl.pallas_call(
        body, name="fox_attn_fwd", grid=(H, T // bq),
        in_specs=[q_spec, kv_spec, kv_spec, col_spec, row_spec] + [any_spec] * n,
        out_specs=[q_spec, q_spec, col_spec] + [any_spec] * n,
        out_shape=[jax.ShapeDtypeStruct((H, T, N), BF16),
                   jax.ShapeDtypeStruct((H, T, N), F32),
                   jax.ShapeDtypeStruct((H, T, 1), F32)] + _gather_out_shapes(gather_xs),
        scratch_shapes=_comm_semaphores(n),
        compiler_params=pltpu.CompilerParams(dimension_semantics=("arbitrary", "arbitrary"),
                                             vmem_limit_bytes=VMEM_LIMIT),
    )(q, k, v, c_col, c_rows, *gather_xs)


def _attn_bwd_call(q, k, v, c_col, c_rows, o, lse, do, scatter_parts):
    H, T, N = q.shape
    bq, kb = min(ATTN_BLOCK_Q, T), c_rows.shape[3]
    q_spec, kv_spec, col_spec, row_spec = _attn_specs(T, N, bq, kb)
    n = len(scatter_parts)

    def body(q_ref, k_ref, v_ref, cc_ref, cr_ref, o_ref, lse_ref, do_ref, *rest):
        a_refs, (dq_ref, dk_ref, dv_ref, dcr_ref), b_refs, sems = rest[:n], rest[n:n + 4], rest[n + 4:2 * n + 4], rest[2 * n + 4:]
        first, _, last = _grid_marks(H, T // bq)
        start, finish = _scatter_phases(a_refs, b_refs, *sems)
        pl.when(first)(start)
        i = pl.program_id(1)
        q0 = i * bq

        @pl.when(i == 0)
        def _():
            dk_ref[...] = jnp.zeros_like(dk_ref)
            dv_ref[...] = jnp.zeros_like(dv_ref)
            dcr_ref[...] = jnp.zeros_like(dcr_ref)

        qs = (q_ref[0] * (HEAD_DIM ** -0.5)).astype(BF16)
        do = do_ref[0]
        delta = jnp.sum(do.astype(F32) * o_ref[0], axis=-1, keepdims=True)
        row_bias = cc_ref[0] - lse_ref[0]

        def step(j, dq, masked):
            s, kj, ks = _attn_scores(qs, k_ref, cr_ref, row_bias, j, kb, q0, masked)
            p = jnp.exp(s)
            ds = p * (_nt(do, v_ref[0, pl.ds(ks, kb), :]) - delta)
            ds_b = ds.astype(BF16)
            dk_ref[0, pl.ds(ks, kb), :] += _tn(ds_b, qs)
            dv_ref[0, pl.ds(ks, kb), :] += _tn(p, do)
            dcr_ref[0, j] -= jnp.sum(ds, axis=0, keepdims=True)
            return dq + _nn(ds_b, kj)

        n_full = q0 // kb
        dq = lax.fori_loop(0, n_full, functools.partial(step, masked=False), jnp.zeros((bq, N), F32))
        dq_ref[0] = step(n_full, dq, masked=True) * (HEAD_DIM ** -0.5)
        pl.when(last)(finish)

    any_spec = pl.BlockSpec(memory_space=pl.ANY)
    return pl.pallas_call(
        body, name="fox_attn_bwd", grid=(H, T // bq),
        in_specs=[q_spec, kv_spec, kv_spec, col_spec, row_spec, q_spec, col_spec, q_spec] + [any_spec] * n,
        out_specs=[q_spec, kv_spec, kv_spec, row_spec] + [any_spec] * n,
        out_shape=[jax.ShapeDtypeStruct((H, T, N), F32)] * 3 + [jax.ShapeDtypeStruct(c_rows.shape, F32)]
        + [jax.ShapeDtypeStruct(a.shape, a.dtype) for a in scatter_parts],
        scratch_shapes=_comm_semaphores(n),
        compiler_params=pltpu.CompilerParams(dimension_semantics=("arbitrary", "arbitrary"),
                                             vmem_limit_bytes=VMEM_LIMIT),
    )(q, k, v, c_col, c_rows, o, lse, do, *scatter_parts)


@jax.custom_vjp
def _fox_attn(q, k, v, c, late_blocks, carrier):
    return _fox_attn_fwd(q, k, v, c, late_blocks, carrier)[0]


def _attn_bias_views(c):
    H, T = c.shape
    kb = min(ATTN_BLOCK_K, T)
    return c[:, :, None], c.reshape(H, T // kb, 1, kb)


def _fox_attn_fwd(q, k, v, c, late_blocks, carrier):
    o, o32, lse, *gathered = _attn_fwd_call(q, k, v, *_attn_bias_views(c), late_blocks)
    return (o, tuple(gathered)), (q, k, v, c, o32, lse)


def _fox_attn_bwd(res, cts):
    q, k, v, c, o32, lse = res
    do, d_gathered = cts
    dq, dk, dv, dc_rows, *parts = _attn_bwd_call(q, k, v, *_attn_bias_views(c), o32, lse, do, d_gathered)
    no_grad = tuple(jnp.zeros(a.shape[1:], a.dtype) for a in parts)
    return dq, dk, dv, dc_rows.reshape(c.shape), no_grad, tuple(parts)


_fox_attn.defvjp(_fox_attn_fwd, _fox_attn_bwd)


N_PEERS = N_DEV - 1


def _all_gather(xs, name):
    n = len(xs)

    def body(*refs):
        start, relay, finish = _gather_phases(refs[:n], refs[n:2 * n], *refs[2 * n:])
        start()
        relay()
        finish()

    any_spec = pl.BlockSpec(memory_space=pl.ANY)
    return pl.pallas_call(
        body, name=name,
        out_shape=_gather_out_shapes(xs),
        in_specs=[any_spec] * n, out_specs=[any_spec] * n,
        scratch_shapes=_comm_semaphores(n),
    )(*xs)


def _gather_out_shapes(xs):
    return [jax.ShapeDtypeStruct((N_DEV,) + x.shape, x.dtype) for x in xs]


def _comm_semaphores(n):
    return [pltpu.SemaphoreType.DMA((N_PEERS * n,)), pltpu.SemaphoreType.DMA((N_PEERS * n,)),
            pltpu.SemaphoreType.DMA((n,))]


def _gather_phases(x_refs, out_refs, send_sems, recv_sems, local_sems):
    n = len(x_refs)
    x_, y_, c_ = lax.axis_index("x"), lax.axis_index("y"), lax.axis_index("c")
    me, sibling = (x_, y_, c_), (x_, y_, 1 - c_)
    chips = [(1 - x_, y_), (x_, 1 - y_), (1 - x_, 1 - y_)]

    def slot(t, px, py, pc):
        return out_refs[t].at[4 * px + 2 * py + pc]

    def copy(t, k, block, to, src=None):
        return pltpu.make_async_remote_copy(
            src_ref=slot(t, *block) if src is None else src, dst_ref=slot(t, *block),
            send_sem=send_sems.at[k * n + t], recv_sem=recv_sems.at[k * n + t],
            device_id=to, device_id_type=pl.DeviceIdType.MESH)

    def mine():
        return [pltpu.make_async_copy(x_refs[t], slot(t, *me), local_sems.at[t]) for t in range(n)]

    def first():
        return ([copy(t, 0, me, sibling, src=x_refs[t]) for t in range(n)]
                + [copy(t, 1 + j, me, (*chip, c_), src=x_refs[t]) for j, chip in enumerate(chips) for t in range(n)])

    def passed():
        return [copy(t, 4 + j, (*chip, c_), sibling) for j, chip in enumerate(chips) for t in range(n)]

    def start():
        for cp in mine() + first():
            cp.start()

    def relay():
        for j, chip in enumerate(chips):
            for t in range(n):
                copy(t, 1 + j, (*chip, c_), me).wait_recv()
                copy(t, 4 + j, (*chip, c_), sibling).start()

    def finish():
        for t in range(n):
            copy(t, 0, sibling, me).wait_recv()
        for j, chip in enumerate(chips):
            for t in range(n):
                copy(t, 4 + j, (*chip, 1 - c_), me).wait_recv()
        for cp in first() + passed():
            cp.wait_send()
        for cp in mine():
            cp.wait()

    return start, relay, finish


def _all_to_all(parts, name):
    n = len(parts)

    def body(*refs):
        start, finish = _scatter_phases(refs[:n], refs[n:2 * n], *refs[2 * n:])
        start()
        finish()

    any_spec = pl.BlockSpec(memory_space=pl.ANY)
    return pl.pallas_call(
        body, name=name,
        out_shape=[jax.ShapeDtypeStruct(a.shape, a.dtype) for a in parts],
        in_specs=[any_spec] * n, out_specs=[any_spec] * n,
        scratch_shapes=_comm_semaphores(n),
    )(*parts)


def _scatter_phases(a_refs, b_refs, send_sems, recv_sems, local_sems):
    n = len(a_refs)
    x_, y_, c_ = lax.axis_index("x"), lax.axis_index("y"), lax.axis_index("c")
    me_idx = 4 * x_ + 2 * y_ + c_

    def copies():
        out = [pltpu.make_async_copy(a_refs[t].at[me_idx], b_refs[t].at[me_idx], local_sems.at[t]) for t in range(n)]
        for rel in range(1, N_DEV):
            px = 1 - x_ if rel & 4 else x_
            py = 1 - y_ if rel & 2 else y_
            pc = 1 - c_ if rel & 1 else c_
            for t in range(n):
                out.append(pltpu.make_async_remote_copy(
                    src_ref=a_refs[t].at[4 * px + 2 * py + pc], dst_ref=b_refs[t].at[me_idx],
                    send_sem=send_sems.at[(rel - 1) * n + t], recv_sem=recv_sems.at[(rel - 1) * n + t],
                    device_id=(px, py, pc), device_id_type=pl.DeviceIdType.MESH))
        return out

    def start():
        for cp in copies():
            cp.start()

    def finish():
        for cp in copies():
            cp.wait()

    return start, finish


def _reduce_adamw(parts, w, m, v, name):
    R, C = w.shape
    tr = max(t for t in (256, 128, PACK_ROW_QUANTUM) if R % t == 0)

    def body(p_ref, w_ref, m_ref, v_ref, g_out, d_out, m_out, v_out):
        g = p_ref[0]
        for i in range(1, N_DEV):
            g = g + p_ref[i]
        m_new = ADAM_B1 * m_ref[...] + (1.0 - ADAM_B1) * g
        v_new = ADAM_B2 * v_ref[...] + (1.0 - ADAM_B2) * (g * g)
        m_hat = m_new / (1.0 - ADAM_B1 ** ADAM_STEP)
        v_hat = v_new / (1.0 - ADAM_B2 ** ADAM_STEP)
        g_out[...] = g
        d_out[...] = -ADAM_LR * (m_hat / (jnp.sqrt(v_hat) + ADAM_EPS) + ADAM_WD * w_ref[...])
        m_out[...] = m_new
        v_out[...] = v_new

    spec = pl.BlockSpec((tr, C), lambda i: (i, 0))
    return pl.pallas_call(
        body, name=name, grid=(R // tr,),
        in_specs=[pl.BlockSpec((N_DEV, tr, C), lambda i: (0, i, 0)), spec, spec, spec],
        out_specs=[spec] * 4,
        out_shape=[jax.ShapeDtypeStruct((R, C), F32)] * 4,
        compiler_params=pltpu.CompilerParams(dimension_semantics=("parallel",), vmem_limit_bytes=VMEM_LIMIT),
    )(parts, w, m, v)


def _pack(arrays, dtype):
    flat = jnp.concatenate([a.reshape(-1).astype(dtype) for a in arrays])
    rows = _round_up(-(-flat.shape[0] // PACK_COLS), PACK_ROW_QUANTUM)
    flat = jnp.pad(flat, (0, rows * PACK_COLS - flat.shape[0]))
    return flat.reshape(rows, PACK_COLS)


def _unpack(packed, shapes):
    lead = packed.shape[:-2]
    flat = packed.reshape(lead + (-1,))
    out, off = [], 0
    for s in shapes:
        n = math.prod(s)
        out.append(flat[..., off:off + n].reshape(lead + tuple(s)))
        off += n
    return out


def _travel_layout(name, block):
    return block.T if SHARDED[name] else block


REDUCE_BLOCK_BYTES = 4 * 1024 * 1024


def _reduce_parts(parts, name):
    _, R, C = parts.shape
    per_col = N_DEV * R * parts.dtype.itemsize
    tc = next((t for t in range(C - C % LANES, 0, -LANES) if C % t == 0 and t * per_col <= REDUCE_BLOCK_BYTES), C)

    def body(p_ref, o_ref):
        g = p_ref[0].astype(F32)
        for i in range(1, N_DEV):
            g = g + p_ref[i].astype(F32)
        o_ref[...] = g

    return pl.pallas_call(
        body, name=name, grid=(C // tc,),
        in_specs=[pl.BlockSpec((N_DEV, R, tc), lambda j: (0, 0, j))],
        out_specs=pl.BlockSpec((R, tc), lambda j: (0, j)),
        out_shape=jax.ShapeDtypeStruct((R, C), F32),
        compiler_params=pltpu.CompilerParams(dimension_semantics=("parallel",), vmem_limit_bytes=VMEM_LIMIT),
    )(parts)


def _adamw(g, w, m, v, name):
    R, C = w.shape
    tr = next((t for t in (512, 256, 128, 64, 32, 16, 8) if R % t == 0 and t * C * 4 <= 2 * 1024 * 1024), R)

    def body(g_ref, w_ref, m_ref, v_ref, d_out, m_out, v_out):
        g_ = g_ref[...]
        m_new = ADAM_B1 * m_ref[...] + (1.0 - ADAM_B1) * g_
        v_new = ADAM_B2 * v_ref[...] + (1.0 - ADAM_B2) * (g_ * g_)
        m_hat = m_new / (1.0 - ADAM_B1 ** ADAM_STEP)
        v_hat = v_new / (1.0 - ADAM_B2 ** ADAM_STEP)
        d_out[...] = -ADAM_LR * (m_hat / (jnp.sqrt(v_hat) + ADAM_EPS) + ADAM_WD * w_ref[...])
        m_out[...] = m_new
        v_out[...] = v_new

    spec = pl.BlockSpec((tr, C), lambda i: (i, 0))
    return pl.pallas_call(
        body, name=name, grid=(R // tr,),
        in_specs=[spec] * 4, out_specs=[spec] * 3,
        out_shape=[jax.ShapeDtypeStruct((R, C), F32)] * 3,
        compiler_params=pltpu.CompilerParams(dimension_semantics=("parallel",), vmem_limit_bytes=VMEM_LIMIT),
    )(g, w, m, v)


def _to_heads(u):
    return jnp.transpose(u.reshape(u.shape[0], -1, HEAD_DIM), (1, 0, 2))


def _split_cols(u, n):
    width = u.shape[1] // n

    @jax.custom_vjp
    def run(u):
        return tuple(u[:, i * width:(i + 1) * width] for i in range(n))

    def fwd(u):
        return run(u), None

    def bwd(_, cts):
        return (jnp.concatenate(cts, axis=1),)

    run.defvjp(fwd, bwd)
    return run(u)


def _from_heads(uh):
    H, T, N = uh.shape
    return jnp.transpose(uh, (1, 0, 2)).reshape(T, H * N)


def _shift(uh):
    return jnp.pad(uh, ((0, 0), (1, 0), (0, 0)))[:, :-1]


def _pad_cols(a, width):
    return jnp.pad(a, ((0, 0), (0, width - a.shape[1])))


def _split_rows(w, sizes):
    offsets = [sum(sizes[:i]) for i in range(len(sizes))]

    @jax.custom_vjp
    def run(w):
        return tuple(w[o:o + s] for o, s in zip(offsets, sizes))

    def fwd(w):
        return run(w), None

    def bwd(_, cts):
        return (jnp.concatenate(cts, axis=0),)

    run.defvjp(fwd, bwd)
    return run(w)


def _pad_rows(a, height):
    return jnp.pad(a, ((0, height - a.shape[0]), (0, 0)))


def _vec(a):
    return a.reshape(1, 1, -1)


TM_WIDE = 128


def _mixing_half(W, small, x, late_blocks, carrier):
    T, D = x.shape
    vec = _vec
    tm_wide = TM_WIDE
    rw = small['w0'].shape[-1]
    fw = W['w_out'].shape[0] - rw
    heads_r, heads_f = rw // HEAD_DIM, fw // HEAD_DIM
    dl, al, gl = W['w2'].shape[1], W['a2'].shape[1], W['g2'].shape[1]
    dl_p, al_p, gl_p = _round_up(dl, LANES), _round_up(al, LANES), _round_up(gl, LANES)
    f_p = _round_up(heads_f, LANES)
    rwkv_cols = 3 * rw + dl + al + gl
    tm_head = 512 if T % 512 == 0 else T

    o_w, o_a, o_g = 3 * rw, 3 * rw + dl, 3 * rw + dl + al
    w_rkv, w_xw, w_xa, w_xg, w_qkv, w_fg = _split_rows(W['w_in'], (3 * rw, dl, al, gl, 3 * fw, heads_f))
    w_lora = jnp.concatenate([_pad_rows(w_xw, dl_p), _pad_rows(w_xa, al_p), _pad_rows(w_xg, gl_p)], axis=0)
    w_f = _pad_rows(w_fg, f_p)
    mu = small['shift_mu'].reshape(1, -1)
    mu_lora = jnp.concatenate([_pad_cols(mu[:, o_w:o_a], dl_p), _pad_cols(mu[:, o_a:o_g], al_p),
                               _pad_cols(mu[:, o_g:rwkv_cols], gl_p)], axis=1)

    def head_vec(a):
        return a.reshape(-1, 1, HEAD_DIM)

    (xn,) = _stage("attn_norm", _fn_rmsnorm, [x[None]], [vec(small['attn_norm_g'])], tm_wide)
    xn = xn[0]
    u_rkv = _mm_t(xn, w_rkv, "in_rkv")
    u_lora = _mm_t(xn, w_lora, "in_lora")
    u_qkv = _mm_t(xn, w_qkv, "in_qkv")
    f_raw = _mm_t(xn, w_f, "in_f")

    u_lora3 = u_lora[None]
    xw_t, xa_m, xg_s = _stage("lora_mix", _make_fn_lora_mix(dl_p, al_p), [u_lora3, _shift(u_lora3)],
                              [vec(mu_lora)], tm_wide)
    w_lin = _mm_t(xw_t[0], _pad_cols(W['w2'], dl_p), "w2")
    a_lin = _mm_t(xa_m[0], _pad_cols(W['a2'], al_p), "a2")
    gate_r = _mm_t(xg_s[0], _pad_cols(W['g2'], gl_p), "g2")
    ru, ku, vu = (_to_heads(u) for u in _split_cols(u_rkv, 3))
    w_lin_h, a_lin_h, gate_h = _to_heads(w_lin), _to_heads(a_lin), _to_heads(gate_r)
    mu_r, mu_k, mu_v = (head_vec(mu[:, i * rw:(i + 1) * rw]) for i in range(3))
    r, lw, k_mod, v, kk, b = _stage(
        "rwkv_prep", _fn_rwkv_prep,
        [ru, _shift(ru), ku, _shift(ku), vu, _shift(vu), w_lin_h, a_lin_h],
        [mu_r, mu_k, mu_v, head_vec(small['w0']), head_vec(small['a0']), head_vec(small['k_k']),
         head_vec(small['k_a'])], tm_head)
    y_scan = _rwkv_scan(r, lw, k_mod, v, kk, b)
    (y_rwkv,) = _stage("rwkv_post", _fn_rwkv_post, [y_scan, r, k_mod, v, gate_h],
                       [head_vec(small['lnx_g']), head_vec(small['lnx_b']), head_vec(small['r_k'])], tm_head)

    qu, kf, vf = (_to_heads(u) for u in _split_cols(u_qkv, 3))
    qg = jnp.broadcast_to(vec(small['q_norm_g']), (heads_f, 1, HEAD_DIM))
    kg = jnp.broadcast_to(vec(small['k_norm_g']), (heads_f, 1, HEAD_DIM))
    qn, kn = _stage("fox_prep", _fn_fox_prep, [qu, kf], [qg, kg], tm_head)
    fb = _pad_cols(small['fgate_b'].reshape(1, -1), f_p)
    (log_f,) = _stage("log_forget", _fn_log_forget, [f_raw[None]], [vec(fb)], tm_head)
    c = jnp.cumsum(log_f[0][:, :heads_f], axis=0).T
    y_fox, gathered_late = _fox_attn(qn, kn, vf, c, late_blocks, carrier)

    y_cat = jnp.concatenate([_from_heads(y_rwkv), _from_heads(y_fox)], axis=-1)
    return _mm_add(x, y_cat, W['w_out'], "out"), gathered_late


def _channel_half_loss(W, small, h1, p, target):
    vec = _vec
    tm_wide = TM_WIDE
    (hn,) = _stage("ffn_norm", _fn_rmsnorm, [h1[None]], [vec(small['ffn_norm_g'])], tm_wide)
    gate = _mm_t(hn[0], W['w_gate'], "gate")
    up = _mm_t(hn[0], W['w_up'], "up")
    (act,) = _stage("swiglu", _fn_swiglu, [gate[None], up[None]], [], tm_wide)
    h2 = _mm_add(h1, act[0], W['w_down'], "down")
    e_raw = _mm_t(p, W['ple_proj'], "ple_proj")
    (hg,) = _stage("ple_gate_norm", _fn_rmsnorm, [h2[None]], [vec(small['ple_gate_norm_g'])], tm_wide)
    z = _mm(hg[0], W['ple_gate_w'], "ple_gate")
    (loss_rows,) = _stage("final", _fn_final, [z[None], e_raw[None], h2[None], target[None]],
                          [vec(small['ple_gate_b']), vec(small['ple_norm_g'])], tm_wide)
    return jnp.sum(loss_rows)


def kernel(x, p, attn_norm_g, w_in, shift_mu, w0, w2, a0, a2, g2, k_k, k_a, r_k, lnx_g, lnx_b, q_norm_g, k_norm_g, fgate_b, w_out, ffn_norm_g, w_gate, w_up, w_down, ple_proj, ple_norm_g, ple_gate_norm_g, ple_gate_w, ple_gate_b, loss_target, m_attn_norm_g, m_w_in, m_shift_mu, m_w0, m_w2, m_a0, m_a2, m_g2, m_k_k, m_k_a, m_r_k, m_lnx_g, m_lnx_b, m_q_norm_g, m_k_norm_g, m_fgate_b, m_w_out, m_ffn_norm_g, m_w_gate, m_w_up, m_w_down, m_ple_proj, m_ple_norm_g, m_ple_gate_norm_g, m_ple_gate_w, m_ple_gate_b, v_attn_norm_g, v_w_in, v_shift_mu, v_w0, v_w2, v_a0, v_a2, v_g2, v_k_k, v_k_a, v_r_k, v_lnx_g, v_lnx_b, v_q_norm_g, v_k_norm_g, v_fgate_b, v_w_out, v_ffn_norm_g, v_w_gate, v_w_up, v_w_down, v_ple_proj, v_ple_norm_g, v_ple_gate_norm_g, v_ple_gate_w, v_ple_gate_b):
    weights = dict(zip(WEIGHT_NAMES, (attn_norm_g, w_in, shift_mu, w0, w2, a0, a2, g2, k_k, k_a, r_k, lnx_g, lnx_b,
                                      q_norm_g, k_norm_g, fgate_b, w_out, ffn_norm_g, w_gate, w_up, w_down, ple_proj,
                                      ple_norm_g, ple_gate_norm_g, ple_gate_w, ple_gate_b)))
    m_in = dict(zip(WEIGHT_NAMES, (m_attn_norm_g, m_w_in, m_shift_mu, m_w0, m_w2, m_a0, m_a2, m_g2, m_k_k, m_k_a, m_r_k,
                                   m_lnx_g, m_lnx_b, m_q_norm_g, m_k_norm_g, m_fgate_b, m_w_out, m_ffn_norm_g, m_w_gate,
                                   m_w_up, m_w_down, m_ple_proj, m_ple_norm_g, m_ple_gate_norm_g, m_ple_gate_w,
                                   m_ple_gate_b)))
    v_in = dict(zip(WEIGHT_NAMES, (v_attn_norm_g, v_w_in, v_shift_mu, v_w0, v_w2, v_a0, v_a2, v_g2, v_k_k, v_k_a, v_r_k,
                                   v_lnx_g, v_lnx_b, v_q_norm_g, v_k_norm_g, v_fgate_b, v_w_out, v_ffn_norm_g, v_w_gate,
                                   v_w_up, v_w_down, v_ple_proj, v_ple_norm_g, v_ple_gate_norm_g, v_ple_gate_w,
                                   v_ple_gate_b)))
    small_shapes = [weights[n].shape for n in SMALL_NAMES]
    small = {n: weights[n] for n in SMALL_NAMES}

    def whole(stacks, names):
        return {n: g.reshape(N_DEV * g.shape[1], g.shape[2]) for n, g in zip(names, stacks)}

    def stacked(tree, names, like):
        return tuple(tree[n].reshape(g.shape) for n, g in zip(names, like))

    travelling = {n: _travel_layout(n, weights[n][0]).astype(BF16) for n in SHARDED_NAMES}
    gathered_early = _all_gather([travelling[n] for n in EARLY_NAMES], "gather_weights")
    late_blocks = tuple(travelling[n] for n in LATE_NAMES)
    carrier = tuple(jnp.zeros((N_DEV,) + b.shape, b.dtype) for b in late_blocks)

    (h1, gathered_late), mixing_vjp = jax.vjp(_mixing_half, whole(gathered_early, EARLY_NAMES), small, x[0],
                                              late_blocks, carrier)
    loss_local, (d_late, d_small_b, d_h1) = jax.value_and_grad(_channel_half_loss, argnums=(0, 1, 2))(
        whole(gathered_late, LATE_NAMES), small, h1, p[0, 0], loss_target[0])
    d_early, d_small_a, d_x, _, parts_late = mixing_vjp((d_h1, stacked(d_late, LATE_NAMES, gathered_late)))
    d_small = {n: d_small_a[n] + d_small_b[n] for n in SMALL_NAMES}
    loss = lax.psum(loss_local, MESH_AXES)

    parts_early = _all_to_all(stacked(d_early, EARLY_NAMES, gathered_early), "scatter_grads")
    parts = dict(zip(EARLY_NAMES + LATE_NAMES, list(parts_early) + list(parts_late)))
    (small_parts,) = _all_gather([_pack([d_small[n] for n in SMALL_NAMES], F32)], "gather_small_grads")

    def pack_f32(tree, names):
        return _pack([tree[n] for n in names], F32)

    sml = _reduce_adamw(small_parts, pack_f32(weights, SMALL_NAMES), pack_f32(m_in, SMALL_NAMES),
                        pack_f32(v_in, SMALL_NAMES), "adamw_replicated")
    by_kind = [dict(zip(SMALL_NAMES, _unpack(sml[kind], small_shapes))) for kind in range(4)]
    for n in SHARDED_NAMES:
        g = _travel_layout(n, _reduce_parts(parts[n], "reduce_" + n))
        upd = _adamw(g, weights[n][0], m_in[n][0], v_in[n][0], "adamw_" + n)
        for kind, val in enumerate((g, *upd)):
            by_kind[kind][n] = val[None]
    outs = [by_kind[kind][n] for kind in range(4) for n in WEIGHT_NAMES]
    return (loss, d_x[None], *outs)
```

```python
import functools
import math

import jax
import jax.numpy as jnp
from jax import lax
from jax.experimental import pallas as pl
from jax.experimental.pallas import tpu as pltpu

F32 = jnp.float32
BF16 = jnp.bfloat16
HIGHEST = lax.Precision.HIGHEST

N_DEV = 8
MESH_AXES = ("x", "y", "c")
HEAD_DIM = 64
SCAN_CHUNK = 64
SCAN_HEADS_PER_STEP = 16
ATTN_BLOCK_Q = 512
ATTN_BLOCK_K = 512
LANES = 128
PACK_COLS = 1024
PACK_ROW_QUANTUM = 64
VMEM_LIMIT = 48 * 1024 * 1024
RMS_EPS = 1e-6
GN_EPS = 64e-5
ADAM_LR, ADAM_B1, ADAM_B2, ADAM_EPS, ADAM_WD, ADAM_STEP = 0.001, 0.9, 0.999, 1e-08, 0.01, 10

WEIGHT_NAMES = ['attn_norm_g', 'w_in', 'shift_mu', 'w0', 'w2', 'a0', 'a2', 'g2', 'k_k', 'k_a', 'r_k', 'lnx_g', 'lnx_b',
                'q_norm_g', 'k_norm_g', 'fgate_b', 'w_out', 'ffn_norm_g', 'w_gate', 'w_up', 'w_down', 'ple_proj',
                'ple_norm_g', 'ple_gate_norm_g', 'ple_gate_w', 'ple_gate_b']
SHARDED = {'w_in': True, 'w2': True, 'a2': True, 'g2': True, 'w_out': False, 'w_gate': True, 'w_up': True,
           'w_down': False, 'ple_proj': True, 'ple_gate_w': False}
SHARDED_NAMES = [n for n in WEIGHT_NAMES if n in SHARDED]
SMALL_NAMES = [n for n in WEIGHT_NAMES if n not in SHARDED]
EARLY_NAMES = ['w_in', 'w2', 'a2', 'g2', 'w_out']
LATE_NAMES = [n for n in SHARDED_NAMES if n not in EARLY_NAMES]


def _round_up(n, q):
    return -(-n // q) * q


def _tile_candidates(n, cap):
    sizes = {t for t in range(LANES, min(n, cap) + 1, LANES) if n % t == 0}
    return sorted(sizes | ({n} if n <= cap or not sizes else set()))


MM_TILE_CAP = 2048
MM_VMEM_BUDGET = 38 * 1024 * 1024
MM_STEP_COST_BYTES = 1024 * 1024


def _mm_tiles(I, J, C, a_size, b_size, o_size):
    best = None
    for ti in _tile_candidates(I, MM_TILE_CAP):
        for tj in _tile_candidates(J, MM_TILE_CAP):
            for tc in _tile_candidates(C, MM_TILE_CAP):
                n_c = C // tc
                blocks = 2 * (ti * tc * a_size + tc * tj * b_size + ti * tj * o_size)
                temporaries = (ti * tc + tc * tj) * 2 + ti * tj * 4 * (2 if n_c > 1 else 1)
                if blocks + temporaries > MM_VMEM_BUDGET:
                    continue
                traffic = (I * C * a_size * (1 if n_c == 1 else J // tj) + C * J * b_size * (I // ti)
                           + I * J * o_size)
                cost = traffic + (I // ti) * (J // tj) * n_c * MM_STEP_COST_BYTES
                if best is None or cost < best[0]:
                    best = (cost, ti, tj, tc)
    return best[1:]


def _mm_call(a, b, mode, name, out_dtype, addend=None):
    if mode == "nn":
        (I, C), (_, J) = a.shape, b.shape
    elif mode == "nt":
        (I, C), (J, _) = a.shape, b.shape
    else:
        (C, I), (_, J) = a.shape, b.shape
    ti, tj, tc = _mm_tiles(I, J, C, a.dtype.itemsize, b.dtype.itemsize, jnp.dtype(out_dtype).itemsize)
    n_c = C // tc
    if mode == "nn":
        a_spec = pl.BlockSpec((ti, tc), lambda i, j, c: (i, c))
        b_spec = pl.BlockSpec((tc, tj), lambda i, j, c: (c, j))
        dims = (((1,), (0,)), ((), ()))
    elif mode == "nt":
        a_spec = pl.BlockSpec((ti, tc), lambda i, j, c: (i, c))
        b_spec = pl.BlockSpec((tj, tc), lambda i, j, c: (j, c))
        dims = (((1,), (1,)), ((), ()))
    else:
        a_spec = pl.BlockSpec((tc, ti), lambda i, j, c: (c, i))
        b_spec = pl.BlockSpec((tc, tj), lambda i, j, c: (c, j))
        dims = (((0,), (0,)), ((), ()))

    def product(a_ref, b_ref):
        return lax.dot_general(a_ref[...].astype(BF16), b_ref[...].astype(BF16), dims, preferred_element_type=F32)

    def finish(o_ref, r_refs, value):
        for r_ref in r_refs:
            value = r_ref[...] + value
        o_ref[...] = value.astype(o_ref.dtype)

    def body_single(a_ref, b_ref, *rest):
        finish(rest[-1], rest[:-1], product(a_ref, b_ref))

    def body_accumulate(a_ref, b_ref, *rest):
        (*r_refs, o_ref, acc), c = rest, pl.program_id(2)

        @pl.when(c == 0)
        def _():
            acc[...] = jnp.zeros_like(acc)

        acc[...] += product(a_ref, b_ref)

        @pl.when(c == n_c - 1)
        def _():
            finish(o_ref, r_refs, acc[...])

    out_spec = pl.BlockSpec((ti, tj), lambda i, j, c: (i, j))
    addends = [] if addend is None else [addend]
    return pl.pallas_call(
        body_single if n_c == 1 else body_accumulate, name=name, grid=(I // ti, J // tj, n_c),
        in_specs=[a_spec, b_spec] + [out_spec] * len(addends),
        out_specs=out_spec,
        out_shape=jax.ShapeDtypeStruct((I, J), out_dtype),
        scratch_shapes=[] if n_c == 1 else [pltpu.VMEM((ti, tj), F32)],
        compiler_params=pltpu.CompilerParams(dimension_semantics=("parallel", "parallel", "arbitrary"),
                                             vmem_limit_bytes=VMEM_LIMIT),
    )(a, b, *addends)


def _mm_add(residual, a, b, name):
    @jax.custom_vjp
    def run(residual, a, b):
        return _mm_call(a, b, "nn", "mm_" + name, F32, addend=residual)

    def fwd(residual, a, b):
        return run(residual, a, b), (a, b)

    def bwd(res, g):
        a, b = res
        return (g, _mm_call(g, b, "nt", "mm_" + name + "_da", a.dtype),
                _mm_call(a, g, "tn", "mm_" + name + "_db", b.dtype))

    run.defvjp(fwd, bwd)
    return run(residual, a, b)


def _mm(a, b, name):
    @jax.custom_vjp
    def run(a, b):
        return _mm_call(a, b, "nn", "mm_" + name, F32)

    def fwd(a, b):
        return run(a, b), (a, b)

    def bwd(res, g):
        a, b = res
        return (_mm_call(g, b, "nt", "mm_" + name + "_da", a.dtype),
                _mm_call(a, g, "tn", "mm_" + name + "_db", b.dtype))

    run.defvjp(fwd, bwd)
    return run(a, b)


def _mm_t(a, wt, name):
    @jax.custom_vjp
    def run(a, wt):
        return _mm_call(a, wt, "nt", "mmt_" + name, F32)

    def fwd(a, wt):
        return run(a, wt), (a, wt)

    def bwd(res, g):
        a, wt = res
        return (_mm_call(g, wt, "nn", "mmt_" + name + "_da", a.dtype),
                _mm_call(g, a, "tn", "mmt_" + name + "_dw", wt.dtype))

    run.defvjp(fwd, bwd)
    return run(a, wt)


def _stage_fwd_call(name, fn, rows, params, tm):
    G, T, _ = rows[0].shape
    nr, npar = len(rows), len(params)
    out_avals = jax.eval_shape(
        lambda *a: tuple(fn(*a)),
        *[jax.ShapeDtypeStruct((tm, r.shape[2]), r.dtype) for r in rows],
        *[jax.ShapeDtypeStruct((1, p.shape[2]), p.dtype) for p in params])

    def body(*refs):
        vals = [r[0] for r in refs[:nr + npar]]
        for o_ref, o in zip(refs[nr + npar:], fn(*vals)):
            o_ref[0] = o

    def row_spec(c):
        return pl.BlockSpec((1, tm, c), lambda g, t: (g, t, 0))

    def par_spec(c):
        return pl.BlockSpec((1, 1, c), lambda g, t: (g, 0, 0))

    return pl.pallas_call(
        body, name=name, grid=(G, T // tm),
        in_specs=[row_spec(r.shape[2]) for r in rows] + [par_spec(p.shape[2]) for p in params],
        out_specs=[row_spec(o.shape[1]) for o in out_avals],
        out_shape=[jax.ShapeDtypeStruct((G, T, o.shape[1]), o.dtype) for o in out_avals],
        compiler_params=pltpu.CompilerParams(dimension_semantics=("parallel", "parallel"),
                                             vmem_limit_bytes=VMEM_LIMIT),
    )(*rows, *params)


def _stage_bwd_call(name, fn, rows, params, cts, tm):
    G, T, _ = rows[0].shape
    nr, npar, nout = len(rows), len(params), len(cts)

    def body(*refs):
        vals = [r[0] for r in refs[:nr + npar]]
        ct_vals = tuple(r[0] for r in refs[nr + npar:nr + npar + nout])
        d_refs = refs[nr + npar + nout:]
        _, vjp_fn = jax.vjp(lambda *a: tuple(fn(*a)), *vals)
        grads = vjp_fn(ct_vals)
        for i in range(nr):
            d_refs[i][0] = grads[i]

        if npar:
            @pl.when(pl.program_id(1) == 0)
            def _():
                for j in range(npar):
                    d_refs[nr + j][...] = jnp.zeros_like(d_refs[nr + j])

        for j in range(npar):
            d_refs[nr + j][0] += grads[nr + j]

    def row_spec(c):
        return pl.BlockSpec((1, tm, c), lambda g, t: (g, t, 0))

    def par_spec(c):
        return pl.BlockSpec((1, 1, c), lambda g, t: (g, 0, 0))

    outs = pl.pallas_call(
        body, name=name + "_bwd", grid=(G, T // tm),
        in_specs=([row_spec(r.shape[2]) for r in rows] + [par_spec(p.shape[2]) for p in params]
                  + [row_spec(c.shape[2]) for c in cts]),
        out_specs=[row_spec(r.shape[2]) for r in rows] + [par_spec(p.shape[2]) for p in params],
        out_shape=([jax.ShapeDtypeStruct(r.shape, r.dtype) for r in rows]
                   + [jax.ShapeDtypeStruct(p.shape, p.dtype) for p in params]),
        compiler_params=pltpu.CompilerParams(dimension_semantics=("parallel", "arbitrary"),
                                             vmem_limit_bytes=VMEM_LIMIT),
    )(*rows, *params, *cts)
    return tuple(outs[:nr]), tuple(outs[nr:])


def _stage(name, fn, rows, params, tm):
    @jax.custom_vjp
    def run(rows, params):
        return tuple(_stage_fwd_call(name, fn, rows, params, tm))

    def fwd(rows, params):
        return run(rows, params), (rows, params)

    def bwd(res, cts):
        rows, params = res
        return _stage_bwd_call(name, fn, rows, params, tuple(cts), tm)

    run.defvjp(fwd, bwd)
    return run(tuple(rows), tuple(params))


def _sigmoid(x):
    return 0.5 * (jnp.tanh(0.5 * x) + 1.0)


def _softplus(x):
    return jnp.maximum(x, 0.0) + jnp.log(1.0 + jnp.exp(-jnp.abs(x)))


def _rms(x, g, eps=RMS_EPS):
    return x * lax.rsqrt(jnp.mean(x * x, axis=-1, keepdims=True) + eps) * g


def _fn_rmsnorm(x, g):
    return (_rms(x, g).astype(BF16),)


def _fn_swiglu(gate, up):
    return ((gate * _sigmoid(gate) * up).astype(BF16),)


def _make_fn_lora_mix(p1, p2):
    def fn(u, u_prev, mu):
        um = u + (u_prev - u) * mu
        return (jnp.tanh(um[:, :p1]).astype(BF16), um[:, p1:p1 + p2].astype(BF16),
                _sigmoid(um[:, p1 + p2:]).astype(BF16))
    return fn


def _fn_rwkv_prep(ru, ru_p, ku, ku_p, vu, vu_p, w_lin, a_lin, mu_r, mu_k, mu_v, w0, a0, k_k, k_a):
    r = ru + (ru_p - ru) * mu_r
    k = ku + (ku_p - ku) * mu_k
    v = vu + (vu_p - vu) * mu_v
    w_log = -_softplus(-(w0 + w_lin)) - 0.5
    lw = -jnp.exp(w_log)
    a = _sigmoid(a0 + a_lin)
    kk = k * k_k
    kk = kk / jnp.maximum(jnp.sqrt(jnp.sum(kk * kk, axis=-1, keepdims=True)), 1e-12)
    k_mod = k * (1.0 + (a - 1.0) * k_a)
    return r, lw, k_mod, v, kk, kk * a


def _fn_rwkv_post(y, r, k_mod, v, g, lnx_g, lnx_b, r_k):
    mean = jnp.mean(y, axis=-1, keepdims=True)
    yc = y - mean
    var = jnp.mean(yc * yc, axis=-1, keepdims=True)
    yn = yc * lax.rsqrt(var + GN_EPS) * lnx_g + lnx_b
    bonus = jnp.sum(r * k_mod * r_k, axis=-1, keepdims=True) * v
    return (((yn + bonus) * g).astype(BF16),)


def _fn_fox_prep(q, k, qg, kg):
    return _rms(q, qg), _rms(k, kg)


def _fn_log_forget(f_raw, b):
    x = f_raw + b
    return (jnp.minimum(x, 0.0) - jnp.log(1.0 + jnp.exp(-jnp.abs(x))),)


def _fn_final(z, e_raw, h2, target, gate_b, ple_g):
    gate = _sigmoid(z + gate_b)
    out = h2 + gate * _rms(e_raw, ple_g)
    err = out - target
    return (0.5 * jnp.mean(err * err, axis=-1, keepdims=True),)


def _dot_bf16(a, b, ca, cb):
    return lax.dot_general(a.astype(BF16), b.astype(BF16), (((ca,), (cb,)), ((0,), (0,))),
                           preferred_element_type=F32)


@functools.partial(jax.custom_vjp, nondiff_argnums=(2, 3))
def _dot(a, b, ca, cb):
    return _dot_bf16(a, b, ca, cb)


def _dot_fwd(a, b, ca, cb):
    return _dot_bf16(a, b, ca, cb), (a, b)


def _dot_bwd(ca, cb, res, g):
    a, b = res
    ia, jb = 3 - ca, 3 - cb
    da = _dot_bf16(g, b, 2, jb) if ca == 2 else _dot_bf16(b, g, jb, 2)
    db = _dot_bf16(a, g, ia, 1) if cb == 1 else _dot_bf16(g, a, 1, ia)
    return da, db


_dot.defvjp(_dot_fwd, _dot_bwd)


def _scan_chunk(S0, r, lw, k, v, kk, b):
    B, L, _ = r.shape
    row = lax.broadcasted_iota(jnp.int32, (B, L, L), 1)
    col = lax.broadcasted_iota(jnp.int32, (B, L, L), 2)
    incl = col <= row
    strict = col < row
    cum = lax.dot_general(incl.astype(F32), lw, (((2,), (1,)), ((0,), (0,))), precision=HIGHEST,
                          preferred_element_type=F32)
    g_in, g_ex, g_inv = jnp.exp(cum), jnp.exp(cum - lw), jnp.exp(-cum)
    kkg, kd, bd, rg = kk * g_ex, k * g_inv, b * g_inv, r * g_in
    a_k = jnp.where(strict, _dot(kkg, kd, 2, 2), 0.0)
    a_b = jnp.where(strict, _dot(kkg, bd, 2, 2), 0.0)
    pw = -a_b
    inv = (row == col).astype(F32) + pw
    for _ in range(int(math.log2(L)) - 1):
        pw = _dot(pw, pw, 2, 1)
        inv = inv + _dot(inv, pw, 2, 1)
    sa = -_dot(inv, _dot(kkg, S0, 2, 2) + _dot(a_k, v, 2, 1), 2, 1)
    r_k = jnp.where(incl, _dot(rg, kd, 2, 2), 0.0)
    r_b = jnp.where(incl, _dot(rg, bd, 2, 2), 0.0)
    y = _dot(rg, S0, 2, 2) + _dot(r_k, v, 2, 1) + _dot(r_b, sa, 2, 1)
    g_end = jnp.exp(jnp.sum(lw, axis=1, keepdims=True))
    S1 = S0 * g_end + _dot(v, kd * g_end, 1, 1) + _dot(sa, bd * g_end, 1, 1)
    return y, S1


def _scan_heads_per_step(H):
    return next(hb for hb in (SCAN_HEADS_PER_STEP, 2, 1) if H % hb == 0)


def _scan_fwd_call(r, lw, k, v, kk, b):
    H, T, N = r.shape
    L = SCAN_CHUNK
    n_chunks = T // L
    hb = _scan_heads_per_step(H)

    def body(r_ref, lw_ref, k_ref, v_ref, kk_ref, b_ref, y_ref, s0_ref, state):
        @pl.when(pl.program_id(1) == 0)
        def _():
            state[...] = jnp.zeros_like(state)

        S0 = state[...]
        s0_ref[:, 0] = S0
        y, S1 = _scan_chunk(S0, r_ref[...], lw_ref[...], k_ref[...], v_ref[...], kk_ref[...], b_ref[...])
        y_ref[...] = y
        state[...] = S1

    blk = pl.BlockSpec((hb, L, N), lambda h, c: (h, c, 0))
    return pl.pallas_call(
        body, name="rwkv_scan_fwd", grid=(H // hb, n_chunks),
        in_specs=[blk] * 6,
        out_specs=[blk, pl.BlockSpec((hb, 1, N, N), lambda h, c: (h, c, 0, 0))],
        out_shape=[jax.ShapeDtypeStruct((H, T, N), F32), jax.ShapeDtypeStruct((H, n_chunks, N, N), F32)],
        scratch_shapes=[pltpu.VMEM((hb, N, N), F32)],
        compiler_params=pltpu.CompilerParams(dimension_semantics=("parallel", "arbitrary")),
    )(r, lw, k, v, kk, b)


def _scan_bwd_call(r, lw, k, v, kk, b, s0s, dy):
    H, T, N = r.shape
    L = SCAN_CHUNK
    n_chunks = T // L
    hb = _scan_heads_per_step(H)

    def body(r_ref, lw_ref, k_ref, v_ref, kk_ref, b_ref, s0_ref, dy_ref, dr, dlw, dk, dv, dkk, db, d_state):
        @pl.when(pl.program_id(1) == 0)
        def _():
            d_state[...] = jnp.zeros_like(d_state)

        _, vjp_fn = jax.vjp(_scan_chunk, s0_ref[:, 0], r_ref[...], lw_ref[...], k_ref[...], v_ref[...], kk_ref[...],
                            b_ref[...])
        grads = vjp_fn((dy_ref[...], d_state[...]))
        d_state[...] = grads[0]
        for o_ref, g in zip((dr, dlw, dk, dv, dkk, db), grads[1:]):
            o_ref[...] = g

    blk = pl.BlockSpec((hb, L, N), lambda h, c: (h, n_chunks - 1 - c, 0))
    return pl.pallas_call(
        body, name="rwkv_scan_bwd", grid=(H // hb, n_chunks),
        in_specs=[blk] * 6 + [pl.BlockSpec((hb, 1, N, N), lambda h, c: (h, n_chunks - 1 - c, 0, 0)), blk],
        out_specs=[blk] * 6,
        out_shape=[jax.ShapeDtypeStruct((H, T, N), F32)] * 6,
        scratch_shapes=[pltpu.VMEM((hb, N, N), F32)],
        compiler_params=pltpu.CompilerParams(dimension_semantics=("parallel", "arbitrary")),
    )(r, lw, k, v, kk, b, s0s, dy)


@jax.custom_vjp
def _rwkv_scan(r, lw, k, v, kk, b):
    return _scan_fwd_call(r, lw, k, v, kk, b)[0]


def _rwkv_scan_fwd(r, lw, k, v, kk, b):
    y, s0s = _scan_fwd_call(r, lw, k, v, kk, b)
    return y, (r, lw, k, v, kk, b, s0s)


def _rwkv_scan_bwd(res, dy):
    return tuple(_scan_bwd_call(*res, dy))


_rwkv_scan.defvjp(_rwkv_scan_fwd, _rwkv_scan_bwd)


def _nt(a, b):
    return lax.dot_general(a.astype(BF16), b.astype(BF16), (((1,), (1,)), ((), ())), preferred_element_type=F32)


def _nn(a, b):
    return lax.dot_general(a.astype(BF16), b.astype(BF16), (((1,), (0,)), ((), ())), preferred_element_type=F32)


def _tn(a, b):
    return lax.dot_general(a.astype(BF16), b.astype(BF16), (((0,), (0,)), ((), ())), preferred_element_type=F32)


def _attn_scores(qs, k_ref, cr_ref, row_bias, j, kb, q0, masked):
    ks = pl.multiple_of(j * kb, kb)
    kj = k_ref[0, pl.ds(ks, kb), :]
    s = _nt(qs, kj) + row_bias - cr_ref[0, j]
    if masked:
        qi = q0 + lax.broadcasted_iota(jnp.int32, s.shape, 0)
        ki = ks + lax.broadcasted_iota(jnp.int32, s.shape, 1)
        s = jnp.where(ki <= qi, s, -jnp.inf)
    return s, kj, ks


def _attn_specs(T, N, bq, kb):
    q_spec = pl.BlockSpec((1, bq, N), lambda h, i: (h, i, 0))
    kv_spec = pl.BlockSpec((1, T, N), lambda h, i: (h, 0, 0))
    col_spec = pl.BlockSpec((1, bq, 1), lambda h, i: (h, i, 0))
    row_spec = pl.BlockSpec((1, T // kb, 1, kb), lambda h, i: (h, 0, 0, 0))
    return q_spec, kv_spec, col_spec, row_spec


def _grid_marks(H, n_q):
    h, i = pl.program_id(0), pl.program_id(1)
    return (h == 0) & (i == 0), (h == H // 2) & (i == 0), (h == H - 1) & (i == n_q - 1)


def _attn_fwd_call(q, k, v, c_col, c_rows, gather_xs):
    H, T, N = q.shape
    bq, kb = min(ATTN_BLOCK_Q, T), c_rows.shape[3]
    q_spec, kv_spec, col_spec, row_spec = _attn_specs(T, N, bq, kb)
    n = len(gather_xs)

    def body(q_ref, k_ref, v_ref, cc_ref, cr_ref, *rest):
        x_refs, (o_ref, o32_ref, lse_ref), out_refs, sems = rest[:n], rest[n:n + 3], rest[n + 3:2 * n + 3], rest[2 * n + 3:]
        first, middle, last = _grid_marks(H, T // bq)
        start, relay, finish = _gather_phases(x_refs, out_refs, *sems)
        pl.when(first)(start)
        pl.when(middle)(relay)
        q0 = pl.program_id(1) * bq
        qs = (q_ref[0] * (HEAD_DIM ** -0.5)).astype(BF16)
        cc = cc_ref[0]

        def step(j, carry, masked):
            m, l, acc = carry
            s, _, ks = _attn_scores(qs, k_ref, cr_ref, cc, j, kb, q0, masked)
            m_new = jnp.maximum(m, jnp.max(s, axis=-1, keepdims=True))
            alpha = jnp.exp(m - m_new)
            p = jnp.exp(s - m_new)
            l = alpha * l + jnp.sum(p, axis=-1, keepdims=True)
            p_hi = p.astype(BF16)
            p_lo = p - p_hi.astype(F32)
            vj = v_ref[0, pl.ds(ks, kb), :]
            acc = alpha * acc + (_nn(p_hi, vj) + _nn(p_lo, vj))
            return m_new, l, acc

        n_full = q0 // kb
        init = (jnp.full((bq, 1), -jnp.inf, F32), jnp.zeros((bq, 1), F32), jnp.zeros((bq, N), F32))
        carry = lax.fori_loop(0, n_full, functools.partial(step, masked=False), init)
        m, l, acc = step(n_full, carry, masked=True)
        o = acc / l
        o_ref[0] = o.astype(o_ref.dtype)
        o32_ref[0] = o
        lse_ref[0] = m + jnp.log(l)
        pl.when(last)(finish)

    any_spec = pl.BlockSpec(memory_space=pl.ANY)
    return pl.pallas_call(
        body, name="fox_attn_fwd", grid=(H, T // bq),
        in_specs=[q_spec, kv_spec, kv_spec, col_spec, row_spec] + [any_spec] * n,
        out_specs=[q_spec, q_spec, col_spec] + [any_spec] * n,
        out_shape=[jax.ShapeDtypeStruct((H, T, N), BF16),
                   jax.ShapeDtypeStruct((H, T, N), F32),
                   jax.ShapeDtypeStruct((H, T, 1), F32)] + _gather_out_shapes(gather_xs),
        scratch_shapes=_comm_semaphores(n),
        compiler_params=pltpu.CompilerParams(dimension_semantics=("arbitrary", "arbitrary"),
                                             vmem_limit_bytes=VMEM_LIMIT),
    )(q, k, v, c_col, c_rows, *gather_xs)


def _attn_bwd_call(q, k, v, c_col, c_rows, o, lse, do, scatter_parts):
    H, T, N = q.shape
    bq, kb = min(ATTN_BLOCK_Q, T), c_rows.shape[3]
    q_spec, kv_spec, col_spec, row_spec = _attn_specs(T, N, bq, kb)
    n = len(scatter_parts)

    def body(q_ref, k_ref, v_ref, cc_ref, cr_ref, o_ref, lse_ref, do_ref, *rest):
        a_refs, (dq_ref, dk_ref, dv_ref, dcr_ref), b_refs, sems = rest[:n], rest[n:n + 4], rest[n + 4:2 * n + 4], rest[2 * n + 4:]
        first, _, last = _grid_marks(H, T // bq)
        start, finish = _scatter_phases(a_refs, b_refs, *sems)
        pl.when(first)(start)
        i = pl.program_id(1)
        q0 = i * bq

        @pl.when(i == 0)
        def _():
            dk_ref[...] = jnp.zeros_like(dk_ref)
            dv_ref[...] = jnp.zeros_like(dv_ref)
            dcr_ref[...] = jnp.zeros_like(dcr_ref)

        qs = (q_ref[0] * (HEAD_DIM ** -0.5)).astype(BF16)
        do = do_ref[0]
        delta = jnp.sum(do.astype(F32) * o_ref[0], axis=-1, keepdims=True)
        row_bias = cc_ref[0] - lse_ref[0]

        def step(j, dq, masked):
            s, kj, ks = _attn_scores(qs, k_ref, cr_ref, row_bias, j, kb, q0, masked)
            p = jnp.exp(s)
            ds = p * (_nt(do, v_ref[0, pl.ds(ks, kb), :]) - delta)
            ds_b = ds.astype(BF16)
            dk_ref[0, pl.ds(ks, kb), :] += _tn(ds_b, qs)
            dv_ref[0, pl.ds(ks, kb), :] += _tn(p, do)
            dcr_ref[0, j] -= jnp.sum(ds, axis=0, keepdims=True)
            return dq + _nn(ds_b, kj)

        n_full = q0 // kb
        dq = lax.fori_loop(0, n_full, functools.partial(step, masked=False), jnp.zeros((bq, N), F32))
        dq_ref[0] = step(n_full, dq, masked=True) * (HEAD_DIM ** -0.5)
        pl.when(last)(finish)

    any_spec = pl.BlockSpec(memory_space=pl.ANY)
    return pl.pallas_call(
        body, name="fox_attn_bwd", grid=(H, T // bq),
        in_specs=[q_spec, kv_spec, kv_spec, col_spec, row_spec, q_spec, col_spec, q_spec] + [any_spec] * n,
        out_specs=[q_spec, kv_spec, kv_spec, row_spec] + [any_spec] * n,
        out_shape=[jax.ShapeDtypeStruct((H, T, N), F32)] * 3 + [jax.ShapeDtypeStruct(c_rows.shape, F32)]
        + [jax.ShapeDtypeStruct(a.shape, a.dtype) for a in scatter_parts],
        scratch_shapes=_comm_semaphores(n),
        compiler_params=pltpu.CompilerParams(dimension_semantics=("arbitrary", "arbitrary"),
                                             vmem_limit_bytes=VMEM_LIMIT),
    )(q, k, v, c_col, c_rows, o, lse, do, *scatter_parts)


@jax.custom_vjp
def _fox_attn(q, k, v, c, late_blocks, carrier):
    return _fox_attn_fwd(q, k, v, c, late_blocks, carrier)[0]


def _attn_bias_views(c):
    H, T = c.shape
    kb = min(ATTN_BLOCK_K, T)
    return c[:, :, None], c.reshape(H, T // kb, 1, kb)


def _fox_attn_fwd(q, k, v, c, late_blocks, carrier):
    o, o32, lse, *gathered = _attn_fwd_call(q, k, v, *_attn_bias_views(c), late_blocks)
    return (o, tuple(gathered)), (q, k, v, c, o32, lse)


def _fox_attn_bwd(res, cts):
    q, k, v, c, o32, lse = res
    do, d_gathered = cts
    dq, dk, dv, dc_rows, *parts = _attn_bwd_call(q, k, v, *_attn_bias_views(c), o32, lse, do, d_gathered)
    no_grad = tuple(jnp.zeros(a.shape[1:], a.dtype) for a in parts)
    return dq, dk, dv, dc_rows.reshape(c.shape), no_grad, tuple(parts)


_fox_attn.defvjp(_fox_attn_fwd, _fox_attn_bwd)


N_PEERS = N_DEV - 1


def _all_gather(xs, name):
    n = len(xs)

    def body(*refs):
        start, relay, finish = _gather_phases(refs[:n], refs[n:2 * n], *refs[2 * n:])
        start()
        relay()
        finish()

    any_spec = pl.BlockSpec(memory_space=pl.ANY)
    return pl.pallas_call(
        body, name=name,
        out_shape=_gather_out_shapes(xs),
        in_specs=[any_spec] * n, out_specs=[any_spec] * n,
        scratch_shapes=_comm_semaphores(n),
    )(*xs)


def _gather_out_shapes(xs):
    return [jax.ShapeDtypeStruct((N_DEV,) + x.shape, x.dtype) for x in xs]


def _comm_semaphores(n):
    return [pltpu.SemaphoreType.DMA((N_PEERS * n,)), pltpu.SemaphoreType.DMA((N_PEERS * n,)),
            pltpu.SemaphoreType.DMA((n,))]


def _gather_phases(x_refs, out_refs, send_sems, recv_sems, local_sems):
    n = len(x_refs)
    x_, y_, c_ = lax.axis_index("x"), lax.axis_index("y"), lax.axis_index("c")
    me, sibling = (x_, y_, c_), (x_, y_, 1 - c_)
    chips = [(1 - x_, y_), (x_, 1 - y_), (1 - x_, 1 - y_)]

    def slot(t, px, py, pc):
        return out_refs[t].at[4 * px + 2 * py + pc]

    def copy(t, k, block, to, src=None):
        return pltpu.make_async_remote_copy(
            src_ref=slot(t, *block) if src is None else src, dst_ref=slot(t, *block),
            send_sem=send_sems.at[k * n + t], recv_sem=recv_sems.at[k * n + t],
            device_id=to, device_id_type=pl.DeviceIdType.MESH)

    def mine():
        return [pltpu.make_async_copy(x_refs[t], slot(t, *me), local_sems.at[t]) for t in range(n)]

    def first():
        return ([copy(t, 0, me, sibling, src=x_refs[t]) for t in range(n)]
                + [copy(t, 1 + j, me, (*chip, c_), src=x_refs[t]) for j, chip in enumerate(chips) for t in range(n)])

    def passed():
        return [copy(t, 4 + j, (*chip, c_), sibling) for j, chip in enumerate(chips) for t in range(n)]

    def start():
        for cp in mine() + first():
            cp.start()

    def relay():
        for j, chip in enumerate(chips):
            for t in range(n):
                copy(t, 1 + j, (*chip, c_), me).wait_recv()
                copy(t, 4 + j, (*chip, c_), sibling).start()

    def finish():
        for t in range(n):
            copy(t, 0, sibling, me).wait_recv()
        for j, chip in enumerate(chips):
            for t in range(n):
                copy(t, 4 + j, (*chip, 1 - c_), me).wait_recv()
        for cp in first() + passed():
            cp.wait_send()
        for cp in mine():
            cp.wait()

    return start, relay, finish


def _all_to_all(parts, name):
    n = len(parts)

    def body(*refs):
        start, finish = _scatter_phases(refs[:n], refs[n:2 * n], *refs[2 * n:])
        start()
        finish()

    any_spec = pl.BlockSpec(memory_space=pl.ANY)
    return pl.pallas_call(
        body, name=name,
        out_shape=[jax.ShapeDtypeStruct(a.shape, a.dtype) for a in parts],
        in_specs=[any_spec] * n, out_specs=[any_spec] * n,
        scratch_shapes=_comm_semaphores(n),
    )(*parts)


def _scatter_phases(a_refs, b_refs, send_sems, recv_sems, local_sems):
    n = len(a_refs)
    x_, y_, c_ = lax.axis_index("x"), lax.axis_index("y"), lax.axis_index("c")
    me_idx = 4 * x_ + 2 * y_ + c_

    def copies():
        out = [pltpu.make_async_copy(a_refs[t].at[me_idx], b_refs[t].at[me_idx], local_sems.at[t]) for t in range(n)]
        for rel in range(1, N_DEV):
            px = 1 - x_ if rel & 4 else x_
            py = 1 - y_ if rel & 2 else y_
            pc = 1 - c_ if rel & 1 else c_
            for t in range(n):
                out.append(pltpu.make_async_remote_copy(
                    src_ref=a_refs[t].at[4 * px + 2 * py + pc], dst_ref=b_refs[t].at[me_idx],
                    send_sem=send_sems.at[(rel - 1) * n + t], recv_sem=recv_sems.at[(rel - 1) * n + t],
                    device_id=(px, py, pc), device_id_type=pl.DeviceIdType.MESH))
        return out

    def start():
        for cp in copies():
            cp.start()

    def finish():
        for cp in copies():
            cp.wait()

    return start, finish


def _reduce_adamw(parts, w, m, v, name):
    R, C = w.shape
    tr = max(t for t in (256, 128, PACK_ROW_QUANTUM) if R % t == 0)

    def body(p_ref, w_ref, m_ref, v_ref, g_out, d_out, m_out, v_out):
        g = p_ref[0]
        for i in range(1, N_DEV):
            g = g + p_ref[i]
        m_new = ADAM_B1 * m_ref[...] + (1.0 - ADAM_B1) * g
        v_new = ADAM_B2 * v_ref[...] + (1.0 - ADAM_B2) * (g * g)
        m_hat = m_new / (1.0 - ADAM_B1 ** ADAM_STEP)
        v_hat = v_new / (1.0 - ADAM_B2 ** ADAM_STEP)
        g_out[...] = g
        d_out[...] = -ADAM_LR * (m_hat / (jnp.sqrt(v_hat) + ADAM_EPS) + ADAM_WD * w_ref[...])
        m_out[...] = m_new
        v_out[...] = v_new

    spec = pl.BlockSpec((tr, C), lambda i: (i, 0))
    return pl.pallas_call(
        body, name=name, grid=(R // tr,),
        in_specs=[pl.BlockSpec((N_DEV, tr, C), lambda i: (0, i, 0)), spec, spec, spec],
        out_specs=[spec] * 4,
        out_shape=[jax.ShapeDtypeStruct((R, C), F32)] * 4,
        compiler_params=pltpu.CompilerParams(dimension_semantics=("parallel",), vmem_limit_bytes=VMEM_LIMIT),
    )(parts, w, m, v)


def _pack(arrays, dtype):
    flat = jnp.concatenate([a.reshape(-1).astype(dtype) for a in arrays])
    rows = _round_up(-(-flat.shape[0] // PACK_COLS), PACK_ROW_QUANTUM)
    flat = jnp.pad(flat, (0, rows * PACK_COLS - flat.shape[0]))
    return flat.reshape(rows, PACK_COLS)


def _unpack(packed, shapes):
    lead = packed.shape[:-2]
    flat = packed.reshape(lead + (-1,))
    out, off = [], 0
    for s in shapes:
        n = math.prod(s)
        out.append(flat[..., off:off + n].reshape(lead + tuple(s)))
        off += n
    return out


def _travel_layout(name, block):
    return block.T if SHARDED[name] else block


REDUCE_BLOCK_BYTES = 4 * 1024 * 1024


def _reduce_parts(parts, name):
    _, R, C = parts.shape
    per_col = N_DEV * R * parts.dtype.itemsize
    tc = next((t for t in range(C - C % LANES, 0, -LANES) if C % t == 0 and t * per_col <= REDUCE_BLOCK_BYTES), C)

    def body(p_ref, o_ref):
        g = p_ref[0].astype(F32)
        for i in range(1, N_DEV):
            g = g + p_ref[i].astype(F32)
        o_ref[...] = g

    return pl.pallas_call(
        body, name=name, grid=(C // tc,),
        in_specs=[pl.BlockSpec((N_DEV, R, tc), lambda j: (0, 0, j))],
        out_specs=pl.BlockSpec((R, tc), lambda j: (0, j)),
        out_shape=jax.ShapeDtypeStruct((R, C), F32),
        compiler_params=pltpu.CompilerParams(dimension_semantics=("parallel",), vmem_limit_bytes=VMEM_LIMIT),
    )(parts)


def _adamw(g, w, m, v, name):
    R, C = w.shape
    tr = next((t for t in (512, 256, 128, 64, 32, 16, 8) if R % t == 0 and t * C * 4 <= 2 * 1024 * 1024), R)

    def body(g_ref, w_ref, m_ref, v_ref, d_out, m_out, v_out):
        g_ = g_ref[...]
        m_new = ADAM_B1 * m_ref[...] + (1.0 - ADAM_B1) * g_
        v_new = ADAM_B2 * v_ref[...] + (1.0 - ADAM_B2) * (g_ * g_)
        m_hat = m_new / (1.0 - ADAM_B1 ** ADAM_STEP)
        v_hat = v_new / (1.0 - ADAM_B2 ** ADAM_STEP)
        d_out[...] = -ADAM_LR * (m_hat / (jnp.sqrt(v_hat) + ADAM_EPS) + ADAM_WD * w_ref[...])
        m_out[...] = m_new
        v_out[...] = v_new

    spec = pl.BlockSpec((tr, C), lambda i: (i, 0))
    return pl.pallas_call(
        body, name=name, grid=(R // tr,),
        in_specs=[spec] * 4, out_specs=[spec] * 3,
        out_shape=[jax.ShapeDtypeStruct((R, C), F32)] * 3,
        compiler_params=pltpu.CompilerParams(dimension_semantics=("parallel",), vmem_limit_bytes=VMEM_LIMIT),
    )(g, w, m, v)


def _to_heads(u):
    return jnp.transpose(u.reshape(u.shape[0], -1, HEAD_DIM), (1, 0, 2))


def _split_cols(u, n):
    width = u.shape[1] // n

    @jax.custom_vjp
    def run(u):
        return tuple(u[:, i * width:(i + 1) * width] for i in range(n))

    def fwd(u):
        return run(u), None

    def bwd(_, cts):
        return (jnp.concatenate(cts, axis=1),)

    run.defvjp(fwd, bwd)
    return run(u)


def _from_heads(uh):
    H, T, N = uh.shape
    return jnp.transpose(uh, (1, 0, 2)).reshape(T, H * N)


def _shift(uh):
    return jnp.pad(uh, ((0, 0), (1, 0), (0, 0)))[:, :-1]


def _pad_cols(a, width):
    return jnp.pad(a, ((0, 0), (0, width - a.shape[1])))


def _split_rows(w, sizes):
    offsets = [sum(sizes[:i]) for i in range(len(sizes))]

    @jax.custom_vjp
    def run(w):
        return tuple(w[o:o + s] for o, s in zip(offsets, sizes))

    def fwd(w):
        return run(w), None

    def bwd(_, cts):
        return (jnp.concatenate(cts, axis=0),)

    run.defvjp(fwd, bwd)
    return run(w)


def _pad_rows(a, height):
    return jnp.pad(a, ((0, height - a.shape[0]), (0, 0)))


def _vec(a):
    return a.reshape(1, 1, -1)


TM_WIDE = 128


def _mixing_half(W, small, x, late_blocks, carrier):
    T, D = x.shape
    vec = _vec
    tm_wide = TM_WIDE
    rw = small['w0'].shape[-1]
    fw = W['w_out'].shape[0] - rw
    heads_r, heads_f = rw // HEAD_DIM, fw // HEAD_DIM
    dl, al, gl = W['w2'].shape[1], W['a2'].shape[1], W['g2'].shape[1]
    dl_p, al_p, gl_p = _round_up(dl, LANES), _round_up(al, LANES), _round_up(gl, LANES)
    f_p = _round_up(heads_f, LANES)
    rwkv_cols = 3 * rw + dl + al + gl
    tm_head = 512 if T % 512 == 0 else T

    o_w, o_a, o_g = 3 * rw, 3 * rw + dl, 3 * rw + dl + al
    w_rkv, w_xw, w_xa, w_xg, w_qkv, w_fg = _split_rows(W['w_in'], (3 * rw, dl, al, gl, 3 * fw, heads_f))
    w_lora = jnp.concatenate([_pad_rows(w_xw, dl_p), _pad_rows(w_xa, al_p), _pad_rows(w_xg, gl_p)], axis=0)
    w_f = _pad_rows(w_fg, f_p)
    mu = small['shift_mu'].reshape(1, -1)
    mu_lora = jnp.concatenate([_pad_cols(mu[:, o_w:o_a], dl_p), _pad_cols(mu[:, o_a:o_g], al_p),
                               _pad_cols(mu[:, o_g:rwkv_cols], gl_p)], axis=1)

    def head_vec(a):
        return a.reshape(-1, 1, HEAD_DIM)

    (xn,) = _stage("attn_norm", _fn_rmsnorm, [x[None]], [vec(small['attn_norm_g'])], tm_wide)
    xn = xn[0]
    u_rkv = _mm_t(xn, w_rkv, "in_rkv")
    u_lora = _mm_t(xn, w_lora, "in_lora")
    u_qkv = _mm_t(xn, w_qkv, "in_qkv")
    f_raw = _mm_t(xn, w_f, "in_f")

    u_lora3 = u_lora[None]
    xw_t, xa_m, xg_s = _stage("lora_mix", _make_fn_lora_mix(dl_p, al_p), [u_lora3, _shift(u_lora3)],
                              [vec(mu_lora)], tm_wide)
    w_lin = _mm_t(xw_t[0], _pad_cols(W['w2'], dl_p), "w2")
    a_lin = _mm_t(xa_m[0], _pad_cols(W['a2'], al_p), "a2")
    gate_r = _mm_t(xg_s[0], _pad_cols(W['g2'], gl_p), "g2")
    ru, ku, vu = (_to_heads(u) for u in _split_cols(u_rkv, 3))
    w_lin_h, a_lin_h, gate_h = _to_heads(w_lin), _to_heads(a_lin), _to_heads(gate_r)
    mu_r, mu_k, mu_v = (head_vec(mu[:, i * rw:(i + 1) * rw]) for i in range(3))
    r, lw, k_mod, v, kk, b = _stage(
        "rwkv_prep", _fn_rwkv_prep,
        [ru, _shift(ru), ku, _shift(ku), vu, _shift(vu), w_lin_h, a_lin_h],
        [mu_r, mu_k, mu_v, head_vec(small['w0']), head_vec(small['a0']), head_vec(small['k_k']),
         head_vec(small['k_a'])], tm_head)
    y_scan = _rwkv_scan(r, lw, k_mod, v, kk, b)
    (y_rwkv,) = _stage("rwkv_post", _fn_rwkv_post, [y_scan, r, k_mod, v, gate_h],
                       [head_vec(small['lnx_g']), head_vec(small['lnx_b']), head_vec(small['r_k'])], tm_head)

    qu, kf, vf = (_to_heads(u) for u in _split_cols(u_qkv, 3))
    qg = jnp.broadcast_to(vec(small['q_norm_g']), (heads_f, 1, HEAD_DIM))
    kg = jnp.broadcast_to(vec(small['k_norm_g']), (heads_f, 1, HEAD_DIM))
    qn, kn = _stage("fox_prep", _fn_fox_prep, [qu, kf], [qg, kg], tm_head)
    fb = _pad_cols(small['fgate_b'].reshape(1, -1), f_p)
    (log_f,) = _stage("log_forget", _fn_log_forget, [f_raw[None]], [vec(fb)], tm_head)
    c = jnp.cumsum(log_f[0][:, :heads_f], axis=0).T
    y_fox, gathered_late = _fox_attn(qn, kn, vf, c, late_blocks, carrier)

    y_cat = jnp.concatenate([_from_heads(y_rwkv), _from_heads(y_fox)], axis=-1)
    return _mm_add(x, y_cat, W['w_out'], "out"), gathered_late


def _channel_half_loss(W, small, h1, p, target):
    vec = _vec
    tm_wide = TM_WIDE
    (hn,) = _stage("ffn_norm", _fn_rmsnorm, [h1[None]], [vec(small['ffn_norm_g'])], tm_wide)
    gate = _mm_t(hn[0], W['w_gate'], "gate")
    up = _mm_t(hn[0], W['w_up'], "up")
    (act,) = _stage("swiglu", _fn_swiglu, [gate[None], up[None]], [], tm_wide)
    h2 = _mm_add(h1, act[0], W['w_down'], "down")
    e_raw = _mm_t(p, W['ple_proj'], "ple_proj")
    (hg,) = _stage("ple_gate_norm", _fn_rmsnorm, [h2[None]], [vec(small['ple_gate_norm_g'])], tm_wide)
    z = _mm(hg[0], W['ple_gate_w'], "ple_gate")
    (loss_rows,) = _stage("final", _fn_final, [z[None], e_raw[None], h2[None], target[None]],
                          [vec(small['ple_gate_b']), vec(small['ple_norm_g'])], tm_wide)
    return jnp.sum(loss_rows)


def kernel(x, p, attn_norm_g, w_in, shift_mu, w0, w2, a0, a2, g2, k_k, k_a, r_k, lnx_g, lnx_b, q_norm_g, k_norm_g, fgate_b, w_out, ffn_norm_g, w_gate, w_up, w_down, ple_proj, ple_norm_g, ple_gate_norm_g, ple_gate_w, ple_gate_b, loss_target, m_attn_norm_g, m_w_in, m_shift_mu, m_w0, m_w2, m_a0, m_a2, m_g2, m_k_k, m_k_a, m_r_k, m_lnx_g, m_lnx_b, m_q_norm_g, m_k_norm_g, m_fgate_b, m_w_out, m_ffn_norm_g, m_w_gate, m_w_up, m_w_down, m_ple_proj, m_ple_norm_g, m_ple_gate_norm_g, m_ple_gate_w, m_ple_gate_b, v_attn_norm_g, v_w_in, v_shift_mu, v_w0, v_w2, v_a0, v_a2, v_g2, v_k_k, v_k_a, v_r_k, v_lnx_g, v_lnx_b, v_q_norm_g, v_k_norm_g, v_fgate_b, v_w_out, v_ffn_norm_g, v_w_gate, v_w_up, v_w_down, v_ple_proj, v_ple_norm_g, v_ple_gate_norm_g, v_ple_gate_w, v_ple_gate_b):
    weights = dict(zip(WEIGHT_NAMES, (attn_norm_g, w_in, shift_mu, w0, w2, a0, a2, g2, k_k, k_a, r_k, lnx_g, lnx_b,
                                      q_norm_g, k_norm_g, fgate_b, w_out, ffn_norm_g, w_gate, w_up, w_down, ple_proj,
                                      ple_norm_g, ple_gate_norm_g, ple_gate_w, ple_gate_b)))
    m_in = dict(zip(WEIGHT_NAMES, (m_attn_norm_g, m_w_in, m_shift_mu, m_w0, m_w2, m_a0, m_a2, m_g2, m_k_k, m_k_a, m_r_k,
                                   m_lnx_g, m_lnx_b, m_q_norm_g, m_k_norm_g, m_fgate_b, m_w_out, m_ffn_norm_g, m_w_gate,
                                   m_w_up, m_w_down, m_ple_proj, m_ple_norm_g, m_ple_gate_norm_g, m_ple_gate_w,
                                   m_ple_gate_b)))
    v_in = dict(zip(WEIGHT_NAMES, (v_attn_norm_g, v_w_in, v_shift_mu, v_w0, v_w2, v_a0, v_a2, v_g2, v_k_k, v_k_a, v_r_k,
                                   v_lnx_g, v_lnx_b, v_q_norm_g, v_k_norm_g, v_fgate_b, v_w_out, v_ffn_norm_g, v_w_gate,
                                   v_w_up, v_w_down, v_ple_proj, v_ple_norm_g, v_ple_gate_norm_g, v_ple_gate_w,
                                   v_ple_gate_b)))
    small_shapes = [weights[n].shape for n in SMALL_NAMES]
    small = {n: weights[n] for n in SMALL_NAMES}

    def whole(stacks, names):
        return {n: g.reshape(N_DEV * g.shape[1], g.shape[2]) for n, g in zip(names, stacks)}

    def stacked(tree, names, like):
        return tuple(tree[n].reshape(g.shape) for n, g in zip(names, like))

    travelling = {n: _travel_layout(n, weights[n][0]).astype(BF16) for n in SHARDED_NAMES}
    gathered_early = _all_gather([travelling[n] for n in EARLY_NAMES], "gather_weights")
    late_blocks = tuple(travelling[n] for n in LATE_NAMES)
    carrier = tuple(jnp.zeros((N_DEV,) + b.shape, b.dtype) for b in late_blocks)

    (h1, gathered_late), mixing_vjp = jax.vjp(_mixing_half, whole(gathered_early, EARLY_NAMES), small, x[0],
                                              late_blocks, carrier)
    loss_local, (d_late, d_small_b, d_h1) = jax.value_and_grad(_channel_half_loss, argnums=(0, 1, 2))(
        whole(gathered_late, LATE_NAMES), small, h1, p[0, 0], loss_target[0])
    d_early, d_small_a, d_x, _, parts_late = mixing_vjp((d_h1, stacked(d_late, LATE_NAMES, gathered_late)))
    d_small = {n: d_small_a[n] + d_small_b[n] for n in SMALL_NAMES}
    loss = lax.psum(loss_local, MESH_AXES)

    parts_early = _all_to_all(stacked(d_early, EARLY_NAMES, gathered_early), "scatter_grads")
    parts = dict(zip(EARLY_NAMES + LATE_NAMES, list(parts_early) + list(parts_late)))
    (small_parts,) = _all_gather([_pack([d_small[n] for n in SMALL_NAMES], F32)], "gather_small_grads")

    def pack_f32(tree, names):
        return _pack([tree[n] for n in names], F32)

    sml = _reduce_adamw(small_parts, pack_f32(weights, SMALL_NAMES), pack_f32(m_in, SMALL_NAMES),
                        pack_f32(v_in, SMALL_NAMES), "adamw_replicated")
    by_kind = [dict(zip(SMALL_NAMES, _unpack(sml[kind], small_shapes))) for kind in range(4)]
    for n in SHARDED_NAMES:
        g = _travel_layout(n, _reduce_parts(parts[n], "reduce_" + n))
        upd = _adamw(g, weights[n][0], m_in[n][0], v_in[n][0], "adamw_" + n)
        for kind, val in enumerate((g, *upd)):
            by_kind[kind][n] = val[None]
    outs = [by_kind[kind][n] for kind in range(4) for n in WEIGHT_NAMES]
    return (loss, d_x[None], *outs)
```

```python
import functools
import math

import jax
import jax.numpy as jnp
from jax import lax
from jax.experimental import pallas as pl
from jax.experimental.pallas import tpu as pltpu

F32 = jnp.float32
BF16 = jnp.bfloat16
HIGHEST = lax.Precision.HIGHEST

N_DEV = 8
MESH_AXES = ("x", "y", "c")
HEAD_DIM = 64
SCAN_CHUNK = 64
SCAN_HEADS_PER_STEP = 16
ATTN_BLOCK_Q = 512
ATTN_BLOCK_K = 512
LANES = 128
HEAD_PAIR = 2 * HEAD_DIM
PACK_COLS = 1024
PACK_ROW_QUANTUM = 64
VMEM_LIMIT = 48 * 1024 * 1024
RMS_EPS = 1e-6
GN_EPS = 64e-5
ADAM_LR, ADAM_B1, ADAM_B2, ADAM_EPS, ADAM_WD, ADAM_STEP = 0.001, 0.9, 0.999, 1e-08, 0.01, 10

WEIGHT_NAMES = ['attn_norm_g', 'w_in', 'shift_mu', 'w0', 'w2', 'a0', 'a2', 'g2', 'k_k', 'k_a', 'r_k', 'lnx_g', 'lnx_b',
                'q_norm_g', 'k_norm_g', 'fgate_b', 'w_out', 'ffn_norm_g', 'w_gate', 'w_up', 'w_down', 'ple_proj',
                'ple_norm_g', 'ple_gate_norm_g', 'ple_gate_w', 'ple_gate_b']
SHARDED = {'w_in': True, 'w2': True, 'a2': True, 'g2': True, 'w_out': False, 'w_gate': True, 'w_up': True,
           'w_down': False, 'ple_proj': True, 'ple_gate_w': False}
SHARDED_NAMES = [n for n in WEIGHT_NAMES if n in SHARDED]
SMALL_NAMES = [n for n in WEIGHT_NAMES if n not in SHARDED]
EARLY_NAMES = ['w_in', 'w2', 'a2', 'g2', 'w_out']
LATE_NAMES = [n for n in SHARDED_NAMES if n not in EARLY_NAMES]


def _round_up(n, q):
    return -(-n // q) * q


def _tile_candidates(n, cap):
    sizes = {t for t in range(LANES, min(n, cap) + 1, LANES) if n % t == 0}
    return sorted(sizes | ({n} if n <= cap or not sizes else set()))


MM_TILE_CAP = 2048
MM_VMEM_BUDGET = 38 * 1024 * 1024
MM_STEP_COST_BYTES = 1024 * 1024


def _mm_tiles(I, J, C, a_size, b_size, o_size):
    best = None
    for ti in _tile_candidates(I, MM_TILE_CAP):
        for tj in _tile_candidates(J, MM_TILE_CAP):
            for tc in _tile_candidates(C, MM_TILE_CAP):
                n_c = C // tc
                blocks = 2 * (ti * tc * a_size + tc * tj * b_size + ti * tj * o_size)
                temporaries = (ti * tc + tc * tj) * 2 + ti * tj * 4 * (2 if n_c > 1 else 1)
                if blocks + temporaries > MM_VMEM_BUDGET:
                    continue
                traffic = (I * C * a_size * (1 if n_c == 1 else J // tj) + C * J * b_size * (I // ti)
                           + I * J * o_size)
                cost = traffic + (I // ti) * (J // tj) * n_c * MM_STEP_COST_BYTES
                if best is None or cost < best[0]:
                    best = (cost, ti, tj, tc)
    return best[1:]


def _mm_call(a, b, mode, name, out_dtype, addend=None):
    if mode == "nn":
        (I, C), (_, J) = a.shape, b.shape
    elif mode == "nt":
        (I, C), (J, _) = a.shape, b.shape
    else:
        (C, I), (_, J) = a.shape, b.shape
    ti, tj, tc = _mm_tiles(I, J, C, a.dtype.itemsize, b.dtype.itemsize, jnp.dtype(out_dtype).itemsize)
    n_c = C // tc
    if mode == "nn":
        a_spec = pl.BlockSpec((ti, tc), lambda i, j, c: (i, c))
        b_spec = pl.BlockSpec((tc, tj), lambda i, j, c: (c, j))
        dims = (((1,), (0,)), ((), ()))
    elif mode == "nt":
        a_spec = pl.BlockSpec((ti, tc), lambda i, j, c: (i, c))
        b_spec = pl.BlockSpec((tj, tc), lambda i, j, c: (j, c))
        dims = (((1,), (1,)), ((), ()))
    else:
        a_spec = pl.BlockSpec((tc, ti), lambda i, j, c: (c, i))
        b_spec = pl.BlockSpec((tc, tj), lambda i, j, c: (c, j))
        dims = (((0,), (0,)), ((), ()))

    def product(a_ref, b_ref):
        return lax.dot_general(a_ref[...].astype(BF16), b_ref[...].astype(BF16), dims, preferred_element_type=F32)

    def finish(o_ref, r_refs, value):
        for r_ref in r_refs:
            value = r_ref[...] + value
        o_ref[...] = value.astype(o_ref.dtype)

    def body_single(a_ref, b_ref, *rest):
        finish(rest[-1], rest[:-1], product(a_ref, b_ref))

    def body_accumulate(a_ref, b_ref, *rest):
        (*r_refs, o_ref, acc), c = rest, pl.program_id(2)

        @pl.when(c == 0)
        def _():
            acc[...] = jnp.zeros_like(acc)

        acc[...] += product(a_ref, b_ref)

        @pl.when(c == n_c - 1)
        def _():
            finish(o_ref, r_refs, acc[...])

    out_spec = pl.BlockSpec((ti, tj), lambda i, j, c: (i, j))
    addends = [] if addend is None else [addend]
    return pl.pallas_call(
        body_single if n_c == 1 else body_accumulate, name=name, grid=(I // ti, J // tj, n_c),
        in_specs=[a_spec, b_spec] + [out_spec] * len(addends),
        out_specs=out_spec,
        out_shape=jax.ShapeDtypeStruct((I, J), out_dtype),
        scratch_shapes=[] if n_c == 1 else [pltpu.VMEM((ti, tj), F32)],
        compiler_params=pltpu.CompilerParams(dimension_semantics=("parallel", "parallel", "arbitrary"),
                                             vmem_limit_bytes=VMEM_LIMIT),
    )(a, b, *addends)


def _mm_add(residual, a, b, name):
    @jax.custom_vjp
    def run(residual, a, b):
        return _mm_call(a, b, "nn", "mm_" + name, F32, addend=residual)

    def fwd(residual, a, b):
        return run(residual, a, b), (a, b)

    def bwd(res, g):
        a, b = res
        return (g, _mm_call(g, b, "nt", "mm_" + name + "_da", a.dtype),
                _mm_call(a, g, "tn", "mm_" + name + "_db", b.dtype))

    run.defvjp(fwd, bwd)
    return run(residual, a, b)


def _mm(a, b, name):
    @jax.custom_vjp
    def run(a, b):
        return _mm_call(a, b, "nn", "mm_" + name, F32)

    def fwd(a, b):
        return run(a, b), (a, b)

    def bwd(res, g):
        a, b = res
        return (_mm_call(g, b, "nt", "mm_" + name + "_da", a.dtype),
                _mm_call(a, g, "tn", "mm_" + name + "_db", b.dtype))

    run.defvjp(fwd, bwd)
    return run(a, b)


def _mm_t(a, wt, name):
    @jax.custom_vjp
    def run(a, wt):
        return _mm_call(a, wt, "nt", "mmt_" + name, F32)

    def fwd(a, wt):
        return run(a, wt), (a, wt)

    def bwd(res, g):
        a, wt = res
        return (_mm_call(g, wt, "nn", "mmt_" + name + "_da", a.dtype),
                _mm_call(g, a, "tn", "mmt_" + name + "_dw", wt.dtype))

    run.defvjp(fwd, bwd)
    return run(a, wt)


def _stage_layout(rows, tm, cols):
    G, T, C = rows[0].shape
    if cols is None:
        return ((G, T // tm), lambda c: c, lambda c: pl.BlockSpec((1, tm, c), lambda g, t: (g, t, 0)),
                lambda c: pl.BlockSpec((1, 1, c), lambda g, t: (g, 0, 0)), lambda c: (G, T, c))
    return ((C // cols, T // tm), lambda c: cols, lambda c: pl.BlockSpec((1, tm, cols), lambda g, t: (0, t, g)),
            lambda c: pl.BlockSpec((1, 1, cols), lambda g, t: (0, 0, g)), lambda c: (1, T, C))


def _stage_fwd_call(name, fn, rows, params, tm, cols):
    grid, width, row_spec, par_spec, out_dims = _stage_layout(rows, tm, cols)
    nr, npar = len(rows), len(params)
    out_avals = jax.eval_shape(
        lambda *a: tuple(fn(*a)),
        *[jax.ShapeDtypeStruct((tm, width(r.shape[2])), r.dtype) for r in rows],
        *[jax.ShapeDtypeStruct((1, width(p.shape[2])), p.dtype) for p in params])

    def body(*refs):
        vals = [r[0] for r in refs[:nr + npar]]
        for o_ref, o in zip(refs[nr + npar:], fn(*vals)):
            o_ref[0] = o

    return pl.pallas_call(
        body, name=name, grid=grid,
        in_specs=[row_spec(r.shape[2]) for r in rows] + [par_spec(p.shape[2]) for p in params],
        out_specs=[row_spec(o.shape[1]) for o in out_avals],
        out_shape=[jax.ShapeDtypeStruct(out_dims(o.shape[1]), o.dtype) for o in out_avals],
        compiler_params=pltpu.CompilerParams(dimension_semantics=("parallel", "parallel"),
                                             vmem_limit_bytes=VMEM_LIMIT),
    )(*rows, *params)


def _stage_bwd_call(name, fn, rows, params, cts, tm, cols):
    grid, _, row_spec, par_spec, _ = _stage_layout(rows, tm, cols)
    nr, npar, nout = len(rows), len(params), len(cts)

    def body(*refs):
        vals = [r[0] for r in refs[:nr + npar]]
        ct_vals = tuple(r[0] for r in refs[nr + npar:nr + npar + nout])
        d_refs = refs[nr + npar + nout:]
        _, vjp_fn = jax.vjp(lambda *a: tuple(fn(*a)), *vals)
        grads = vjp_fn(ct_vals)
        for i in range(nr):
            d_refs[i][0] = grads[i]

        if npar:
            @pl.when(pl.program_id(1) == 0)
            def _():
                for j in range(npar):
                    d_refs[nr + j][...] = jnp.zeros_like(d_refs[nr + j])

        for j in range(npar):
            d_refs[nr + j][0] += grads[nr + j]

    outs = pl.pallas_call(
        body, name=name + "_bwd", grid=grid,
        in_specs=([row_spec(r.shape[2]) for r in rows] + [par_spec(p.shape[2]) for p in params]
                  + [row_spec(c.shape[2]) for c in cts]),
        out_specs=[row_spec(r.shape[2]) for r in rows] + [par_spec(p.shape[2]) for p in params],
        out_shape=([jax.ShapeDtypeStruct(r.shape, r.dtype) for r in rows]
                   + [jax.ShapeDtypeStruct(p.shape, p.dtype) for p in params]),
        compiler_params=pltpu.CompilerParams(dimension_semantics=("parallel", "arbitrary"),
                                             vmem_limit_bytes=VMEM_LIMIT),
    )(*rows, *params, *cts)
    return tuple(outs[:nr]), tuple(outs[nr:])


def _stage(name, fn, rows, params, tm, cols=None):
    @jax.custom_vjp
    def run(rows, params):
        return tuple(_stage_fwd_call(name, fn, rows, params, tm, cols))

    def fwd(rows, params):
        return run(rows, params), (rows, params)

    def bwd(res, cts):
        rows, params = res
        return _stage_bwd_call(name, fn, rows, params, tuple(cts), tm, cols)

    run.defvjp(fwd, bwd)
    return run(tuple(rows), tuple(params))


def _sigmoid(x):
    return 0.5 * (jnp.tanh(0.5 * x) + 1.0)


def _softplus(x):
    return jnp.maximum(x, 0.0) + jnp.log(1.0 + jnp.exp(-jnp.abs(x)))


def _rms(x, g, eps=RMS_EPS):
    return x * lax.rsqrt(jnp.mean(x * x, axis=-1, keepdims=True) + eps) * g


def _head_sum_pieces(x):
    n = x.shape[-1]
    same_head = (lax.broadcasted_iota(jnp.int32, (n, n), 0) // HEAD_DIM
                 == lax.broadcasted_iota(jnp.int32, (n, n), 1) // HEAD_DIM).astype(BF16)
    hi = x.astype(BF16)
    lo = (x - hi.astype(F32)).astype(BF16)
    dims = (((1,), (0,)), ((), ()))
    return (lax.dot_general(hi, same_head, dims, preferred_element_type=F32)
            + lax.dot_general(lo, same_head, dims, preferred_element_type=F32))


@jax.custom_vjp
def _head_sum(x):
    return _head_sum_pieces(x)


def _head_sum_fwd(x):
    return _head_sum_pieces(x), None


def _head_sum_bwd(_, g):
    return (_head_sum_pieces(g),)


_head_sum.defvjp(_head_sum_fwd, _head_sum_bwd)


def _head_mean(x):
    return _head_sum(x) * (1.0 / HEAD_DIM)


def _fn_rmsnorm(x, g):
    return (_rms(x, g).astype(BF16),)


def _fn_swiglu(gate, up):
    return ((gate * _sigmoid(gate) * up).astype(BF16),)


def _make_fn_lora_mix(p1, p2):
    def fn(u, u_prev, mu):
        um = u + (u_prev - u) * mu
        return (jnp.tanh(um[:, :p1]).astype(BF16), um[:, p1:p1 + p2].astype(BF16),
                _sigmoid(um[:, p1 + p2:]).astype(BF16))
    return fn


def _fn_rwkv_prep(ru, ru_p, ku, ku_p, vu, vu_p, w_lin, a_lin, mu_r, mu_k, mu_v, w0, a0, k_k, k_a):
    r = ru + (ru_p - ru) * mu_r
    k = ku + (ku_p - ku) * mu_k
    v = vu + (vu_p - vu) * mu_v
    w_log = -_softplus(-(w0 + w_lin)) - 0.5
    lw = -jnp.exp(w_log)
    a = _sigmoid(a0 + a_lin)
    kk = k * k_k
    kk = kk / jnp.maximum(jnp.sqrt(_head_sum(kk * kk)), 1e-12)
    k_mod = k * (1.0 + (a - 1.0) * k_a)
    return r, lw, k_mod, v, kk, kk * a


def _fn_rwkv_post(y, r, k_mod, v, g, lnx_g, lnx_b, r_k):
    yc = y - _head_mean(y)
    yn = yc * lax.rsqrt(_head_mean(yc * yc) + GN_EPS) * lnx_g + lnx_b
    bonus = _head_sum(r * k_mod * r_k) * v
    return (((yn + bonus) * g).astype(BF16),)


def _head_rms(x, g):
    return x * lax.rsqrt(_head_mean(x * x) + RMS_EPS) * g


def _fn_fox_prep(q, k, qg, kg):
    return _head_rms(q, qg), _head_rms(k, kg)


def _fn_log_forget(f_raw, b):
    x = f_raw + b
    return (jnp.minimum(x, 0.0) - jnp.log(1.0 + jnp.exp(-jnp.abs(x))),)


def _fn_final(z, e_raw, h2, target, gate_b, ple_g):
    gate = _sigmoid(z + gate_b)
    out = h2 + gate * _rms(e_raw, ple_g)
    err = out - target
    return (0.5 * jnp.mean(err * err, axis=-1, keepdims=True),)


def _dot_bf16(a, b, ca, cb):
    return lax.dot_general(a.astype(BF16), b.astype(BF16), (((ca,), (cb,)), ((0,), (0,))),
                           preferred_element_type=F32)


@functools.partial(jax.custom_vjp, nondiff_argnums=(2, 3))
def _dot(a, b, ca, cb):
    return _dot_bf16(a, b, ca, cb)


def _dot_fwd(a, b, ca, cb):
    return _dot_bf16(a, b, ca, cb), (a, b)


def _dot_bwd(ca, cb, res, g):
    a, b = res
    ia, jb = 3 - ca, 3 - cb
    da = _dot_bf16(g, b, 2, jb) if ca == 2 else _dot_bf16(b, g, jb, 2)
    db = _dot_bf16(a, g, ia, 1) if cb == 1 else _dot_bf16(g, a, 1, ia)
    return da, db


_dot.defvjp(_dot_fwd, _dot_bwd)


def _scan_chunk(S0, r, lw, k, v, kk, b):
    B, L, _ = r.shape
    row = lax.broadcasted_iota(jnp.int32, (B, L, L), 1)
    col = lax.broadcasted_iota(jnp.int32, (B, L, L), 2)
    incl = col <= row
    strict = col < row
    cum = lax.dot_general(incl.astype(F32), lw, (((2,), (1,)), ((0,), (0,))), precision=HIGHEST,
                          preferred_element_type=F32)
    g_in, g_ex, g_inv = jnp.exp(cum), jnp.exp(cum - lw), jnp.exp(-cum)
    kkg, kd, bd, rg = kk * g_ex, k * g_inv, b * g_inv, r * g_in
    a_k = jnp.where(strict, _dot(kkg, kd, 2, 2), 0.0)
    a_b = jnp.where(strict, _dot(kkg, bd, 2, 2), 0.0)
    pw = -a_b
    inv = (row == col).astype(F32) + pw
    for _ in range(int(math.log2(L)) - 1):
        pw = _dot(pw, pw, 2, 1)
        inv = inv + _dot(inv, pw, 2, 1)
    sa = -_dot(inv, _dot(kkg, S0, 2, 2) + _dot(a_k, v, 2, 1), 2, 1)
    r_k = jnp.where(incl, _dot(rg, kd, 2, 2), 0.0)
    r_b = jnp.where(incl, _dot(rg, bd, 2, 2), 0.0)
    y = _dot(rg, S0, 2, 2) + _dot(r_k, v, 2, 1) + _dot(r_b, sa, 2, 1)
    g_end = jnp.exp(jnp.sum(lw, axis=1, keepdims=True))
    S1 = S0 * g_end + _dot(v, kd * g_end, 1, 1) + _dot(sa, bd * g_end, 1, 1)
    return y, S1


def _scan_heads_per_step(H):
    return next(hb for hb in (SCAN_HEADS_PER_STEP, 2, 1) if H % hb == 0)


def _scan_fwd_call(r, lw, k, v, kk, b):
    H, T, N = r.shape
    L = SCAN_CHUNK
    n_chunks = T // L
    hb = _scan_heads_per_step(H)

    def body(r_ref, lw_ref, k_ref, v_ref, kk_ref, b_ref, y_ref, s0_ref, state):
        @pl.when(pl.program_id(1) == 0)
        def _():
            state[...] = jnp.zeros_like(state)

        S0 = state[...]
        s0_ref[:, 0] = S0
        y, S1 = _scan_chunk(S0, r_ref[...], lw_ref[...], k_ref[...], v_ref[...], kk_ref[...], b_ref[...])
        y_ref[...] = y
        state[...] = S1

    blk = pl.BlockSpec((hb, L, N), lambda h, c: (h, c, 0))
    return pl.pallas_call(
        body, name="rwkv_scan_fwd", grid=(H // hb, n_chunks),
        in_specs=[blk] * 6,
        out_specs=[blk, pl.BlockSpec((hb, 1, N, N), lambda h, c: (h, c, 0, 0))],
        out_shape=[jax.ShapeDtypeStruct((H, T, N), F32), jax.ShapeDtypeStruct((H, n_chunks, N, N), F32)],
        scratch_shapes=[pltpu.VMEM((hb, N, N), F32)],
        compiler_params=pltpu.CompilerParams(dimension_semantics=("parallel", "arbitrary")),
    )(r, lw, k, v, kk, b)


def _scan_bwd_call(r, lw, k, v, kk, b, s0s, dy):
    H, T, N = r.shape
    L = SCAN_CHUNK
    n_chunks = T // L
    hb = _scan_heads_per_step(H)

    def body(r_ref, lw_ref, k_ref, v_ref, kk_ref, b_ref, s0_ref, dy_ref, dr, dlw, dk, dv, dkk, db, d_state):
        @pl.when(pl.program_id(1) == 0)
        def _():
            d_state[...] = jnp.zeros_like(d_state)

        _, vjp_fn = jax.vjp(_scan_chunk, s0_ref[:, 0], r_ref[...], lw_ref[...], k_ref[...], v_ref[...], kk_ref[...],
                            b_ref[...])
        grads = vjp_fn((dy_ref[...], d_state[...]))
        d_state[...] = grads[0]
        for o_ref, g in zip((dr, dlw, dk, dv, dkk, db), grads[1:]):
            o_ref[...] = g

    blk = pl.BlockSpec((hb, L, N), lambda h, c: (h, n_chunks - 1 - c, 0))
    return pl.pallas_call(
        body, name="rwkv_scan_bwd", grid=(H // hb, n_chunks),
        in_specs=[blk] * 6 + [pl.BlockSpec((hb, 1, N, N), lambda h, c: (h, n_chunks - 1 - c, 0, 0)), blk],
        out_specs=[blk] * 6,
        out_shape=[jax.ShapeDtypeStruct((H, T, N), F32)] * 6,
        scratch_shapes=[pltpu.VMEM((hb, N, N), F32)],
        compiler_params=pltpu.CompilerParams(dimension_semantics=("parallel", "arbitrary")),
    )(r, lw, k, v, kk, b, s0s, dy)


@jax.custom_vjp
def _rwkv_scan(r, lw, k, v, kk, b):
    return _scan_fwd_call(r, lw, k, v, kk, b)[0]


def _rwkv_scan_fwd(r, lw, k, v, kk, b):
    y, s0s = _scan_fwd_call(r, lw, k, v, kk, b)
    return y, (r, lw, k, v, kk, b, s0s)


def _rwkv_scan_bwd(res, dy):
    return tuple(_scan_bwd_call(*res, dy))


_rwkv_scan.defvjp(_rwkv_scan_fwd, _rwkv_scan_bwd)


def _nt(a, b):
    return lax.dot_general(a.astype(BF16), b.astype(BF16), (((1,), (1,)), ((), ())), preferred_element_type=F32)


def _nn(a, b):
    return lax.dot_general(a.astype(BF16), b.astype(BF16), (((1,), (0,)), ((), ())), preferred_element_type=F32)


def _tn(a, b):
    return lax.dot_general(a.astype(BF16), b.astype(BF16), (((0,), (0,)), ((), ())), preferred_element_type=F32)


def _attn_scores(qs, k_ref, cr_ref, row_bias, j, kb, q0, masked):
    ks = pl.multiple_of(j * kb, kb)
    kj = k_ref[0, pl.ds(ks, kb), :]
    s = _nt(qs, kj) + row_bias - cr_ref[0, j]
    if masked:
        qi = q0 + lax.broadcasted_iota(jnp.int32, s.shape, 0)
        ki = ks + lax.broadcasted_iota(jnp.int32, s.shape, 1)
        s = jnp.where(ki <= qi, s, -jnp.inf)
    return s, kj, ks


def _attn_specs(T, N, bq, kb):
    q_spec = pl.BlockSpec((1, bq, N), lambda h, i: (h, i, 0))
    kv_spec = pl.BlockSpec((1, T, N), lambda h, i: (h, 0, 0))
    col_spec = pl.BlockSpec((1, bq, 1), lambda h, i: (h, i, 0))
    row_spec = pl.BlockSpec((1, T // kb, 1, kb), lambda h, i: (h, 0, 0, 0))
    return q_spec, kv_spec, col_spec, row_spec


def _grid_marks(H, n_q):
    h, i = pl.program_id(0), pl.program_id(1)
    return (h == 0) & (i == 0), (h == H // 2) & (i == 0), (h == H - 1) & (i == n_q - 1)


def _attn_fwd_call(q, k, v, c_col, c_rows, gather_xs):
    H, T, N = q.shape
    bq, kb = min(ATTN_BLOCK_Q, T), c_rows.shape[3]
    q_spec, kv_spec, col_spec, row_spec = _attn_specs(T, N, bq, kb)
    n = len(gather_xs)

    def body(q_ref, k_ref, v_ref, cc_ref, cr_ref, *rest):
        x_refs, (o_ref, o32_ref, lse_ref), out_refs, sems = rest[:n], rest[n:n + 3], rest[n + 3:2 * n + 3], rest[2 * n + 3:]
        first, middle, last = _grid_marks(H, T // bq)
        start, relay, finish = _gather_phases(x_refs, out_refs, *sems)
        pl.when(first)(start)
        pl.when(middle)(relay)
        q0 = pl.program_id(1) * bq
        qs = (q_ref[0] * (HEAD_DIM ** -0.5)).astype(BF16)
        cc = cc_ref[0]

        def step(j, carry, masked):
            m, l, acc = carry
            s, _, ks = _attn_scores(qs, k_ref, cr_ref, cc, j, kb, q0, masked)
            m_new = jnp.maximum(m, jnp.max(s, axis=-1, keepdims=True))
            alpha = jnp.exp(m - m_new)
            p = jnp.exp(s - m_new)
            l = alpha * l + jnp.sum(p, axis=-1, keepdims=True)
            p_hi = p.astype(BF16)
            p_lo = p - p_hi.astype(F32)
            vj = v_ref[0, pl.ds(ks, kb), :]
            acc = alpha * acc + (_nn(p_hi, vj) + _nn(p_lo, vj))
            return m_new, l, acc

        n_full = q0 // kb
        init = (jnp.full((bq, 1), -jnp.inf, F32), jnp.zeros((bq, 1), F32), jnp.zeros((bq, N), F32))
        carry = lax.fori_loop(0, n_full, functools.partial(step, masked=False), init)
        m, l, acc = step(n_full, carry, masked=True)
        o = acc / l
        o_ref[0] = o.astype(o_ref.dtype)
        o32_ref[0] = o
        lse_ref[0] = m + jnp.log(l)
        pl.when(last)(finish)

    any_spec = pl.BlockSpec(memory_space=pl.ANY)
    return pl.pallas_call(
        body, name="fox_attn_fwd", grid=(H, T // bq),
        in_specs=[q_spec, kv_spec, kv_spec, col_spec, row_spec] + [any_spec] * n,
        out_specs=[q_spec, q_spec, col_spec] + [any_spec] * n,
        out_shape=[jax.ShapeDtypeStruct((H, T, N), BF16),
                   jax.ShapeDtypeStruct((H, T, N), F32),
                   jax.ShapeDtypeStruct((H, T, 1), F32)] + _gather_out_shapes(gather_xs),
        scratch_shapes=_comm_semaphores(n),
        compiler_params=pltpu.CompilerParams(dimension_semantics=("arbitrary", "arbitrary"),
                                             vmem_limit_bytes=VMEM_LIMIT),
    )(q, k, v, c_col, c_rows, *gather_xs)


def _attn_bwd_call(q, k, v, c_col, c_rows, o, lse, do, scatter_parts):
    H, T, N = q.shape
    bq, kb = min(ATTN_BLOCK_Q, T), c_rows.shape[3]
    q_spec, kv_spec, col_spec, row_spec = _attn_specs(T, N, bq, kb)
    n = len(scatter_parts)

    def body(q_ref, k_ref, v_ref, cc_ref, cr_ref, o_ref, lse_ref, do_ref, *rest):
        a_refs, (dq_ref, dk_ref, dv_ref, dcr_ref), b_refs, sems = rest[:n], rest[n:n + 4], rest[n + 4:2 * n + 4], rest[2 * n + 4:]
        first, _, last = _grid_marks(H, T // bq)
        start, finish = _scatter_phases(a_refs, b_refs, *sems)
        pl.when(first)(start)
        i = pl.program_id(1)
        q0 = i * bq

        @pl.when(i == 0)
        def _():
            dk_ref[...] = jnp.zeros_like(dk_ref)
            dv_ref[...] = jnp.zeros_like(dv_ref)
            dcr_ref[...] = jnp.zeros_like(dcr_ref)

        qs = (q_ref[0] * (HEAD_DIM ** -0.5)).astype(BF16)
        do = do_ref[0]
        delta = jnp.sum(do.astype(F32) * o_ref[0], axis=-1, keepdims=True)
        row_bias = cc_ref[0] - lse_ref[0]

        def step(j, dq, masked):
            s, kj, ks = _attn_scores(qs, k_ref, cr_ref, row_bias, j, kb, q0, masked)
            p = jnp.exp(s)
            ds = p * (_nt(do, v_ref[0, pl.ds(ks, kb), :]) - delta)
            ds_b = ds.astype(BF16)
            dk_ref[0, pl.ds(ks, kb), :] += _tn(ds_b, qs)
            dv_ref[0, pl.ds(ks, kb), :] += _tn(p, do)
            dcr_ref[0, j] -= jnp.sum(ds, axis=0, keepdims=True)
            return dq + _nn(ds_b, kj)

        n_full = q0 // kb
        dq = lax.fori_loop(0, n_full, functools.partial(step, masked=False), jnp.zeros((bq, N), F32))
        dq_ref[0] = step(n_full, dq, masked=True) * (HEAD_DIM ** -0.5)
        pl.when(last)(finish)

    any_spec = pl.BlockSpec(memory_space=pl.ANY)
    return pl.pallas_call(
        body, name="fox_attn_bwd", grid=(H, T // bq),
        in_specs=[q_spec, kv_spec, kv_spec, col_spec, row_spec, q_spec, col_spec, q_spec] + [any_spec] * n,
        out_specs=[q_spec, kv_spec, kv_spec, row_spec] + [any_spec] * n,
        out_shape=[jax.ShapeDtypeStruct((H, T, N), F32)] * 3 + [jax.ShapeDtypeStruct(c_rows.shape, F32)]
        + [jax.ShapeDtypeStruct(a.shape, a.dtype) for a in scatter_parts],
        scratch_shapes=_comm_semaphores(n),
        compiler_params=pltpu.CompilerParams(dimension_semantics=("arbitrary", "arbitrary"),
                                             vmem_limit_bytes=VMEM_LIMIT),
    )(q, k, v, c_col, c_rows, o, lse, do, *scatter_parts)


@jax.custom_vjp
def _fox_attn(q, k, v, c, late_blocks, carrier):
    return _fox_attn_fwd(q, k, v, c, late_blocks, carrier)[0]


def _attn_bias_views(c):
    H, T = c.shape
    kb = min(ATTN_BLOCK_K, T)
    return c[:, :, None], c.reshape(H, T // kb, 1, kb)


def _fox_attn_fwd(q, k, v, c, late_blocks, carrier):
    o, o32, lse, *gathered = _attn_fwd_call(q, k, v, *_attn_bias_views(c), late_blocks)
    return (o, tuple(gathered)), (q, k, v, c, o32, lse)


def _fox_attn_bwd(res, cts):
    q, k, v, c, o32, lse = res
    do, d_gathered = cts
    dq, dk, dv, dc_rows, *parts = _attn_bwd_call(q, k, v, *_attn_bias_views(c), o32, lse, do, d_gathered)
    no_grad = tuple(jnp.zeros(a.shape[1:], a.dtype) for a in parts)
    return dq, dk, dv, dc_rows.reshape(c.shape), no_grad, tuple(parts)


_fox_attn.defvjp(_fox_attn_fwd, _fox_attn_bwd)


N_PEERS = N_DEV - 1


def _all_gather(xs, name):
    n = len(xs)

    def body(*refs):
        start, relay, finish = _gather_phases(refs[:n], refs[n:2 * n], *refs[2 * n:])
        start()
        relay()
        finish()

    any_spec = pl.BlockSpec(memory_space=pl.ANY)
    return pl.pallas_call(
        body, name=name,
        out_shape=_gather_out_shapes(xs),
        in_specs=[any_spec] * n, out_specs=[any_spec] * n,
        scratch_shapes=_comm_semaphores(n),
    )(*xs)


def _gather_out_shapes(xs):
    return [jax.ShapeDtypeStruct((N_DEV,) + x.shape, x.dtype) for x in xs]


def _comm_semaphores(n):
    return [pltpu.SemaphoreType.DMA((N_PEERS * n,)), pltpu.SemaphoreType.DMA((N_PEERS * n,)),
            pltpu.SemaphoreType.DMA((n,))]


def _gather_phases(x_refs, out_refs, send_sems, recv_sems, local_sems):
    n = len(x_refs)
    x_, y_, c_ = lax.axis_index("x"), lax.axis_index("y"), lax.axis_index("c")
    me, sibling = (x_, y_, c_), (x_, y_, 1 - c_)
    chips = [(1 - x_, y_), (x_, 1 - y_), (1 - x_, 1 - y_)]

    def slot(t, px, py, pc):
        return out_refs[t].at[4 * px + 2 * py + pc]

    def copy(t, k, block, to, src=None):
        return pltpu.make_async_remote_copy(
            src_ref=slot(t, *block) if src is None else src, dst_ref=slot(t, *block),
            send_sem=send_sems.at[k * n + t], recv_sem=recv_sems.at[k * n + t],
            device_id=to, device_id_type=pl.DeviceIdType.MESH)

    def mine():
        return [pltpu.make_async_copy(x_refs[t], slot(t, *me), local_sems.at[t]) for t in range(n)]

    def first():
        return ([copy(t, 0, me, sibling, src=x_refs[t]) for t in range(n)]
                + [copy(t, 1 + j, me, (*chip, c_), src=x_refs[t]) for j, chip in enumerate(chips) for t in range(n)])

    def passed():
        return [copy(t, 4 + j, (*chip, c_), sibling) for j, chip in enumerate(chips) for t in range(n)]

    def start():
        for cp in mine() + first():
            cp.start()

    def relay():
        for j, chip in enumerate(chips):
            for t in range(n):
                copy(t, 1 + j, (*chip, c_), me).wait_recv()
                copy(t, 4 + j, (*chip, c_), sibling).start()

    def finish():
        for t in range(n):
            copy(t, 0, sibling, me).wait_recv()
        for j, chip in enumerate(chips):
            for t in range(n):
                copy(t, 4 + j, (*chip, 1 - c_), me).wait_recv()
        for cp in first() + passed():
            cp.wait_send()
        for cp in mine():
            cp.wait()

    return start, relay, finish


def _all_to_all(parts, name):
    n = len(parts)

    def body(*refs):
        start, finish = _scatter_phases(refs[:n], refs[n:2 * n], *refs[2 * n:])
        start()
        finish()

    any_spec = pl.BlockSpec(memory_space=pl.ANY)
    return pl.pallas_call(
        body, name=name,
        out_shape=[jax.ShapeDtypeStruct(a.shape, a.dtype) for a in parts],
        in_specs=[any_spec] * n, out_specs=[any_spec] * n,
        scratch_shapes=_comm_semaphores(n),
    )(*parts)


def _scatter_phases(a_refs, b_refs, send_sems, recv_sems, local_sems):
    n = len(a_refs)
    x_, y_, c_ = lax.axis_index("x"), lax.axis_index("y"), lax.axis_index("c")
    me_idx = 4 * x_ + 2 * y_ + c_

    def copies():
        out = [pltpu.make_async_copy(a_refs[t].at[me_idx], b_refs[t].at[me_idx], local_sems.at[t]) for t in range(n)]
        for rel in range(1, N_DEV):
            px = 1 - x_ if rel & 4 else x_
            py = 1 - y_ if rel & 2 else y_
            pc = 1 - c_ if rel & 1 else c_
            for t in range(n):
                out.append(pltpu.make_async_remote_copy(
                    src_ref=a_refs[t].at[4 * px + 2 * py + pc], dst_ref=b_refs[t].at[me_idx],
                    send_sem=send_sems.at[(rel - 1) * n + t], recv_sem=recv_sems.at[(rel - 1) * n + t],
                    device_id=(px, py, pc), device_id_type=pl.DeviceIdType.MESH))
        return out

    def start():
        for cp in copies():
            cp.start()

    def finish():
        for cp in copies():
            cp.wait()

    return start, finish


def _reduce_adamw(parts, w, m, v, name):
    R, C = w.shape
    tr = max(t for t in (256, 128, PACK_ROW_QUANTUM) if R % t == 0)

    def body(p_ref, w_ref, m_ref, v_ref, g_out, d_out, m_out, v_out):
        g = p_ref[0]
        for i in range(1, N_DEV):
            g = g + p_ref[i]
        m_new = ADAM_B1 * m_ref[...] + (1.0 - ADAM_B1) * g
        v_new = ADAM_B2 * v_ref[...] + (1.0 - ADAM_B2) * (g * g)
        m_hat = m_new / (1.0 - ADAM_B1 ** ADAM_STEP)
        v_hat = v_new / (1.0 - ADAM_B2 ** ADAM_STEP)
        g_out[...] = g
        d_out[...] = -ADAM_LR * (m_hat / (jnp.sqrt(v_hat) + ADAM_EPS) + ADAM_WD * w_ref[...])
        m_out[...] = m_new
        v_out[...] = v_new

    spec = pl.BlockSpec((tr, C), lambda i: (i, 0))
    return pl.pallas_call(
        body, name=name, grid=(R // tr,),
        in_specs=[pl.BlockSpec((N_DEV, tr, C), lambda i: (0, i, 0)), spec, spec, spec],
        out_specs=[spec] * 4,
        out_shape=[jax.ShapeDtypeStruct((R, C), F32)] * 4,
        compiler_params=pltpu.CompilerParams(dimension_semantics=("parallel",), vmem_limit_bytes=VMEM_LIMIT),
    )(parts, w, m, v)


def _pack(arrays, dtype):
    flat = jnp.concatenate([a.reshape(-1).astype(dtype) for a in arrays])
    rows = _round_up(-(-flat.shape[0] // PACK_COLS), PACK_ROW_QUANTUM)
    flat = jnp.pad(flat, (0, rows * PACK_COLS - flat.shape[0]))
    return flat.reshape(rows, PACK_COLS)


def _unpack(packed, shapes):
    lead = packed.shape[:-2]
    flat = packed.reshape(lead + (-1,))
    out, off = [], 0
    for s in shapes:
        n = math.prod(s)
        out.append(flat[..., off:off + n].reshape(lead + tuple(s)))
        off += n
    return out


def _travel_layout(name, block):
    return block.T if SHARDED[name] else block


REDUCE_BLOCK_BYTES = 4 * 1024 * 1024


def _reduce_parts(parts, name):
    _, R, C = parts.shape
    per_col = N_DEV * R * parts.dtype.itemsize
    tc = next((t for t in range(C - C % LANES, 0, -LANES) if C % t == 0 and t * per_col <= REDUCE_BLOCK_BYTES), C)

    def body(p_ref, o_ref):
        g = p_ref[0].astype(F32)
        for i in range(1, N_DEV):
            g = g + p_ref[i].astype(F32)
        o_ref[...] = g

    return pl.pallas_call(
        body, name=name, grid=(C // tc,),
        in_specs=[pl.BlockSpec((N_DEV, R, tc), lambda j: (0, 0, j))],
        out_specs=pl.BlockSpec((R, tc), lambda j: (0, j)),
        out_shape=jax.ShapeDtypeStruct((R, C), F32),
        compiler_params=pltpu.CompilerParams(dimension_semantics=("parallel",), vmem_limit_bytes=VMEM_LIMIT),
    )(parts)


def _adamw(g, w, m, v, name):
    R, C = w.shape
    tr = next((t for t in (512, 256, 128, 64, 32, 16, 8) if R % t == 0 and t * C * 4 <= 2 * 1024 * 1024), R)

    def body(g_ref, w_ref, m_ref, v_ref, d_out, m_out, v_out):
        g_ = g_ref[...]
        m_new = ADAM_B1 * m_ref[...] + (1.0 - ADAM_B1) * g_
        v_new = ADAM_B2 * v_ref[...] + (1.0 - ADAM_B2) * (g_ * g_)
        m_hat = m_new / (1.0 - ADAM_B1 ** ADAM_STEP)
        v_hat = v_new / (1.0 - ADAM_B2 ** ADAM_STEP)
        d_out[...] = -ADAM_LR * (m_hat / (jnp.sqrt(v_hat) + ADAM_EPS) + ADAM_WD * w_ref[...])
        m_out[...] = m_new
        v_out[...] = v_new

    spec = pl.BlockSpec((tr, C), lambda i: (i, 0))
    return pl.pallas_call(
        body, name=name, grid=(R // tr,),
        in_specs=[spec] * 4, out_specs=[spec] * 3,
        out_shape=[jax.ShapeDtypeStruct((R, C), F32)] * 3,
        compiler_params=pltpu.CompilerParams(dimension_semantics=("parallel",), vmem_limit_bytes=VMEM_LIMIT),
    )(g, w, m, v)


def _to_heads(u):
    return jnp.transpose(u.reshape(u.shape[0], -1, HEAD_DIM), (1, 0, 2))


def _split_cols(u, n):
    width = u.shape[1] // n

    @jax.custom_vjp
    def run(u):
        return tuple(u[:, i * width:(i + 1) * width] for i in range(n))

    def fwd(u):
        return run(u), None

    def bwd(_, cts):
        return (jnp.concatenate(cts, axis=1),)

    run.defvjp(fwd, bwd)
    return run(u)


def _from_heads(uh):
    H, T, N = uh.shape
    return jnp.transpose(uh, (1, 0, 2)).reshape(T, H * N)


def _shift(uh):
    return jnp.pad(uh, ((0, 0), (1, 0), (0, 0)))[:, :-1]


def _pad_cols(a, width):
    return jnp.pad(a, ((0, 0), (0, width - a.shape[1])))


def _split_rows(w, sizes):
    offsets = [sum(sizes[:i]) for i in range(len(sizes))]

    @jax.custom_vjp
    def run(w):
        return tuple(w[o:o + s] for o, s in zip(offsets, sizes))

    def fwd(w):
        return run(w), None

    def bwd(_, cts):
        return (jnp.concatenate(cts, axis=0),)

    run.defvjp(fwd, bwd)
    return run(w)


def _pad_rows(a, height):
    return jnp.pad(a, ((0, height - a.shape[0]), (0, 0)))


def _vec(a):
    return a.reshape(1, 1, -1)


TM_WIDE = 128


def _mixing_half(W, small, x, late_blocks, carrier):
    T, D = x.shape
    vec = _vec
    tm_wide = TM_WIDE
    rw = small['w0'].shape[-1]
    fw = W['w_out'].shape[0] - rw
    heads_r, heads_f = rw // HEAD_DIM, fw // HEAD_DIM
    dl, al, gl = W['w2'].shape[1], W['a2'].shape[1], W['g2'].shape[1]
    dl_p, al_p, gl_p = _round_up(dl, LANES), _round_up(al, LANES), _round_up(gl, LANES)
    f_p = _round_up(heads_f, LANES)
    rwkv_cols = 3 * rw + dl + al + gl
    tm_head = 512 if T % 512 == 0 else T

    o_w, o_a, o_g = 3 * rw, 3 * rw + dl, 3 * rw + dl + al
    w_rkv, w_xw, w_xa, w_xg, w_qkv, w_fg = _split_rows(W['w_in'], (3 * rw, dl, al, gl, 3 * fw, heads_f))
    w_lora = jnp.concatenate([_pad_rows(w_xw, dl_p), _pad_rows(w_xa, al_p), _pad_rows(w_xg, gl_p)], axis=0)
    w_f = _pad_rows(w_fg, f_p)
    mu = small['shift_mu'].reshape(1, -1)
    mu_lora = jnp.concatenate([_pad_cols(mu[:, o_w:o_a], dl_p), _pad_cols(mu[:, o_a:o_g], al_p),
                               _pad_cols(mu[:, o_g:rwkv_cols], gl_p)], axis=1)

    (xn,) = _stage("attn_norm", _fn_rmsnorm, [x[None]], [vec(small['attn_norm_g'])], tm_wide)
    xn = xn[0]
    u_rkv = _mm_t(xn, w_rkv, "in_rkv")
    u_lora = _mm_t(xn, w_lora, "in_lora")
    u_qkv = _mm_t(xn, w_qkv, "in_qkv")
    f_raw = _mm_t(xn, w_f, "in_f")

    u_lora3 = u_lora[None]
    xw_t, xa_m, xg_s = _stage("lora_mix", _make_fn_lora_mix(dl_p, al_p), [u_lora3, _shift(u_lora3)],
                              [vec(mu_lora)], tm_wide)
    w_lin = _mm_t(xw_t[0], _pad_cols(W['w2'], dl_p), "w2")
    a_lin = _mm_t(xa_m[0], _pad_cols(W['a2'], al_p), "a2")
    gate_r = _mm_t(xg_s[0], _pad_cols(W['g2'], gl_p), "g2")
    ru, ku, vu = (u[None] for u in _split_cols(u_rkv, 3))
    mu_r, mu_k, mu_v = (vec(mu[:, i * rw:(i + 1) * rw]) for i in range(3))
    prepped = _stage(
        "rwkv_prep", _fn_rwkv_prep,
        [ru, _shift(ru), ku, _shift(ku), vu, _shift(vu), w_lin[None], a_lin[None]],
        [mu_r, mu_k, mu_v, vec(small['w0']), vec(small['a0']), vec(small['k_k']), vec(small['k_a'])],
        tm_head, cols=HEAD_PAIR)
    r, lw, k_mod, v, kk, b = prepped
    y_scan = _rwkv_scan(*(_to_heads(t[0]) for t in prepped))
    (y_rwkv,) = _stage("rwkv_post", _fn_rwkv_post, [_from_heads(y_scan)[None], r, k_mod, v, gate_r[None]],
                       [vec(small['lnx_g']), vec(small['lnx_b']), vec(small['r_k'])], tm_head, cols=HEAD_PAIR)

    qu, kf, vf = _split_cols(u_qkv, 3)
    qg = vec(jnp.tile(small['q_norm_g'].reshape(-1), heads_f))
    kg = vec(jnp.tile(small['k_norm_g'].reshape(-1), heads_f))
    qn, kn = _stage("fox_prep", _fn_fox_prep, [qu[None], kf[None]], [qg, kg], tm_head, cols=HEAD_PAIR)
    fb = _pad_cols(small['fgate_b'].reshape(1, -1), f_p)
    (log_f,) = _stage("log_forget", _fn_log_forget, [f_raw[None]], [vec(fb)], tm_head)
    c = jnp.cumsum(log_f[0][:, :heads_f], axis=0).T
    y_fox, gathered_late = _fox_attn(_to_heads(qn[0]), _to_heads(kn[0]), _to_heads(vf), c, late_blocks, carrier)

    y_cat = jnp.concatenate([y_rwkv[0], _from_heads(y_fox)], axis=-1)
    return _mm_add(x, y_cat, W['w_out'], "out"), gathered_late


def _channel_half_loss(W, small, h1, p, target):
    vec = _vec
    tm_wide = TM_WIDE
    (hn,) = _stage("ffn_norm", _fn_rmsnorm, [h1[None]], [vec(small['ffn_norm_g'])], tm_wide)
    gate = _mm_t(hn[0], W['w_gate'], "gate")
    up = _mm_t(hn[0], W['w_up'], "up")
    (act,) = _stage("swiglu", _fn_swiglu, [gate[None], up[None]], [], tm_wide)
    h2 = _mm_add(h1, act[0], W['w_down'], "down")
    e_raw = _mm_t(p, W['ple_proj'], "ple_proj")
    (hg,) = _stage("ple_gate_norm", _fn_rmsnorm, [h2[None]], [vec(small['ple_gate_norm_g'])], tm_wide)
    z = _mm(hg[0], W['ple_gate_w'], "ple_gate")
    (loss_rows,) = _stage("final", _fn_final, [z[None], e_raw[None], h2[None], target[None]],
                          [vec(small['ple_gate_b']), vec(small['ple_norm_g'])], tm_wide)
    return jnp.sum(loss_rows)


def kernel(x, p, attn_norm_g, w_in, shift_mu, w0, w2, a0, a2, g2, k_k, k_a, r_k, lnx_g, lnx_b, q_norm_g, k_norm_g, fgate_b, w_out, ffn_norm_g, w_gate, w_up, w_down, ple_proj, ple_norm_g, ple_gate_norm_g, ple_gate_w, ple_gate_b, loss_target, m_attn_norm_g, m_w_in, m_shift_mu, m_w0, m_w2, m_a0, m_a2, m_g2, m_k_k, m_k_a, m_r_k, m_lnx_g, m_lnx_b, m_q_norm_g, m_k_norm_g, m_fgate_b, m_w_out, m_ffn_norm_g, m_w_gate, m_w_up, m_w_down, m_ple_proj, m_ple_norm_g, m_ple_gate_norm_g, m_ple_gate_w, m_ple_gate_b, v_attn_norm_g, v_w_in, v_shift_mu, v_w0, v_w2, v_a0, v_a2, v_g2, v_k_k, v_k_a, v_r_k, v_lnx_g, v_lnx_b, v_q_norm_g, v_k_norm_g, v_fgate_b, v_w_out, v_ffn_norm_g, v_w_gate, v_w_up, v_w_down, v_ple_proj, v_ple_norm_g, v_ple_gate_norm_g, v_ple_gate_w, v_ple_gate_b):
    weights = dict(zip(WEIGHT_NAMES, (attn_norm_g, w_in, shift_mu, w0, w2, a0, a2, g2, k_k, k_a, r_k, lnx_g, lnx_b,
                                      q_norm_g, k_norm_g, fgate_b, w_out, ffn_norm_g, w_gate, w_up, w_down, ple_proj,
                                      ple_norm_g, ple_gate_norm_g, ple_gate_w, ple_gate_b)))
    m_in = dict(zip(WEIGHT_NAMES, (m_attn_norm_g, m_w_in, m_shift_mu, m_w0, m_w2, m_a0, m_a2, m_g2, m_k_k, m_k_a, m_r_k,
                                   m_lnx_g, m_lnx_b, m_q_norm_g, m_k_norm_g, m_fgate_b, m_w_out, m_ffn_norm_g, m_w_gate,
                                   m_w_up, m_w_down, m_ple_proj, m_ple_norm_g, m_ple_gate_norm_g, m_ple_gate_w,
                                   m_ple_gate_b)))
    v_in = dict(zip(WEIGHT_NAMES, (v_attn_norm_g, v_w_in, v_shift_mu, v_w0, v_w2, v_a0, v_a2, v_g2, v_k_k, v_k_a, v_r_k,
                                   v_lnx_g, v_lnx_b, v_q_norm_g, v_k_norm_g, v_fgate_b, v_w_out, v_ffn_norm_g, v_w_gate,
                                   v_w_up, v_w_down, v_ple_proj, v_ple_norm_g, v_ple_gate_norm_g, v_ple_gate_w,
                                   v_ple_gate_b)))
    small_shapes = [weights[n].shape for n in SMALL_NAMES]
    small = {n: weights[n] for n in SMALL_NAMES}

    def whole(stacks, names):
        return {n: g.reshape(N_DEV * g.shape[1], g.shape[2]) for n, g in zip(names, stacks)}

    def stacked(tree, names, like):
        return tuple(tree[n].reshape(g.shape) for n, g in zip(names, like))

    travelling = {n: _travel_layout(n, weights[n][0]).astype(BF16) for n in SHARDED_NAMES}
    gathered_early = _all_gather([travelling[n] for n in EARLY_NAMES], "gather_weights")
    late_blocks = tuple(travelling[n] for n in LATE_NAMES)
    carrier = tuple(jnp.zeros((N_DEV,) + b.shape, b.dtype) for b in late_blocks)

    (h1, gathered_late), mixing_vjp = jax.vjp(_mixing_half, whole(gathered_early, EARLY_NAMES), small, x[0],
                                              late_blocks, carrier)
    loss_local, (d_late, d_small_b, d_h1) = jax.value_and_grad(_channel_half_loss, argnums=(0, 1, 2))(
        whole(gathered_late, LATE_NAMES), small, h1, p[0, 0], loss_target[0])
    d_early, d_small_a, d_x, _, parts_late = mixing_vjp((d_h1, stacked(d_late, LATE_NAMES, gathered_late)))
    d_small = {n: d_small_a[n] + d_small_b[n] for n in SMALL_NAMES}
    loss = lax.psum(loss_local, MESH_AXES)

    parts_early = _all_to_all(stacked(d_early, EARLY_NAMES, gathered_early), "scatter_grads")
    parts = dict(zip(EARLY_NAMES + LATE_NAMES, list(parts_early) + list(parts_late)))
    (small_parts,) = _all_gather([_pack([d_small[n] for n in SMALL_NAMES], F32)], "gather_small_grads")

    def pack_f32(tree, names):
        return _pack([tree[n] for n in names], F32)

    sml = _reduce_adamw(small_parts, pack_f32(weights, SMALL_NAMES), pack_f32(m_in, SMALL_NAMES),
                        pack_f32(v_in, SMALL_NAMES), "adamw_replicated")
    by_kind = [dict(zip(SMALL_NAMES, _unpack(sml[kind], small_shapes))) for kind in range(4)]
    for n in SHARDED_NAMES:
        g = _travel_layout(n, _reduce_parts(parts[n], "reduce_" + n))
        upd = _adamw(g, weights[n][0], m_in[n][0], v_in[n][0], "adamw_" + n)
        for kind, val in enumerate((g, *upd)):
            by_kind[kind][n] = val[None]
    outs = [by_kind[kind][n] for kind in range(4) for n in WEIGHT_NAMES]
    return (loss, d_x[None], *outs)
```

```python
import functools
import math

import jax
import jax.numpy as jnp
from jax import lax
from jax.experimental import pallas as pl
from jax.experimental.pallas import tpu as pltpu

F32 = jnp.float32
BF16 = jnp.bfloat16
HIGHEST = lax.Precision.HIGHEST

N_DEV = 8
MESH_AXES = ("x", "y", "c")
HEAD_DIM = 64
SCAN_CHUNK = 64
SCAN_HEADS_PER_STEP = 16
ATTN_BLOCK_Q = 1024
ATTN_BLOCK_K = 1024
LANES = 128
HEAD_PAIR = 2 * HEAD_DIM
PACK_COLS = 1024
PACK_ROW_QUANTUM = 64
VMEM_LIMIT = 48 * 1024 * 1024
RMS_EPS = 1e-6
GN_EPS = 64e-5
ADAM_LR, ADAM_B1, ADAM_B2, ADAM_EPS, ADAM_WD, ADAM_STEP = 0.001, 0.9, 0.999, 1e-08, 0.01, 10

WEIGHT_NAMES = ['attn_norm_g', 'w_in', 'shift_mu', 'w0', 'w2', 'a0', 'a2', 'g2', 'k_k', 'k_a', 'r_k', 'lnx_g', 'lnx_b',
                'q_norm_g', 'k_norm_g', 'fgate_b', 'w_out', 'ffn_norm_g', 'w_gate', 'w_up', 'w_down', 'ple_proj',
                'ple_norm_g', 'ple_gate_norm_g', 'ple_gate_w', 'ple_gate_b']
SHARDED = {'w_in': True, 'w2': True, 'a2': True, 'g2': True, 'w_out': False, 'w_gate': True, 'w_up': True,
           'w_down': False, 'ple_proj': True, 'ple_gate_w': False}
SHARDED_NAMES = [n for n in WEIGHT_NAMES if n in SHARDED]
SMALL_NAMES = [n for n in WEIGHT_NAMES if n not in SHARDED]
EARLY_NAMES = ['w_in', 'w2', 'a2', 'g2', 'w_out']
LATE_NAMES = [n for n in SHARDED_NAMES if n not in EARLY_NAMES]


def _round_up(n, q):
    return -(-n // q) * q


def _tile_candidates(n, cap):
    sizes = {t for t in range(LANES, min(n, cap) + 1, LANES) if n % t == 0}
    return sorted(sizes | ({n} if n <= cap or not sizes else set()))


MM_TILE_CAP = 2048
MM_VMEM_BUDGET = 38 * 1024 * 1024
MM_STEP_COST_BYTES = 1024 * 1024


def _mm_tiles(I, J, C, a_size, b_size, o_size):
    best = None
    for ti in _tile_candidates(I, MM_TILE_CAP):
        for tj in _tile_candidates(J, MM_TILE_CAP):
            for tc in _tile_candidates(C, MM_TILE_CAP):
                n_c = C // tc
                blocks = 2 * (ti * tc * a_size + tc * tj * b_size + ti * tj * o_size)
                temporaries = (ti * tc + tc * tj) * 2 + ti * tj * 4 * (2 if n_c > 1 else 1)
                if blocks + temporaries > MM_VMEM_BUDGET:
                    continue
                traffic = (I * C * a_size * (1 if n_c == 1 else J // tj) + C * J * b_size * (I // ti)
                           + I * J * o_size)
                cost = traffic + (I // ti) * (J // tj) * n_c * MM_STEP_COST_BYTES
                if best is None or cost < best[0]:
                    best = (cost, ti, tj, tc)
    return best[1:]


def _mm_call(a, b, mode, name, out_dtype, addend=None):
    if mode == "nn":
        (I, C), (_, J) = a.shape, b.shape
    elif mode == "nt":
        (I, C), (J, _) = a.shape, b.shape
    else:
        (C, I), (_, J) = a.shape, b.shape
    ti, tj, tc = _mm_tiles(I, J, C, a.dtype.itemsize, b.dtype.itemsize, jnp.dtype(out_dtype).itemsize)
    n_c = C // tc
    if mode == "nn":
        a_spec = pl.BlockSpec((ti, tc), lambda i, j, c: (i, c))
        b_spec = pl.BlockSpec((tc, tj), lambda i, j, c: (c, j))
        dims = (((1,), (0,)), ((), ()))
    elif mode == "nt":
        a_spec = pl.BlockSpec((ti, tc), lambda i, j, c: (i, c))
        b_spec = pl.BlockSpec((tj, tc), lambda i, j, c: (j, c))
        dims = (((1,), (1,)), ((), ()))
    else:
        a_spec = pl.BlockSpec((tc, ti), lambda i, j, c: (c, i))
        b_spec = pl.BlockSpec((tc, tj), lambda i, j, c: (c, j))
        dims = (((0,), (0,)), ((), ()))

    def product(a_ref, b_ref):
        return lax.dot_general(a_ref[...].astype(BF16), b_ref[...].astype(BF16), dims, preferred_element_type=F32)

    def finish(o_ref, r_refs, value):
        for r_ref in r_refs:
            value = r_ref[...] + value
        o_ref[...] = value.astype(o_ref.dtype)

    def body_single(a_ref, b_ref, *rest):
        finish(rest[-1], rest[:-1], product(a_ref, b_ref))

    def body_accumulate(a_ref, b_ref, *rest):
        (*r_refs, o_ref, acc), c = rest, pl.program_id(2)

        @pl.when(c == 0)
        def _():
            acc[...] = jnp.zeros_like(acc)

        acc[...] += product(a_ref, b_ref)

        @pl.when(c == n_c - 1)
        def _():
            finish(o_ref, r_refs, acc[...])

    out_spec = pl.BlockSpec((ti, tj), lambda i, j, c: (i, j))
    addends = [] if addend is None else [addend]
    return pl.pallas_call(
        body_single if n_c == 1 else body_accumulate, name=name, grid=(I // ti, J // tj, n_c),
        in_specs=[a_spec, b_spec] + [out_spec] * len(addends),
        out_specs=out_spec,
        out_shape=jax.ShapeDtypeStruct((I, J), out_dtype),
        scratch_shapes=[] if n_c == 1 else [pltpu.VMEM((ti, tj), F32)],
        compiler_params=pltpu.CompilerParams(dimension_semantics=("parallel", "parallel", "arbitrary"),
                                             vmem_limit_bytes=VMEM_LIMIT),
    )(a, b, *addends)


def _mm_add(residual, a, b, name):
    @jax.custom_vjp
    def run(residual, a, b):
        return _mm_call(a, b, "nn", "mm_" + name, F32, addend=residual)

    def fwd(residual, a, b):
        return run(residual, a, b), (a, b)

    def bwd(res, g):
        a, b = res
        return (g, _mm_call(g, b, "nt", "mm_" + name + "_da", a.dtype),
                _mm_call(a, g, "tn", "mm_" + name + "_db", b.dtype))

    run.defvjp(fwd, bwd)
    return run(residual, a, b)


def _mm(a, b, name):
    @jax.custom_vjp
    def run(a, b):
        return _mm_call(a, b, "nn", "mm_" + name, F32)

    def fwd(a, b):
        return run(a, b), (a, b)

    def bwd(res, g):
        a, b = res
        return (_mm_call(g, b, "nt", "mm_" + name + "_da", a.dtype),
                _mm_call(a, g, "tn", "mm_" + name + "_db", b.dtype))

    run.defvjp(fwd, bwd)
    return run(a, b)


def _mm_t(a, wt, name):
    @jax.custom_vjp
    def run(a, wt):
        return _mm_call(a, wt, "nt", "mmt_" + name, F32)

    def fwd(a, wt):
        return run(a, wt), (a, wt)

    def bwd(res, g):
        a, wt = res
        return (_mm_call(g, wt, "nn", "mmt_" + name + "_da", a.dtype),
                _mm_call(g, a, "tn", "mmt_" + name + "_dw", wt.dtype))

    run.defvjp(fwd, bwd)
    return run(a, wt)


def _stage_layout(rows, tm, cols):
    G, T, C = rows[0].shape
    if cols is None:
        return ((G, T // tm), lambda c: c, lambda c: pl.BlockSpec((1, tm, c), lambda g, t: (g, t, 0)),
                lambda c: pl.BlockSpec((1, 1, c), lambda g, t: (g, 0, 0)), lambda c: (G, T, c))
    return ((C // cols, T // tm), lambda c: cols, lambda c: pl.BlockSpec((1, tm, cols), lambda g, t: (0, t, g)),
            lambda c: pl.BlockSpec((1, 1, cols), lambda g, t: (0, 0, g)), lambda c: (1, T, C))


def _stage_fwd_call(name, fn, rows, params, tm, cols):
    grid, width, row_spec, par_spec, out_dims = _stage_layout(rows, tm, cols)
    nr, npar = len(rows), len(params)
    out_avals = jax.eval_shape(
        lambda *a: tuple(fn(*a)),
        *[jax.ShapeDtypeStruct((tm, width(r.shape[2])), r.dtype) for r in rows],
        *[jax.ShapeDtypeStruct((1, width(p.shape[2])), p.dtype) for p in params])

    def body(*refs):
        vals = [r[0] for r in refs[:nr + npar]]
        for o_ref, o in zip(refs[nr + npar:], fn(*vals)):
            o_ref[0] = o

    return pl.pallas_call(
        body, name=name, grid=grid,
        in_specs=[row_spec(r.shape[2]) for r in rows] + [par_spec(p.shape[2]) for p in params],
        out_specs=[row_spec(o.shape[1]) for o in out_avals],
        out_shape=[jax.ShapeDtypeStruct(out_dims(o.shape[1]), o.dtype) for o in out_avals],
        compiler_params=pltpu.CompilerParams(dimension_semantics=("parallel", "parallel"),
                                             vmem_limit_bytes=VMEM_LIMIT),
    )(*rows, *params)


def _stage_bwd_call(name, fn, rows, params, cts, tm, cols):
    grid, _, row_spec, par_spec, _ = _stage_layout(rows, tm, cols)
    nr, npar, nout = len(rows), len(params), len(cts)

    def body(*refs):
        vals = [r[0] for r in refs[:nr + npar]]
        ct_vals = tuple(r[0] for r in refs[nr + npar:nr + npar + nout])
        d_refs = refs[nr + npar + nout:]
        _, vjp_fn = jax.vjp(lambda *a: tuple(fn(*a)), *vals)
        grads = vjp_fn(ct_vals)
        for i in range(nr):
            d_refs[i][0] = grads[i]

        if npar:
            @pl.when(pl.program_id(1) == 0)
            def _():
                for j in range(npar):
                    d_refs[nr + j][...] = jnp.zeros_like(d_refs[nr + j])

        for j in range(npar):
            d_refs[nr + j][0] += grads[nr + j]

    outs = pl.pallas_call(
        body, name=name + "_bwd", grid=grid,
        in_specs=([row_spec(r.shape[2]) for r in rows] + [par_spec(p.shape[2]) for p in params]
                  + [row_spec(c.shape[2]) for c in cts]),
        out_specs=[row_spec(r.shape[2]) for r in rows] + [par_spec(p.shape[2]) for p in params],
        out_shape=([jax.ShapeDtypeStruct(r.shape, r.dtype) for r in rows]
                   + [jax.ShapeDtypeStruct(p.shape, p.dtype) for p in params]),
        compiler_params=pltpu.CompilerParams(dimension_semantics=("parallel", "arbitrary"),
                                             vmem_limit_bytes=VMEM_LIMIT),
    )(*rows, *params, *cts)
    return tuple(outs[:nr]), tuple(outs[nr:])


def _stage(name, fn, rows, params, tm, cols=None):
    @jax.custom_vjp
    def run(rows, params):
        return tuple(_stage_fwd_call(name, fn, rows, params, tm, cols))

    def fwd(rows, params):
        return run(rows, params), (rows, params)

    def bwd(res, cts):
        rows, params = res
        return _stage_bwd_call(name, fn, rows, params, tuple(cts), tm, cols)

    run.defvjp(fwd, bwd)
    return run(tuple(rows), tuple(params))


def _sigmoid(x):
    return 0.5 * (jnp.tanh(0.5 * x) + 1.0)


def _softplus(x):
    return jnp.maximum(x, 0.0) + jnp.log(1.0 + jnp.exp(-jnp.abs(x)))


def _rms(x, g, eps=RMS_EPS):
    return x * lax.rsqrt(jnp.mean(x * x, axis=-1, keepdims=True) + eps) * g


def _head_sum_pieces(x):
    n = x.shape[-1]
    same_head = (lax.broadcasted_iota(jnp.int32, (n, n), 0) // HEAD_DIM
                 == lax.broadcasted_iota(jnp.int32, (n, n), 1) // HEAD_DIM).astype(BF16)
    hi = x.astype(BF16)
    lo = (x - hi.astype(F32)).astype(BF16)
    dims = (((1,), (0,)), ((), ()))
    return (lax.dot_general(hi, same_head, dims, preferred_element_type=F32)
            + lax.dot_general(lo, same_head, dims, preferred_element_type=F32))


@jax.custom_vjp
def _head_sum(x):
    return _head_sum_pieces(x)


def _head_sum_fwd(x):
    return _head_sum_pieces(x), None


def _head_sum_bwd(_, g):
    return (_head_sum_pieces(g),)


_head_sum.defvjp(_head_sum_fwd, _head_sum_bwd)


def _head_mean(x):
    return _head_sum(x) * (1.0 / HEAD_DIM)


def _fn_rmsnorm(x, g):
    return (_rms(x, g).astype(BF16),)


def _fn_swiglu(gate, up):
    return ((gate * _sigmoid(gate) * up).astype(BF16),)


def _make_fn_lora_mix(p1, p2):
    def fn(u, u_prev, mu):
        um = u + (u_prev - u) * mu
        return (jnp.tanh(um[:, :p1]).astype(BF16), um[:, p1:p1 + p2].astype(BF16),
                _sigmoid(um[:, p1 + p2:]).astype(BF16))
    return fn


def _fn_rwkv_prep(ru, ru_p, ku, ku_p, vu, vu_p, w_lin, a_lin, mu_r, mu_k, mu_v, w0, a0, k_k, k_a):
    r = ru + (ru_p - ru) * mu_r
    k = ku + (ku_p - ku) * mu_k
    v = vu + (vu_p - vu) * mu_v
    w_log = -_softplus(-(w0 + w_lin)) - 0.5
    lw = -jnp.exp(w_log)
    a = _sigmoid(a0 + a_lin)
    kk = k * k_k
    kk = kk / jnp.maximum(jnp.sqrt(_head_sum(kk * kk)), 1e-12)
    k_mod = k * (1.0 + (a - 1.0) * k_a)
    return r, lw, k_mod, v, kk, kk * a


def _fn_rwkv_post(y, r, k_mod, v, g, lnx_g, lnx_b, r_k):
    yc = y - _head_mean(y)
    yn = yc * lax.rsqrt(_head_mean(yc * yc) + GN_EPS) * lnx_g + lnx_b
    bonus = _head_sum(r * k_mod * r_k) * v
    return (((yn + bonus) * g).astype(BF16),)


def _head_rms(x, g):
    return x * lax.rsqrt(_head_mean(x * x) + RMS_EPS) * g


def _fn_fox_prep(q, k, qg, kg):
    return _head_rms(q, qg), _head_rms(k, kg)


def _fn_log_forget(f_raw, b):
    x = f_raw + b
    return (jnp.minimum(x, 0.0) - jnp.log(1.0 + jnp.exp(-jnp.abs(x))),)


def _fn_final(z, e_raw, h2, target, gate_b, ple_g):
    gate = _sigmoid(z + gate_b)
    out = h2 + gate * _rms(e_raw, ple_g)
    err = out - target
    return (0.5 * jnp.mean(err * err, axis=-1, keepdims=True),)


def _dot_bf16(a, b, ca, cb):
    return lax.dot_general(a.astype(BF16), b.astype(BF16), (((ca,), (cb,)), ((0,), (0,))),
                           preferred_element_type=F32)


@functools.partial(jax.custom_vjp, nondiff_argnums=(2, 3))
def _dot(a, b, ca, cb):
    return _dot_bf16(a, b, ca, cb)


def _dot_fwd(a, b, ca, cb):
    return _dot_bf16(a, b, ca, cb), (a, b)


def _dot_bwd(ca, cb, res, g):
    a, b = res
    ia, jb = 3 - ca, 3 - cb
    da = _dot_bf16(g, b, 2, jb) if ca == 2 else _dot_bf16(b, g, jb, 2)
    db = _dot_bf16(a, g, ia, 1) if cb == 1 else _dot_bf16(g, a, 1, ia)
    return da, db


_dot.defvjp(_dot_fwd, _dot_bwd)


def _scan_chunk(S0, r, lw, k, v, kk, b):
    B, L, _ = r.shape
    row = lax.broadcasted_iota(jnp.int32, (B, L, L), 1)
    col = lax.broadcasted_iota(jnp.int32, (B, L, L), 2)
    incl = col <= row
    strict = col < row
    cum = lax.dot_general(incl.astype(F32), lw, (((2,), (1,)), ((0,), (0,))), precision=HIGHEST,
                          preferred_element_type=F32)
    g_in, g_ex, g_inv = jnp.exp(cum), jnp.exp(cum - lw), jnp.exp(-cum)
    kkg, kd, bd, rg = kk * g_ex, k * g_inv, b * g_inv, r * g_in
    a_k = jnp.where(strict, _dot(kkg, kd, 2, 2), 0.0)
    a_b = jnp.where(strict, _dot(kkg, bd, 2, 2), 0.0)
    pw = -a_b
    inv = (row == col).astype(F32) + pw
    for _ in range(int(math.log2(L)) - 1):
        pw = _dot(pw, pw, 2, 1)
        inv = inv + _dot(inv, pw, 2, 1)
    sa = -_dot(inv, _dot(kkg, S0, 2, 2) + _dot(a_k, v, 2, 1), 2, 1)
    r_k = jnp.where(incl, _dot(rg, kd, 2, 2), 0.0)
    r_b = jnp.where(incl, _dot(rg, bd, 2, 2), 0.0)
    y = _dot(rg, S0, 2, 2) + _dot(r_k, v, 2, 1) + _dot(r_b, sa, 2, 1)
    g_end = jnp.exp(jnp.sum(lw, axis=1, keepdims=True))
    S1 = S0 * g_end + _dot(v, kd * g_end, 1, 1) + _dot(sa, bd * g_end, 1, 1)
    return y, S1


def _scan_heads_per_step(H):
    return next(hb for hb in (SCAN_HEADS_PER_STEP, 2, 1) if H % hb == 0)


def _scan_fwd_call(r, lw, k, v, kk, b):
    H, T, N = r.shape
    L = SCAN_CHUNK
    n_chunks = T // L
    hb = _scan_heads_per_step(H)

    def body(r_ref, lw_ref, k_ref, v_ref, kk_ref, b_ref, y_ref, s0_ref, state):
        @pl.when(pl.program_id(1) == 0)
        def _():
            state[...] = jnp.zeros_like(state)

        S0 = state[...]
        s0_ref[:, 0] = S0
        y, S1 = _scan_chunk(S0, r_ref[...], lw_ref[...], k_ref[...], v_ref[...], kk_ref[...], b_ref[...])
        y_ref[...] = y
        state[...] = S1

    blk = pl.BlockSpec((hb, L, N), lambda h, c: (h, c, 0))
    return pl.pallas_call(
        body, name="rwkv_scan_fwd", grid=(H // hb, n_chunks),
        in_specs=[blk] * 6,
        out_specs=[blk, pl.BlockSpec((hb, 1, N, N), lambda h, c: (h, c, 0, 0))],
        out_shape=[jax.ShapeDtypeStruct((H, T, N), F32), jax.ShapeDtypeStruct((H, n_chunks, N, N), F32)],
        scratch_shapes=[pltpu.VMEM((hb, N, N), F32)],
        compiler_params=pltpu.CompilerParams(dimension_semantics=("parallel", "arbitrary")),
    )(r, lw, k, v, kk, b)


def _scan_bwd_call(r, lw, k, v, kk, b, s0s, dy):
    H, T, N = r.shape
    L = SCAN_CHUNK
    n_chunks = T // L
    hb = _scan_heads_per_step(H)

    def body(r_ref, lw_ref, k_ref, v_ref, kk_ref, b_ref, s0_ref, dy_ref, dr, dlw, dk, dv, dkk, db, d_state):
        @pl.when(pl.program_id(1) == 0)
        def _():
            d_state[...] = jnp.zeros_like(d_state)

        _, vjp_fn = jax.vjp(_scan_chunk, s0_ref[:, 0], r_ref[...], lw_ref[...], k_ref[...], v_ref[...], kk_ref[...],
                            b_ref[...])
        grads = vjp_fn((dy_ref[...], d_state[...]))
        d_state[...] = grads[0]
        for o_ref, g in zip((dr, dlw, dk, dv, dkk, db), grads[1:]):
            o_ref[...] = g

    blk = pl.BlockSpec((hb, L, N), lambda h, c: (h, n_chunks - 1 - c, 0))
    return pl.pallas_call(
        body, name="rwkv_scan_bwd", grid=(H // hb, n_chunks),
        in_specs=[blk] * 6 + [pl.BlockSpec((hb, 1, N, N), lambda h, c: (h, n_chunks - 1 - c, 0, 0)), blk],
        out_specs=[blk] * 6,
        out_shape=[jax.ShapeDtypeStruct((H, T, N), F32)] * 6,
        scratch_shapes=[pltpu.VMEM((hb, N, N), F32)],
        compiler_params=pltpu.CompilerParams(dimension_semantics=("parallel", "arbitrary")),
    )(r, lw, k, v, kk, b, s0s, dy)


@jax.custom_vjp
def _rwkv_scan(r, lw, k, v, kk, b):
    return _scan_fwd_call(r, lw, k, v, kk, b)[0]


def _rwkv_scan_fwd(r, lw, k, v, kk, b):
    y, s0s = _scan_fwd_call(r, lw, k, v, kk, b)
    return y, (r, lw, k, v, kk, b, s0s)


def _rwkv_scan_bwd(res, dy):
    return tuple(_scan_bwd_call(*res, dy))


_rwkv_scan.defvjp(_rwkv_scan_fwd, _rwkv_scan_bwd)


def _nt(a, b):
    return lax.dot_general(a.astype(BF16), b.astype(BF16), (((1,), (1,)), ((), ())), preferred_element_type=F32)


def _nn(a, b):
    return lax.dot_general(a.astype(BF16), b.astype(BF16), (((1,), (0,)), ((), ())), preferred_element_type=F32)


def _tn(a, b):
    return lax.dot_general(a.astype(BF16), b.astype(BF16), (((0,), (0,)), ((), ())), preferred_element_type=F32)


def _attn_scores(qs, k_ref, cr_ref, row_bias, j, kb, q0, masked):
    ks = pl.multiple_of(j * kb, kb)
    kj = k_ref[0, pl.ds(ks, kb), :]
    s = _nt(qs, kj) + row_bias - cr_ref[0, j]
    if masked:
        qi = q0 + lax.broadcasted_iota(jnp.int32, s.shape, 0)
        ki = ks + lax.broadcasted_iota(jnp.int32, s.shape, 1)
        s = jnp.where(ki <= qi, s, -jnp.inf)
    return s, kj, ks


def _attn_specs(T, N, bq, kb):
    q_spec = pl.BlockSpec((1, bq, N), lambda h, i: (h, i, 0))
    kv_spec = pl.BlockSpec((1, T, N), lambda h, i: (h, 0, 0))
    col_spec = pl.BlockSpec((1, bq, 1), lambda h, i: (h, i, 0))
    row_spec = pl.BlockSpec((1, T // kb, 1, kb), lambda h, i: (h, 0, 0, 0))
    return q_spec, kv_spec, col_spec, row_spec


def _grid_marks(H, n_q):
    h, i = pl.program_id(0), pl.program_id(1)
    return (h == 0) & (i == 0), (h == H // 2) & (i == 0), (h == H - 1) & (i == n_q - 1)


def _attn_fwd_call(q, k, v, c_col, c_rows, gather_xs):
    H, T, N = q.shape
    bq, kb = min(ATTN_BLOCK_Q, T), c_rows.shape[3]
    q_spec, kv_spec, col_spec, row_spec = _attn_specs(T, N, bq, kb)
    n = len(gather_xs)

    def body(q_ref, k_ref, v_ref, cc_ref, cr_ref, *rest):
        x_refs, (o_ref, o32_ref, lse_ref), out_refs, sems = rest[:n], rest[n:n + 3], rest[n + 3:2 * n + 3], rest[2 * n + 3:]
        first, middle, last = _grid_marks(H, T // bq)
        start, relay, finish = _gather_phases(x_refs, out_refs, *sems)
        pl.when(first)(start)
        pl.when(middle)(relay)
        q0 = pl.program_id(1) * bq
        qs = (q_ref[0] * (HEAD_DIM ** -0.5)).astype(BF16)
        cc = cc_ref[0]

        def step(j, carry, masked):
            m, l, acc = carry
            s, _, ks = _attn_scores(qs, k_ref, cr_ref, cc, j, kb, q0, masked)
            m_new = jnp.maximum(m, jnp.max(s, axis=-1, keepdims=True))
            alpha = jnp.exp(m - m_new)
            p = jnp.exp(s - m_new)
            l = alpha * l + jnp.sum(p, axis=-1, keepdims=True)
            p_hi = p.astype(BF16)
            p_lo = p - p_hi.astype(F32)
            vj = v_ref[0, pl.ds(ks, kb), :]
            acc = alpha * acc + (_nn(p_hi, vj) + _nn(p_lo, vj))
            return m_new, l, acc

        n_full = q0 // kb
        init = (jnp.full((bq, 1), -jnp.inf, F32), jnp.zeros((bq, 1), F32), jnp.zeros((bq, N), F32))
        carry = lax.fori_loop(0, n_full, functools.partial(step, masked=False), init)
        m, l, acc = step(n_full, carry, masked=True)
        o = acc / l
        o_ref[0] = o.astype(o_ref.dtype)
        o32_ref[0] = o
        lse_ref[0] = m + jnp.log(l)
        pl.when(last)(finish)

    any_spec = pl.BlockSpec(memory_space=pl.ANY)
    return pl.pallas_call(
        body, name="fox_attn_fwd", grid=(H, T // bq),
        in_specs=[q_spec, kv_spec, kv_spec, col_spec, row_spec] + [any_spec] * n,
        out_specs=[q_spec, q_spec, col_spec] + [any_spec] * n,
        out_shape=[jax.ShapeDtypeStruct((H, T, N), BF16),
                   jax.ShapeDtypeStruct((H, T, N), F32),
                   jax.ShapeDtypeStruct((H, T, 1), F32)] + _gather_out_shapes(gather_xs),
        scratch_shapes=_comm_semaphores(n),
        compiler_params=pltpu.CompilerParams(dimension_semantics=("arbitrary", "arbitrary"),
                                             vmem_limit_bytes=VMEM_LIMIT),
    )(q, k, v, c_col, c_rows, *gather_xs)


def _attn_bwd_call(q, k, v, c_col, c_rows, o, lse, do, scatter_parts):
    H, T, N = q.shape
    bq, kb = min(ATTN_BLOCK_Q, T), c_rows.shape[3]
    q_spec, kv_spec, col_spec, row_spec = _attn_specs(T, N, bq, kb)
    n = len(scatter_parts)

    def body(q_ref, k_ref, v_ref, cc_ref, cr_ref, o_ref, lse_ref, do_ref, *rest):
        a_refs, (dq_ref, dk_ref, dv_ref, dcr_ref), b_refs, sems = rest[:n], rest[n:n + 4], rest[n + 4:2 * n + 4], rest[2 * n + 4:]
        first, _, last = _grid_marks(H, T // bq)
        start, finish = _scatter_phases(a_refs, b_refs, *sems)
        pl.when(first)(start)
        i = pl.program_id(1)
        q0 = i * bq

        @pl.when(i == 0)
        def _():
            dk_ref[...] = jnp.zeros_like(dk_ref)
            dv_ref[...] = jnp.zeros_like(dv_ref)
            dcr_ref[...] = jnp.zeros_like(dcr_ref)

        qs = (q_ref[0] * (HEAD_DIM ** -0.5)).astype(BF16)
        do = do_ref[0]
        delta = jnp.sum(do.astype(F32) * o_ref[0], axis=-1, keepdims=True)
        row_bias = cc_ref[0] - lse_ref[0]

        def step(j, dq, masked):
            s, kj, ks = _attn_scores(qs, k_ref, cr_ref, row_bias, j, kb, q0, masked)
            p = jnp.exp(s)
            ds = p * (_nt(do, v_ref[0, pl.ds(ks, kb), :]) - delta)
            ds_b = ds.astype(BF16)
            dk_ref[0, pl.ds(ks, kb), :] += _tn(ds_b, qs)
            dv_ref[0, pl.ds(ks, kb), :] += _tn(p, do)
            dcr_ref[0, j] -= jnp.sum(ds, axis=0, keepdims=True)
            return dq + _nn(ds_b, kj)

        n_full = q0 // kb
        dq = lax.fori_loop(0, n_full, functools.partial(step, masked=False), jnp.zeros((bq, N), F32))
        dq_ref[0] = step(n_full, dq, masked=True) * (HEAD_DIM ** -0.5)
        pl.when(last)(finish)

    any_spec = pl.BlockSpec(memory_space=pl.ANY)
    return pl.pallas_call(
        body, name="fox_attn_bwd", grid=(H, T // bq),
        in_specs=[q_spec, kv_spec, kv_spec, col_spec, row_spec, q_spec, col_spec, q_spec] + [any_spec] * n,
        out_specs=[q_spec, kv_spec, kv_spec, row_spec] + [any_spec] * n,
        out_shape=[jax.ShapeDtypeStruct((H, T, N), F32)] * 3 + [jax.ShapeDtypeStruct(c_rows.shape, F32)]
        + [jax.ShapeDtypeStruct(a.shape, a.dtype) for a in scatter_parts],
        scratch_shapes=_comm_semaphores(n),
        compiler_params=pltpu.CompilerParams(dimension_semantics=("arbitrary", "arbitrary"),
                                             vmem_limit_bytes=VMEM_LIMIT),
    )(q, k, v, c_col, c_rows, o, lse, do, *scatter_parts)


@jax.custom_vjp
def _fox_attn(q, k, v, c, late_blocks, carrier):
    return _fox_attn_fwd(q, k, v, c, late_blocks, carrier)[0]


def _attn_bias_views(c):
    H, T = c.shape
    kb = min(ATTN_BLOCK_K, T)
    return c[:, :, None], c.reshape(H, T // kb, 1, kb)


def _fox_attn_fwd(q, k, v, c, late_blocks, carrier):
    o, o32, lse, *gathered = _attn_fwd_call(q, k, v, *_attn_bias_views(c), late_blocks)
    return (o, tuple(gathered)), (q, k, v, c, o32, lse)


def _fox_attn_bwd(res, cts):
    q, k, v, c, o32, lse = res
    do, d_gathered = cts
    dq, dk, dv, dc_rows, *parts = _attn_bwd_call(q, k, v, *_attn_bias_views(c), o32, lse, do, d_gathered)
    no_grad = tuple(jnp.zeros(a.shape[1:], a.dtype) for a in parts)
    return dq, dk, dv, dc_rows.reshape(c.shape), no_grad, tuple(parts)


_fox_attn.defvjp(_fox_attn_fwd, _fox_attn_bwd)


N_PEERS = N_DEV - 1


def _all_gather(xs, name):
    n = len(xs)

    def body(*refs):
        start, relay, finish = _gather_phases(refs[:n], refs[n:2 * n], *refs[2 * n:])
        start()
        relay()
        finish()

    any_spec = pl.BlockSpec(memory_space=pl.ANY)
    return pl.pallas_call(
        body, name=name,
        out_shape=_gather_out_shapes(xs),
        in_specs=[any_spec] * n, out_specs=[any_spec] * n,
        scratch_shapes=_comm_semaphores(n),
    )(*xs)


def _gather_out_shapes(xs):
    return [jax.ShapeDtypeStruct((N_DEV,) + x.shape, x.dtype) for x in xs]


def _comm_semaphores(n):
    return [pltpu.SemaphoreType.DMA((N_PEERS * n,)), pltpu.SemaphoreType.DMA((N_PEERS * n,)),
            pltpu.SemaphoreType.DMA((n,))]


def _gather_phases(x_refs, out_refs, send_sems, recv_sems, local_sems):
    n = len(x_refs)
    x_, y_, c_ = lax.axis_index("x"), lax.axis_index("y"), lax.axis_index("c")
    me, sibling = (x_, y_, c_), (x_, y_, 1 - c_)
    chips = [(1 - x_, y_), (x_, 1 - y_), (1 - x_, 1 - y_)]

    def slot(t, px, py, pc):
        return out_refs[t].at[4 * px + 2 * py + pc]

    def copy(t, k, block, to, src=None):
        return pltpu.make_async_remote_copy(
            src_ref=slot(t, *block) if src is None else src, dst_ref=slot(t, *block),
            send_sem=send_sems.at[k * n + t], recv_sem=recv_sems.at[k * n + t],
            device_id=to, device_id_type=pl.DeviceIdType.MESH)

    def mine():
        return [pltpu.make_async_copy(x_refs[t], slot(t, *me), local_sems.at[t]) for t in range(n)]

    def first():
        return ([copy(t, 0, me, sibling, src=x_refs[t]) for t in range(n)]
                + [copy(t, 1 + j, me, (*chip, c_), src=x_refs[t]) for j, chip in enumerate(chips) for t in range(n)])

    def passed():
        return [copy(t, 4 + j, (*chip, c_), sibling) for j, chip in enumerate(chips) for t in range(n)]

    def start():
        for cp in mine() + first():
            cp.start()

    def relay():
        for j, chip in enumerate(chips):
            for t in range(n):
                copy(t, 1 + j, (*chip, c_), me).wait_recv()
                copy(t, 4 + j, (*chip, c_), sibling).start()

    def finish():
        for t in range(n):
            copy(t, 0, sibling, me).wait_recv()
        for j, chip in enumerate(chips):
            for t in range(n):
                copy(t, 4 + j, (*chip, 1 - c_), me).wait_recv()
        for cp in first() + passed():
            cp.wait_send()
        for cp in mine():
            cp.wait()

    return start, relay, finish


def _all_to_all(parts, name):
    n = len(parts)

    def body(*refs):
        start, finish = _scatter_phases(refs[:n], refs[n:2 * n], *refs[2 * n:])
        start()
        finish()

    any_spec = pl.BlockSpec(memory_space=pl.ANY)
    return pl.pallas_call(
        body, name=name,
        out_shape=[jax.ShapeDtypeStruct(a.shape, a.dtype) for a in parts],
        in_specs=[any_spec] * n, out_specs=[any_spec] * n,
        scratch_shapes=_comm_semaphores(n),
    )(*parts)


def _scatter_phases(a_refs, b_refs, send_sems, recv_sems, local_sems):
    n = len(a_refs)
    x_, y_, c_ = lax.axis_index("x"), lax.axis_index("y"), lax.axis_index("c")
    me_idx = 4 * x_ + 2 * y_ + c_

    def copies():
        out = [pltpu.make_async_copy(a_refs[t].at[me_idx], b_refs[t].at[me_idx], local_sems.at[t]) for t in range(n)]
        for rel in range(1, N_DEV):
            px = 1 - x_ if rel & 4 else x_
            py = 1 - y_ if rel & 2 else y_
            pc = 1 - c_ if rel & 1 else c_
            for t in range(n):
                out.append(pltpu.make_async_remote_copy(
                    src_ref=a_refs[t].at[4 * px + 2 * py + pc], dst_ref=b_refs[t].at[me_idx],
                    send_sem=send_sems.at[(rel - 1) * n + t], recv_sem=recv_sems.at[(rel - 1) * n + t],
                    device_id=(px, py, pc), device_id_type=pl.DeviceIdType.MESH))
        return out

    def start():
        for cp in copies():
            cp.start()

    def finish():
        for cp in copies():
            cp.wait()

    return start, finish


def _reduce_adamw(parts, w, m, v, name):
    R, C = w.shape
    tr = max(t for t in (256, 128, PACK_ROW_QUANTUM) if R % t == 0)

    def body(p_ref, w_ref, m_ref, v_ref, g_out, d_out, m_out, v_out):
        g = p_ref[0]
        for i in range(1, N_DEV):
            g = g + p_ref[i]
        m_new = ADAM_B1 * m_ref[...] + (1.0 - ADAM_B1) * g
        v_new = ADAM_B2 * v_ref[...] + (1.0 - ADAM_B2) * (g * g)
        m_hat = m_new / (1.0 - ADAM_B1 ** ADAM_STEP)
        v_hat = v_new / (1.0 - ADAM_B2 ** ADAM_STEP)
        g_out[...] = g
        d_out[...] = -ADAM_LR * (m_hat / (jnp.sqrt(v_hat) + ADAM_EPS) + ADAM_WD * w_ref[...])
        m_out[...] = m_new
        v_out[...] = v_new

    spec = pl.BlockSpec((tr, C), lambda i: (i, 0))
    return pl.pallas_call(
        body, name=name, grid=(R // tr,),
        in_specs=[pl.BlockSpec((N_DEV, tr, C), lambda i: (0, i, 0)), spec, spec, spec],
        out_specs=[spec] * 4,
        out_shape=[jax.ShapeDtypeStruct((R, C), F32)] * 4,
        compiler_params=pltpu.CompilerParams(dimension_semantics=("parallel",), vmem_limit_bytes=VMEM_LIMIT),
    )(parts, w, m, v)


def _pack(arrays, dtype):
    flat = jnp.concatenate([a.reshape(-1).astype(dtype) for a in arrays])
    rows = _round_up(-(-flat.shape[0] // PACK_COLS), PACK_ROW_QUANTUM)
    flat = jnp.pad(flat, (0, rows * PACK_COLS - flat.shape[0]))
    return flat.reshape(rows, PACK_COLS)


def _unpack(packed, shapes):
    lead = packed.shape[:-2]
    flat = packed.reshape(lead + (-1,))
    out, off = [], 0
    for s in shapes:
        n = math.prod(s)
        out.append(flat[..., off:off + n].reshape(lead + tuple(s)))
        off += n
    return out


def _travel_layout(name, block):
    return block.T if SHARDED[name] else block


REDUCE_BLOCK_BYTES = 4 * 1024 * 1024


def _reduce_parts(parts, name):
    _, R, C = parts.shape
    per_col = N_DEV * R * parts.dtype.itemsize
    tc = next((t for t in range(C - C % LANES, 0, -LANES) if C % t == 0 and t * per_col <= REDUCE_BLOCK_BYTES), C)

    def body(p_ref, o_ref):
        g = p_ref[0].astype(F32)
        for i in range(1, N_DEV):
            g = g + p_ref[i].astype(F32)
        o_ref[...] = g

    return pl.pallas_call(
        body, name=name, grid=(C // tc,),
        in_specs=[pl.BlockSpec((N_DEV, R, tc), lambda j: (0, 0, j))],
        out_specs=pl.BlockSpec((R, tc), lambda j: (0, j)),
        out_shape=jax.ShapeDtypeStruct((R, C), F32),
        compiler_params=pltpu.CompilerParams(dimension_semantics=("parallel",), vmem_limit_bytes=VMEM_LIMIT),
    )(parts)


def _adamw(g, w, m, v, name):
    R, C = w.shape
    tr = next((t for t in (512, 256, 128, 64, 32, 16, 8) if R % t == 0 and t * C * 4 <= 2 * 1024 * 1024), R)

    def body(g_ref, w_ref, m_ref, v_ref, d_out, m_out, v_out):
        g_ = g_ref[...]
        m_new = ADAM_B1 * m_ref[...] + (1.0 - ADAM_B1) * g_
        v_new = ADAM_B2 * v_ref[...] + (1.0 - ADAM_B2) * (g_ * g_)
        m_hat = m_new / (1.0 - ADAM_B1 ** ADAM_STEP)
        v_hat = v_new / (1.0 - ADAM_B2 ** ADAM_STEP)
        d_out[...] = -ADAM_LR * (m_hat / (jnp.sqrt(v_hat) + ADAM_EPS) + ADAM_WD * w_ref[...])
        m_out[...] = m_new
        v_out[...] = v_new

    spec = pl.BlockSpec((tr, C), lambda i: (i, 0))
    return pl.pallas_call(
        body, name=name, grid=(R // tr,),
        in_specs=[spec] * 4, out_specs=[spec] * 3,
        out_shape=[jax.ShapeDtypeStruct((R, C), F32)] * 3,
        compiler_params=pltpu.CompilerParams(dimension_semantics=("parallel",), vmem_limit_bytes=VMEM_LIMIT),
    )(g, w, m, v)


def _to_heads(u):
    return jnp.transpose(u.reshape(u.shape[0], -1, HEAD_DIM), (1, 0, 2))


def _split_cols(u, n):
    width = u.shape[1] // n

    @jax.custom_vjp
    def run(u):
        return tuple(u[:, i * width:(i + 1) * width] for i in range(n))

    def fwd(u):
        return run(u), None

    def bwd(_, cts):
        return (jnp.concatenate(cts, axis=1),)

    run.defvjp(fwd, bwd)
    return run(u)


def _from_heads(uh):
    H, T, N = uh.shape
    return jnp.transpose(uh, (1, 0, 2)).reshape(T, H * N)


def _shift(uh):
    return jnp.pad(uh, ((0, 0), (1, 0), (0, 0)))[:, :-1]


def _pad_cols(a, width):
    return jnp.pad(a, ((0, 0), (0, width - a.shape[1])))


def _split_rows(w, sizes):
    offsets = [sum(sizes[:i]) for i in range(len(sizes))]

    @jax.custom_vjp
    def run(w):
        return tuple(w[o:o + s] for o, s in zip(offsets, sizes))

    def fwd(w):
        return run(w), None

    def bwd(_, cts):
        return (jnp.concatenate(cts, axis=0),)

    run.defvjp(fwd, bwd)
    return run(w)


def _pad_rows(a, height):
    return jnp.pad(a, ((0, height - a.shape[0]), (0, 0)))


def _vec(a):
    return a.reshape(1, 1, -1)


TM_WIDE = 128


def _mixing_half(W, small, x, late_blocks, carrier):
    T, D = x.shape
    vec = _vec
    tm_wide = TM_WIDE
    rw = small['w0'].shape[-1]
    fw = W['w_out'].shape[0] - rw
    heads_r, heads_f = rw // HEAD_DIM, fw // HEAD_DIM
    dl, al, gl = W['w2'].shape[1], W['a2'].shape[1], W['g2'].shape[1]
    dl_p, al_p, gl_p = _round_up(dl, LANES), _round_up(al, LANES), _round_up(gl, LANES)
    f_p = _round_up(heads_f, LANES)
    rwkv_cols = 3 * rw + dl + al + gl
    tm_head = next((t for t in (1024, 512) if T % t == 0), T)

    o_w, o_a, o_g = 3 * rw, 3 * rw + dl, 3 * rw + dl + al
    w_rkv, w_xw, w_xa, w_xg, w_qkv, w_fg = _split_rows(W['w_in'], (3 * rw, dl, al, gl, 3 * fw, heads_f))
    w_lora = jnp.concatenate([_pad_rows(w_xw, dl_p), _pad_rows(w_xa, al_p), _pad_rows(w_xg, gl_p)], axis=0)
    w_f = _pad_rows(w_fg, f_p)
    mu = small['shift_mu'].reshape(1, -1)
    mu_lora = jnp.concatenate([_pad_cols(mu[:, o_w:o_a], dl_p), _pad_cols(mu[:, o_a:o_g], al_p),
                               _pad_cols(mu[:, o_g:rwkv_cols], gl_p)], axis=1)

    (xn,) = _stage("attn_norm", _fn_rmsnorm, [x[None]], [vec(small['attn_norm_g'])], tm_wide)
    xn = xn[0]
    u_rkv = _mm_t(xn, w_rkv, "in_rkv")
    u_lora = _mm_t(xn, w_lora, "in_lora")
    u_qkv = _mm_t(xn, w_qkv, "in_qkv")
    f_raw = _mm_t(xn, w_f, "in_f")

    u_lora3 = u_lora[None]
    xw_t, xa_m, xg_s = _stage("lora_mix", _make_fn_lora_mix(dl_p, al_p), [u_lora3, _shift(u_lora3)],
                              [vec(mu_lora)], tm_wide)
    w_lin = _mm_t(xw_t[0], _pad_cols(W['w2'], dl_p), "w2")
    a_lin = _mm_t(xa_m[0], _pad_cols(W['a2'], al_p), "a2")
    gate_r = _mm_t(xg_s[0], _pad_cols(W['g2'], gl_p), "g2")
    ru, ku, vu = (u[None] for u in _split_cols(u_rkv, 3))
    mu_r, mu_k, mu_v = (vec(mu[:, i * rw:(i + 1) * rw]) for i in range(3))
    prepped = _stage(
        "rwkv_prep", _fn_rwkv_prep,
        [ru, _shift(ru), ku, _shift(ku), vu, _shift(vu), w_lin[None], a_lin[None]],
        [mu_r, mu_k, mu_v, vec(small['w0']), vec(small['a0']), vec(small['k_k']), vec(small['k_a'])],
        tm_head, cols=HEAD_PAIR)
    r, lw, k_mod, v, kk, b = prepped
    y_scan = _rwkv_scan(*(_to_heads(t[0]) for t in prepped))
    (y_rwkv,) = _stage("rwkv_post", _fn_rwkv_post, [_from_heads(y_scan)[None], r, k_mod, v, gate_r[None]],
                       [vec(small['lnx_g']), vec(small['lnx_b']), vec(small['r_k'])], tm_head, cols=HEAD_PAIR)

    qu, kf, vf = _split_cols(u_qkv, 3)
    qg = vec(jnp.tile(small['q_norm_g'].reshape(-1), heads_f))
    kg = vec(jnp.tile(small['k_norm_g'].reshape(-1), heads_f))
    qn, kn = _stage("fox_prep", _fn_fox_prep, [qu[None], kf[None]], [qg, kg], tm_head, cols=HEAD_PAIR)
    fb = _pad_cols(small['fgate_b'].reshape(1, -1), f_p)
    (log_f,) = _stage("log_forget", _fn_log_forget, [f_raw[None]], [vec(fb)], tm_head)
    c = jnp.cumsum(log_f[0][:, :heads_f], axis=0).T
    y_fox, gathered_late = _fox_attn(_to_heads(qn[0]), _to_heads(kn[0]), _to_heads(vf), c, late_blocks, carrier)

    y_cat = jnp.concatenate([y_rwkv[0], _from_heads(y_fox)], axis=-1)
    return _mm_add(x, y_cat, W['w_out'], "out"), gathered_late


def _channel_half_loss(W, small, h1, p, target):
    vec = _vec
    tm_wide = TM_WIDE
    (hn,) = _stage("ffn_norm", _fn_rmsnorm, [h1[None]], [vec(small['ffn_norm_g'])], tm_wide)
    gate = _mm_t(hn[0], W['w_gate'], "gate")
    up = _mm_t(hn[0], W['w_up'], "up")
    (act,) = _stage("swiglu", _fn_swiglu, [gate[None], up[None]], [], tm_wide)
    h2 = _mm_add(h1, act[0], W['w_down'], "down")
    e_raw = _mm_t(p, W['ple_proj'], "ple_proj")
    (hg,) = _stage("ple_gate_norm", _fn_rmsnorm, [h2[None]], [vec(small['ple_gate_norm_g'])], tm_wide)
    z = _mm(hg[0], W['ple_gate_w'], "ple_gate")
    (loss_rows,) = _stage("final", _fn_final, [z[None], e_raw[None], h2[None], target[None]],
                          [vec(small['ple_gate_b']), vec(small['ple_norm_g'])], tm_wide)
    return jnp.sum(loss_rows)


def kernel(x, p, attn_norm_g, w_in, shift_mu, w0, w2, a0, a2, g2, k_k, k_a, r_k, lnx_g, lnx_b, q_norm_g, k_norm_g, fgate_b, w_out, ffn_norm_g, w_gate, w_up, w_down, ple_proj, ple_norm_g, ple_gate_norm_g, ple_gate_w, ple_gate_b, loss_target, m_attn_norm_g, m_w_in, m_shift_mu, m_w0, m_w2, m_a0, m_a2, m_g2, m_k_k, m_k_a, m_r_k, m_lnx_g, m_lnx_b, m_q_norm_g, m_k_norm_g, m_fgate_b, m_w_out, m_ffn_norm_g, m_w_gate, m_w_up, m_w_down, m_ple_proj, m_ple_norm_g, m_ple_gate_norm_g, m_ple_gate_w, m_ple_gate_b, v_attn_norm_g, v_w_in, v_shift_mu, v_w0, v_w2, v_a0, v_a2, v_g2, v_k_k, v_k_a, v_r_k, v_lnx_g, v_lnx_b, v_q_norm_g, v_k_norm_g, v_fgate_b, v_w_out, v_ffn_norm_g, v_w_gate, v_w_up, v_w_down, v_ple_proj, v_ple_norm_g, v_ple_gate_norm_g, v_ple_gate_w, v_ple_gate_b):
    weights = dict(zip(WEIGHT_NAMES, (attn_norm_g, w_in, shift_mu, w0, w2, a0, a2, g2, k_k, k_a, r_k, lnx_g, lnx_b,
                                      q_norm_g, k_norm_g, fgate_b, w_out, ffn_norm_g, w_gate, w_up, w_down, ple_proj,
                                      ple_norm_g, ple_gate_norm_g, ple_gate_w, ple_gate_b)))
    m_in = dict(zip(WEIGHT_NAMES, (m_attn_norm_g, m_w_in, m_shift_mu, m_w0, m_w2, m_a0, m_a2, m_g2, m_k_k, m_k_a, m_r_k,
                                   m_lnx_g, m_lnx_b, m_q_norm_g, m_k_norm_g, m_fgate_b, m_w_out, m_ffn_norm_g, m_w_gate,
                                   m_w_up, m_w_down, m_ple_proj, m_ple_norm_g, m_ple_gate_norm_g, m_ple_gate_w,
                                   m_ple_gate_b)))
    v_in = dict(zip(WEIGHT_NAMES, (v_attn_norm_g, v_w_in, v_shift_mu, v_w0, v_w2, v_a0, v_a2, v_g2, v_k_k, v_k_a, v_r_k,
                                   v_lnx_g, v_lnx_b, v_q_norm_g, v_k_norm_g, v_fgate_b, v_w_out, v_ffn_norm_g, v_w_gate,
                                   v_w_up, v_w_down, v_ple_proj, v_ple_norm_g, v_ple_gate_norm_g, v_ple_gate_w,
                                   v_ple_gate_b)))
    small_shapes = [weights[n].shape for n in SMALL_NAMES]
    small = {n: weights[n] for n in SMALL_NAMES}

    def whole(stacks, names):
        return {n: g.reshape(N_DEV * g.shape[1], g.shape[2]) for n, g in zip(names, stacks)}

    def stacked(tree, names, like):
        return tuple(tree[n].reshape(g.shape) for n, g in zip(names, like))

    travelling = {n: _travel_layout(n, weights[n][0]).astype(BF16) for n in SHARDED_NAMES}
    gathered_early = _all_gather([travelling[n] for n in EARLY_NAMES], "gather_weights")
    late_blocks = tuple(travelling[n] for n in LATE_NAMES)
    carrier = tuple(jnp.zeros((N_DEV,) + b.shape, b.dtype) for b in late_blocks)

    (h1, gathered_late), mixing_vjp = jax.vjp(_mixing_half, whole(gathered_early, EARLY_NAMES), small, x[0],
                                              late_blocks, carrier)
    loss_local, (d_late, d_small_b, d_h1) = jax.value_and_grad(_channel_half_loss, argnums=(0, 1, 2))(
        whole(gathered_late, LATE_NAMES), small, h1, p[0, 0], loss_target[0])
    d_early, d_small_a, d_x, _, parts_late = mixing_vjp((d_h1, stacked(d_late, LATE_NAMES, gathered_late)))
    d_small = {n: d_small_a[n] + d_small_b[n] for n in SMALL_NAMES}
    loss = lax.psum(loss_local, MESH_AXES)

    parts_early = _all_to_all(stacked(d_early, EARLY_NAMES, gathered_early), "scatter_grads")
    parts = dict(zip(EARLY_NAMES + LATE_NAMES, list(parts_early) + list(parts_late)))
    (small_parts,) = _all_gather([_pack([d_small[n] for n in SMALL_NAMES], F32)], "gather_small_grads")

    def pack_f32(tree, names):
        return _pack([tree[n] for n in names], F32)

    sml = _reduce_adamw(small_parts, pack_f32(weights, SMALL_NAMES), pack_f32(m_in, SMALL_NAMES),
                        pack_f32(v_in, SMALL_NAMES), "adamw_replicated")
    by_kind = [dict(zip(SMALL_NAMES, _unpack(sml[kind], small_shapes))) for kind in range(4)]
    for n in SHARDED_NAMES:
        g = _travel_layout(n, _reduce_parts(parts[n], "reduce_" + n))
        upd = _adamw(g, weights[n][0], m_in[n][0], v_in[n][0], "adamw_" + n)
        for kind, val in enumerate((g, *upd)):
            by_kind[kind][n] = val[None]
    outs = [by_kind[kind][n] for kind in range(4) for n in WEIGHT_NAMES]
    return (loss, d_x[None], *outs)
```

```python
import functools
import math

import jax
import jax.numpy as jnp
from jax import lax
from jax.experimental import pallas as pl
from jax.experimental.pallas import tpu as pltpu

F32 = jnp.float32
BF16 = jnp.bfloat16
HIGHEST = lax.Precision.HIGHEST

N_DEV = 8
MESH_AXES = ("x", "y", "c")
HEAD_DIM = 64
SCAN_CHUNK = 64
SCAN_HEADS_PER_STEP = 16
ATTN_BLOCK_Q = 1024
ATTN_BLOCK_K = 1024
LANES = 128
HEAD_PAIR = 2 * HEAD_DIM
PACK_COLS = 1024
PACK_ROW_QUANTUM = 64
VMEM_LIMIT = 48 * 1024 * 1024
RMS_EPS = 1e-6
GN_EPS = 64e-5
ADAM_LR, ADAM_B1, ADAM_B2, ADAM_EPS, ADAM_WD, ADAM_STEP = 0.001, 0.9, 0.999, 1e-08, 0.01, 10

WEIGHT_NAMES = ['attn_norm_g', 'w_in', 'shift_mu', 'w0', 'w2', 'a0', 'a2', 'g2', 'k_k', 'k_a', 'r_k', 'lnx_g', 'lnx_b',
                'q_norm_g', 'k_norm_g', 'fgate_b', 'w_out', 'ffn_norm_g', 'w_gate', 'w_up', 'w_down', 'ple_proj',
                'ple_norm_g', 'ple_gate_norm_g', 'ple_gate_w', 'ple_gate_b']
SHARDED = {'w_in': True, 'w2': True, 'a2': True, 'g2': True, 'w_out': False, 'w_gate': True, 'w_up': True,
           'w_down': False, 'ple_proj': True, 'ple_gate_w': False}
SHARDED_NAMES = [n for n in WEIGHT_NAMES if n in SHARDED]
SMALL_NAMES = [n for n in WEIGHT_NAMES if n not in SHARDED]
EARLY_NAMES = ['w_in', 'w2', 'a2', 'g2', 'w_out']
LATE_NAMES = [n for n in SHARDED_NAMES if n not in EARLY_NAMES]


def _round_up(n, q):
    return -(-n // q) * q


def _tile_candidates(n, cap):
    sizes = {t for t in range(LANES, min(n, cap) + 1, LANES) if n % t == 0}
    return sorted(sizes | ({n} if n <= cap or not sizes else set()))


MM_TILE_CAP = 2048
MM_VMEM_BUDGET = 38 * 1024 * 1024
MM_STEP_COST_BYTES = 1024 * 1024


def _mm_tiles(I, J, C, a_size, b_size, o_size):
    best = None
    for ti in _tile_candidates(I, MM_TILE_CAP):
        for tj in _tile_candidates(J, MM_TILE_CAP):
            for tc in _tile_candidates(C, MM_TILE_CAP):
                n_c = C // tc
                blocks = 2 * (ti * tc * a_size + tc * tj * b_size + ti * tj * o_size)
                temporaries = (ti * tc + tc * tj) * 2 + ti * tj * 4 * (2 if n_c > 1 else 1)
                if blocks + temporaries > MM_VMEM_BUDGET:
                    continue
                traffic = (I * C * a_size * (1 if n_c == 1 else J // tj) + C * J * b_size * (I // ti)
                           + I * J * o_size)
                cost = traffic + (I // ti) * (J // tj) * n_c * MM_STEP_COST_BYTES
                if best is None or cost < best[0]:
                    best = (cost, ti, tj, tc)
    return best[1:]


def _mm_call(a, b, mode, name, out_dtype, addend=None):
    if mode == "nn":
        (I, C), (_, J) = a.shape, b.shape
    elif mode == "nt":
        (I, C), (J, _) = a.shape, b.shape
    else:
        (C, I), (_, J) = a.shape, b.shape
    ti, tj, tc = _mm_tiles(I, J, C, a.dtype.itemsize, b.dtype.itemsize, jnp.dtype(out_dtype).itemsize)
    n_c = C // tc
    if mode == "nn":
        a_spec = pl.BlockSpec((ti, tc), lambda i, j, c: (i, c))
        b_spec = pl.BlockSpec((tc, tj), lambda i, j, c: (c, j))
        dims = (((1,), (0,)), ((), ()))
    elif mode == "nt":
        a_spec = pl.BlockSpec((ti, tc), lambda i, j, c: (i, c))
        b_spec = pl.BlockSpec((tj, tc), lambda i, j, c: (j, c))
        dims = (((1,), (1,)), ((), ()))
    else:
        a_spec = pl.BlockSpec((tc, ti), lambda i, j, c: (c, i))
        b_spec = pl.BlockSpec((tc, tj), lambda i, j, c: (c, j))
        dims = (((0,), (0,)), ((), ()))

    def product(a_ref, b_ref):
        return lax.dot_general(a_ref[...].astype(BF16), b_ref[...].astype(BF16), dims, preferred_element_type=F32)

    def finish(o_ref, r_refs, value):
        for r_ref in r_refs:
            value = r_ref[...] + value
        o_ref[...] = value.astype(o_ref.dtype)

    def body_single(a_ref, b_ref, *rest):
        finish(rest[-1], rest[:-1], product(a_ref, b_ref))

    def body_accumulate(a_ref, b_ref, *rest):
        (*r_refs, o_ref, acc), c = rest, pl.program_id(2)

        @pl.when(c == 0)
        def _():
            acc[...] = jnp.zeros_like(acc)

        acc[...] += product(a_ref, b_ref)

        @pl.when(c == n_c - 1)
        def _():
            finish(o_ref, r_refs, acc[...])

    out_spec = pl.BlockSpec((ti, tj), lambda i, j, c: (i, j))
    addends = [] if addend is None else [addend]
    return pl.pallas_call(
        body_single if n_c == 1 else body_accumulate, name=name, grid=(I // ti, J // tj, n_c),
        in_specs=[a_spec, b_spec] + [out_spec] * len(addends),
        out_specs=out_spec,
        out_shape=jax.ShapeDtypeStruct((I, J), out_dtype),
        scratch_shapes=[] if n_c == 1 else [pltpu.VMEM((ti, tj), F32)],
        compiler_params=pltpu.CompilerParams(dimension_semantics=("parallel", "parallel", "arbitrary"),
                                             vmem_limit_bytes=VMEM_LIMIT),
    )(a, b, *addends)


def _mm_add(residual, a, b, name):
    @jax.custom_vjp
    def run(residual, a, b):
        return _mm_call(a, b, "nn", "mm_" + name, F32, addend=residual)

    def fwd(residual, a, b):
        return run(residual, a, b), (a, b)

    def bwd(res, g):
        a, b = res
        return (g, _mm_call(g, b, "nt", "mm_" + name + "_da", a.dtype),
                _mm_call(a, g, "tn", "mm_" + name + "_db", b.dtype))

    run.defvjp(fwd, bwd)
    return run(residual, a, b)


def _mm(a, b, name):
    @jax.custom_vjp
    def run(a, b):
        return _mm_call(a, b, "nn", "mm_" + name, F32)

    def fwd(a, b):
        return run(a, b), (a, b)

    def bwd(res, g):
        a, b = res
        return (_mm_call(g, b, "nt", "mm_" + name + "_da", a.dtype),
                _mm_call(a, g, "tn", "mm_" + name + "_db", b.dtype))

    run.defvjp(fwd, bwd)
    return run(a, b)


def _mm_t(a, wt, name):
    @jax.custom_vjp
    def run(a, wt):
        return _mm_call(a, wt, "nt", "mmt_" + name, F32)

    def fwd(a, wt):
        return run(a, wt), (a, wt)

    def bwd(res, g):
        a, wt = res
        return (_mm_call(g, wt, "nn", "mmt_" + name + "_da", a.dtype),
                _mm_call(g, a, "tn", "mmt_" + name + "_dw", wt.dtype))

    run.defvjp(fwd, bwd)
    return run(a, wt)


def _stage_layout(rows, tm, cols):
    G, T, C = rows[0].shape
    if cols is None:
        return ((G, T // tm), lambda c: c, lambda c: pl.BlockSpec((1, tm, c), lambda g, t: (g, t, 0)),
                lambda c: pl.BlockSpec((1, 1, c), lambda g, t: (g, 0, 0)), lambda c: (G, T, c))
    return ((C // cols, T // tm), lambda c: cols, lambda c: pl.BlockSpec((1, tm, cols), lambda g, t: (0, t, g)),
            lambda c: pl.BlockSpec((1, 1, cols), lambda g, t: (0, 0, g)), lambda c: (1, T, C))


def _stage_fwd_call(name, fn, rows, params, tm, cols):
    grid, width, row_spec, par_spec, out_dims = _stage_layout(rows, tm, cols)
    nr, npar = len(rows), len(params)
    out_avals = jax.eval_shape(
        lambda *a: tuple(fn(*a)),
        *[jax.ShapeDtypeStruct((tm, width(r.shape[2])), r.dtype) for r in rows],
        *[jax.ShapeDtypeStruct((1, width(p.shape[2])), p.dtype) for p in params])

    def body(*refs):
        vals = [r[0] for r in refs[:nr + npar]]
        for o_ref, o in zip(refs[nr + npar:], fn(*vals)):
            o_ref[0] = o

    return pl.pallas_call(
        body, name=name, grid=grid,
        in_specs=[row_spec(r.shape[2]) for r in rows] + [par_spec(p.shape[2]) for p in params],
        out_specs=[row_spec(o.shape[1]) for o in out_avals],
        out_shape=[jax.ShapeDtypeStruct(out_dims(o.shape[1]), o.dtype) for o in out_avals],
        compiler_params=pltpu.CompilerParams(dimension_semantics=("parallel", "parallel"),
                                             vmem_limit_bytes=VMEM_LIMIT),
    )(*rows, *params)


def _stage_bwd_call(name, fn, rows, params, cts, tm, cols):
    grid, _, row_spec, par_spec, _ = _stage_layout(rows, tm, cols)
    nr, npar, nout = len(rows), len(params), len(cts)

    def body(*refs):
        vals = [r[0] for r in refs[:nr + npar]]
        ct_vals = tuple(r[0] for r in refs[nr + npar:nr + npar + nout])
        d_refs = refs[nr + npar + nout:]
        _, vjp_fn = jax.vjp(lambda *a: tuple(fn(*a)), *vals)
        grads = vjp_fn(ct_vals)
        for i in range(nr):
            d_refs[i][0] = grads[i]

        if npar:
            @pl.when(pl.program_id(1) == 0)
            def _():
                for j in range(npar):
                    d_refs[nr + j][...] = jnp.zeros_like(d_refs[nr + j])

        for j in range(npar):
            d_refs[nr + j][0] += grads[nr + j]

    outs = pl.pallas_call(
        body, name=name + "_bwd", grid=grid,
        in_specs=([row_spec(r.shape[2]) for r in rows] + [par_spec(p.shape[2]) for p in params]
                  + [row_spec(c.shape[2]) for c in cts]),
        out_specs=[row_spec(r.shape[2]) for r in rows] + [par_spec(p.shape[2]) for p in params],
        out_shape=([jax.ShapeDtypeStruct(r.shape, r.dtype) for r in rows]
                   + [jax.ShapeDtypeStruct(p.shape, p.dtype) for p in params]),
        compiler_params=pltpu.CompilerParams(dimension_semantics=("parallel", "arbitrary"),
                                             vmem_limit_bytes=VMEM_LIMIT),
    )(*rows, *params, *cts)
    return tuple(outs[:nr]), tuple(outs[nr:])


def _stage(name, fn, rows, params, tm, cols=None):
    @jax.custom_vjp
    def run(rows, params):
        return tuple(_stage_fwd_call(name, fn, rows, params, tm, cols))

    def fwd(rows, params):
        return run(rows, params), (rows, params)

    def bwd(res, cts):
        rows, params = res
        return _stage_bwd_call(name, fn, rows, params, tuple(cts), tm, cols)

    run.defvjp(fwd, bwd)
    return run(tuple(rows), tuple(params))


def _sigmoid(x):
    return 0.5 * (jnp.tanh(0.5 * x) + 1.0)


def _softplus(x):
    return jnp.maximum(x, 0.0) + jnp.log(1.0 + jnp.exp(-jnp.abs(x)))


def _rms(x, g, eps=RMS_EPS):
    return x * lax.rsqrt(jnp.mean(x * x, axis=-1, keepdims=True) + eps) * g


def _head_sum_pieces(x):
    n = x.shape[-1]
    same_head = (lax.broadcasted_iota(jnp.int32, (n, n), 0) // HEAD_DIM
                 == lax.broadcasted_iota(jnp.int32, (n, n), 1) // HEAD_DIM).astype(BF16)
    hi = x.astype(BF16)
    lo = (x - hi.astype(F32)).astype(BF16)
    dims = (((1,), (0,)), ((), ()))
    return (lax.dot_general(hi, same_head, dims, preferred_element_type=F32)
            + lax.dot_general(lo, same_head, dims, preferred_element_type=F32))


@jax.custom_vjp
def _head_sum(x):
    return _head_sum_pieces(x)


def _head_sum_fwd(x):
    return _head_sum_pieces(x), None


def _head_sum_bwd(_, g):
    return (_head_sum_pieces(g),)


_head_sum.defvjp(_head_sum_fwd, _head_sum_bwd)


def _head_mean(x):
    return _head_sum(x) * (1.0 / HEAD_DIM)


def _fn_rmsnorm(x, g):
    return (_rms(x, g).astype(BF16),)


def _fn_swiglu(gate, up):
    return ((gate * _sigmoid(gate) * up).astype(BF16),)


def _make_fn_lora_mix(p1, p2):
    def fn(u, u_prev, mu):
        um = u + (u_prev - u) * mu
        return (jnp.tanh(um[:, :p1]).astype(BF16), um[:, p1:p1 + p2].astype(BF16),
                _sigmoid(um[:, p1 + p2:]).astype(BF16))
    return fn


def _fn_rwkv_prep(ru, ru_p, ku, ku_p, vu, vu_p, w_lin, a_lin, mu_r, mu_k, mu_v, w0, a0, k_k, k_a):
    r = ru + (ru_p - ru) * mu_r
    k = ku + (ku_p - ku) * mu_k
    v = vu + (vu_p - vu) * mu_v
    w_log = -_softplus(-(w0 + w_lin)) - 0.5
    lw = -jnp.exp(w_log)
    a = _sigmoid(a0 + a_lin)
    kk = k * k_k
    kk = kk / jnp.maximum(jnp.sqrt(_head_sum(kk * kk)), 1e-12)
    k_mod = k * (1.0 + (a - 1.0) * k_a)
    return r, lw, k_mod, v, kk, kk * a


def _fn_rwkv_post(y, r, k_mod, v, g, lnx_g, lnx_b, r_k):
    yc = y - _head_mean(y)
    yn = yc * lax.rsqrt(_head_mean(yc * yc) + GN_EPS) * lnx_g + lnx_b
    bonus = _head_sum(r * k_mod * r_k) * v
    return (((yn + bonus) * g).astype(BF16),)


def _head_rms(x, g):
    return x * lax.rsqrt(_head_mean(x * x) + RMS_EPS) * g


def _fn_fox_prep(q, k, qg, kg):
    return _head_rms(q, qg), _head_rms(k, kg)


def _fn_log_forget(f_raw, b):
    x = f_raw + b
    return (jnp.minimum(x, 0.0) - jnp.log(1.0 + jnp.exp(-jnp.abs(x))),)


def _fn_final(z, e_raw, h2, target, gate_b, ple_g):
    gate = _sigmoid(z + gate_b)
    out = h2 + gate * _rms(e_raw, ple_g)
    err = out - target
    return (0.5 * jnp.mean(err * err, axis=-1, keepdims=True),)


def _dot_bf16(a, b, ca, cb):
    return lax.dot_general(a.astype(BF16), b.astype(BF16), (((ca,), (cb,)), ((0,), (0,))),
                           preferred_element_type=F32)


@functools.partial(jax.custom_vjp, nondiff_argnums=(2, 3))
def _dot(a, b, ca, cb):
    return _dot_bf16(a, b, ca, cb)


def _dot_fwd(a, b, ca, cb):
    return _dot_bf16(a, b, ca, cb), (a, b)


def _dot_bwd(ca, cb, res, g):
    a, b = res
    ia, jb = 3 - ca, 3 - cb
    da = _dot_bf16(g, b, 2, jb) if ca == 2 else _dot_bf16(b, g, jb, 2)
    db = _dot_bf16(a, g, ia, 1) if cb == 1 else _dot_bf16(g, a, 1, ia)
    return da, db


_dot.defvjp(_dot_fwd, _dot_bwd)


def _scan_chunk(S0, r, lw, k, v, kk, b):
    B, L, _ = r.shape
    row = lax.broadcasted_iota(jnp.int32, (B, L, L), 1)
    col = lax.broadcasted_iota(jnp.int32, (B, L, L), 2)
    incl = col <= row
    strict = col < row
    cum = lax.dot_general(incl.astype(F32), lw, (((2,), (1,)), ((0,), (0,))), precision=HIGHEST,
                          preferred_element_type=F32)
    g_in, g_ex, g_inv = jnp.exp(cum), jnp.exp(cum - lw), jnp.exp(-cum)
    kkg, kd, bd, rg = kk * g_ex, k * g_inv, b * g_inv, r * g_in
    a_k = jnp.where(strict, _dot(kkg, kd, 2, 2), 0.0)
    a_b = jnp.where(strict, _dot(kkg, bd, 2, 2), 0.0)
    pw = -a_b
    inv = (row == col).astype(F32) + pw
    for _ in range(int(math.log2(L)) - 1):
        pw = _dot(pw, pw, 2, 1)
        inv = inv + _dot(inv, pw, 2, 1)
    sa = -_dot(inv, _dot(kkg, S0, 2, 2) + _dot(a_k, v, 2, 1), 2, 1)
    r_k = jnp.where(incl, _dot(rg, kd, 2, 2), 0.0)
    r_b = jnp.where(incl, _dot(rg, bd, 2, 2), 0.0)
    y = _dot(rg, S0, 2, 2) + _dot(r_k, v, 2, 1) + _dot(r_b, sa, 2, 1)
    g_end = jnp.exp(jnp.sum(lw, axis=1, keepdims=True))
    S1 = S0 * g_end + _dot(v, kd * g_end, 1, 1) + _dot(sa, bd * g_end, 1, 1)
    return y, S1


def _scan_heads_per_step(H):
    return next(hb for hb in (SCAN_HEADS_PER_STEP, 2, 1) if H % hb == 0)


def _scan_fwd_call(r, lw, k, v, kk, b):
    H, T, N = r.shape
    L = SCAN_CHUNK
    n_chunks = T // L
    hb = _scan_heads_per_step(H)

    def body(r_ref, lw_ref, k_ref, v_ref, kk_ref, b_ref, y_ref, s0_ref, state):
        @pl.when(pl.program_id(1) == 0)
        def _():
            state[...] = jnp.zeros_like(state)

        S0 = state[...]
        s0_ref[:, 0] = S0
        y, S1 = _scan_chunk(S0, r_ref[...], lw_ref[...], k_ref[...], v_ref[...], kk_ref[...], b_ref[...])
        y_ref[...] = y
        state[...] = S1

    blk = pl.BlockSpec((hb, L, N), lambda h, c: (h, c, 0))
    return pl.pallas_call(
        body, name="rwkv_scan_fwd", grid=(H // hb, n_chunks),
        in_specs=[blk] * 6,
        out_specs=[blk, pl.BlockSpec((hb, 1, N, N), lambda h, c: (h, c, 0, 0))],
        out_shape=[jax.ShapeDtypeStruct((H, T, N), F32), jax.ShapeDtypeStruct((H, n_chunks, N, N), F32)],
        scratch_shapes=[pltpu.VMEM((hb, N, N), F32)],
        compiler_params=pltpu.CompilerParams(dimension_semantics=("parallel", "arbitrary")),
    )(r, lw, k, v, kk, b)


def _scan_bwd_call(r, lw, k, v, kk, b, s0s, dy):
    H, T, N = r.shape
    L = SCAN_CHUNK
    n_chunks = T // L
    hb = _scan_heads_per_step(H)

    def body(r_ref, lw_ref, k_ref, v_ref, kk_ref, b_ref, s0_ref, dy_ref, dr, dlw, dk, dv, dkk, db, d_state):
        @pl.when(pl.program_id(1) == 0)
        def _():
            d_state[...] = jnp.zeros_like(d_state)

        _, vjp_fn = jax.vjp(_scan_chunk, s0_ref[:, 0], r_ref[...], lw_ref[...], k_ref[...], v_ref[...], kk_ref[...],
                            b_ref[...])
        grads = vjp_fn((dy_ref[...], d_state[...]))
        d_state[...] = grads[0]
        for o_ref, g in zip((dr, dlw, dk, dv, dkk, db), grads[1:]):
            o_ref[...] = g

    blk = pl.BlockSpec((hb, L, N), lambda h, c: (h, n_chunks - 1 - c, 0))
    return pl.pallas_call(
        body, name="rwkv_scan_bwd", grid=(H // hb, n_chunks),
        in_specs=[blk] * 6 + [pl.BlockSpec((hb, 1, N, N), lambda h, c: (h, n_chunks - 1 - c, 0, 0)), blk],
        out_specs=[blk] * 6,
        out_shape=[jax.ShapeDtypeStruct((H, T, N), F32)] * 6,
        scratch_shapes=[pltpu.VMEM((hb, N, N), F32)],
        compiler_params=pltpu.CompilerParams(dimension_semantics=("parallel", "arbitrary")),
    )(r, lw, k, v, kk, b, s0s, dy)


@jax.custom_vjp
def _rwkv_scan(r, lw, k, v, kk, b):
    return _scan_fwd_call(r, lw, k, v, kk, b)[0]


def _rwkv_scan_fwd(r, lw, k, v, kk, b):
    y, s0s = _scan_fwd_call(r, lw, k, v, kk, b)
    return y, (r, lw, k, v, kk, b, s0s)


def _rwkv_scan_bwd(res, dy):
    return tuple(_scan_bwd_call(*res, dy))


_rwkv_scan.defvjp(_rwkv_scan_fwd, _rwkv_scan_bwd)


def _nt(a, b):
    return lax.dot_general(a.astype(BF16), b.astype(BF16), (((1,), (1,)), ((), ())), preferred_element_type=F32)


def _nn(a, b):
    return lax.dot_general(a.astype(BF16), b.astype(BF16), (((1,), (0,)), ((), ())), preferred_element_type=F32)


def _tn(a, b):
    return lax.dot_general(a.astype(BF16), b.astype(BF16), (((0,), (0,)), ((), ())), preferred_element_type=F32)


def _attn_scores(qs, k_ref, cr_ref, row_bias, j, kb, q0, masked):
    ks = pl.multiple_of(j * kb, kb)
    kj = k_ref[0, pl.ds(ks, kb), :]
    s = _nt(qs, kj) + row_bias - cr_ref[0, j]
    if masked:
        qi = q0 + lax.broadcasted_iota(jnp.int32, s.shape, 0)
        ki = ks + lax.broadcasted_iota(jnp.int32, s.shape, 1)
        s = jnp.where(ki <= qi, s, -jnp.inf)
    return s, kj, ks


def _attn_specs(T, N, bq, kb):
    q_spec = pl.BlockSpec((1, bq, N), lambda h, i: (h, i, 0))
    kv_spec = pl.BlockSpec((1, T, N), lambda h, i: (h, 0, 0))
    col_spec = pl.BlockSpec((1, bq, 1), lambda h, i: (h, i, 0))
    row_spec = pl.BlockSpec((1, T // kb, 1, kb), lambda h, i: (h, 0, 0, 0))
    return q_spec, kv_spec, col_spec, row_spec


def _grid_marks(H, n_q):
    h, i = pl.program_id(0), pl.program_id(1)
    return (h == 0) & (i == 0), (h == (5 * H) // 8) & (i == 0), (h == H - 1) & (i == n_q - 1)


def _attn_fwd_call(q, k, v, c_col, c_rows, gather_xs):
    H, T, N = q.shape
    bq, kb = min(ATTN_BLOCK_Q, T), c_rows.shape[3]
    q_spec, kv_spec, col_spec, row_spec = _attn_specs(T, N, bq, kb)
    n = len(gather_xs)

    def body(q_ref, k_ref, v_ref, cc_ref, cr_ref, *rest):
        x_refs, (o_ref, o32_ref, lse_ref), out_refs, sems = rest[:n], rest[n:n + 3], rest[n + 3:2 * n + 3], rest[2 * n + 3:]
        first, middle, last = _grid_marks(H, T // bq)
        start, relay, finish = _gather_phases(x_refs, out_refs, *sems)
        pl.when(first)(start)
        pl.when(middle)(relay)
        q0 = pl.program_id(1) * bq
        qs = (q_ref[0] * (HEAD_DIM ** -0.5)).astype(BF16)
        cc = cc_ref[0]

        def step(j, carry, masked):
            m, l, acc = carry
            s, _, ks = _attn_scores(qs, k_ref, cr_ref, cc, j, kb, q0, masked)
            m_new = jnp.maximum(m, jnp.max(s, axis=-1, keepdims=True))
            alpha = jnp.exp(m - m_new)
            p = jnp.exp(s - m_new)
            l = alpha * l + jnp.sum(p, axis=-1, keepdims=True)
            p_hi = p.astype(BF16)
            p_lo = p - p_hi.astype(F32)
            vj = v_ref[0, pl.ds(ks, kb), :]
            acc = alpha * acc + (_nn(p_hi, vj) + _nn(p_lo, vj))
            return m_new, l, acc

        n_full = q0 // kb
        init = (jnp.full((bq, 1), -jnp.inf, F32), jnp.zeros((bq, 1), F32), jnp.zeros((bq, N), F32))
        carry = lax.fori_loop(0, n_full, functools.partial(step, masked=False), init)
        m, l, acc = step(n_full, carry, masked=True)
        o = acc / l
        o_ref[0] = o.astype(o_ref.dtype)
        o32_ref[0] = o
        lse_ref[0] = m + jnp.log(l)
        pl.when(last)(finish)

    any_spec = pl.BlockSpec(memory_space=pl.ANY)
    return pl.pallas_call(
        body, name="fox_attn_fwd", grid=(H, T // bq),
        in_specs=[q_spec, kv_spec, kv_spec, col_spec, row_spec] + [any_spec] * n,
        out_specs=[q_spec, q_spec, col_spec] + [any_spec] * n,
        out_shape=[jax.ShapeDtypeStruct((H, T, N), BF16),
                   jax.ShapeDtypeStruct((H, T, N), F32),
                   jax.ShapeDtypeStruct((H, T, 1), F32)] + _gather_out_shapes(gather_xs),
        scratch_shapes=_comm_semaphores(n),
        compiler_params=pltpu.CompilerParams(dimension_semantics=("arbitrary", "arbitrary"),
                                             vmem_limit_bytes=VMEM_LIMIT),
    )(q, k, v, c_col, c_rows, *gather_xs)


def _attn_bwd_call(q, k, v, c_col, c_rows, o, lse, do, scatter_parts):
    H, T, N = q.shape
    bq, kb = min(ATTN_BLOCK_Q, T), c_rows.shape[3]
    q_spec, kv_spec, col_spec, row_spec = _attn_specs(T, N, bq, kb)
    n = len(scatter_parts)

    def body(q_ref, k_ref, v_ref, cc_ref, cr_ref, o_ref, lse_ref, do_ref, *rest):
        a_refs, (dq_ref, dk_ref, dv_ref, dcr_ref), b_refs, sems = rest[:n], rest[n:n + 4], rest[n + 4:2 * n + 4], rest[2 * n + 4:]
        first, _, last = _grid_marks(H, T // bq)
        start, finish = _scatter_phases(a_refs, b_refs, *sems)
        pl.when(first)(start)
        i = pl.program_id(1)
        q0 = i * bq

        @pl.when(i == 0)
        def _():
            dk_ref[...] = jnp.zeros_like(dk_ref)
            dv_ref[...] = jnp.zeros_like(dv_ref)
            dcr_ref[...] = jnp.zeros_like(dcr_ref)

        qs = (q_ref[0] * (HEAD_DIM ** -0.5)).astype(BF16)
        do = do_ref[0]
        delta = jnp.sum(do.astype(F32) * o_ref[0], axis=-1, keepdims=True)
        row_bias = cc_ref[0] - lse_ref[0]

        def step(j, dq, masked):
            s, kj, ks = _attn_scores(qs, k_ref, cr_ref, row_bias, j, kb, q0, masked)
            p = jnp.exp(s)
            ds = p * (_nt(do, v_ref[0, pl.ds(ks, kb), :]) - delta)
            ds_b = ds.astype(BF16)
            dk_ref[0, pl.ds(ks, kb), :] += _tn(ds_b, qs)
            dv_ref[0, pl.ds(ks, kb), :] += _tn(p, do)
            dcr_ref[0, j] -= jnp.sum(ds, axis=0, keepdims=True)
            return dq + _nn(ds_b, kj)

        n_full = q0 // kb
        dq = lax.fori_loop(0, n_full, functools.partial(step, masked=False), jnp.zeros((bq, N), F32))
        dq_ref[0] = step(n_full, dq, masked=True) * (HEAD_DIM ** -0.5)
        pl.when(last)(finish)

    any_spec = pl.BlockSpec(memory_space=pl.ANY)
    return pl.pallas_call(
        body, name="fox_attn_bwd", grid=(H, T // bq),
        in_specs=[q_spec, kv_spec, kv_spec, col_spec, row_spec, q_spec, col_spec, q_spec] + [any_spec] * n,
        out_specs=[q_spec, kv_spec, kv_spec, row_spec] + [any_spec] * n,
        out_shape=[jax.ShapeDtypeStruct((H, T, N), F32)] * 3 + [jax.ShapeDtypeStruct(c_rows.shape, F32)]
        + [jax.ShapeDtypeStruct(a.shape, a.dtype) for a in scatter_parts],
        scratch_shapes=_comm_semaphores(n),
        compiler_params=pltpu.CompilerParams(dimension_semantics=("arbitrary", "arbitrary"),
                                             vmem_limit_bytes=VMEM_LIMIT),
    )(q, k, v, c_col, c_rows, o, lse, do, *scatter_parts)


@jax.custom_vjp
def _fox_attn(q, k, v, c, late_blocks, carrier):
    return _fox_attn_fwd(q, k, v, c, late_blocks, carrier)[0]


def _attn_bias_views(c):
    H, T = c.shape
    kb = min(ATTN_BLOCK_K, T)
    return c[:, :, None], c.reshape(H, T // kb, 1, kb)


def _fox_attn_fwd(q, k, v, c, late_blocks, carrier):
    o, o32, lse, *gathered = _attn_fwd_call(q, k, v, *_attn_bias_views(c), late_blocks)
    return (o, tuple(gathered)), (q, k, v, c, o32, lse)


def _fox_attn_bwd(res, cts):
    q, k, v, c, o32, lse = res
    do, d_gathered = cts
    dq, dk, dv, dc_rows, *parts = _attn_bwd_call(q, k, v, *_attn_bias_views(c), o32, lse, do, d_gathered)
    no_grad = tuple(jnp.zeros(a.shape[1:], a.dtype) for a in parts)
    return dq, dk, dv, dc_rows.reshape(c.shape), no_grad, tuple(parts)


_fox_attn.defvjp(_fox_attn_fwd, _fox_attn_bwd)


N_PEERS = N_DEV - 1


def _all_gather(xs, name):
    n = len(xs)

    def body(*refs):
        start, relay, finish = _gather_phases(refs[:n], refs[n:2 * n], *refs[2 * n:])
        start()
        relay()
        finish()

    any_spec = pl.BlockSpec(memory_space=pl.ANY)
    return pl.pallas_call(
        body, name=name,
        out_shape=_gather_out_shapes(xs),
        in_specs=[any_spec] * n, out_specs=[any_spec] * n,
        scratch_shapes=_comm_semaphores(n),
    )(*xs)


def _gather_out_shapes(xs):
    return [jax.ShapeDtypeStruct((N_DEV,) + x.shape, x.dtype) for x in xs]


def _comm_semaphores(n):
    return [pltpu.SemaphoreType.DMA((N_PEERS * n,)), pltpu.SemaphoreType.DMA((N_PEERS * n,)),
            pltpu.SemaphoreType.DMA((n,))]


def _gather_phases(x_refs, out_refs, send_sems, recv_sems, local_sems):
    n = len(x_refs)
    x_, y_, c_ = lax.axis_index("x"), lax.axis_index("y"), lax.axis_index("c")
    me, sibling = (x_, y_, c_), (x_, y_, 1 - c_)
    chips = [(1 - x_, y_), (x_, 1 - y_), (1 - x_, 1 - y_)]

    def slot(t, px, py, pc):
        return out_refs[t].at[4 * px + 2 * py + pc]

    def copy(t, k, block, to, src=None):
        return pltpu.make_async_remote_copy(
            src_ref=slot(t, *block) if src is None else src, dst_ref=slot(t, *block),
            send_sem=send_sems.at[k * n + t], recv_sem=recv_sems.at[k * n + t],
            device_id=to, device_id_type=pl.DeviceIdType.MESH)

    def mine():
        return [pltpu.make_async_copy(x_refs[t], slot(t, *me), local_sems.at[t]) for t in range(n)]

    def first():
        return ([copy(t, 0, me, sibling, src=x_refs[t]) for t in range(n)]
                + [copy(t, 1 + j, me, (*chip, c_), src=x_refs[t]) for j, chip in enumerate(chips) for t in range(n)])

    def passed():
        return [copy(t, 4 + j, (*chip, c_), sibling) for j, chip in enumerate(chips) for t in range(n)]

    def start():
        for cp in mine() + first():
            cp.start()

    def relay():
        for j, chip in enumerate(chips):
            for t in range(n):
                copy(t, 1 + j, (*chip, c_), me).wait_recv()
                copy(t, 4 + j, (*chip, c_), sibling).start()

    def finish():
        for t in range(n):
            copy(t, 0, sibling, me).wait_recv()
        for j, chip in enumerate(chips):
            for t in range(n):
                copy(t, 4 + j, (*chip, 1 - c_), me).wait_recv()
        for cp in first() + passed():
            cp.wait_send()
        for cp in mine():
            cp.wait()

    return start, relay, finish


def _all_to_all(parts, name):
    n = len(parts)

    def body(*refs):
        start, finish = _scatter_phases(refs[:n], refs[n:2 * n], *refs[2 * n:])
        start()
        finish()

    any_spec = pl.BlockSpec(memory_space=pl.ANY)
    return pl.pallas_call(
        body, name=name,
        out_shape=[jax.ShapeDtypeStruct(a.shape, a.dtype) for a in parts],
        in_specs=[any_spec] * n, out_specs=[any_spec] * n,
        scratch_shapes=_comm_semaphores(n),
    )(*parts)


def _scatter_phases(a_refs, b_refs, send_sems, recv_sems, local_sems):
    n = len(a_refs)
    x_, y_, c_ = lax.axis_index("x"), lax.axis_index("y"), lax.axis_index("c")
    me_idx = 4 * x_ + 2 * y_ + c_

    def copies():
        out = [pltpu.make_async_copy(a_refs[t].at[me_idx], b_refs[t].at[me_idx], local_sems.at[t]) for t in range(n)]
        for rel in range(1, N_DEV):
            px = 1 - x_ if rel & 4 else x_
            py = 1 - y_ if rel & 2 else y_
            pc = 1 - c_ if rel & 1 else c_
            for t in range(n):
                out.append(pltpu.make_async_remote_copy(
                    src_ref=a_refs[t].at[4 * px + 2 * py + pc], dst_ref=b_refs[t].at[me_idx],
                    send_sem=send_sems.at[(rel - 1) * n + t], recv_sem=recv_sems.at[(rel - 1) * n + t],
                    device_id=(px, py, pc), device_id_type=pl.DeviceIdType.MESH))
        return out

    def start():
        for cp in copies():
            cp.start()

    def finish():
        for cp in copies():
            cp.wait()

    return start, finish


def _reduce_adamw(parts, w, m, v, name):
    R, C = w.shape
    tr = max(t for t in (256, 128, PACK_ROW_QUANTUM) if R % t == 0)

    def body(p_ref, w_ref, m_ref, v_ref, g_out, d_out, m_out, v_out):
        g = p_ref[0]
        for i in range(1, N_DEV):
            g = g + p_ref[i]
        m_new = ADAM_B1 * m_ref[...] + (1.0 - ADAM_B1) * g
        v_new = ADAM_B2 * v_ref[...] + (1.0 - ADAM_B2) * (g * g)
        m_hat = m_new / (1.0 - ADAM_B1 ** ADAM_STEP)
        v_hat = v_new / (1.0 - ADAM_B2 ** ADAM_STEP)
        g_out[...] = g
        d_out[...] = -ADAM_LR * (m_hat / (jnp.sqrt(v_hat) + ADAM_EPS) + ADAM_WD * w_ref[...])
        m_out[...] = m_new
        v_out[...] = v_new

    spec = pl.BlockSpec((tr, C), lambda i: (i, 0))
    return pl.pallas_call(
        body, name=name, grid=(R // tr,),
        in_specs=[pl.BlockSpec((N_DEV, tr, C), lambda i: (0, i, 0)), spec, spec, spec],
        out_specs=[spec] * 4,
        out_shape=[jax.ShapeDtypeStruct((R, C), F32)] * 4,
        compiler_params=pltpu.CompilerParams(dimension_semantics=("parallel",), vmem_limit_bytes=VMEM_LIMIT),
    )(parts, w, m, v)


def _pack(arrays, dtype):
    flat = jnp.concatenate([a.reshape(-1).astype(dtype) for a in arrays])
    rows = _round_up(-(-flat.shape[0] // PACK_COLS), PACK_ROW_QUANTUM)
    flat = jnp.pad(flat, (0, rows * PACK_COLS - flat.shape[0]))
    return flat.reshape(rows, PACK_COLS)


def _unpack(packed, shapes):
    lead = packed.shape[:-2]
    flat = packed.reshape(lead + (-1,))
    out, off = [], 0
    for s in shapes:
        n = math.prod(s)
        out.append(flat[..., off:off + n].reshape(lead + tuple(s)))
        off += n
    return out


def _travel_layout(name, block):
    return block.T if SHARDED[name] else block


REDUCE_BLOCK_BYTES = 4 * 1024 * 1024


def _reduce_parts(parts, name):
    _, R, C = parts.shape
    per_col = N_DEV * R * parts.dtype.itemsize
    tc = next((t for t in range(C - C % LANES, 0, -LANES) if C % t == 0 and t * per_col <= REDUCE_BLOCK_BYTES), C)

    def body(p_ref, o_ref):
        g = p_ref[0].astype(F32)
        for i in range(1, N_DEV):
            g = g + p_ref[i].astype(F32)
        o_ref[...] = g

    return pl.pallas_call(
        body, name=name, grid=(C // tc,),
        in_specs=[pl.BlockSpec((N_DEV, R, tc), lambda j: (0, 0, j))],
        out_specs=pl.BlockSpec((R, tc), lambda j: (0, j)),
        out_shape=jax.ShapeDtypeStruct((R, C), F32),
        compiler_params=pltpu.CompilerParams(dimension_semantics=("parallel",), vmem_limit_bytes=VMEM_LIMIT),
    )(parts)


def _adamw(g, w, m, v, name):
    R, C = w.shape
    tr = next((t for t in (512, 256, 128, 64, 32, 16, 8) if R % t == 0 and t * C * 4 <= 2 * 1024 * 1024), R)

    def body(g_ref, w_ref, m_ref, v_ref, d_out, m_out, v_out):
        g_ = g_ref[...]
        m_new = ADAM_B1 * m_ref[...] + (1.0 - ADAM_B1) * g_
        v_new = ADAM_B2 * v_ref[...] + (1.0 - ADAM_B2) * (g_ * g_)
        m_hat = m_new / (1.0 - ADAM_B1 ** ADAM_STEP)
        v_hat = v_new / (1.0 - ADAM_B2 ** ADAM_STEP)
        d_out[...] = -ADAM_LR * (m_hat / (jnp.sqrt(v_hat) + ADAM_EPS) + ADAM_WD * w_ref[...])
        m_out[...] = m_new
        v_out[...] = v_new

    spec = pl.BlockSpec((tr, C), lambda i: (i, 0))
    return pl.pallas_call(
        body, name=name, grid=(R // tr,),
        in_specs=[spec] * 4, out_specs=[spec] * 3,
        out_shape=[jax.ShapeDtypeStruct((R, C), F32)] * 3,
        compiler_params=pltpu.CompilerParams(dimension_semantics=("parallel",), vmem_limit_bytes=VMEM_LIMIT),
    )(g, w, m, v)


def _to_heads(u):
    return jnp.transpose(u.reshape(u.shape[0], -1, HEAD_DIM), (1, 0, 2))


def _split_cols(u, n):
    width = u.shape[1] // n

    @jax.custom_vjp
    def run(u):
        return tuple(u[:, i * width:(i + 1) * width] for i in range(n))

    def fwd(u):
        return run(u), None

    def bwd(_, cts):
        return (jnp.concatenate(cts, axis=1),)

    run.defvjp(fwd, bwd)
    return run(u)


def _from_heads(uh):
    H, T, N = uh.shape
    return jnp.transpose(uh, (1, 0, 2)).reshape(T, H * N)


def _shift(uh):
    return jnp.pad(uh, ((0, 0), (1, 0), (0, 0)))[:, :-1]


def _pad_cols(a, width):
    return jnp.pad(a, ((0, 0), (0, width - a.shape[1])))


def _split_rows(w, sizes):
    offsets = [sum(sizes[:i]) for i in range(len(sizes))]

    @jax.custom_vjp
    def run(w):
        return tuple(w[o:o + s] for o, s in zip(offsets, sizes))

    def fwd(w):
        return run(w), None

    def bwd(_, cts):
        return (jnp.concatenate(cts, axis=0),)

    run.defvjp(fwd, bwd)
    return run(w)


def _pad_rows(a, height):
    return jnp.pad(a, ((0, height - a.shape[0]), (0, 0)))


def _vec(a):
    return a.reshape(1, 1, -1)


TM_WIDE = 128


def _mixing_half(W, small, x, late_blocks, carrier):
    T, D = x.shape
    vec = _vec
    tm_wide = TM_WIDE
    rw = small['w0'].shape[-1]
    fw = W['w_out'].shape[0] - rw
    heads_f = fw // HEAD_DIM
    dl, al, gl = W['w2'].shape[1], W['a2'].shape[1], W['g2'].shape[1]
    dl_p, al_p, gl_p = _round_up(dl, LANES), _round_up(al, LANES), _round_up(gl, LANES)
    f_p = _round_up(heads_f, LANES)
    rwkv_cols = 3 * rw + dl + al + gl
    tm_head = next((t for t in (1024, 512) if T % t == 0), T)

    o_w, o_a, o_g = 3 * rw, 3 * rw + dl, 3 * rw + dl + al
    w_rkv, w_xw, w_xa, w_xg, w_qkv, w_fg = _split_rows(W['w_in'], (3 * rw, dl, al, gl, 3 * fw, heads_f))
    w_lora = jnp.concatenate([_pad_rows(w_xw, dl_p), _pad_rows(w_xa, al_p), _pad_rows(w_xg, gl_p)], axis=0)
    w_f = _pad_rows(w_fg, f_p)
    mu = small['shift_mu'].reshape(1, -1)
    mu_lora = jnp.concatenate([_pad_cols(mu[:, o_w:o_a], dl_p), _pad_cols(mu[:, o_a:o_g], al_p),
                               _pad_cols(mu[:, o_g:rwkv_cols], gl_p)], axis=1)

    (xn,) = _stage("attn_norm", _fn_rmsnorm, [x[None]], [vec(small['attn_norm_g'])], tm_wide)
    xn = xn[0]
    u_rkv = _mm_t(xn, w_rkv, "in_rkv")
    u_lora = _mm_t(xn, w_lora, "in_lora")
    u_qkv = _mm_t(xn, w_qkv, "in_qkv")
    f_raw = _mm_t(xn, w_f, "in_f")

    u_lora3 = u_lora[None]
    xw_t, xa_m, xg_s = _stage("lora_mix", _make_fn_lora_mix(dl_p, al_p), [u_lora3, _shift(u_lora3)],
                              [vec(mu_lora)], tm_wide)
    w_lin = _mm_t(xw_t[0], _pad_cols(W['w2'], dl_p), "w2")
    a_lin = _mm_t(xa_m[0], _pad_cols(W['a2'], al_p), "a2")
    gate_r = _mm_t(xg_s[0], _pad_cols(W['g2'], gl_p), "g2")
    ru, ku, vu = (u[None] for u in _split_cols(u_rkv, 3))
    mu_r, mu_k, mu_v = (vec(mu[:, i * rw:(i + 1) * rw]) for i in range(3))
    prepped = _stage(
        "rwkv_prep", _fn_rwkv_prep,
        [ru, _shift(ru), ku, _shift(ku), vu, _shift(vu), w_lin[None], a_lin[None]],
        [mu_r, mu_k, mu_v, vec(small['w0']), vec(small['a0']), vec(small['k_k']), vec(small['k_a'])],
        tm_head, cols=HEAD_PAIR)
    r, lw, k_mod, v, kk, b = prepped
    y_scan = _rwkv_scan(*(_to_heads(t[0]) for t in prepped))
    (y_rwkv,) = _stage("rwkv_post", _fn_rwkv_post, [_from_heads(y_scan)[None], r, k_mod, v, gate_r[None]],
                       [vec(small['lnx_g']), vec(small['lnx_b']), vec(small['r_k'])], tm_head, cols=HEAD_PAIR)

    qu, kf, vf = _split_cols(u_qkv, 3)
    qg = vec(jnp.tile(small['q_norm_g'].reshape(-1), heads_f))
    kg = vec(jnp.tile(small['k_norm_g'].reshape(-1), heads_f))
    qn, kn = _stage("fox_prep", _fn_fox_prep, [qu[None], kf[None]], [qg, kg], tm_head, cols=HEAD_PAIR)
    fb = _pad_cols(small['fgate_b'].reshape(1, -1), f_p)
    (log_f,) = _stage("log_forget", _fn_log_forget, [f_raw[None]], [vec(fb)], tm_head)
    c = jnp.cumsum(log_f[0][:, :heads_f], axis=0).T
    y_fox, gathered_late = _fox_attn(_to_heads(qn[0]), _to_heads(kn[0]), _to_heads(vf), c, late_blocks, carrier)

    y_cat = jnp.concatenate([y_rwkv[0], _from_heads(y_fox)], axis=-1)
    return _mm_add(x, y_cat, W['w_out'], "out"), gathered_late


def _channel_half_loss(W, small, h1, p, target):
    vec = _vec
    tm_wide = TM_WIDE
    (hn,) = _stage("ffn_norm", _fn_rmsnorm, [h1[None]], [vec(small['ffn_norm_g'])], tm_wide)
    gate = _mm_t(hn[0], W['w_gate'], "gate")
    up = _mm_t(hn[0], W['w_up'], "up")
    (act,) = _stage("swiglu", _fn_swiglu, [gate[None], up[None]], [], tm_wide)
    h2 = _mm_add(h1, act[0], W['w_down'], "down")
    e_raw = _mm_t(p, W['ple_proj'], "ple_proj")
    (hg,) = _stage("ple_gate_norm", _fn_rmsnorm, [h2[None]], [vec(small['ple_gate_norm_g'])], tm_wide)
    z = _mm(hg[0], W['ple_gate_w'], "ple_gate")
    (loss_rows,) = _stage("final", _fn_final, [z[None], e_raw[None], h2[None], target[None]],
                          [vec(small['ple_gate_b']), vec(small['ple_norm_g'])], tm_wide)
    return jnp.sum(loss_rows)


def kernel(x, p, attn_norm_g, w_in, shift_mu, w0, w2, a0, a2, g2, k_k, k_a, r_k, lnx_g, lnx_b, q_norm_g, k_norm_g, fgate_b, w_out, ffn_norm_g, w_gate, w_up, w_down, ple_proj, ple_norm_g, ple_gate_norm_g, ple_gate_w, ple_gate_b, loss_target, m_attn_norm_g, m_w_in, m_shift_mu, m_w0, m_w2, m_a0, m_a2, m_g2, m_k_k, m_k_a, m_r_k, m_lnx_g, m_lnx_b, m_q_norm_g, m_k_norm_g, m_fgate_b, m_w_out, m_ffn_norm_g, m_w_gate, m_w_up, m_w_down, m_ple_proj, m_ple_norm_g, m_ple_gate_norm_g, m_ple_gate_w, m_ple_gate_b, v_attn_norm_g, v_w_in, v_shift_mu, v_w0, v_w2, v_a0, v_a2, v_g2, v_k_k, v_k_a, v_r_k, v_lnx_g, v_lnx_b, v_q_norm_g, v_k_norm_g, v_fgate_b, v_w_out, v_ffn_norm_g, v_w_gate, v_w_up, v_w_down, v_ple_proj, v_ple_norm_g, v_ple_gate_norm_g, v_ple_gate_w, v_ple_gate_b):
    weights = dict(zip(WEIGHT_NAMES, (attn_norm_g, w_in, shift_mu, w0, w2, a0, a2, g2, k_k, k_a, r_k, lnx_g, lnx_b,
                                      q_norm_g, k_norm_g, fgate_b, w_out, ffn_norm_g, w_gate, w_up, w_down, ple_proj,
                                      ple_norm_g, ple_gate_norm_g, ple_gate_w, ple_gate_b)))
    m_in = dict(zip(WEIGHT_NAMES, (m_attn_norm_g, m_w_in, m_shift_mu, m_w0, m_w2, m_a0, m_a2, m_g2, m_k_k, m_k_a, m_r_k,
                                   m_lnx_g, m_lnx_b, m_q_norm_g, m_k_norm_g, m_fgate_b, m_w_out, m_ffn_norm_g, m_w_gate,
                                   m_w_up, m_w_down, m_ple_proj, m_ple_norm_g, m_ple_gate_norm_g, m_ple_gate_w,
                                   m_ple_gate_b)))
    v_in = dict(zip(WEIGHT_NAMES, (v_attn_norm_g, v_w_in, v_shift_mu, v_w0, v_w2, v_a0, v_a2, v_g2, v_k_k, v_k_a, v_r_k,
                                   v_lnx_g, v_lnx_b, v_q_norm_g, v_k_norm_g, v_fgate_b, v_w_out, v_ffn_norm_g, v_w_gate,
                                   v_w_up, v_w_down, v_ple_proj, v_ple_norm_g, v_ple_gate_norm_g, v_ple_gate_w,
                                   v_ple_gate_b)))
    small_shapes = [weights[n].shape for n in SMALL_NAMES]
    small = {n: weights[n] for n in SMALL_NAMES}

    def whole(stacks, names):
        return {n: g.reshape(N_DEV * g.shape[1], g.shape[2]) for n, g in zip(names, stacks)}

    def stacked(tree, names, like):
        return tuple(tree[n].reshape(g.shape) for n, g in zip(names, like))

    travelling = {n: _travel_layout(n, weights[n][0]).astype(BF16) for n in SHARDED_NAMES}
    gathered_early = _all_gather([travelling[n] for n in EARLY_NAMES], "gather_weights")
    late_blocks = tuple(travelling[n] for n in LATE_NAMES)
    carrier = tuple(jnp.zeros((N_DEV,) + b.shape, b.dtype) for b in late_blocks)

    (h1, gathered_late), mixing_vjp = jax.vjp(_mixing_half, whole(gathered_early, EARLY_NAMES), small, x[0],
                                              late_blocks, carrier)
    loss_local, (d_late, d_small_b, d_h1) = jax.value_and_grad(_channel_half_loss, argnums=(0, 1, 2))(
        whole(gathered_late, LATE_NAMES), small, h1, p[0, 0], loss_target[0])
    d_early, d_small_a, d_x, _, parts_late = mixing_vjp((d_h1, stacked(d_late, LATE_NAMES, gathered_late)))
    d_small = {n: d_small_a[n] + d_small_b[n] for n in SMALL_NAMES}
    loss = lax.psum(loss_local, MESH_AXES)

    parts_early = _all_to_all(stacked(d_early, EARLY_NAMES, gathered_early), "scatter_grads")
    parts = dict(zip(EARLY_NAMES + LATE_NAMES, list(parts_early) + list(parts_late)))
    (small_parts,) = _all_gather([_pack([d_small[n] for n in SMALL_NAMES], F32)], "gather_small_grads")

    def pack_f32(tree, names):
        return _pack([tree[n] for n in names], F32)

    sml = _reduce_adamw(small_parts, pack_f32(weights, SMALL_NAMES), pack_f32(m_in, SMALL_NAMES),
                        pack_f32(v_in, SMALL_NAMES), "adamw_replicated")
    by_kind = [dict(zip(SMALL_NAMES, _unpack(sml[kind], small_shapes))) for kind in range(4)]
    for n in SHARDED_NAMES:
        g = _travel_layout(n, _reduce_parts(parts[n], "reduce_" + n))
        upd = _adamw(g, weights[n][0], m_in[n][0], v_in[n][0], "adamw_" + n)
        for kind, val in enumerate((g, *upd)):
            by_kind[kind][n] = val[None]
    outs = [by_kind[kind][n] for kind in range(4) for n in WEIGHT_NAMES]
    return (loss, d_x[None], *outs)
```

```python
import functools
import math

import jax
import jax.numpy as jnp
from jax import lax
from jax.experimental import pallas as pl
from jax.experimental.pallas import tpu as pltpu

F32 = jnp.float32
BF16 = jnp.bfloat16
HIGHEST = lax.Precision.HIGHEST

N_DEV = 8
MESH_AXES = ("x", "y", "c")
HEAD_DIM = 64
SCAN_CHUNK = 64
SCAN_HEADS_PER_STEP = 16
ATTN_BLOCK_Q = 1024
ATTN_BLOCK_K = 1024
LANES = 128
HEAD_PAIR = 2 * HEAD_DIM
PACK_COLS = 1024
PACK_ROW_QUANTUM = 64
VMEM_LIMIT = 48 * 1024 * 1024
RMS_EPS = 1e-6
GN_EPS = 64e-5
ADAM_LR, ADAM_B1, ADAM_B2, ADAM_EPS, ADAM_WD, ADAM_STEP = 0.001, 0.9, 0.999, 1e-08, 0.01, 10

WEIGHT_NAMES = ['attn_norm_g', 'w_in', 'shift_mu', 'w0', 'w2', 'a0', 'a2', 'g2', 'k_k', 'k_a', 'r_k', 'lnx_g', 'lnx_b',
                'q_norm_g', 'k_norm_g', 'fgate_b', 'w_out', 'ffn_norm_g', 'w_gate', 'w_up', 'w_down', 'ple_proj',
                'ple_norm_g', 'ple_gate_norm_g', 'ple_gate_w', 'ple_gate_b']
SHARDED = {'w_in': True, 'w2': True, 'a2': True, 'g2': True, 'w_out': False, 'w_gate': True, 'w_up': True,
           'w_down': False, 'ple_proj': True, 'ple_gate_w': False}
SHARDED_NAMES = [n for n in WEIGHT_NAMES if n in SHARDED]
SMALL_NAMES = [n for n in WEIGHT_NAMES if n not in SHARDED]
EARLY_NAMES = ['w_in', 'w2', 'a2', 'g2', 'w_out']
LATE_NAMES = [n for n in SHARDED_NAMES if n not in EARLY_NAMES]


def _round_up(n, q):
    return -(-n // q) * q


def _tile_candidates(n, cap):
    sizes = {t for t in range(LANES, min(n, cap) + 1, LANES) if n % t == 0}
    return sorted(sizes | ({n} if n <= cap or not sizes else set()))


MM_TILE_CAP = 2048
MM_VMEM_BUDGET = 38 * 1024 * 1024
MM_STEP_COST_BYTES = 1024 * 1024


def _mm_tiles(I, J, C, a_size, b_size, o_size):
    best = None
    for ti in _tile_candidates(I, MM_TILE_CAP):
        for tj in _tile_candidates(J, MM_TILE_CAP):
            for tc in _tile_candidates(C, MM_TILE_CAP):
                n_c = C // tc
                blocks = 2 * (ti * tc * a_size + tc * tj * b_size + ti * tj * o_size)
                temporaries = (ti * tc + tc * tj) * 2 + ti * tj * 4 * (2 if n_c > 1 else 1)
                if blocks + temporaries > MM_VMEM_BUDGET:
                    continue
                traffic = (I * C * a_size * (1 if n_c == 1 else J // tj) + C * J * b_size * (I // ti)
                           + I * J * o_size)
                cost = traffic + (I // ti) * (J // tj) * n_c * MM_STEP_COST_BYTES
                if best is None or cost < best[0]:
                    best = (cost, ti, tj, tc)
    return best[1:]


def _mm_call(a, b, mode, name, out_dtype, addend=None):
    if mode == "nn":
        (I, C), (_, J) = a.shape, b.shape
    elif mode == "nt":
        (I, C), (J, _) = a.shape, b.shape
    else:
        (C, I), (_, J) = a.shape, b.shape
    ti, tj, tc = _mm_tiles(I, J, C, a.dtype.itemsize, b.dtype.itemsize, jnp.dtype(out_dtype).itemsize)
    n_c = C // tc
    if mode == "nn":
        a_spec = pl.BlockSpec((ti, tc), lambda i, j, c: (i, c))
        b_spec = pl.BlockSpec((tc, tj), lambda i, j, c: (c, j))
        dims = (((1,), (0,)), ((), ()))
    elif mode == "nt":
        a_spec = pl.BlockSpec((ti, tc), lambda i, j, c: (i, c))
        b_spec = pl.BlockSpec((tj, tc), lambda i, j, c: (j, c))
        dims = (((1,), (1,)), ((), ()))
    else:
        a_spec = pl.BlockSpec((tc, ti), lambda i, j, c: (c, i))
        b_spec = pl.BlockSpec((tc, tj), lambda i, j, c: (c, j))
        dims = (((0,), (0,)), ((), ()))

    def product(a_ref, b_ref):
        return lax.dot_general(a_ref[...].astype(BF16), b_ref[...].astype(BF16), dims, preferred_element_type=F32)

    def finish(o_ref, r_refs, value):
        for r_ref in r_refs:
            value = r_ref[...] + value
        o_ref[...] = value.astype(o_ref.dtype)

    def body_single(a_ref, b_ref, *rest):
        finish(rest[-1], rest[:-1], product(a_ref, b_ref))

    def body_accumulate(a_ref, b_ref, *rest):
        (*r_refs, o_ref, acc), c = rest, pl.program_id(2)

        @pl.when(c == 0)
        def _():
            acc[...] = jnp.zeros_like(acc)

        acc[...] += product(a_ref, b_ref)

        @pl.when(c == n_c - 1)
        def _():
            finish(o_ref, r_refs, acc[...])

    out_spec = pl.BlockSpec((ti, tj), lambda i, j, c: (i, j))
    addends = [] if addend is None else [addend]
    return pl.pallas_call(
        body_single if n_c == 1 else body_accumulate, name=name, grid=(I // ti, J // tj, n_c),
        in_specs=[a_spec, b_spec] + [out_spec] * len(addends),
        out_specs=out_spec,
        out_shape=jax.ShapeDtypeStruct((I, J), out_dtype),
        scratch_shapes=[] if n_c == 1 else [pltpu.VMEM((ti, tj), F32)],
        compiler_params=pltpu.CompilerParams(dimension_semantics=("parallel", "parallel", "arbitrary"),
                                             vmem_limit_bytes=VMEM_LIMIT),
    )(a, b, *addends)


def _mm_add(residual, a, b, name):
    @jax.custom_vjp
    def run(residual, a, b):
        return _mm_call(a, b, "nn", "mm_" + name, F32, addend=residual)

    def fwd(residual, a, b):
        return run(residual, a, b), (a, b)

    def bwd(res, g):
        a, b = res
        return (g, _mm_call(g, b, "nt", "mm_" + name + "_da", a.dtype),
                _mm_call(a, g, "tn", "mm_" + name + "_db", b.dtype))

    run.defvjp(fwd, bwd)
    return run(residual, a, b)


def _mm(a, b, name):
    @jax.custom_vjp
    def run(a, b):
        return _mm_call(a, b, "nn", "mm_" + name, F32)

    def fwd(a, b):
        return run(a, b), (a, b)

    def bwd(res, g):
        a, b = res
        return (_mm_call(g, b, "nt", "mm_" + name + "_da", a.dtype),
                _mm_call(a, g, "tn", "mm_" + name + "_db", b.dtype))

    run.defvjp(fwd, bwd)
    return run(a, b)


def _mm_t(a, wt, name):
    @jax.custom_vjp
    def run(a, wt):
        return _mm_call(a, wt, "nt", "mmt_" + name, F32)

    def fwd(a, wt):
        return run(a, wt), (a, wt)

    def bwd(res, g):
        a, wt = res
        return (_mm_call(g, wt, "nn", "mmt_" + name + "_da", a.dtype),
                _mm_call(g, a, "tn", "mmt_" + name + "_dw", wt.dtype))

    run.defvjp(fwd, bwd)
    return run(a, wt)


def _stage_layout(rows, tm, cols):
    G, T, C = rows[0].shape
    if cols is None:
        return ((G, T // tm), lambda c: c, lambda c: pl.BlockSpec((1, tm, c), lambda g, t: (g, t, 0)),
                lambda c: pl.BlockSpec((1, 1, c), lambda g, t: (g, 0, 0)), lambda c: (G, T, c))
    return ((C // cols, T // tm), lambda c: cols, lambda c: pl.BlockSpec((1, tm, cols), lambda g, t: (0, t, g)),
            lambda c: pl.BlockSpec((1, 1, cols), lambda g, t: (0, 0, g)), lambda c: (1, T, C))


def _stage_fwd_call(name, fn, rows, params, tm, cols):
    grid, width, row_spec, par_spec, out_dims = _stage_layout(rows, tm, cols)
    nr, npar = len(rows), len(params)
    out_avals = jax.eval_shape(
        lambda *a: tuple(fn(*a)),
        *[jax.ShapeDtypeStruct((tm, width(r.shape[2])), r.dtype) for r in rows],
        *[jax.ShapeDtypeStruct((1, width(p.shape[2])), p.dtype) for p in params])

    def body(*refs):
        vals = [r[0] for r in refs[:nr + npar]]
        for o_ref, o in zip(refs[nr + npar:], fn(*vals)):
            o_ref[0] = o

    return pl.pallas_call(
        body, name=name, grid=grid,
        in_specs=[row_spec(r.shape[2]) for r in rows] + [par_spec(p.shape[2]) for p in params],
        out_specs=[row_spec(o.shape[1]) for o in out_avals],
        out_shape=[jax.ShapeDtypeStruct(out_dims(o.shape[1]), o.dtype) for o in out_avals],
        compiler_params=pltpu.CompilerParams(dimension_semantics=("parallel", "parallel"),
                                             vmem_limit_bytes=VMEM_LIMIT),
    )(*rows, *params)


def _stage_bwd_call(name, fn, rows, params, cts, tm, cols):
    grid, _, row_spec, par_spec, _ = _stage_layout(rows, tm, cols)
    nr, npar, nout = len(rows), len(params), len(cts)

    def body(*refs):
        vals = [r[0] for r in refs[:nr + npar]]
        ct_vals = tuple(r[0] for r in refs[nr + npar:nr + npar + nout])
        d_refs = refs[nr + npar + nout:]
        _, vjp_fn = jax.vjp(lambda *a: tuple(fn(*a)), *vals)
        grads = vjp_fn(ct_vals)
        for i in range(nr):
            d_refs[i][0] = grads[i]

        if npar:
            @pl.when(pl.program_id(1) == 0)
            def _():
                for j in range(npar):
                    d_refs[nr + j][...] = jnp.zeros_like(d_refs[nr + j])

        for j in range(npar):
            d_refs[nr + j][0] += grads[nr + j]

    outs = pl.pallas_call(
        body, name=name + "_bwd", grid=grid,
        in_specs=([row_spec(r.shape[2]) for r in rows] + [par_spec(p.shape[2]) for p in params]
                  + [row_spec(c.shape[2]) for c in cts]),
        out_specs=[row_spec(r.shape[2]) for r in rows] + [par_spec(p.shape[2]) for p in params],
        out_shape=([jax.ShapeDtypeStruct(r.shape, r.dtype) for r in rows]
                   + [jax.ShapeDtypeStruct(p.shape, p.dtype) for p in params]),
        compiler_params=pltpu.CompilerParams(dimension_semantics=("parallel", "arbitrary"),
                                             vmem_limit_bytes=VMEM_LIMIT),
    )(*rows, *params, *cts)
    return tuple(outs[:nr]), tuple(outs[nr:])


def _stage(name, fn, rows, params, tm, cols=None):
    @jax.custom_vjp
    def run(rows, params):
        return tuple(_stage_fwd_call(name, fn, rows, params, tm, cols))

    def fwd(rows, params):
        return run(rows, params), (rows, params)

    def bwd(res, cts):
        rows, params = res
        return _stage_bwd_call(name, fn, rows, params, tuple(cts), tm, cols)

    run.defvjp(fwd, bwd)
    return run(tuple(rows), tuple(params))


def _sigmoid(x):
    return 0.5 * (jnp.tanh(0.5 * x) + 1.0)


def _softplus(x):
    return jnp.maximum(x, 0.0) + jnp.log(1.0 + jnp.exp(-jnp.abs(x)))


def _rms(x, g, eps=RMS_EPS):
    return x * lax.rsqrt(jnp.mean(x * x, axis=-1, keepdims=True) + eps) * g


def _head_sum_pieces(x):
    n = x.shape[-1]
    same_head = (lax.broadcasted_iota(jnp.int32, (n, n), 0) // HEAD_DIM
                 == lax.broadcasted_iota(jnp.int32, (n, n), 1) // HEAD_DIM).astype(BF16)
    hi = x.astype(BF16)
    lo = (x - hi.astype(F32)).astype(BF16)
    dims = (((1,), (0,)), ((), ()))
    return (lax.dot_general(hi, same_head, dims, preferred_element_type=F32)
            + lax.dot_general(lo, same_head, dims, preferred_element_type=F32))


@jax.custom_vjp
def _head_sum(x):
    return _head_sum_pieces(x)


def _head_sum_fwd(x):
    return _head_sum_pieces(x), None


def _head_sum_bwd(_, g):
    return (_head_sum_pieces(g),)


_head_sum.defvjp(_head_sum_fwd, _head_sum_bwd)


def _head_mean(x):
    return _head_sum(x) * (1.0 / HEAD_DIM)


def _fn_rmsnorm(x, g):
    return (_rms(x, g).astype(BF16),)


def _fn_swiglu(gate, up):
    return ((gate * _sigmoid(gate) * up).astype(BF16),)


def _make_fn_lora_mix(p1, p2):
    def fn(u, u_prev, mu):
        um = u + (u_prev - u) * mu
        return (jnp.tanh(um[:, :p1]).astype(BF16), um[:, p1:p1 + p2].astype(BF16),
                _sigmoid(um[:, p1 + p2:]).astype(BF16))
    return fn


def _fn_rwkv_prep(ru, ru_p, ku, ku_p, vu, vu_p, w_lin, a_lin, mu_r, mu_k, mu_v, w0, a0, k_k, k_a):
    r = ru + (ru_p - ru) * mu_r
    k = ku + (ku_p - ku) * mu_k
    v = vu + (vu_p - vu) * mu_v
    w_log = -_softplus(-(w0 + w_lin)) - 0.5
    lw = -jnp.exp(w_log)
    a = _sigmoid(a0 + a_lin)
    kk = k * k_k
    kk = kk / jnp.maximum(jnp.sqrt(_head_sum(kk * kk)), 1e-12)
    k_mod = k * (1.0 + (a - 1.0) * k_a)
    return r, lw, k_mod, v, kk, kk * a


def _fn_rwkv_post(y, r, k_mod, v, g, lnx_g, lnx_b, r_k):
    yc = y - _head_mean(y)
    yn = yc * lax.rsqrt(_head_mean(yc * yc) + GN_EPS) * lnx_g + lnx_b
    bonus = _head_sum(r * k_mod * r_k) * v
    return (((yn + bonus) * g).astype(BF16),)


def _head_rms(x, g):
    return x * lax.rsqrt(_head_mean(x * x) + RMS_EPS) * g


def _fn_fox_prep(q, k, qg, kg):
    return _head_rms(q, qg), _head_rms(k, kg)


def _fn_log_forget(f_raw, b):
    x = f_raw + b
    return (jnp.minimum(x, 0.0) - jnp.log(1.0 + jnp.exp(-jnp.abs(x))),)


def _fn_final(z, e_raw, h2, target, gate_b, ple_g):
    gate = _sigmoid(z + gate_b)
    out = h2 + gate * _rms(e_raw, ple_g)
    err = out - target
    return (0.5 * jnp.mean(err * err, axis=-1, keepdims=True),)


def _dot_bf16(a, b, ca, cb):
    return lax.dot_general(a.astype(BF16), b.astype(BF16), (((ca,), (cb,)), ((0,), (0,))),
                           preferred_element_type=F32)


@functools.partial(jax.custom_vjp, nondiff_argnums=(2, 3))
def _dot(a, b, ca, cb):
    return _dot_bf16(a, b, ca, cb)


def _dot_fwd(a, b, ca, cb):
    return _dot_bf16(a, b, ca, cb), (a, b)


def _dot_bwd(ca, cb, res, g):
    a, b = res
    ia, jb = 3 - ca, 3 - cb
    da = _dot_bf16(g, b, 2, jb) if ca == 2 else _dot_bf16(b, g, jb, 2)
    db = _dot_bf16(a, g, ia, 1) if cb == 1 else _dot_bf16(g, a, 1, ia)
    return da, db


_dot.defvjp(_dot_fwd, _dot_bwd)


def _scan_chunk(S0, r, lw, k, v, kk, b):
    B, L, _ = r.shape
    row = lax.broadcasted_iota(jnp.int32, (B, L, L), 1)
    col = lax.broadcasted_iota(jnp.int32, (B, L, L), 2)
    incl = col <= row
    strict = col < row
    cum = lax.dot_general(incl.astype(F32), lw, (((2,), (1,)), ((0,), (0,))), precision=HIGHEST,
                          preferred_element_type=F32)
    g_in, g_ex, g_inv = jnp.exp(cum), jnp.exp(cum - lw), jnp.exp(-cum)
    kkg, kd, bd, rg = kk * g_ex, k * g_inv, b * g_inv, r * g_in
    a_k = jnp.where(strict, _dot(kkg, kd, 2, 2), 0.0)
    a_b = jnp.where(strict, _dot(kkg, bd, 2, 2), 0.0)
    pw = -a_b
    inv = (row == col).astype(F32) + pw
    for _ in range(int(math.log2(L)) - 1):
        pw = _dot(pw, pw, 2, 1)
        inv = inv + _dot(inv, pw, 2, 1)
    sa = -_dot(inv, _dot(kkg, S0, 2, 2) + _dot(a_k, v, 2, 1), 2, 1)
    r_k = jnp.where(incl, _dot(rg, kd, 2, 2), 0.0)
    r_b = jnp.where(incl, _dot(rg, bd, 2, 2), 0.0)
    y = _dot(rg, S0, 2, 2) + _dot(r_k, v, 2, 1) + _dot(r_b, sa, 2, 1)
    g_end = jnp.exp(jnp.sum(lw, axis=1, keepdims=True))
    S1 = S0 * g_end + _dot(v, kd * g_end, 1, 1) + _dot(sa, bd * g_end, 1, 1)
    return y, S1


def _scan_heads_per_step(H):
    return next(hb for hb in (SCAN_HEADS_PER_STEP, 2, 1) if H % hb == 0)


def _scan_fwd_call(r, lw, k, v, kk, b):
    H, T, N = r.shape
    L = SCAN_CHUNK
    n_chunks = T // L
    hb = _scan_heads_per_step(H)

    def body(r_ref, lw_ref, k_ref, v_ref, kk_ref, b_ref, y_ref, s0_ref, state):
        @pl.when(pl.program_id(1) == 0)
        def _():
            state[...] = jnp.zeros_like(state)

        S0 = state[...]
        s0_ref[:, 0] = S0
        y, S1 = _scan_chunk(S0, r_ref[...], lw_ref[...], k_ref[...], v_ref[...], kk_ref[...], b_ref[...])
        y_ref[...] = y
        state[...] = S1

    blk = pl.BlockSpec((hb, L, N), lambda h, c: (h, c, 0))
    return pl.pallas_call(
        body, name="rwkv_scan_fwd", grid=(H // hb, n_chunks),
        in_specs=[blk] * 6,
        out_specs=[blk, pl.BlockSpec((hb, 1, N, N), lambda h, c: (h, c, 0, 0))],
        out_shape=[jax.ShapeDtypeStruct((H, T, N), F32), jax.ShapeDtypeStruct((H, n_chunks, N, N), F32)],
        scratch_shapes=[pltpu.VMEM((hb, N, N), F32)],
        compiler_params=pltpu.CompilerParams(dimension_semantics=("parallel", "arbitrary")),
    )(r, lw, k, v, kk, b)


def _scan_bwd_call(r, lw, k, v, kk, b, s0s, dy):
    H, T, N = r.shape
    L = SCAN_CHUNK
    n_chunks = T // L
    hb = _scan_heads_per_step(H)

    def body(r_ref, lw_ref, k_ref, v_ref, kk_ref, b_ref, s0_ref, dy_ref, dr, dlw, dk, dv, dkk, db, d_state):
        @pl.when(pl.program_id(1) == 0)
        def _():
            d_state[...] = jnp.zeros_like(d_state)

        _, vjp_fn = jax.vjp(_scan_chunk, s0_ref[:, 0], r_ref[...], lw_ref[...], k_ref[...], v_ref[...], kk_ref[...],
                            b_ref[...])
        grads = vjp_fn((dy_ref[...], d_state[...]))
        d_state[...] = grads[0]
        for o_ref, g in zip((dr, dlw, dk, dv, dkk, db), grads[1:]):
            o_ref[...] = g

    blk = pl.BlockSpec((hb, L, N), lambda h, c: (h, n_chunks - 1 - c, 0))
    return pl.pallas_call(
        body, name="rwkv_scan_bwd", grid=(H // hb, n_chunks),
        in_specs=[blk] * 6 + [pl.BlockSpec((hb, 1, N, N), lambda h, c: (h, n_chunks - 1 - c, 0, 0)), blk],
        out_specs=[blk] * 6,
        out_shape=[jax.ShapeDtypeStruct((H, T, N), F32)] * 6,
        scratch_shapes=[pltpu.VMEM((hb, N, N), F32)],
        compiler_params=pltpu.CompilerParams(dimension_semantics=("parallel", "arbitrary")),
    )(r, lw, k, v, kk, b, s0s, dy)


@jax.custom_vjp
def _rwkv_scan(r, lw, k, v, kk, b):
    return _scan_fwd_call(r, lw, k, v, kk, b)[0]


def _rwkv_scan_fwd(r, lw, k, v, kk, b):
    y, s0s = _scan_fwd_call(r, lw, k, v, kk, b)
    return y, (r, lw, k, v, kk, b, s0s)


def _rwkv_scan_bwd(res, dy):
    return tuple(_scan_bwd_call(*res, dy))


_rwkv_scan.defvjp(_rwkv_scan_fwd, _rwkv_scan_bwd)


def _nt(a, b):
    return lax.dot_general(a.astype(BF16), b.astype(BF16), (((1,), (1,)), ((), ())), preferred_element_type=F32)


def _nn(a, b):
    return lax.dot_general(a.astype(BF16), b.astype(BF16), (((1,), (0,)), ((), ())), preferred_element_type=F32)


def _tn(a, b):
    return lax.dot_general(a.astype(BF16), b.astype(BF16), (((0,), (0,)), ((), ())), preferred_element_type=F32)


def _attn_scores(qs, k_ref, cr_ref, row_bias, j, kb, q0, masked):
    ks = pl.multiple_of(j * kb, kb)
    kj = k_ref[0, pl.ds(ks, kb), :]
    s = _nt(qs, kj) + row_bias - cr_ref[0, j]
    if masked:
        qi = q0 + lax.broadcasted_iota(jnp.int32, s.shape, 0)
        ki = ks + lax.broadcasted_iota(jnp.int32, s.shape, 1)
        s = jnp.where(ki <= qi, s, -jnp.inf)
    return s, kj, ks


def _attn_specs(T, N, bq, kb):
    q_spec = pl.BlockSpec((1, bq, N), lambda h, i: (h, i, 0))
    kv_spec = pl.BlockSpec((1, T, N), lambda h, i: (h, 0, 0))
    col_spec = pl.BlockSpec((1, bq, 1), lambda h, i: (h, i, 0))
    row_spec = pl.BlockSpec((1, T // kb, 1, kb), lambda h, i: (h, 0, 0, 0))
    return q_spec, kv_spec, col_spec, row_spec


def _grid_marks(H, n_q):
    h, i = pl.program_id(0), pl.program_id(1)
    return (h == 0) & (i == 0), (h == (3 * H) // 4) & (i == 0), (h == H - 1) & (i == n_q - 1)


def _attn_fwd_call(q, k, v, c_col, c_rows, gather_xs):
    H, T, N = q.shape
    bq, kb = min(ATTN_BLOCK_Q, T), c_rows.shape[3]
    q_spec, kv_spec, col_spec, row_spec = _attn_specs(T, N, bq, kb)
    n = len(gather_xs)

    def body(q_ref, k_ref, v_ref, cc_ref, cr_ref, *rest):
        x_refs, (o_ref, o32_ref, lse_ref), out_refs, sems = rest[:n], rest[n:n + 3], rest[n + 3:2 * n + 3], rest[2 * n + 3:]
        first, middle, last = _grid_marks(H, T // bq)
        start, relay, finish = _gather_phases(x_refs, out_refs, *sems)
        pl.when(first)(start)
        pl.when(middle)(relay)
        q0 = pl.program_id(1) * bq
        qs = (q_ref[0] * (HEAD_DIM ** -0.5)).astype(BF16)
        cc = cc_ref[0]

        def step(j, carry, masked):
            m, l, acc = carry
            s, _, ks = _attn_scores(qs, k_ref, cr_ref, cc, j, kb, q0, masked)
            m_new = jnp.maximum(m, jnp.max(s, axis=-1, keepdims=True))
            alpha = jnp.exp(m - m_new)
            p = jnp.exp(s - m_new)
            l = alpha * l + jnp.sum(p, axis=-1, keepdims=True)
            p_hi = p.astype(BF16)
            p_lo = p - p_hi.astype(F32)
            vj = v_ref[0, pl.ds(ks, kb), :]
            acc = alpha * acc + (_nn(p_hi, vj) + _nn(p_lo, vj))
            return m_new, l, acc

        n_full = q0 // kb
        init = (jnp.full((bq, 1), -jnp.inf, F32), jnp.zeros((bq, 1), F32), jnp.zeros((bq, N), F32))
        carry = lax.fori_loop(0, n_full, functools.partial(step, masked=False), init)
        m, l, acc = step(n_full, carry, masked=True)
        o = acc / l
        o_ref[0] = o.astype(o_ref.dtype)
        o32_ref[0] = o
        lse_ref[0] = m + jnp.log(l)
        pl.when(last)(finish)

    any_spec = pl.BlockSpec(memory_space=pl.ANY)
    return pl.pallas_call(
        body, name="fox_attn_fwd", grid=(H, T // bq),
        in_specs=[q_spec, kv_spec, kv_spec, col_spec, row_spec] + [any_spec] * n,
        out_specs=[q_spec, q_spec, col_spec] + [any_spec] * n,
        out_shape=[jax.ShapeDtypeStruct((H, T, N), BF16),
                   jax.ShapeDtypeStruct((H, T, N), F32),
                   jax.ShapeDtypeStruct((H, T, 1), F32)] + _gather_out_shapes(gather_xs),
        scratch_shapes=_comm_semaphores(n),
        compiler_params=pltpu.CompilerParams(dimension_semantics=("arbitrary", "arbitrary"),
                                             vmem_limit_bytes=VMEM_LIMIT),
    )(q, k, v, c_col, c_rows, *gather_xs)


def _attn_bwd_call(q, k, v, c_col, c_rows, o, lse, do, scatter_parts):
    H, T, N = q.shape
    bq, kb = min(ATTN_BLOCK_Q, T), c_rows.shape[3]
    q_spec, kv_spec, col_spec, row_spec = _attn_specs(T, N, bq, kb)
    n = len(scatter_parts)

    def body(q_ref, k_ref, v_ref, cc_ref, cr_ref, o_ref, lse_ref, do_ref, *rest):
        a_refs, (dq_ref, dk_ref, dv_ref, dcr_ref), b_refs, sems = rest[:n], rest[n:n + 4], rest[n + 4:2 * n + 4], rest[2 * n + 4:]
        first, _, last = _grid_marks(H, T // bq)
        start, finish = _scatter_phases(a_refs, b_refs, *sems)
        pl.when(first)(start)
        i = pl.program_id(1)
        q0 = i * bq

        @pl.when(i == 0)
        def _():
            dk_ref[...] = jnp.zeros_like(dk_ref)
            dv_ref[...] = jnp.zeros_like(dv_ref)
            dcr_ref[...] = jnp.zeros_like(dcr_ref)

        qs = (q_ref[0] * (HEAD_DIM ** -0.5)).astype(BF16)
        do = do_ref[0]
        delta = jnp.sum(do.astype(F32) * o_ref[0], axis=-1, keepdims=True)
        row_bias = cc_ref[0] - lse_ref[0]

        def step(j, dq, masked):
            s, kj, ks = _attn_scores(qs, k_ref, cr_ref, row_bias, j, kb, q0, masked)
            p = jnp.exp(s)
            ds = p * (_nt(do, v_ref[0, pl.ds(ks, kb), :]) - delta)
            ds_b = ds.astype(BF16)
            dk_ref[0, pl.ds(ks, kb), :] += _tn(ds_b, qs)
            dv_ref[0, pl.ds(ks, kb), :] += _tn(p, do)
            dcr_ref[0, j] -= jnp.sum(ds, axis=0, keepdims=True)
            return dq + _nn(ds_b, kj)

        n_full = q0 // kb
        dq = lax.fori_loop(0, n_full, functools.partial(step, masked=False), jnp.zeros((bq, N), F32))
        dq_ref[0] = step(n_full, dq, masked=True) * (HEAD_DIM ** -0.5)
        pl.when(last)(finish)

    any_spec = pl.BlockSpec(memory_space=pl.ANY)
    return pl.pallas_call(
        body, name="fox_attn_bwd", grid=(H, T // bq),
        in_specs=[q_spec, kv_spec, kv_spec, col_spec, row_spec, q_spec, col_spec, q_spec] + [any_spec] * n,
        out_specs=[q_spec, kv_spec, kv_spec, row_spec] + [any_spec] * n,
        out_shape=[jax.ShapeDtypeStruct((H, T, N), F32)] * 3 + [jax.ShapeDtypeStruct(c_rows.shape, F32)]
        + [jax.ShapeDtypeStruct(a.shape, a.dtype) for a in scatter_parts],
        scratch_shapes=_comm_semaphores(n),
        compiler_params=pltpu.CompilerParams(dimension_semantics=("arbitrary", "arbitrary"),
                                             vmem_limit_bytes=VMEM_LIMIT),
    )(q, k, v, c_col, c_rows, o, lse, do, *scatter_parts)


@jax.custom_vjp
def _fox_attn(q, k, v, c, late_blocks, carrier):
    return _fox_attn_fwd(q, k, v, c, late_blocks, carrier)[0]


def _attn_bias_views(c):
    H, T = c.shape
    kb = min(ATTN_BLOCK_K, T)
    return c[:, :, None], c.reshape(H, T // kb, 1, kb)


def _fox_attn_fwd(q, k, v, c, late_blocks, carrier):
    o, o32, lse, *gathered = _attn_fwd_call(q, k, v, *_attn_bias_views(c), late_blocks)
    return (o, tuple(gathered)), (q, k, v, c, o32, lse)


def _fox_attn_bwd(res, cts):
    q, k, v, c, o32, lse = res
    do, d_gathered = cts
    dq, dk, dv, dc_rows, *parts = _attn_bwd_call(q, k, v, *_attn_bias_views(c), o32, lse, do, d_gathered)
    no_grad = tuple(jnp.zeros(a.shape[1:], a.dtype) for a in parts)
    return dq, dk, dv, dc_rows.reshape(c.shape), no_grad, tuple(parts)


_fox_attn.defvjp(_fox_attn_fwd, _fox_attn_bwd)


N_PEERS = N_DEV - 1


def _all_gather(xs, name):
    n = len(xs)

    def body(*refs):
        start, relay, finish = _gather_phases(refs[:n], refs[n:2 * n], *refs[2 * n:])
        start()
        relay()
        finish()

    any_spec = pl.BlockSpec(memory_space=pl.ANY)
    return pl.pallas_call(
        body, name=name,
        out_shape=_gather_out_shapes(xs),
        in_specs=[any_spec] * n, out_specs=[any_spec] * n,
        scratch_shapes=_comm_semaphores(n),
    )(*xs)


def _gather_out_shapes(xs):
    return [jax.ShapeDtypeStruct((N_DEV,) + x.shape, x.dtype) for x in xs]


def _comm_semaphores(n):
    return [pltpu.SemaphoreType.DMA((N_PEERS * n,)), pltpu.SemaphoreType.DMA((N_PEERS * n,)),
            pltpu.SemaphoreType.DMA((n,))]


def _gather_phases(x_refs, out_refs, send_sems, recv_sems, local_sems):
    n = len(x_refs)
    x_, y_, c_ = lax.axis_index("x"), lax.axis_index("y"), lax.axis_index("c")
    me, sibling = (x_, y_, c_), (x_, y_, 1 - c_)
    chips = [(1 - x_, y_), (x_, 1 - y_), (1 - x_, 1 - y_)]

    def slot(t, px, py, pc):
        return out_refs[t].at[4 * px + 2 * py + pc]

    def copy(t, k, block, to, src=None):
        return pltpu.make_async_remote_copy(
            src_ref=slot(t, *block) if src is None else src, dst_ref=slot(t, *block),
            send_sem=send_sems.at[k * n + t], recv_sem=recv_sems.at[k * n + t],
            device_id=to, device_id_type=pl.DeviceIdType.MESH)

    def mine():
        return [pltpu.make_async_copy(x_refs[t], slot(t, *me), local_sems.at[t]) for t in range(n)]

    def first():
        return ([copy(t, 0, me, sibling, src=x_refs[t]) for t in range(n)]
                + [copy(t, 1 + j, me, (*chip, c_), src=x_refs[t]) for j, chip in enumerate(chips) for t in range(n)])

    def passed():
        return [copy(t, 4 + j, (*chip, c_), sibling) for j, chip in enumerate(chips) for t in range(n)]

    def start():
        for cp in mine() + first():
            cp.start()

    def relay():
        for j, chip in enumerate(chips):
            for t in range(n):
                copy(t, 1 + j, (*chip, c_), me).wait_recv()
                copy(t, 4 + j, (*chip, c_), sibling).start()

    def finish():
        for t in range(n):
            copy(t, 0, sibling, me).wait_recv()
        for j, chip in enumerate(chips):
            for t in range(n):
                copy(t, 4 + j, (*chip, 1 - c_), me).wait_recv()
        for cp in first() + passed():
            cp.wait_send()
        for cp in mine():
            cp.wait()

    return start, relay, finish


def _all_to_all(parts, name):
    n = len(parts)

    def body(*refs):
        start, finish = _scatter_phases(refs[:n], refs[n:2 * n], *refs[2 * n:])
        start()
        finish()

    any_spec = pl.BlockSpec(memory_space=pl.ANY)
    return pl.pallas_call(
        body, name=name,
        out_shape=[jax.ShapeDtypeStruct(a.shape, a.dtype) for a in parts],
        in_specs=[any_spec] * n, out_specs=[any_spec] * n,
        scratch_shapes=_comm_semaphores(n),
    )(*parts)


def _scatter_phases(a_refs, b_refs, send_sems, recv_sems, local_sems):
    n = len(a_refs)
    x_, y_, c_ = lax.axis_index("x"), lax.axis_index("y"), lax.axis_index("c")
    me_idx = 4 * x_ + 2 * y_ + c_

    def copies():
        out = [pltpu.make_async_copy(a_refs[t].at[me_idx], b_refs[t].at[me_idx], local_sems.at[t]) for t in range(n)]
        for rel in range(1, N_DEV):
            px = 1 - x_ if rel & 4 else x_
            py = 1 - y_ if rel & 2 else y_
            pc = 1 - c_ if rel & 1 else c_
            for t in range(n):
                out.append(pltpu.make_async_remote_copy(
                    src_ref=a_refs[t].at[4 * px + 2 * py + pc], dst_ref=b_refs[t].at[me_idx],
                    send_sem=send_sems.at[(rel - 1) * n + t], recv_sem=recv_sems.at[(rel - 1) * n + t],
                    device_id=(px, py, pc), device_id_type=pl.DeviceIdType.MESH))
        return out

    def start():
        for cp in copies():
            cp.start()

    def finish():
        for cp in copies():
            cp.wait()

    return start, finish


def _reduce_adamw(parts, w, m, v, name):
    R, C = w.shape
    tr = max(t for t in (256, 128, PACK_ROW_QUANTUM) if R % t == 0)

    def body(p_ref, w_ref, m_ref, v_ref, g_out, d_out, m_out, v_out):
        g = p_ref[0]
        for i in range(1, N_DEV):
            g = g + p_ref[i]
        m_new = ADAM_B1 * m_ref[...] + (1.0 - ADAM_B1) * g
        v_new = ADAM_B2 * v_ref[...] + (1.0 - ADAM_B2) * (g * g)
        m_hat = m_new / (1.0 - ADAM_B1 ** ADAM_STEP)
        v_hat = v_new / (1.0 - ADAM_B2 ** ADAM_STEP)
        g_out[...] = g
        d_out[...] = -ADAM_LR * (m_hat / (jnp.sqrt(v_hat) + ADAM_EPS) + ADAM_WD * w_ref[...])
        m_out[...] = m_new
        v_out[...] = v_new

    spec = pl.BlockSpec((tr, C), lambda i: (i, 0))
    return pl.pallas_call(
        body, name=name, grid=(R // tr,),
        in_specs=[pl.BlockSpec((N_DEV, tr, C), lambda i: (0, i, 0)), spec, spec, spec],
        out_specs=[spec] * 4,
        out_shape=[jax.ShapeDtypeStruct((R, C), F32)] * 4,
        compiler_params=pltpu.CompilerParams(dimension_semantics=("parallel",), vmem_limit_bytes=VMEM_LIMIT),
    )(parts, w, m, v)


def _pack(arrays, dtype):
    flat = jnp.concatenate([a.reshape(-1).astype(dtype) for a in arrays])
    rows = _round_up(-(-flat.shape[0] // PACK_COLS), PACK_ROW_QUANTUM)
    flat = jnp.pad(flat, (0, rows * PACK_COLS - flat.shape[0]))
    return flat.reshape(rows, PACK_COLS)


def _unpack(packed, shapes):
    lead = packed.shape[:-2]
    flat = packed.reshape(lead + (-1,))
    out, off = [], 0
    for s in shapes:
        n = math.prod(s)
        out.append(flat[..., off:off + n].reshape(lead + tuple(s)))
        off += n
    return out


def _travel_layout(name, block):
    return block.T if SHARDED[name] else block


REDUCE_BLOCK_BYTES = 4 * 1024 * 1024


def _reduce_parts(parts, name):
    _, R, C = parts.shape
    per_col = N_DEV * R * parts.dtype.itemsize
    tc = next((t for t in range(C - C % LANES, 0, -LANES) if C % t == 0 and t * per_col <= REDUCE_BLOCK_BYTES), C)

    def body(p_ref, o_ref):
        g = p_ref[0].astype(F32)
        for i in range(1, N_DEV):
            g = g + p_ref[i].astype(F32)
        o_ref[...] = g

    return pl.pallas_call(
        body, name=name, grid=(C // tc,),
        in_specs=[pl.BlockSpec((N_DEV, R, tc), lambda j: (0, 0, j))],
        out_specs=pl.BlockSpec((R, tc), lambda j: (0, j)),
        out_shape=jax.ShapeDtypeStruct((R, C), F32),
        compiler_params=pltpu.CompilerParams(dimension_semantics=("parallel",), vmem_limit_bytes=VMEM_LIMIT),
    )(parts)


def _adamw(g, w, m, v, name):
    R, C = w.shape
    tr = next((t for t in (512, 256, 128, 64, 32, 16, 8) if R % t == 0 and t * C * 4 <= 2 * 1024 * 1024), R)

    def body(g_ref, w_ref, m_ref, v_ref, d_out, m_out, v_out):
        g_ = g_ref[...]
        m_new = ADAM_B1 * m_ref[...] + (1.0 - ADAM_B1) * g_
        v_new = ADAM_B2 * v_ref[...] + (1.0 - ADAM_B2) * (g_ * g_)
        m_hat = m_new / (1.0 - ADAM_B1 ** ADAM_STEP)
        v_hat = v_new / (1.0 - ADAM_B2 ** ADAM_STEP)
        d_out[...] = -ADAM_LR * (m_hat / (jnp.sqrt(v_hat) + ADAM_EPS) + ADAM_WD * w_ref[...])
        m_out[...] = m_new
        v_out[...] = v_new

    spec = pl.BlockSpec((tr, C), lambda i: (i, 0))
    return pl.pallas_call(
        body, name=name, grid=(R // tr,),
        in_specs=[spec] * 4, out_specs=[spec] * 3,
        out_shape=[jax.ShapeDtypeStruct((R, C), F32)] * 3,
        compiler_params=pltpu.CompilerParams(dimension_semantics=("parallel",), vmem_limit_bytes=VMEM_LIMIT),
    )(g, w, m, v)


def _to_heads(u):
    return jnp.transpose(u.reshape(u.shape[0], -1, HEAD_DIM), (1, 0, 2))


def _split_cols(u, n):
    width = u.shape[1] // n

    @jax.custom_vjp
    def run(u):
        return tuple(u[:, i * width:(i + 1) * width] for i in range(n))

    def fwd(u):
        return run(u), None

    def bwd(_, cts):
        return (jnp.concatenate(cts, axis=1),)

    run.defvjp(fwd, bwd)
    return run(u)


def _from_heads(uh):
    H, T, N = uh.shape
    return jnp.transpose(uh, (1, 0, 2)).reshape(T, H * N)


def _shift(uh):
    return jnp.pad(uh, ((0, 0), (1, 0), (0, 0)))[:, :-1]


def _pad_cols(a, width):
    return jnp.pad(a, ((0, 0), (0, width - a.shape[1])))


def _split_rows(w, sizes):
    offsets = [sum(sizes[:i]) for i in range(len(sizes))]

    @jax.custom_vjp
    def run(w):
        return tuple(w[o:o + s] for o, s in zip(offsets, sizes))

    def fwd(w):
        return run(w), None

    def bwd(_, cts):
        return (jnp.concatenate(cts, axis=0),)

    run.defvjp(fwd, bwd)
    return run(w)


def _pad_rows(a, height):
    return jnp.pad(a, ((0, height - a.shape[0]), (0, 0)))


def _vec(a):
    return a.reshape(1, 1, -1)


TM_WIDE = 128


def _mixing_half(W, small, x, late_blocks, carrier):
    T, D = x.shape
    vec = _vec
    tm_wide = TM_WIDE
    rw = small['w0'].shape[-1]
    fw = W['w_out'].shape[0] - rw
    heads_f = fw // HEAD_DIM
    dl, al, gl = W['w2'].shape[1], W['a2'].shape[1], W['g2'].shape[1]
    dl_p, al_p, gl_p = _round_up(dl, LANES), _round_up(al, LANES), _round_up(gl, LANES)
    f_p = _round_up(heads_f, LANES)
    rwkv_cols = 3 * rw + dl + al + gl
    tm_head = next((t for t in (1024, 512) if T % t == 0), T)

    o_w, o_a, o_g = 3 * rw, 3 * rw + dl, 3 * rw + dl + al
    w_rkv, w_xw, w_xa, w_xg, w_qkv, w_fg = _split_rows(W['w_in'], (3 * rw, dl, al, gl, 3 * fw, heads_f))
    w_lora = jnp.concatenate([_pad_rows(w_xw, dl_p), _pad_rows(w_xa, al_p), _pad_rows(w_xg, gl_p)], axis=0)
    w_f = _pad_rows(w_fg, f_p)
    mu = small['shift_mu'].reshape(1, -1)
    mu_lora = jnp.concatenate([_pad_cols(mu[:, o_w:o_a], dl_p), _pad_cols(mu[:, o_a:o_g], al_p),
                               _pad_cols(mu[:, o_g:rwkv_cols], gl_p)], axis=1)

    (xn,) = _stage("attn_norm", _fn_rmsnorm, [x[None]], [vec(small['attn_norm_g'])], tm_wide)
    xn = xn[0]
    u_rkv = _mm_t(xn, w_rkv, "in_rkv")
    u_lora = _mm_t(xn, w_lora, "in_lora")
    u_qkv = _mm_t(xn, w_qkv, "in_qkv")
    f_raw = _mm_t(xn, w_f, "in_f")

    u_lora3 = u_lora[None]
    xw_t, xa_m, xg_s = _stage("lora_mix", _make_fn_lora_mix(dl_p, al_p), [u_lora3, _shift(u_lora3)],
                              [vec(mu_lora)], tm_wide)
    w_lin = _mm_t(xw_t[0], _pad_cols(W['w2'], dl_p), "w2")
    a_lin = _mm_t(xa_m[0], _pad_cols(W['a2'], al_p), "a2")
    gate_r = _mm_t(xg_s[0], _pad_cols(W['g2'], gl_p), "g2")
    ru, ku, vu = (u[None] for u in _split_cols(u_rkv, 3))
    mu_r, mu_k, mu_v = (vec(mu[:, i * rw:(i + 1) * rw]) for i in range(3))
    prepped = _stage(
        "rwkv_prep", _fn_rwkv_prep,
        [ru, _shift(ru), ku, _shift(ku), vu, _shift(vu), w_lin[None], a_lin[None]],
        [mu_r, mu_k, mu_v, vec(small['w0']), vec(small['a0']), vec(small['k_k']), vec(small['k_a'])],
        tm_head, cols=HEAD_PAIR)
    r, lw, k_mod, v, kk, b = prepped
    y_scan = _rwkv_scan(*(_to_heads(t[0]) for t in prepped))
    (y_rwkv,) = _stage("rwkv_post", _fn_rwkv_post, [_from_heads(y_scan)[None], r, k_mod, v, gate_r[None]],
                       [vec(small['lnx_g']), vec(small['lnx_b']), vec(small['r_k'])], tm_head, cols=HEAD_PAIR)

    qu, kf, vf = _split_cols(u_qkv, 3)
    qg = vec(jnp.tile(small['q_norm_g'].reshape(-1), heads_f))
    kg = vec(jnp.tile(small['k_norm_g'].reshape(-1), heads_f))
    qn, kn = _stage("fox_prep", _fn_fox_prep, [qu[None], kf[None]], [qg, kg], tm_head, cols=HEAD_PAIR)
    fb = _pad_cols(small['fgate_b'].reshape(1, -1), f_p)
    (log_f,) = _stage("log_forget", _fn_log_forget, [f_raw[None]], [vec(fb)], tm_head)
    c = jnp.cumsum(log_f[0][:, :heads_f], axis=0).T
    y_fox, gathered_late = _fox_attn(_to_heads(qn[0]), _to_heads(kn[0]), _to_heads(vf), c, late_blocks, carrier)

    y_cat = jnp.concatenate([y_rwkv[0], _from_heads(y_fox)], axis=-1)
    return _mm_add(x, y_cat, W['w_out'], "out"), gathered_late


def _channel_half_loss(W, small, h1, p, target):
    vec = _vec
    tm_wide = TM_WIDE
    (hn,) = _stage("ffn_norm", _fn_rmsnorm, [h1[None]], [vec(small['ffn_norm_g'])], tm_wide)
    gate = _mm_t(hn[0], W['w_gate'], "gate")
    up = _mm_t(hn[0], W['w_up'], "up")
    (act,) = _stage("swiglu", _fn_swiglu, [gate[None], up[None]], [], tm_wide)
    h2 = _mm_add(h1, act[0], W['w_down'], "down")
    e_raw = _mm_t(p, W['ple_proj'], "ple_proj")
    (hg,) = _stage("ple_gate_norm", _fn_rmsnorm, [h2[None]], [vec(small['ple_gate_norm_g'])], tm_wide)
    z = _mm(hg[0], W['ple_gate_w'], "ple_gate")
    (loss_rows,) = _stage("final", _fn_final, [z[None], e_raw[None], h2[None], target[None]],
                          [vec(small['ple_gate_b']), vec(small['ple_norm_g'])], tm_wide)
    return jnp.sum(loss_rows)


def kernel(x, p, attn_norm_g, w_in, shift_mu, w0, w2, a0, a2, g2, k_k, k_a, r_k, lnx_g, lnx_b, q_norm_g, k_norm_g, fgate_b, w_out, ffn_norm_g, w_gate, w_up, w_down, ple_proj, ple_norm_g, ple_gate_norm_g, ple_gate_w, ple_gate_b, loss_target, m_attn_norm_g, m_w_in, m_shift_mu, m_w0, m_w2, m_a0, m_a2, m_g2, m_k_k, m_k_a, m_r_k, m_lnx_g, m_lnx_b, m_q_norm_g, m_k_norm_g, m_fgate_b, m_w_out, m_ffn_norm_g, m_w_gate, m_w_up, m_w_down, m_ple_proj, m_ple_norm_g, m_ple_gate_norm_g, m_ple_gate_w, m_ple_gate_b, v_attn_norm_g, v_w_in, v_shift_mu, v_w0, v_w2, v_a0, v_a2, v_g2, v_k_k, v_k_a, v_r_k, v_lnx_g, v_lnx_b, v_q_norm_g, v_k_norm_g, v_fgate_b, v_w_out, v_ffn_norm_g, v_w_gate, v_w_up, v_w_down, v_ple_proj, v_ple_norm_g, v_ple_gate_norm_g, v_ple_gate_w, v_ple_gate_b):
    weights = dict(zip(WEIGHT_NAMES, (attn_norm_g, w_in, shift_mu, w0, w2, a0, a2, g2, k_k, k_a, r_k, lnx_g, lnx_b,
                                      q_norm_g, k_norm_g, fgate_b, w_out, ffn_norm_g, w_gate, w_up, w_down, ple_proj,
                                      ple_norm_g, ple_gate_norm_g, ple_gate_w, ple_gate_b)))
    m_in = dict(zip(WEIGHT_NAMES, (m_attn_norm_g, m_w_in, m_shift_mu, m_w0, m_w2, m_a0, m_a2, m_g2, m_k_k, m_k_a, m_r_k,
                                   m_lnx_g, m_lnx_b, m_q_norm_g, m_k_norm_g, m_fgate_b, m_w_out, m_ffn_norm_g, m_w_gate,
                                   m_w_up, m_w_down, m_ple_proj, m_ple_norm_g, m_ple_gate_norm_g, m_ple_gate_w,
                                   m_ple_gate_b)))
    v_in = dict(zip(WEIGHT_NAMES, (v_attn_norm_g, v_w_in, v_shift_mu, v_w0, v_w2, v_a0, v_a2, v_g2, v_k_k, v_k_a, v_r_k,
                                   v_lnx_g, v_lnx_b, v_q_norm_g, v_k_norm_g, v_fgate_b, v_w_out, v_ffn_norm_g, v_w_gate,
                                   v_w_up, v_w_down, v_ple_proj, v_ple_norm_g, v_ple_gate_norm_g, v_ple_gate_w,
                                   v_ple_gate_b)))
    small_shapes = [weights[n].shape for n in SMALL_NAMES]
    small = {n: weights[n] for n in SMALL_NAMES}

    def whole(stacks, names):
        return {n: g.reshape(N_DEV * g.shape[1], g.shape[2]) for n, g in zip(names, stacks)}

    def stacked(tree, names, like):
        return tuple(tree[n].reshape(g.shape) for n, g in zip(names, like))

    travelling = {n: _travel_layout(n, weights[n][0]).astype(BF16) for n in SHARDED_NAMES}
    gathered_early = _all_gather([travelling[n] for n in EARLY_NAMES], "gather_weights")
    late_blocks = tuple(travelling[n] for n in LATE_NAMES)
    carrier = tuple(jnp.zeros((N_DEV,) + b.shape, b.dtype) for b in late_blocks)

    (h1, gathered_late), mixing_vjp = jax.vjp(_mixing_half, whole(gathered_early, EARLY_NAMES), small, x[0],
                                              late_blocks, carrier)
    loss_local, (d_late, d_small_b, d_h1) = jax.value_and_grad(_channel_half_loss, argnums=(0, 1, 2))(
        whole(gathered_late, LATE_NAMES), small, h1, p[0, 0], loss_target[0])
    d_early, d_small_a, d_x, _, parts_late = mixing_vjp((d_h1, stacked(d_late, LATE_NAMES, gathered_late)))
    d_small = {n: d_small_a[n] + d_small_b[n] for n in SMALL_NAMES}
    loss = lax.psum(loss_local, MESH_AXES)

    parts_early = _all_to_all(stacked(d_early, EARLY_NAMES, gathered_early), "scatter_grads")
    parts = dict(zip(EARLY_NAMES + LATE_NAMES, list(parts_early) + list(parts_late)))
    (small_parts,) = _all_gather([_pack([d_small[n] for n in SMALL_NAMES], F32)], "gather_small_grads")

    def pack_f32(tree, names):
        return _pack([tree[n] for n in names], F32)

    sml = _reduce_adamw(small_parts, pack_f32(weights, SMALL_NAMES), pack_f32(m_in, SMALL_NAMES),
                        pack_f32(v_in, SMALL_NAMES), "adamw_replicated")
    by_kind = [dict(zip(SMALL_NAMES, _unpack(sml[kind], small_shapes))) for kind in range(4)]
    for n in SHARDED_NAMES:
        g = _travel_layout(n, _reduce_parts(parts[n], "reduce_" + n))
        upd = _adamw(g, weights[n][0], m_in[n][0], v_in[n][0], "adamw_" + n)
        for kind, val in enumerate((g, *upd)):
            by_kind[kind][n] = val[None]
    outs = [by_kind[kind][n] for kind in range(4) for n in WEIGHT_NAMES]
    return (loss, d_x[None], *outs)
```

```python
import functools
import math

import jax
import jax.numpy as jnp
from jax import lax
from jax.experimental import pallas as pl
from jax.experimental.pallas import tpu as pltpu

F32 = jnp.float32
BF16 = jnp.bfloat16
HIGHEST = lax.Precision.HIGHEST

N_DEV = 8
MESH_AXES = ("x", "y", "c")
HEAD_DIM = 64
SCAN_CHUNK = 64
SCAN_HEADS_PER_STEP = 16
ATTN_BLOCK_Q = 1024
ATTN_BLOCK_K = 1024
LANES = 128
HEAD_PAIR = 2 * HEAD_DIM
PACK_COLS = 1024
PACK_ROW_QUANTUM = 64
VMEM_LIMIT = 48 * 1024 * 1024
RMS_EPS = 1e-6
GN_EPS = 64e-5
ADAM_LR, ADAM_B1, ADAM_B2, ADAM_EPS, ADAM_WD, ADAM_STEP = 0.001, 0.9, 0.999, 1e-08, 0.01, 10

WEIGHT_NAMES = ['attn_norm_g', 'w_in', 'shift_mu', 'w0', 'w2', 'a0', 'a2', 'g2', 'k_k', 'k_a', 'r_k', 'lnx_g', 'lnx_b',
                'q_norm_g', 'k_norm_g', 'fgate_b', 'w_out', 'ffn_norm_g', 'w_gate', 'w_up', 'w_down', 'ple_proj',
                'ple_norm_g', 'ple_gate_norm_g', 'ple_gate_w', 'ple_gate_b']
SHARDED = {'w_in': True, 'w2': True, 'a2': True, 'g2': True, 'w_out': False, 'w_gate': True, 'w_up': True,
           'w_down': False, 'ple_proj': True, 'ple_gate_w': False}
SHARDED_NAMES = [n for n in WEIGHT_NAMES if n in SHARDED]
SMALL_NAMES = [n for n in WEIGHT_NAMES if n not in SHARDED]
EARLY_NAMES = ['w_in', 'w2', 'a2', 'g2', 'w_out']
LATE_NAMES = [n for n in SHARDED_NAMES if n not in EARLY_NAMES]


def _round_up(n, q):
    return -(-n // q) * q


def _tile_candidates(n, cap):
    sizes = {t for t in range(LANES, min(n, cap) + 1, LANES) if n % t == 0}
    return sorted(sizes | ({n} if n <= cap or not sizes else set()))


MM_TILE_CAP = 2048
MM_VMEM_BUDGET = 38 * 1024 * 1024
MM_STEP_COST_BYTES = 1024 * 1024


def _mm_tiles(I, J, C, a_size, b_size, o_size):
    best = None
    for ti in _tile_candidates(I, MM_TILE_CAP):
        for tj in _tile_candidates(J, MM_TILE_CAP):
            for tc in _tile_candidates(C, MM_TILE_CAP):
                n_c = C // tc
                blocks = 2 * (ti * tc * a_size + tc * tj * b_size + ti * tj * o_size)
                temporaries = (ti * tc + tc * tj) * 2 + ti * tj * 4 * (2 if n_c > 1 else 1)
                if blocks + temporaries > MM_VMEM_BUDGET:
                    continue
                traffic = (I * C * a_size * (1 if n_c == 1 else J // tj) + C * J * b_size * (I // ti)
                           + I * J * o_size)
                cost = traffic + (I // ti) * (J // tj) * n_c * MM_STEP_COST_BYTES
                if best is None or cost < best[0]:
                    best = (cost, ti, tj, tc)
    return best[1:]


def _mm_call(a, b, mode, name, out_dtype, addend=None):
    if mode == "nn":
        (I, C), (_, J) = a.shape, b.shape
    elif mode == "nt":
        (I, C), (J, _) = a.shape, b.shape
    else:
        (C, I), (_, J) = a.shape, b.shape
    ti, tj, tc = _mm_tiles(I, J, C, a.dtype.itemsize, b.dtype.itemsize, jnp.dtype(out_dtype).itemsize)
    n_c = C // tc
    if mode == "nn":
        a_spec = pl.BlockSpec((ti, tc), lambda i, j, c: (i, c))
        b_spec = pl.BlockSpec((tc, tj), lambda i, j, c: (c, j))
        dims = (((1,), (0,)), ((), ()))
    elif mode == "nt":
        a_spec = pl.BlockSpec((ti, tc), lambda i, j, c: (i, c))
        b_spec = pl.BlockSpec((tj, tc), lambda i, j, c: (j, c))
        dims = (((1,), (1,)), ((), ()))
    else:
        a_spec = pl.BlockSpec((tc, ti), lambda i, j, c: (c, i))
        b_spec = pl.BlockSpec((tc, tj), lambda i, j, c: (c, j))
        dims = (((0,), (0,)), ((), ()))

    def product(a_ref, b_ref):
        return lax.dot_general(a_ref[...].astype(BF16), b_ref[...].astype(BF16), dims, preferred_element_type=F32)

    def finish(o_ref, r_refs, value):
        for r_ref in r_refs:
            value = r_ref[...] + value
        o_ref[...] = value.astype(o_ref.dtype)

    def body_single(a_ref, b_ref, *rest):
        finish(rest[-1], rest[:-1], product(a_ref, b_ref))

    def body_accumulate(a_ref, b_ref, *rest):
        (*r_refs, o_ref, acc), c = rest, pl.program_id(2)

        @pl.when(c == 0)
        def _():
            acc[...] = jnp.zeros_like(acc)

        acc[...] += product(a_ref, b_ref)

        @pl.when(c == n_c - 1)
        def _():
            finish(o_ref, r_refs, acc[...])

    out_spec = pl.BlockSpec((ti, tj), lambda i, j, c: (i, j))
    addends = [] if addend is None else [addend]
    return pl.pallas_call(
        body_single if n_c == 1 else body_accumulate, name=name, grid=(I // ti, J // tj, n_c),
        in_specs=[a_spec, b_spec] + [out_spec] * len(addends),
        out_specs=out_spec,
        out_shape=jax.ShapeDtypeStruct((I, J), out_dtype),
        scratch_shapes=[] if n_c == 1 else [pltpu.VMEM((ti, tj), F32)],
        compiler_params=pltpu.CompilerParams(dimension_semantics=("parallel", "parallel", "arbitrary"),
                                             vmem_limit_bytes=VMEM_LIMIT),
    )(a, b, *addends)


def _mm_add(residual, a, b, name):
    @jax.custom_vjp
    def run(residual, a, b):
        return _mm_call(a, b, "nn", "mm_" + name, F32, addend=residual)

    def fwd(residual, a, b):
        return run(residual, a, b), (a, b)

    def bwd(res, g):
        a, b = res
        return (g, _mm_call(g, b, "nt", "mm_" + name + "_da", a.dtype),
                _mm_call(a, g, "tn", "mm_" + name + "_db", b.dtype))

    run.defvjp(fwd, bwd)
    return run(residual, a, b)


def _mm(a, b, name):
    @jax.custom_vjp
    def run(a, b):
        return _mm_call(a, b, "nn", "mm_" + name, F32)

    def fwd(a, b):
        return run(a, b), (a, b)

    def bwd(res, g):
        a, b = res
        return (_mm_call(g, b, "nt", "mm_" + name + "_da", a.dtype),
                _mm_call(a, g, "tn", "mm_" + name + "_db", b.dtype))

    run.defvjp(fwd, bwd)
    return run(a, b)


def _mm_t(a, wt, name):
    @jax.custom_vjp
    def run(a, wt):
        return _mm_call(a, wt, "nt", "mmt_" + name, F32)

    def fwd(a, wt):
        return run(a, wt), (a, wt)

    def bwd(res, g):
        a, wt = res
        return (_mm_call(g, wt, "nn", "mmt_" + name + "_da", a.dtype),
                _mm_call(g, a, "tn", "mmt_" + name + "_dw", wt.dtype))

    run.defvjp(fwd, bwd)
    return run(a, wt)


def _stage_layout(rows, tm, cols):
    G, T, C = rows[0].shape
    if cols is None:
        return ((G, T // tm), lambda c: c, lambda c: pl.BlockSpec((1, tm, c), lambda g, t: (g, t, 0)),
                lambda c: pl.BlockSpec((1, 1, c), lambda g, t: (g, 0, 0)), lambda c: (G, T, c))
    return ((C // cols, T // tm), lambda c: cols, lambda c: pl.BlockSpec((1, tm, cols), lambda g, t: (0, t, g)),
            lambda c: pl.BlockSpec((1, 1, cols), lambda g, t: (0, 0, g)), lambda c: (1, T, C))


def _stage_fwd_call(name, fn, rows, params, tm, cols):
    grid, width, row_spec, par_spec, out_dims = _stage_layout(rows, tm, cols)
    nr, npar = len(rows), len(params)
    out_avals = jax.eval_shape(
        lambda *a: tuple(fn(*a)),
        *[jax.ShapeDtypeStruct((tm, width(r.shape[2])), r.dtype) for r in rows],
        *[jax.ShapeDtypeStruct((1, width(p.shape[2])), p.dtype) for p in params])

    def body(*refs):
        vals = [r[0] for r in refs[:nr + npar]]
        for o_ref, o in zip(refs[nr + npar:], fn(*vals)):
            o_ref[0] = o

    return pl.pallas_call(
        body, name=name, grid=grid,
        in_specs=[row_spec(r.shape[2]) for r in rows] + [par_spec(p.shape[2]) for p in params],
        out_specs=[row_spec(o.shape[1]) for o in out_avals],
        out_shape=[jax.ShapeDtypeStruct(out_dims(o.shape[1]), o.dtype) for o in out_avals],
        compiler_params=pltpu.CompilerParams(dimension_semantics=("parallel", "parallel"),
                                             vmem_limit_bytes=VMEM_LIMIT),
    )(*rows, *params)


def _stage_bwd_call(name, fn, rows, params, cts, tm, cols):
    grid, _, row_spec, par_spec, _ = _stage_layout(rows, tm, cols)
    nr, npar, nout = len(rows), len(params), len(cts)

    def body(*refs):
        vals = [r[0] for r in refs[:nr + npar]]
        ct_vals = tuple(r[0] for r in refs[nr + npar:nr + npar + nout])
        d_refs = refs[nr + npar + nout:]
        _, vjp_fn = jax.vjp(lambda *a: tuple(fn(*a)), *vals)
        grads = vjp_fn(ct_vals)
        for i in range(nr):
            d_refs[i][0] = grads[i]

        if npar:
            @pl.when(pl.program_id(1) == 0)
            def _():
                for j in range(npar):
                    d_refs[nr + j][...] = jnp.zeros_like(d_refs[nr + j])

        for j in range(npar):
            d_refs[nr + j][0] += grads[nr + j]

    outs = pl.pallas_call(
        body, name=name + "_bwd", grid=grid,
        in_specs=([row_spec(r.shape[2]) for r in rows] + [par_spec(p.shape[2]) for p in params]
                  + [row_spec(c.shape[2]) for c in cts]),
        out_specs=[row_spec(r.shape[2]) for r in rows] + [par_spec(p.shape[2]) for p in params],
        out_shape=([jax.ShapeDtypeStruct(r.shape, r.dtype) for r in rows]
                   + [jax.ShapeDtypeStruct(p.shape, p.dtype) for p in params]),
        compiler_params=pltpu.CompilerParams(dimension_semantics=("parallel", "arbitrary"),
                                             vmem_limit_bytes=VMEM_LIMIT),
    )(*rows, *params, *cts)
    return tuple(outs[:nr]), tuple(outs[nr:])


def _stage(name, fn, rows, params, tm, cols=None):
    @jax.custom_vjp
    def run(rows, params):
        return tuple(_stage_fwd_call(name, fn, rows, params, tm, cols))

    def fwd(rows, params):
        return run(rows, params), (rows, params)

    def bwd(res, cts):
        rows, params = res
        return _stage_bwd_call(name, fn, rows, params, tuple(cts), tm, cols)

    run.defvjp(fwd, bwd)
    return run(tuple(rows), tuple(params))


def _sigmoid(x):
    return 0.5 * (jnp.tanh(0.5 * x) + 1.0)


def _softplus(x):
    return jnp.maximum(x, 0.0) + jnp.log(1.0 + jnp.exp(-jnp.abs(x)))


def _rms(x, g, eps=RMS_EPS):
    return x * lax.rsqrt(jnp.mean(x * x, axis=-1, keepdims=True) + eps) * g


def _head_sum_pieces(x):
    n = x.shape[-1]
    same_head = (lax.broadcasted_iota(jnp.int32, (n, n), 0) // HEAD_DIM
                 == lax.broadcasted_iota(jnp.int32, (n, n), 1) // HEAD_DIM).astype(BF16)
    hi = x.astype(BF16)
    lo = (x - hi.astype(F32)).astype(BF16)
    dims = (((1,), (0,)), ((), ()))
    return (lax.dot_general(hi, same_head, dims, preferred_element_type=F32)
            + lax.dot_general(lo, same_head, dims, preferred_element_type=F32))


@jax.custom_vjp
def _head_sum(x):
    return _head_sum_pieces(x)


def _head_sum_fwd(x):
    return _head_sum_pieces(x), None


def _head_sum_bwd(_, g):
    return (_head_sum_pieces(g),)


_head_sum.defvjp(_head_sum_fwd, _head_sum_bwd)


def _head_mean(x):
    return _head_sum(x) * (1.0 / HEAD_DIM)


def _fn_rmsnorm(x, g):
    return (_rms(x, g).astype(BF16),)


def _fn_swiglu(gate, up):
    return ((gate * _sigmoid(gate) * up).astype(BF16),)


def _make_fn_lora_mix(p1, p2):
    def fn(u, u_prev, mu):
        um = u + (u_prev - u) * mu
        return (jnp.tanh(um[:, :p1]).astype(BF16), um[:, p1:p1 + p2].astype(BF16),
                _sigmoid(um[:, p1 + p2:]).astype(BF16))
    return fn


def _fn_rwkv_prep(ru, ru_p, ku, ku_p, vu, vu_p, w_lin, a_lin, mu_r, mu_k, mu_v, w0, a0, k_k, k_a):
    r = ru + (ru_p - ru) * mu_r
    k = ku + (ku_p - ku) * mu_k
    v = vu + (vu_p - vu) * mu_v
    w_log = -_softplus(-(w0 + w_lin)) - 0.5
    lw = -jnp.exp(w_log)
    a = _sigmoid(a0 + a_lin)
    kk = k * k_k
    kk = kk / jnp.maximum(jnp.sqrt(_head_sum(kk * kk)), 1e-12)
    k_mod = k * (1.0 + (a - 1.0) * k_a)
    return r, lw, k_mod, v, kk, kk * a


def _fn_rwkv_post(y, r, k_mod, v, g, lnx_g, lnx_b, r_k):
    yc = y - _head_mean(y)
    yn = yc * lax.rsqrt(_head_mean(yc * yc) + GN_EPS) * lnx_g + lnx_b
    bonus = _head_sum(r * k_mod * r_k) * v
    return (((yn + bonus) * g).astype(BF16),)


def _head_rms(x, g):
    return x * lax.rsqrt(_head_mean(x * x) + RMS_EPS) * g


def _fn_fox_prep(q, k, qg, kg):
    return _head_rms(q, qg), _head_rms(k, kg)


def _fn_log_forget(f_raw, b):
    x = f_raw + b
    return (jnp.minimum(x, 0.0) - jnp.log(1.0 + jnp.exp(-jnp.abs(x))),)


def _fn_final(z, e_raw, h2, target, gate_b, ple_g):
    gate = _sigmoid(z + gate_b)
    out = h2 + gate * _rms(e_raw, ple_g)
    err = out - target
    return (0.5 * jnp.mean(err * err, axis=-1, keepdims=True),)


def _dot_bf16(a, b, ca, cb):
    return lax.dot_general(a.astype(BF16), b.astype(BF16), (((ca,), (cb,)), ((0,), (0,))),
                           preferred_element_type=F32)


@functools.partial(jax.custom_vjp, nondiff_argnums=(2, 3))
def _dot(a, b, ca, cb):
    return _dot_bf16(a, b, ca, cb)


def _dot_fwd(a, b, ca, cb):
    return _dot_bf16(a, b, ca, cb), (a, b)


def _dot_bwd(ca, cb, res, g):
    a, b = res
    ia, jb = 3 - ca, 3 - cb
    da = _dot_bf16(g, b, 2, jb) if ca == 2 else _dot_bf16(b, g, jb, 2)
    db = _dot_bf16(a, g, ia, 1) if cb == 1 else _dot_bf16(g, a, 1, ia)
    return da, db


_dot.defvjp(_dot_fwd, _dot_bwd)


def _scan_chunk(S0, r, lw, k, v, kk, b):
    B, L, _ = r.shape
    row = lax.broadcasted_iota(jnp.int32, (B, L, L), 1)
    col = lax.broadcasted_iota(jnp.int32, (B, L, L), 2)
    incl = col <= row
    strict = col < row
    cum = lax.dot_general(incl.astype(F32), lw, (((2,), (1,)), ((0,), (0,))), precision=HIGHEST,
                          preferred_element_type=F32)
    g_in, g_ex, g_inv = jnp.exp(cum), jnp.exp(cum - lw), jnp.exp(-cum)
    kkg, kd, bd, rg = kk * g_ex, k * g_inv, b * g_inv, r * g_in
    a_k = jnp.where(strict, _dot(kkg, kd, 2, 2), 0.0)
    a_b = jnp.where(strict, _dot(kkg, bd, 2, 2), 0.0)
    pw = -a_b
    inv = (row == col).astype(F32) + pw
    for _ in range(int(math.log2(L)) - 1):
        pw = _dot(pw, pw, 2, 1)
        inv = inv + _dot(inv, pw, 2, 1)
    sa = -_dot(inv, _dot(kkg, S0, 2, 2) + _dot(a_k, v, 2, 1), 2, 1)
    r_k = jnp.where(incl, _dot(rg, kd, 2, 2), 0.0)
    r_b = jnp.where(incl, _dot(rg, bd, 2, 2), 0.0)
    y = _dot(rg, S0, 2, 2) + _dot(r_k, v, 2, 1) + _dot(r_b, sa, 2, 1)
    g_end = jnp.exp(jnp.sum(lw, axis=1, keepdims=True))
    S1 = S0 * g_end + _dot(v, kd * g_end, 1, 1) + _dot(sa, bd * g_end, 1, 1)
    return y, S1


def _scan_heads_per_step(H):
    return next(hb for hb in (SCAN_HEADS_PER_STEP, 2, 1) if H % hb == 0)


def _scan_fwd_call(r, lw, k, v, kk, b):
    H, T, N = r.shape
    L = SCAN_CHUNK
    n_chunks = T // L
    hb = _scan_heads_per_step(H)

    def body(r_ref, lw_ref, k_ref, v_ref, kk_ref, b_ref, y_ref, s0_ref, state):
        @pl.when(pl.program_id(1) == 0)
        def _():
            state[...] = jnp.zeros_like(state)

        S0 = state[...]
        s0_ref[:, 0] = S0
        y, S1 = _scan_chunk(S0, r_ref[...], lw_ref[...], k_ref[...], v_ref[...], kk_ref[...], b_ref[...])
        y_ref[...] = y
        state[...] = S1

    blk = pl.BlockSpec((hb, L, N), lambda h, c: (h, c, 0))
    return pl.pallas_call(
        body, name="rwkv_scan_fwd", grid=(H // hb, n_chunks),
        in_specs=[blk] * 6,
        out_specs=[blk, pl.BlockSpec((hb, 1, N, N), lambda h, c: (h, c, 0, 0))],
        out_shape=[jax.ShapeDtypeStruct((H, T, N), F32), jax.ShapeDtypeStruct((H, n_chunks, N, N), F32)],
        scratch_shapes=[pltpu.VMEM((hb, N, N), F32)],
        compiler_params=pltpu.CompilerParams(dimension_semantics=("parallel", "arbitrary")),
    )(r, lw, k, v, kk, b)


def _scan_bwd_call(r, lw, k, v, kk, b, s0s, dy):
    H, T, N = r.shape
    L = SCAN_CHUNK
    n_chunks = T // L
    hb = _scan_heads_per_step(H)

    def body(r_ref, lw_ref, k_ref, v_ref, kk_ref, b_ref, s0_ref, dy_ref, dr, dlw, dk, dv, dkk, db, d_state):
        @pl.when(pl.program_id(1) == 0)
        def _():
            d_state[...] = jnp.zeros_like(d_state)

        _, vjp_fn = jax.vjp(_scan_chunk, s0_ref[:, 0], r_ref[...], lw_ref[...], k_ref[...], v_ref[...], kk_ref[...],
                            b_ref[...])
        grads = vjp_fn((dy_ref[...], d_state[...]))
        d_state[...] = grads[0]
        for o_ref, g in zip((dr, dlw, dk, dv, dkk, db), grads[1:]):
            o_ref[...] = g

    blk = pl.BlockSpec((hb, L, N), lambda h, c: (h, n_chunks - 1 - c, 0))
    return pl.pallas_call(
        body, name="rwkv_scan_bwd", grid=(H // hb, n_chunks),
        in_specs=[blk] * 6 + [pl.BlockSpec((hb, 1, N, N), lambda h, c: (h, n_chunks - 1 - c, 0, 0)), blk],
        out_specs=[blk] * 6,
        out_shape=[jax.ShapeDtypeStruct((H, T, N), F32)] * 6,
        scratch_shapes=[pltpu.VMEM((hb, N, N), F32)],
        compiler_params=pltpu.CompilerParams(dimension_semantics=("parallel", "arbitrary")),
    )(r, lw, k, v, kk, b, s0s, dy)


@jax.custom_vjp
def _rwkv_scan(r, lw, k, v, kk, b):
    return _scan_fwd_call(r, lw, k, v, kk, b)[0]


def _rwkv_scan_fwd(r, lw, k, v, kk, b):
    y, s0s = _scan_fwd_call(r, lw, k, v, kk, b)
    return y, (r, lw, k, v, kk, b, s0s)


def _rwkv_scan_bwd(res, dy):
    return tuple(_scan_bwd_call(*res, dy))


_rwkv_scan.defvjp(_rwkv_scan_fwd, _rwkv_scan_bwd)


def _nt(a, b):
    return lax.dot_general(a.astype(BF16), b.astype(BF16), (((1,), (1,)), ((), ())), preferred_element_type=F32)


def _nn(a, b):
    return lax.dot_general(a.astype(BF16), b.astype(BF16), (((1,), (0,)), ((), ())), preferred_element_type=F32)


def _tn(a, b):
    return lax.dot_general(a.astype(BF16), b.astype(BF16), (((0,), (0,)), ((), ())), preferred_element_type=F32)


def _attn_scores(qs, k_ref, cr_ref, row_bias, j, kb, q0, masked):
    ks = pl.multiple_of(j * kb, kb)
    kj = k_ref[0, pl.ds(ks, kb), :]
    s = _nt(qs, kj) + row_bias - cr_ref[0, j]
    if masked:
        qi = q0 + lax.broadcasted_iota(jnp.int32, s.shape, 0)
        ki = ks + lax.broadcasted_iota(jnp.int32, s.shape, 1)
        s = jnp.where(ki <= qi, s, -jnp.inf)
    return s, kj, ks


def _attn_specs(T, N, bq, kb):
    q_spec = pl.BlockSpec((1, bq, N), lambda h, i: (h, i, 0))
    kv_spec = pl.BlockSpec((1, T, N), lambda h, i: (h, 0, 0))
    col_spec = pl.BlockSpec((1, bq, 1), lambda h, i: (h, i, 0))
    row_spec = pl.BlockSpec((1, T // kb, 1, kb), lambda h, i: (h, 0, 0, 0))
    return q_spec, kv_spec, col_spec, row_spec


def _grid_marks(H, n_q):
    h, i = pl.program_id(0), pl.program_id(1)
    return (h == 0) & (i == 0), (h == (3 * H) // 4) & (i == 0), (h == H - 1) & (i == n_q - 1)


def _attn_fwd_call(q, k, v, c_col, c_rows, gather_xs):
    H, T, N = q.shape
    bq, kb = min(ATTN_BLOCK_Q, T), c_rows.shape[3]
    q_spec, kv_spec, col_spec, row_spec = _attn_specs(T, N, bq, kb)
    n = len(gather_xs)

    def body(q_ref, k_ref, v_ref, cc_ref, cr_ref, *rest):
        x_refs, (o_ref, o32_ref, lse_ref), out_refs, sems = rest[:n], rest[n:n + 3], rest[n + 3:2 * n + 3], rest[2 * n + 3:]
        first, middle, last = _grid_marks(H, T // bq)
        start, relay, finish = _gather_phases(x_refs, out_refs, *sems)
        pl.when(first)(start)
        pl.when(middle)(relay)
        q0 = pl.program_id(1) * bq
        qs = (q_ref[0] * (HEAD_DIM ** -0.5)).astype(BF16)
        cc = cc_ref[0]

        def step(j, carry, masked):
            m, l, acc = carry
            s, _, ks = _attn_scores(qs, k_ref, cr_ref, cc, j, kb, q0, masked)
            m_new = jnp.maximum(m, jnp.max(s, axis=-1, keepdims=True))
            alpha = jnp.exp(m - m_new)
            p = jnp.exp(s - m_new)
            l = alpha * l + jnp.sum(p, axis=-1, keepdims=True)
            p_hi = p.astype(BF16)
            p_lo = p - p_hi.astype(F32)
            vj = v_ref[0, pl.ds(ks, kb), :]
            acc = alpha * acc + (_nn(p_hi, vj) + _nn(p_lo, vj))
            return m_new, l, acc

        n_full = q0 // kb
        init = (jnp.full((bq, 1), -jnp.inf, F32), jnp.zeros((bq, 1), F32), jnp.zeros((bq, N), F32))
        carry = lax.fori_loop(0, n_full, functools.partial(step, masked=False), init)
        m, l, acc = step(n_full, carry, masked=True)
        o = acc / l
        o_ref[0] = o.astype(o_ref.dtype)
        o32_ref[0] = o
        lse_ref[0] = m + jnp.log(l)
        pl.when(last)(finish)

    any_spec = pl.BlockSpec(memory_space=pl.ANY)
    return pl.pallas_call(
        body, name="fox_attn_fwd", grid=(H, T // bq),
        in_specs=[q_spec, kv_spec, kv_spec, col_spec, row_spec] + [any_spec] * n,
        out_specs=[q_spec, q_spec, col_spec] + [any_spec] * n,
        out_shape=[jax.ShapeDtypeStruct((H, T, N), BF16),
                   jax.ShapeDtypeStruct((H, T, N), F32),
                   jax.ShapeDtypeStruct((H, T, 1), F32)] + _gather_out_shapes(gather_xs),
        scratch_shapes=_comm_semaphores(n),
        compiler_params=pltpu.CompilerParams(dimension_semantics=("arbitrary", "arbitrary"),
                                             vmem_limit_bytes=VMEM_LIMIT),
    )(q, k, v, c_col, c_rows, *gather_xs)


def _attn_bwd_call(q, k, v, c_col, c_rows, o, lse, do, scatter_parts):
    H, T, N = q.shape
    bq, kb = min(ATTN_BLOCK_Q, T), c_rows.shape[3]
    q_spec, kv_spec, col_spec, row_spec = _attn_specs(T, N, bq, kb)
    n = len(scatter_parts)

    def body(q_ref, k_ref, v_ref, cc_ref, cr_ref, o_ref, lse_ref, do_ref, *rest):
        a_refs, (dq_ref, dk_ref, dv_ref, dcr_ref), b_refs, sems = rest[:n], rest[n:n + 4], rest[n + 4:2 * n + 4], rest[2 * n + 4:]
        first, _, last = _grid_marks(H, T // bq)
        start, finish = _scatter_phases(a_refs, b_refs, *sems)
        pl.when(first)(start)
        i = pl.program_id(1)
        q0 = i * bq

        @pl.when(i == 0)
        def _():
            dk_ref[...] = jnp.zeros_like(dk_ref)
            dv_ref[...] = jnp.zeros_like(dv_ref)
            dcr_ref[...] = jnp.zeros_like(dcr_ref)

        qs = (q_ref[0] * (HEAD_DIM ** -0.5)).astype(BF16)
        do = do_ref[0]
        delta = jnp.sum(do.astype(F32) * o_ref[0], axis=-1, keepdims=True)
        row_bias = cc_ref[0] - lse_ref[0]

        def step(j, dq, masked):
            s, kj, ks = _attn_scores(qs, k_ref, cr_ref, row_bias, j, kb, q0, masked)
            p = jnp.exp(s)
            ds = p * (_nt(do, v_ref[0, pl.ds(ks, kb), :]) - delta)
            ds_b = ds.astype(BF16)
            dk_ref[0, pl.ds(ks, kb), :] += _tn(ds_b, qs)
            dv_ref[0, pl.ds(ks, kb), :] += _tn(p, do)
            dcr_ref[0, j] -= jnp.sum(ds, axis=0, keepdims=True)
            return dq + _nn(ds_b, kj)

        n_full = q0 // kb
        dq = lax.fori_loop(0, n_full, functools.partial(step, masked=False), jnp.zeros((bq, N), F32))
        dq_ref[0] = step(n_full, dq, masked=True) * (HEAD_DIM ** -0.5)
        pl.when(last)(finish)

    any_spec = pl.BlockSpec(memory_space=pl.ANY)
    return pl.pallas_call(
        body, name="fox_attn_bwd", grid=(H, T // bq),
        in_specs=[q_spec, kv_spec, kv_spec, col_spec, row_spec, q_spec, col_spec, q_spec] + [any_spec] * n,
        out_specs=[q_spec, kv_spec, kv_spec, row_spec] + [any_spec] * n,
        out_shape=[jax.ShapeDtypeStruct((H, T, N), F32)] * 3 + [jax.ShapeDtypeStruct(c_rows.shape, F32)]
        + [jax.ShapeDtypeStruct(a.shape, a.dtype) for a in scatter_parts],
        scratch_shapes=_comm_semaphores(n),
        compiler_params=pltpu.CompilerParams(dimension_semantics=("arbitrary", "arbitrary"),
                                             vmem_limit_bytes=VMEM_LIMIT),
    )(q, k, v, c_col, c_rows, o, lse, do, *scatter_parts)


@jax.custom_vjp
def _fox_attn(q, k, v, c, late_blocks, carrier):
    return _fox_attn_fwd(q, k, v, c, late_blocks, carrier)[0]


def _attn_bias_views(c):
    H, T = c.shape
    kb = min(ATTN_BLOCK_K, T)
    return c[:, :, None], c.reshape(H, T // kb, 1, kb)


def _fox_attn_fwd(q, k, v, c, late_blocks, carrier):
    o, o32, lse, *gathered = _attn_fwd_call(q, k, v, *_attn_bias_views(c), late_blocks)
    return (o, tuple(gathered)), (q, k, v, c, o32, lse)


def _fox_attn_bwd(res, cts):
    q, k, v, c, o32, lse = res
    do, d_gathered = cts
    dq, dk, dv, dc_rows, *parts = _attn_bwd_call(q, k, v, *_attn_bias_views(c), o32, lse, do, d_gathered)
    no_grad = tuple(jnp.zeros(a.shape[1:], a.dtype) for a in parts)
    return dq, dk, dv, dc_rows.reshape(c.shape), no_grad, tuple(parts)


_fox_attn.defvjp(_fox_attn_fwd, _fox_attn_bwd)


N_PEERS = N_DEV - 1


def _all_gather(xs, name):
    n = len(xs)

    def body(*refs):
        start, relay, finish = _gather_phases(refs[:n], refs[n:2 * n], *refs[2 * n:])
        start()
        relay()
        finish()

    any_spec = pl.BlockSpec(memory_space=pl.ANY)
    return pl.pallas_call(
        body, name=name,
        out_shape=_gather_out_shapes(xs),
        in_specs=[any_spec] * n, out_specs=[any_spec] * n,
        scratch_shapes=_comm_semaphores(n),
    )(*xs)


def _gather_out_shapes(xs):
    return [jax.ShapeDtypeStruct((N_DEV,) + x.shape, x.dtype) for x in xs]


def _comm_semaphores(n):
    return [pltpu.SemaphoreType.DMA((N_PEERS * n,)), pltpu.SemaphoreType.DMA((N_PEERS * n,)),
            pltpu.SemaphoreType.DMA((n,))]


def _gather_phases(x_refs, out_refs, send_sems, recv_sems, local_sems):
    n = len(x_refs)
    x_, y_, c_ = lax.axis_index("x"), lax.axis_index("y"), lax.axis_index("c")
    me, sibling = (x_, y_, c_), (x_, y_, 1 - c_)
    chips = [(1 - x_, y_), (x_, 1 - y_), (1 - x_, 1 - y_)]

    def slot(t, px, py, pc):
        return out_refs[t].at[4 * px + 2 * py + pc]

    def copy(t, k, block, to, src=None):
        return pltpu.make_async_remote_copy(
            src_ref=slot(t, *block) if src is None else src, dst_ref=slot(t, *block),
            send_sem=send_sems.at[k * n + t], recv_sem=recv_sems.at[k * n + t],
            device_id=to, device_id_type=pl.DeviceIdType.MESH)

    def mine():
        return [pltpu.make_async_copy(x_refs[t], slot(t, *me), local_sems.at[t]) for t in range(n)]

    def first():
        return ([copy(t, 0, me, sibling, src=x_refs[t]) for t in range(n)]
                + [copy(t, 1 + j, me, (*chip, c_), src=x_refs[t]) for j, chip in enumerate(chips) for t in range(n)])

    def passed():
        return [copy(t, 4 + j, (*chip, c_), sibling) for j, chip in enumerate(chips) for t in range(n)]

    def start():
        for cp in mine() + first():
            cp.start()

    def relay():
        for j, chip in enumerate(chips):
            for t in range(n):
                copy(t, 1 + j, (*chip, c_), me).wait_recv()
                copy(t, 4 + j, (*chip, c_), sibling).start()

    def finish():
        for t in range(n):
            copy(t, 0, sibling, me).wait_recv()
        for j, chip in enumerate(chips):
            for t in range(n):
                copy(t, 4 + j, (*chip, 1 - c_), me).wait_recv()
        for cp in first() + passed():
            cp.wait_send()
        for cp in mine():
            cp.wait()

    return start, relay, finish


def _all_to_all(parts, name):
    n = len(parts)

    def body(*refs):
        start, finish = _scatter_phases(refs[:n], refs[n:2 * n], *refs[2 * n:])
        start()
        finish()

    any_spec = pl.BlockSpec(memory_space=pl.ANY)
    return pl.pallas_call(
        body, name=name,
        out_shape=[jax.ShapeDtypeStruct(a.shape, a.dtype) for a in parts],
        in_specs=[any_spec] * n, out_specs=[any_spec] * n,
        scratch_shapes=_comm_semaphores(n),
    )(*parts)


def _exchange_grads(parts, small, name):
    n = len(parts)

    def body(*refs):
        a_refs, x_ref, b_refs, out_ref = refs[:n], refs[n], refs[n + 1:2 * n + 1], refs[2 * n + 1]
        sems = refs[2 * n + 2:]
        scatter_start, scatter_finish = _scatter_phases(a_refs, b_refs, *sems[:3])
        gather_start, gather_relay, gather_finish = _gather_phases([x_ref], [out_ref], *sems[3:])
        scatter_start()
        gather_start()
        gather_relay()
        gather_finish()
        scatter_finish()

    any_spec = pl.BlockSpec(memory_space=pl.ANY)
    outs = pl.pallas_call(
        body, name=name,
        out_shape=[jax.ShapeDtypeStruct(a.shape, a.dtype) for a in parts] + _gather_out_shapes([small]),
        in_specs=[any_spec] * (n + 1), out_specs=[any_spec] * (n + 1),
        scratch_shapes=_comm_semaphores(n) + _comm_semaphores(1),
    )(*parts, small)
    return outs[:n], outs[n]


def _scatter_phases(a_refs, b_refs, send_sems, recv_sems, local_sems):
    n = len(a_refs)
    x_, y_, c_ = lax.axis_index("x"), lax.axis_index("y"), lax.axis_index("c")
    me_idx = 4 * x_ + 2 * y_ + c_

    def copies():
        out = [pltpu.make_async_copy(a_refs[t].at[me_idx], b_refs[t].at[me_idx], local_sems.at[t]) for t in range(n)]
        for rel in range(1, N_DEV):
            px = 1 - x_ if rel & 4 else x_
            py = 1 - y_ if rel & 2 else y_
            pc = 1 - c_ if rel & 1 else c_
            for t in range(n):
                out.append(pltpu.make_async_remote_copy(
                    src_ref=a_refs[t].at[4 * px + 2 * py + pc], dst_ref=b_refs[t].at[me_idx],
                    send_sem=send_sems.at[(rel - 1) * n + t], recv_sem=recv_sems.at[(rel - 1) * n + t],
                    device_id=(px, py, pc), device_id_type=pl.DeviceIdType.MESH))
        return out

    def start():
        for cp in copies():
            cp.start()

    def finish():
        for cp in copies():
            cp.wait()

    return start, finish


def _reduce_adamw(parts, w, m, v, name):
    R, C = w.shape
    tr = max(t for t in (256, 128, PACK_ROW_QUANTUM) if R % t == 0)

    def body(p_ref, w_ref, m_ref, v_ref, g_out, d_out, m_out, v_out):
        g = p_ref[0]
        for i in range(1, N_DEV):
            g = g + p_ref[i]
        m_new = ADAM_B1 * m_ref[...] + (1.0 - ADAM_B1) * g
        v_new = ADAM_B2 * v_ref[...] + (1.0 - ADAM_B2) * (g * g)
        m_hat = m_new / (1.0 - ADAM_B1 ** ADAM_STEP)
        v_hat = v_new / (1.0 - ADAM_B2 ** ADAM_STEP)
        g_out[...] = g
        d_out[...] = -ADAM_LR * (m_hat / (jnp.sqrt(v_hat) + ADAM_EPS) + ADAM_WD * w_ref[...])
        m_out[...] = m_new
        v_out[...] = v_new

    spec = pl.BlockSpec((tr, C), lambda i: (i, 0))
    return pl.pallas_call(
        body, name=name, grid=(R // tr,),
        in_specs=[pl.BlockSpec((N_DEV, tr, C), lambda i: (0, i, 0)), spec, spec, spec],
        out_specs=[spec] * 4,
        out_shape=[jax.ShapeDtypeStruct((R, C), F32)] * 4,
        compiler_params=pltpu.CompilerParams(dimension_semantics=("parallel",), vmem_limit_bytes=VMEM_LIMIT),
    )(parts, w, m, v)


def _pack(arrays, dtype):
    flat = jnp.concatenate([a.reshape(-1).astype(dtype) for a in arrays])
    rows = _round_up(-(-flat.shape[0] // PACK_COLS), PACK_ROW_QUANTUM)
    flat = jnp.pad(flat, (0, rows * PACK_COLS - flat.shape[0]))
    return flat.reshape(rows, PACK_COLS)


def _unpack(packed, shapes):
    lead = packed.shape[:-2]
    flat = packed.reshape(lead + (-1,))
    out, off = [], 0
    for s in shapes:
        n = math.prod(s)
        out.append(flat[..., off:off + n].reshape(lead + tuple(s)))
        off += n
    return out


def _travel_layout(name, block):
    return block.T if SHARDED[name] else block


REDUCE_BLOCK_BYTES = 4 * 1024 * 1024


def _reduce_parts(parts, name):
    _, R, C = parts.shape
    per_col = N_DEV * R * parts.dtype.itemsize
    tc = next((t for t in range(C - C % LANES, 0, -LANES) if C % t == 0 and t * per_col <= REDUCE_BLOCK_BYTES), C)

    def body(p_ref, o_ref):
        g = p_ref[0].astype(F32)
        for i in range(1, N_DEV):
            g = g + p_ref[i].astype(F32)
        o_ref[...] = g

    return pl.pallas_call(
        body, name=name, grid=(C // tc,),
        in_specs=[pl.BlockSpec((N_DEV, R, tc), lambda j: (0, 0, j))],
        out_specs=pl.BlockSpec((R, tc), lambda j: (0, j)),
        out_shape=jax.ShapeDtypeStruct((R, C), F32),
        compiler_params=pltpu.CompilerParams(dimension_semantics=("parallel",), vmem_limit_bytes=VMEM_LIMIT),
    )(parts)


def _adamw(g, w, m, v, name):
    R, C = w.shape
    tr = next((t for t in (512, 256, 128, 64, 32, 16, 8) if R % t == 0 and t * C * 4 <= 2 * 1024 * 1024), R)

    def body(g_ref, w_ref, m_ref, v_ref, d_out, m_out, v_out):
        g_ = g_ref[...]
        m_new = ADAM_B1 * m_ref[...] + (1.0 - ADAM_B1) * g_
        v_new = ADAM_B2 * v_ref[...] + (1.0 - ADAM_B2) * (g_ * g_)
        m_hat = m_new / (1.0 - ADAM_B1 ** ADAM_STEP)
        v_hat = v_new / (1.0 - ADAM_B2 ** ADAM_STEP)
        d_out[...] = -ADAM_LR * (m_hat / (jnp.sqrt(v_hat) + ADAM_EPS) + ADAM_WD * w_ref[...])
        m_out[...] = m_new
        v_out[...] = v_new

    spec = pl.BlockSpec((tr, C), lambda i: (i, 0))
    return pl.pallas_call(
        body, name=name, grid=(R // tr,),
        in_specs=[spec] * 4, out_specs=[spec] * 3,
        out_shape=[jax.ShapeDtypeStruct((R, C), F32)] * 3,
        compiler_params=pltpu.CompilerParams(dimension_semantics=("parallel",), vmem_limit_bytes=VMEM_LIMIT),
    )(g, w, m, v)


def _to_heads(u):
    return jnp.transpose(u.reshape(u.shape[0], -1, HEAD_DIM), (1, 0, 2))


def _split_cols(u, n):
    width = u.shape[1] // n

    @jax.custom_vjp
    def run(u):
        return tuple(u[:, i * width:(i + 1) * width] for i in range(n))

    def fwd(u):
        return run(u), None

    def bwd(_, cts):
        return (jnp.concatenate(cts, axis=1),)

    run.defvjp(fwd, bwd)
    return run(u)


def _from_heads(uh):
    H, T, N = uh.shape
    return jnp.transpose(uh, (1, 0, 2)).reshape(T, H * N)


def _shift(uh):
    return jnp.pad(uh, ((0, 0), (1, 0), (0, 0)))[:, :-1]


def _pad_cols(a, width):
    return jnp.pad(a, ((0, 0), (0, width - a.shape[1])))


def _split_rows(w, sizes):
    offsets = [sum(sizes[:i]) for i in range(len(sizes))]

    @jax.custom_vjp
    def run(w):
        return tuple(w[o:o + s] for o, s in zip(offsets, sizes))

    def fwd(w):
        return run(w), None

    def bwd(_, cts):
        return (jnp.concatenate(cts, axis=0),)

    run.defvjp(fwd, bwd)
    return run(w)


def _pad_rows(a, height):
    return jnp.pad(a, ((0, height - a.shape[0]), (0, 0)))


def _vec(a):
    return a.reshape(1, 1, -1)


TM_WIDE = 128


def _mixing_half(W, small, x, late_blocks, carrier):
    T, D = x.shape
    vec = _vec
    tm_wide = TM_WIDE
    rw = small['w0'].shape[-1]
    fw = W['w_out'].shape[0] - rw
    heads_f = fw // HEAD_DIM
    dl, al, gl = W['w2'].shape[1], W['a2'].shape[1], W['g2'].shape[1]
    dl_p, al_p, gl_p = _round_up(dl, LANES), _round_up(al, LANES), _round_up(gl, LANES)
    f_p = _round_up(heads_f, LANES)
    rwkv_cols = 3 * rw + dl + al + gl
    tm_head = next((t for t in (1024, 512) if T % t == 0), T)

    o_w, o_a, o_g = 3 * rw, 3 * rw + dl, 3 * rw + dl + al
    w_rkv, w_xw, w_xa, w_xg, w_qkv, w_fg = _split_rows(W['w_in'], (3 * rw, dl, al, gl, 3 * fw, heads_f))
    w_lora = jnp.concatenate([_pad_rows(w_xw, dl_p), _pad_rows(w_xa, al_p), _pad_rows(w_xg, gl_p)], axis=0)
    w_f = _pad_rows(w_fg, f_p)
    mu = small['shift_mu'].reshape(1, -1)
    mu_lora = jnp.concatenate([_pad_cols(mu[:, o_w:o_a], dl_p), _pad_cols(mu[:, o_a:o_g], al_p),
                               _pad_cols(mu[:, o_g:rwkv_cols], gl_p)], axis=1)

    (xn,) = _stage("attn_norm", _fn_rmsnorm, [x[None]], [vec(small['attn_norm_g'])], tm_wide)
    xn = xn[0]
    u_rkv = _mm_t(xn, w_rkv, "in_rkv")
    u_lora = _mm_t(xn, w_lora, "in_lora")
    u_qkv = _mm_t(xn, w_qkv, "in_qkv")
    f_raw = _mm_t(xn, w_f, "in_f")

    u_lora3 = u_lora[None]
    xw_t, xa_m, xg_s = _stage("lora_mix", _make_fn_lora_mix(dl_p, al_p), [u_lora3, _shift(u_lora3)],
                              [vec(mu_lora)], tm_wide)
    w_lin = _mm_t(xw_t[0], _pad_cols(W['w2'], dl_p), "w2")
    a_lin = _mm_t(xa_m[0], _pad_cols(W['a2'], al_p), "a2")
    gate_r = _mm_t(xg_s[0], _pad_cols(W['g2'], gl_p), "g2")
    ru, ku, vu = (u[None] for u in _split_cols(u_rkv, 3))
    mu_r, mu_k, mu_v = (vec(mu[:, i * rw:(i + 1) * rw]) for i in range(3))
    prepped = _stage(
        "rwkv_prep", _fn_rwkv_prep,
        [ru, _shift(ru), ku, _shift(ku), vu, _shift(vu), w_lin[None], a_lin[None]],
        [mu_r, mu_k, mu_v, vec(small['w0']), vec(small['a0']), vec(small['k_k']), vec(small['k_a'])],
        tm_head, cols=HEAD_PAIR)
    r, lw, k_mod, v, kk, b = prepped
    y_scan = _rwkv_scan(*(_to_heads(t[0]) for t in prepped))
    (y_rwkv,) = _stage("rwkv_post", _fn_rwkv_post, [_from_heads(y_scan)[None], r, k_mod, v, gate_r[None]],
                       [vec(small['lnx_g']), vec(small['lnx_b']), vec(small['r_k'])], tm_head, cols=HEAD_PAIR)

    qu, kf, vf = _split_cols(u_qkv, 3)
    qg = vec(jnp.tile(small['q_norm_g'].reshape(-1), heads_f))
    kg = vec(jnp.tile(small['k_norm_g'].reshape(-1), heads_f))
    qn, kn = _stage("fox_prep", _fn_fox_prep, [qu[None], kf[None]], [qg, kg], tm_head, cols=HEAD_PAIR)
    fb = _pad_cols(small['fgate_b'].reshape(1, -1), f_p)
    (log_f,) = _stage("log_forget", _fn_log_forget, [f_raw[None]], [vec(fb)], tm_head)
    c = jnp.cumsum(log_f[0][:, :heads_f], axis=0).T
    y_fox, gathered_late = _fox_attn(_to_heads(qn[0]), _to_heads(kn[0]), _to_heads(vf), c, late_blocks, carrier)

    y_cat = jnp.concatenate([y_rwkv[0], _from_heads(y_fox)], axis=-1)
    return _mm_add(x, y_cat, W['w_out'], "out"), gathered_late


def _channel_half_loss(W, small, h1, p, target):
    vec = _vec
    tm_wide = TM_WIDE
    (hn,) = _stage("ffn_norm", _fn_rmsnorm, [h1[None]], [vec(small['ffn_norm_g'])], tm_wide)
    gate = _mm_t(hn[0], W['w_gate'], "gate")
    up = _mm_t(hn[0], W['w_up'], "up")
    (act,) = _stage("swiglu", _fn_swiglu, [gate[None], up[None]], [], tm_wide)
    h2 = _mm_add(h1, act[0], W['w_down'], "down")
    e_raw = _mm_t(p, W['ple_proj'], "ple_proj")
    (hg,) = _stage("ple_gate_norm", _fn_rmsnorm, [h2[None]], [vec(small['ple_gate_norm_g'])], tm_wide)
    z = _mm(hg[0], W['ple_gate_w'], "ple_gate")
    (loss_rows,) = _stage("final", _fn_final, [z[None], e_raw[None], h2[None], target[None]],
                          [vec(small['ple_gate_b']), vec(small['ple_norm_g'])], tm_wide)
    return jnp.sum(loss_rows)


def kernel(x, p, attn_norm_g, w_in, shift_mu, w0, w2, a0, a2, g2, k_k, k_a, r_k, lnx_g, lnx_b, q_norm_g, k_norm_g, fgate_b, w_out, ffn_norm_g, w_gate, w_up, w_down, ple_proj, ple_norm_g, ple_gate_norm_g, ple_gate_w, ple_gate_b, loss_target, m_attn_norm_g, m_w_in, m_shift_mu, m_w0, m_w2, m_a0, m_a2, m_g2, m_k_k, m_k_a, m_r_k, m_lnx_g, m_lnx_b, m_q_norm_g, m_k_norm_g, m_fgate_b, m_w_out, m_ffn_norm_g, m_w_gate, m_w_up, m_w_down, m_ple_proj, m_ple_norm_g, m_ple_gate_norm_g, m_ple_gate_w, m_ple_gate_b, v_attn_norm_g, v_w_in, v_shift_mu, v_w0, v_w2, v_a0, v_a2, v_g2, v_k_k, v_k_a, v_r_k, v_lnx_g, v_lnx_b, v_q_norm_g, v_k_norm_g, v_fgate_b, v_w_out, v_ffn_norm_g, v_w_gate, v_w_up, v_w_down, v_ple_proj, v_ple_norm_g, v_ple_gate_norm_g, v_ple_gate_w, v_ple_gate_b):
    weights = dict(zip(WEIGHT_NAMES, (attn_norm_g, w_in, shift_mu, w0, w2, a0, a2, g2, k_k, k_a, r_k, lnx_g, lnx_b,
                                      q_norm_g, k_norm_g, fgate_b, w_out, ffn_norm_g, w_gate, w_up, w_down, ple_proj,
                                      ple_norm_g, ple_gate_norm_g, ple_gate_w, ple_gate_b)))
    m_in = dict(zip(WEIGHT_NAMES, (m_attn_norm_g, m_w_in, m_shift_mu, m_w0, m_w2, m_a0, m_a2, m_g2, m_k_k, m_k_a, m_r_k,
                                   m_lnx_g, m_lnx_b, m_q_norm_g, m_k_norm_g, m_fgate_b, m_w_out, m_ffn_norm_g, m_w_gate,
                                   m_w_up, m_w_down, m_ple_proj, m_ple_norm_g, m_ple_gate_norm_g, m_ple_gate_w,
                                   m_ple_gate_b)))
    v_in = dict(zip(WEIGHT_NAMES, (v_attn_norm_g, v_w_in, v_shift_mu, v_w0, v_w2, v_a0, v_a2, v_g2, v_k_k, v_k_a, v_r_k,
                                   v_lnx_g, v_lnx_b, v_q_norm_g, v_k_norm_g, v_fgate_b, v_w_out, v_ffn_norm_g, v_w_gate,
                                   v_w_up, v_w_down, v_ple_proj, v_ple_norm_g, v_ple_gate_norm_g, v_ple_gate_w,
                                   v_ple_gate_b)))
    small_shapes = [weights[n].shape for n in SMALL_NAMES]
    small = {n: weights[n] for n in SMALL_NAMES}

    def whole(stacks, names):
        return {n: g.reshape(N_DEV * g.shape[1], g.shape[2]) for n, g in zip(names, stacks)}

    def stacked(tree, names, like):
        return tuple(tree[n].reshape(g.shape) for n, g in zip(names, like))

    travelling = {n: _travel_layout(n, weights[n][0]).astype(BF16) for n in SHARDED_NAMES}
    gathered_early = _all_gather([travelling[n] for n in EARLY_NAMES], "gather_weights")
    late_blocks = tuple(travelling[n] for n in LATE_NAMES)
    carrier = tuple(jnp.zeros((N_DEV,) + b.shape, b.dtype) for b in late_blocks)

    (h1, gathered_late), mixing_vjp = jax.vjp(_mixing_half, whole(gathered_early, EARLY_NAMES), small, x[0],
                                              late_blocks, carrier)
    loss_local, (d_late, d_small_b, d_h1) = jax.value_and_grad(_channel_half_loss, argnums=(0, 1, 2))(
        whole(gathered_late, LATE_NAMES), small, h1, p[0, 0], loss_target[0])
    d_early, d_small_a, d_x, _, parts_late = mixing_vjp((d_h1, stacked(d_late, LATE_NAMES, gathered_late)))
    d_small = {n: d_small_a[n] + d_small_b[n] for n in SMALL_NAMES}
    loss = lax.psum(loss_local, MESH_AXES)

    parts_early, small_parts = _exchange_grads(stacked(d_early, EARLY_NAMES, gathered_early),
                                               _pack([d_small[n] for n in SMALL_NAMES], F32), "exchange_grads")
    parts = dict(zip(EARLY_NAMES + LATE_NAMES, list(parts_early) + list(parts_late)))

    def pack_f32(tree, names):
        return _pack([tree[n] for n in names], F32)

    sml = _reduce_adamw(small_parts, pack_f32(weights, SMALL_NAMES), pack_f32(m_in, SMALL_NAMES),
                        pack_f32(v_in, SMALL_NAMES), "adamw_replicated")
    by_kind = [dict(zip(SMALL_NAMES, _unpack(sml[kind], small_shapes))) for kind in range(4)]
    for n in SHARDED_NAMES:
        g = _travel_layout(n, _reduce_parts(parts[n], "reduce_" + n))
        upd = _adamw(g, weights[n][0], m_in[n][0], v_in[n][0], "adamw_" + n)
        for kind, val in enumerate((g, *upd)):
            by_kind[kind][n] = val[None]
    outs = [by_kind[kind][n] for kind in range(4) for n in WEIGHT_NAMES]
    return (loss, d_x[None], *outs)
```
